```python
import math
import jax, jax.numpy as jnp
from jax import lax
import numpy as np

D_MODEL = 1024
BATCH = 16
SEQ = 2048
DEPTH = 1
DEC_BATCH = 32
DEC_SEQ = 2048
PAST_LEN = 128

HY_WIDTH = D_MODEL // 2
HY_ORDER = 2
HY_EMB_BANDS = 16
HY_EMB_DIM = 2 * HY_EMB_BANDS + 1
HY_FILTER_DIM = 64
HY_DECAY_TARGET = 1e-2
HY_FAST_DECAY_PCT = 0.3
HY_SLOW_DECAY_PCT = 1.5
MLA_HEADS = 8
V_HEAD = HY_WIDTH // MLA_HEADS
QK_NOPE = V_HEAD
QK_ROPE = V_HEAD // 2
QK_HEAD = QK_NOPE + QK_ROPE
Q_LORA = D_MODEL // 4
KV_LORA = D_MODEL // 8
ROPE_THETA = 10000.0
Q_BLOCK = 128
HY_END = 3 * HY_WIDTH
Q_END = HY_END + Q_LORA
KV_END = Q_END + KV_LORA
KR_END = KV_END + QK_ROPE
G1_END = KR_END + D_MODEL
IN_COLS = G1_END + D_MODEL
N_EXPERTS = 32
TOP_K = 4
D_FF_EXPERT = D_MODEL
SWIGLU_LIMIT = 7.0
SWIGLU_ALPHA = 1.702
MOE_BLOCK = 128
EPS = 1e-6
F32 = jnp.float32

kernel_name = 'hyena_mla_gated_moe_encoder'


def rmsnorm(x, g):
    xf = x.astype(F32)
    y = xf * lax.rsqrt(jnp.mean(xf * xf, axis=-1, keepdims=True) + EPS)
    return (y * g.astype(F32)).astype(x.dtype)


def short_conv(u, w, b):
    L = u.shape[1]
    up = jnp.pad(u, ((0, 0), (1, 1), (0, 0)))
    return up[:, :L] * w[0] + up[:, 1:L + 1] * w[1] + up[:, 2:] * w[2] + b


def hyena_filters(L, fw1, fb1, fw2, fb2, fw3, freq, decay):
    pos = jnp.arange(L, dtype=F32)
    t01 = pos / max(L - 1, 1)
    bands = jnp.linspace(1e-4, HY_EMB_BANDS - 1, HY_EMB_BANDS, dtype=F32)
    ang = (2.0 * math.pi / L) * pos[:, None] * bands[None, :]
    feats = jnp.concatenate([t01[:, None], jnp.cos(ang), -jnp.sin(ang)], axis=-1)
    fr = freq.astype(F32)
    z = jnp.sin(fr[0] * (feats @ fw1.astype(F32) + fb1.astype(F32)))
    z = jnp.sin(fr[1] * (z @ fw2.astype(F32) + fb2.astype(F32)))
    h = (z @ fw3.astype(F32)).reshape(L, 2, HY_ORDER, HY_WIDTH)
    window = jnp.exp(-t01[:, None, None] * jnp.abs(decay.astype(F32))[None])
    h = h * window[:, None]
    h_fwd, h_bwd = h[:, 0], h[:, 1]
    zero = jnp.zeros((1, HY_ORDER, HY_WIDTH), F32)
    return jnp.concatenate([h_fwd, zero, h_bwd[1:][::-1]], axis=0)


def fft_conv(z, filt_f, skip):
    L = z.shape[1]
    zf = jnp.fft.rfft(z.astype(F32), n=2 * L, axis=1)
    y = jnp.fft.irfft(zf * filt_f[None], n=2 * L, axis=1)[:, :L]
    return (y + z.astype(F32) * skip.astype(F32)).astype(z.dtype)


def hyena_mixer(u, conv_w, conv_b, fw1, fb1, fw2, fb2, fw3, freq, decay, skip):
    L = u.shape[1]
    u = short_conv(u, conv_w, conv_b)
    x1, x2, v = jnp.split(u, 3, axis=-1)
    filt_f = jnp.fft.rfft(hyena_filters(L, fw1, fb1, fw2, fb2, fw3, freq, decay), axis=0)
    z = v
    for o, gate in enumerate((x1, x2)):
        z = gate * fft_conv(z, filt_f[:, o], skip[o])
    return z


def rope_tables(L):
    inv = 1.0 / (ROPE_THETA ** (jnp.arange(0, QK_ROPE, 2, dtype=F32) / QK_ROPE))
    ang = jnp.arange(L, dtype=F32)[:, None] * inv[None, :]
    return jnp.cos(ang), jnp.sin(ang)


def apply_rope(x, cos, sin):
    xf = x.astype(F32)
    half = QK_ROPE // 2
    x1, x2 = xf[..., :half], xf[..., half:]
    c, s = cos[None, :, None, :], sin[None, :, None, :]
    return jnp.concatenate([x1 * c - x2 * s, x1 * s + x2 * c], axis=-1).astype(x.dtype)


def mla_mixer(c_q, c_kv, k_rope, q_a_norm, w_uq, kv_a_norm, w_ukv, q_norm, k_norm):
    B, L, _ = c_q.shape
    q = (rmsnorm(c_q, q_a_norm) @ w_uq).reshape(B, L, MLA_HEADS, QK_HEAD)
    kv = (rmsnorm(c_kv, kv_a_norm) @ w_ukv).reshape(B, L, MLA_HEADS, QK_NOPE + V_HEAD)
    k_nope, v = kv[..., :QK_NOPE], kv[..., QK_NOPE:]
    k_r = jnp.broadcast_to(k_rope[:, :, None, :], (B, L, MLA_HEADS, QK_ROPE))
    k = jnp.concatenate([k_nope, k_r], axis=-1)
    q = rmsnorm(q, q_norm)
    k = rmsnorm(k, k_norm)
    cos, sin = rope_tables(L)
    q = jnp.concatenate([q[..., :QK_NOPE], apply_rope(q[..., QK_NOPE:], cos, sin)], axis=-1)
    k = jnp.concatenate([k[..., :QK_NOPE], apply_rope(k[..., QK_NOPE:], cos, sin)], axis=-1)
    scale = QK_HEAD ** -0.5
    nb = L // Q_BLOCK
    qb = q.reshape(B, nb, Q_BLOCK, MLA_HEADS, QK_HEAD).transpose(1, 0, 2, 3, 4)

    def attend(qblk):
        s = jnp.einsum('bqhd,bkhd->bhqk', qblk, k, preferred_element_type=F32) * scale
        p = jax.nn.softmax(s, axis=-1)
        return jnp.einsum('bhqk,bkhv->bqhv', p.astype(v.dtype), v)

    o = lax.map(attend, qb)
    return o.transpose(1, 0, 2, 3, 4).reshape(B, L, MLA_HEADS * V_HEAD)


def moe(h, router_w, router_b, w_gu, b_gu, w_down, b_down):
    T, D = h.shape
    logits = h.astype(F32) @ router_w.astype(F32) + router_b.astype(F32)
    top_v, top_e = lax.top_k(logits, TOP_K)
    gates = jax.nn.softmax(top_v, axis=-1)
    flat_e = top_e.reshape(-1).astype(jnp.int32)
    order = jnp.argsort(flat_e)
    e_sorted = flat_e[order]
    counts = jnp.zeros((N_EXPERTS,), jnp.int32).at[flat_e].add(1)
    padded = (counts + MOE_BLOCK - 1) // MOE_BLOCK * MOE_BLOCK
    pad_end = jnp.cumsum(padded)
    pad_start = pad_end - padded
    start = jnp.cumsum(counts) - counts
    n_assign = T * TOP_K
    rank = jnp.arange(n_assign, dtype=jnp.int32) - start[e_sorted]
    dest = pad_start[e_sorted] + rank
    n_blocks = -(-(n_assign + N_EXPERTS * (MOE_BLOCK - 1)) // MOE_BLOCK)
    n_rows = n_blocks * MOE_BLOCK
    row_tok = jnp.zeros((n_rows,), jnp.int32).at[dest].set((order // TOP_K).astype(jnp.int32))
    row_gate = jnp.zeros((n_rows,), F32).at[dest].set(gates.reshape(-1)[order])
    block_start = jnp.arange(n_blocks, dtype=jnp.int32) * MOE_BLOCK
    block_e = jnp.minimum(jnp.searchsorted(pad_end, block_start, side='right'), N_EXPERTS - 1)

    def expert_block(args):
        tok, g, e = args
        xe = h[tok]
        gu = xe @ w_gu[e] + b_gu[e]
        gate = jnp.minimum(gu[:, :D_FF_EXPERT], SWIGLU_LIMIT)
        up = jnp.clip(gu[:, D_FF_EXPERT:], -SWIGLU_LIMIT, SWIGLU_LIMIT)
        act = (up + 1.0) * (gate * jax.nn.sigmoid(SWIGLU_ALPHA * gate))
        return (act @ w_down[e] + b_down[e]).astype(F32) * g[:, None]

    y_rows = lax.map(expert_block, (row_tok.reshape(n_blocks, MOE_BLOCK), row_gate.reshape(n_blocks, MOE_BLOCK), block_e))
    y = jnp.zeros((T, D), F32).at[row_tok].add(y_rows.reshape(n_rows, D))
    return y.astype(h.dtype)


def encoder_layer(x, norm_mix, w_in, hy_conv_w, hy_conv_b, hy_fw1, hy_fb1, hy_fw2, hy_fb2, hy_fw3, hy_freq, hy_decay, hy_skip,
                  q_a_norm, w_uq, kv_a_norm, w_ukv, q_norm, k_norm, w_hy_o, w_mla_o, w_out,
                  norm_ffn, router_w, router_b, w_gu, b_gu, w_down, b_down):
    B, L, D = x.shape
    h = rmsnorm(x, norm_mix)
    proj = h @ w_in
    u_hy, c_q, c_kv, k_rope, g_hy, g_mla = jnp.split(proj, [HY_END, Q_END, KV_END, KR_END, G1_END], axis=-1)
    y_hy = hyena_mixer(u_hy, hy_conv_w, hy_conv_b, hy_fw1, hy_fb1, hy_fw2, hy_fb2, hy_fw3, hy_freq, hy_decay, hy_skip)
    y_mla = mla_mixer(c_q, c_kv, k_rope, q_a_norm, w_uq, kv_a_norm, w_ukv, q_norm, k_norm)
    merged = jax.nn.sigmoid(g_hy) * (y_hy @ w_hy_o) + jax.nn.sigmoid(g_mla) * (y_mla @ w_mla_o)
    x = x + merged @ w_out
    h2 = rmsnorm(x, norm_ffn).reshape(B * L, D)
    return x + moe(h2, router_w, router_b, w_gu, b_gu, w_down, b_down).reshape(B, L, D)


def setup_inputs(seed: int = 0) -> dict:
    key = jax.random.key(seed)
    ks = list(jax.random.split(key, 40))

    def nrm(shape, scale):
        return jax.random.normal(ks.pop(), shape, F32) * scale

    def gain(shape):
        return 1.0 + nrm(shape, 0.02)

    rate_lo = abs(math.log(HY_DECAY_TARGET) / HY_SLOW_DECAY_PCT)
    rate_hi = abs(math.log(HY_DECAY_TARGET) / HY_FAST_DECAY_PCT)
    decay_base = jnp.linspace(rate_lo, rate_hi, HY_WIDTH, dtype=F32)
    return {
        'x_prompt': nrm((BATCH, SEQ, D_MODEL), 1.0),
        'x_sample': nrm((DEC_BATCH, DEC_SEQ, D_MODEL), 1.0),
        'norm_mix': gain((DEPTH, D_MODEL)),
        'w_in': nrm((DEPTH, D_MODEL, IN_COLS), D_MODEL ** -0.5),
        'hy_conv_w': nrm((DEPTH, 3, 3 * HY_WIDTH), 3 ** -0.5),
        'hy_conv_b': nrm((DEPTH, 3 * HY_WIDTH), 0.02),
        'hy_fw1': nrm((DEPTH, HY_EMB_DIM, HY_FILTER_DIM), HY_EMB_DIM ** -0.5),
        'hy_fb1': nrm((DEPTH, HY_FILTER_DIM), 0.1),
        'hy_fw2': nrm((DEPTH, HY_FILTER_DIM, HY_FILTER_DIM), HY_FILTER_DIM ** -0.5),
        'hy_fb2': nrm((DEPTH, HY_FILTER_DIM), 0.1),
        'hy_fw3': nrm((DEPTH, HY_FILTER_DIM, 2 * HY_ORDER * HY_WIDTH), 0.01),
        'hy_freq': gain((DEPTH, 2, HY_FILTER_DIM)),
        'hy_decay': decay_base[None, None, :] + nrm((DEPTH, HY_ORDER, HY_WIDTH), 0.1),
        'hy_skip': nrm((DEPTH, HY_ORDER, HY_WIDTH), 0.1),
        'q_a_norm': gain((DEPTH, Q_LORA)),
        'w_uq': nrm((DEPTH, Q_LORA, MLA_HEADS * QK_HEAD), Q_LORA ** -0.5),
        'kv_a_norm': gain((DEPTH, KV_LORA)),
        'w_ukv': nrm((DEPTH, KV_LORA, MLA_HEADS * (QK_NOPE + V_HEAD)), KV_LORA ** -0.5),
        'q_norm': gain((DEPTH, QK_HEAD)),
        'k_norm': gain((DEPTH, QK_HEAD)),
        'w_hy_o': nrm((DEPTH, HY_WIDTH, D_MODEL), HY_WIDTH ** -0.5),
        'w_mla_o': nrm((DEPTH, MLA_HEADS * V_HEAD, D_MODEL), (MLA_HEADS * V_HEAD) ** -0.5),
        'w_out': nrm((DEPTH, D_MODEL, D_MODEL), D_MODEL ** -0.5),
        'norm_ffn': gain((DEPTH, D_MODEL)),
        'router_w': nrm((DEPTH, D_MODEL, N_EXPERTS), D_MODEL ** -0.5),
        'router_b': nrm((DEPTH, N_EXPERTS), 0.01),
        'w_gu': nrm((DEPTH, N_EXPERTS, D_MODEL, 2 * D_FF_EXPERT), D_MODEL ** -0.5),
        'b_gu': nrm((DEPTH, N_EXPERTS, 2 * D_FF_EXPERT), 0.02),
        'w_down': nrm((DEPTH, N_EXPERTS, D_FF_EXPERT, D_MODEL), D_FF_EXPERT ** -0.5),
        'b_down': nrm((DEPTH, N_EXPERTS, D_MODEL), 0.02),
    }


def reference(x_prompt, x_sample, norm_mix, w_in, hy_conv_w, hy_conv_b, hy_fw1, hy_fb1, hy_fw2, hy_fb2, hy_fw3, hy_freq,
              hy_decay, hy_skip, q_a_norm, w_uq, kv_a_norm, w_ukv, q_norm, k_norm, w_hy_o, w_mla_o, w_out,
              norm_ffn, router_w, router_b, w_gu, b_gu, w_down, b_down):
    def run(x):
        for l in range(DEPTH):
            x = encoder_layer(x, norm_mix[l], w_in[l], hy_conv_w[l], hy_conv_b[l], hy_fw1[l], hy_fb1[l], hy_fw2[l],
                              hy_fb2[l], hy_fw3[l], hy_freq[l], hy_decay[l], hy_skip[l], q_a_norm[l], w_uq[l],
                              kv_a_norm[l], w_ukv[l], q_norm[l], k_norm[l], w_hy_o[l], w_mla_o[l], w_out[l],
                              norm_ffn[l], router_w[l], router_b[l], w_gu[l], b_gu[l], w_down[l], b_down[l])
        return x

    y_prompt = run(x_prompt)
    y_sample = run(x_sample)
    return (y_prompt, y_sample)
```

```python
import functools
import math

import jax
import jax.numpy as jnp
from jax import lax
from jax.experimental import pallas as pl
from jax.experimental.pallas import tpu as pltpu

F32 = jnp.float32
BF16 = jnp.bfloat16

D_MODEL = 1024
HY_WIDTH = 512
HY_ORDER = 2
HY_EMB_BANDS = 16
HY_FILTER_DIM = 64
MLA_HEADS = 8
V_HEAD = 64
QK_NOPE = 64
QK_ROPE = 32
QK_HEAD = QK_NOPE + QK_ROPE
Q_LORA = 256
KV_LORA = 128
ROPE_THETA = 10000.0
HY_END = 3 * HY_WIDTH
Q_END = HY_END + Q_LORA
KV_END = Q_END + KV_LORA
KR_END = KV_END + QK_ROPE
G1_END = KR_END + D_MODEL
N_EXPERTS = 32
TOP_K = 4
D_FF = D_MODEL
SWIGLU_LIMIT = 7.0
SWIGLU_ALPHA = 1.702
EPS = 1e-6

LANE = 128
MXU = 256
HEAD_PAD = LANE
MLA_IN = 512
CONV_CB = 8
MOE_BLOCK = 256
VMEM_LIMIT = 52 * 1024 * 1024


def _cparams(n_axes):
    return pltpu.CompilerParams(dimension_semantics=("arbitrary",) * n_axes, vmem_limit_bytes=VMEM_LIMIT)


def _full(shape):
    n = len(shape)
    return pl.BlockSpec(shape, lambda *_: (0,) * n)


def _inproj_kernel(x_ref, g_ref, why_ref, wm_ref, wg_ref, u_ref, m_ref, gate_ref):
    x = x_ref[...]
    r = lax.rsqrt(jnp.mean(x * x, axis=-1, keepdims=True) + EPS)
    h = (x * r * g_ref[...]).astype(BF16)
    u_ref[...] = jnp.dot(h, why_ref[...], preferred_element_type=F32).astype(BF16)
    m_ref[...] = jnp.dot(h, wm_ref[...], preferred_element_type=F32).astype(BF16)
    gate_ref[...] = jnp.dot(h, wg_ref[...], preferred_element_type=F32).astype(BF16)


def _inproj(x, g, w_hy, w_m, w_g, tm):
    T = x.shape[0]
    return pl.pallas_call(
        _inproj_kernel,
        grid=(T // tm,),
        in_specs=[
            pl.BlockSpec((tm, D_MODEL), lambda i: (i, 0)),
            _full((1, D_MODEL)),
            _full(w_hy.shape),
            _full(w_m.shape),
            _full(w_g.shape),
        ],
        out_specs=[
            pl.BlockSpec((tm, HY_END), lambda i: (i, 0)),
            pl.BlockSpec((tm, MLA_IN), lambda i: (i, 0)),
            pl.BlockSpec((tm, 2 * D_MODEL), lambda i: (i, 0)),
        ],
        out_shape=[
            jax.ShapeDtypeStruct((T, HY_END), BF16),
            jax.ShapeDtypeStruct((T, MLA_IN), BF16),
            jax.ShapeDtypeStruct((T, 2 * D_MODEL), BF16),
        ],
        compiler_params=_cparams(1),
        name="inproj",
    )(x, g, w_hy, w_m, w_g)


def _filter_kernel(bands_ref, w1t_ref, w1c_ref, w1s_ref, b1_ref, w2_ref, b2_ref, fr_ref, w3f_ref, w3b_ref, dec_ref,
                   out_ref, *, L):
    hi = lax.Precision.HIGHEST
    n = 2 * L
    j = lax.broadcasted_iota(jnp.int32, (1, n), 1)
    pos = jnp.abs(j - L).astype(F32)
    t01 = pos / max(L - 1, 1)
    ang = ((2.0 * math.pi / L) * pos) * bands_ref[...]
    pre = (w1t_ref[...] * t01
           + jnp.dot(w1c_ref[...], jnp.cos(ang), precision=hi, preferred_element_type=F32)
           - jnp.dot(w1s_ref[...], jnp.sin(ang), precision=hi, preferred_element_type=F32)
           + b1_ref[...])
    z = jnp.sin(fr_ref[0] * pre)
    z = jnp.sin(fr_ref[1] * (jnp.dot(w2_ref[...], z, precision=hi, preferred_element_type=F32) + b2_ref[...]))
    win = jnp.exp(-t01 * jnp.abs(dec_ref[...]))
    h_b = jnp.dot(w3b_ref[...], z[:, :L], precision=hi, preferred_element_type=F32) * win[:, :L]
    h_f = jnp.dot(w3f_ref[...], z[:, L:], precision=hi, preferred_element_type=F32) * win[:, L:]
    out_ref[:, :L] = jnp.where(j[:, :L] == 0, 0.0, h_b)
    out_ref[:, L:] = h_f


def _hyena_filters(L, fw1, fb1, fw2, fb2, fw3, freq, decay):
    rows = HY_ORDER * HY_WIDTH
    rb = 256
    bands = jnp.linspace(1e-4, HY_EMB_BANDS - 1, HY_EMB_BANDS, dtype=F32).reshape(HY_EMB_BANDS, 1)
    w1 = fw1.astype(F32).T
    w3 = fw3.astype(F32).T
    fd = HY_FILTER_DIM
    return pl.pallas_call(
        functools.partial(_filter_kernel, L=L),
        grid=(rows // rb,),
        in_specs=[
            _full((HY_EMB_BANDS, 1)),
            _full((fd, 1)),
            _full((fd, HY_EMB_BANDS)),
            _full((fd, HY_EMB_BANDS)),
            _full((fd, 1)),
            _full((fd, fd)),
            _full((fd, 1)),
            _full((2, fd, 1)),
            pl.BlockSpec((rb, fd), lambda i: (i, 0)),
            pl.BlockSpec((rb, fd), lambda i: (i, 0)),
            pl.BlockSpec((rb, 1), lambda i: (i, 0)),
        ],
        out_specs=pl.BlockSpec((rb, 2 * L), lambda i: (i, 0)),
        out_shape=jax.ShapeDtypeStruct((rows, 2 * L), F32),
        compiler_params=_cparams(1),
        name="hyena_filters",
    )(bands, w1[:, :1], w1[:, 1:1 + HY_EMB_BANDS], w1[:, 1 + HY_EMB_BANDS:], fb1.astype(F32).reshape(fd, 1),
      fw2.astype(F32).T, fb2.astype(F32).reshape(fd, 1), freq.astype(F32).reshape(2, fd, 1),
      w3[:rows], w3[rows:], decay.astype(F32).reshape(rows, 1))


def _shift_conv(u, w0, w1, w2, b):
    L = u.shape[1]
    lane = lax.broadcasted_iota(jnp.int32, u.shape, 1)
    prev = jnp.where(lane == 0, 0.0, pltpu.roll(u, 1, 1))
    nxt = jnp.where(lane == L - 1, 0.0, pltpu.roll(u, L - 1, 1))
    return prev * w0 + u * w1 + nxt * w2 + b


def _toeplitz_conv(z, tz_ref, L):
    B = z.shape[0]
    nj = L // MXU
    zb = z.astype(BF16)
    tiles = [zb[:, MXU * J:MXU * (J + 1)] for J in range(nj)]
    acc = [None] * nj
    for d in range(-(nj - 1), nj):
        js = [J for J in range(nj) if 0 <= J + d < nj]
        lhs = tiles[js[0]] if len(js) == 1 else jnp.concatenate([tiles[J] for J in js], axis=0)
        res = jnp.dot(lhs, tz_ref[:, L + MXU * d:L + MXU * (d + 1)], preferred_element_type=F32)
        for n, J in enumerate(js):
            part = res[n * B:(n + 1) * B]
            acc[J + d] = part if acc[J + d] is None else acc[J + d] + part
    return acc[0] if nj == 1 else jnp.concatenate(acc, axis=1)


def _hyena_kernel(cw_ref, cb_ref, skip_ref, kf_ref, *refs, L, n_groups):
    in_refs = refs[:3 * n_groups]
    out_refs = refs[3 * n_groups:4 * n_groups]
    tz_ref = refs[4 * n_groups]
    c0 = pl.program_id(0) * CONV_CB

    def channel(c, carry):
        ch = c0 + c
        for o in range(HY_ORDER):
            row = jnp.broadcast_to(kf_ref[o, pl.ds(c, 1), :], (MXU, 2 * L))
            tz_ref[o] = pltpu.roll(row, 0, 1, stride=1, stride_axis=0).astype(BF16)
        for g in range(n_groups):
            x1_ref, x2_ref, v_ref = in_refs[3 * g:3 * g + 3]
            sec = []
            for s, ref in enumerate((x1_ref, x2_ref, v_ref)):
                k = s * HY_WIDTH + ch
                sec.append(_shift_conv(ref[c].astype(F32), cw_ref[0, k], cw_ref[1, k], cw_ref[2, k], cb_ref[k]))
            x1, x2, z = sec
            for o, gate in enumerate((x1, x2)):
                y = _toeplitz_conv(z, tz_ref.at[o], L) + z * skip_ref[o, ch]
                z = gate * y
            out_refs[g][c] = z.astype(BF16)
        return carry

    lax.fori_loop(0, CONV_CB, channel, 0)


def _hyena(u_ts, kf, conv_w, conv_b, skip, L):
    n_groups = len(u_ts)
    nblk = HY_WIDTH // CONV_CB
    smem = pl.BlockSpec(memory_space=pltpu.SMEM)
    in_specs = [smem, smem, smem, pl.BlockSpec((HY_ORDER, CONV_CB, 2 * L), lambda i: (0, i, 0))]
    args = [conv_w.astype(F32), conv_b.astype(F32), skip.astype(F32), kf.reshape(HY_ORDER, HY_WIDTH, 2 * L)]
    out_specs, out_shape = [], []
    for u in u_ts:
        B = u.shape[1]
        for s in range(3):
            in_specs.append(pl.BlockSpec((CONV_CB, B, L), lambda i, s=s: (i + s * nblk, 0, 0)))
            args.append(u)
        out_specs.append(pl.BlockSpec((CONV_CB, B, L), lambda i: (i, 0, 0)))
        out_shape.append(jax.ShapeDtypeStruct((HY_WIDTH, B, L), BF16))
    return pl.pallas_call(
        functools.partial(_hyena_kernel, L=L, n_groups=n_groups),
        grid=(nblk,),
        in_specs=in_specs,
        out_specs=out_specs,
        out_shape=out_shape,
        scratch_shapes=[pltpu.VMEM((HY_ORDER, MXU, 2 * L), BF16)],
        compiler_params=_cparams(1),
        name="hyena_conv",
    )(*args)


def _rms(x, n):
    return lax.rsqrt(jnp.sum(x * x, axis=-1, keepdims=True) * (1.0 / n) + EPS)


def _mla_prep_kernel(m_ref, ct_ref, st_ref, gqa_ref, gkva_ref, gq_ref, gqs_ref, gk_ref, gks_ref,
                     wqa_ref, wqb_ref, wka_ref, ea_ref, eb_ref, wv_ref, q_ref, k_ref, v_ref):
    m = m_ref[0].astype(F32)
    cq = m[:, :Q_LORA]
    ckv = m[:, Q_LORA:Q_LORA + KV_LORA]
    kr = m_ref[0][:, Q_LORA + KV_LORA:Q_LORA + KV_LORA + LANE]
    cqn = (cq * _rms(cq, Q_LORA) * gqa_ref[...]).astype(BF16)
    ckvn = (ckv * _rms(ckv, KV_LORA) * gkva_ref[...]).astype(BF16)
    qa = jnp.dot(cqn, wqa_ref[...], preferred_element_type=F32)
    qb = jnp.dot(cqn, wqb_ref[...], preferred_element_type=F32)
    ka = jnp.dot(ckvn, wka_ref[...], preferred_element_type=F32) + jnp.dot(kr, ea_ref[...], preferred_element_type=F32)
    kb = jnp.dot(kr, eb_ref[...], preferred_element_type=F32)
    v_ref[0] = jnp.dot(ckvn, wv_ref[...], preferred_element_type=F32).astype(BF16)
    ct, st = ct_ref[...], st_ref[...]
    scale = QK_HEAD ** -0.5
    qc, qs = gq_ref[...] * ct * scale, gqs_ref[...] * st * scale
    kc, ks = gk_ref[...] * ct, gks_ref[...] * st
    for h in range(MLA_HEADS):
        sl = slice(h * HEAD_PAD, (h + 1) * HEAD_PAD)
        q_ref[0, h] = (_rms(qa[:, sl], QK_HEAD) * (qa[:, sl] * qc + qb[:, sl] * qs)).astype(BF16)
        k_ref[0, h] = (_rms(ka[:, sl], QK_HEAD) * (ka[:, sl] * kc + kb[:, sl] * ks)).astype(BF16)


def _mla_prep(m, P, tm):
    B, L, _ = m.shape
    w = [P[n] for n in ("wqa", "wqb", "wka", "ea", "eb", "wv")]
    g = [P[n] for n in ("gqa", "gkva", "gq", "gqs", "gk", "gks")]
    hp = MLA_HEADS * HEAD_PAD
    return pl.pallas_call(
        _mla_prep_kernel,
        grid=(B, L // tm),
        in_specs=[pl.BlockSpec((1, tm, MLA_IN), lambda b, i: (b, i, 0)),
                  pl.BlockSpec((tm, HEAD_PAD), lambda b, i: (i, 0)),
                  pl.BlockSpec((tm, HEAD_PAD), lambda b, i: (i, 0))]
        + [_full(a.shape) for a in g] + [_full(a.shape) for a in w],
        out_specs=[
            pl.BlockSpec((1, MLA_HEADS, tm, HEAD_PAD), lambda b, i: (b, 0, i, 0)),
            pl.BlockSpec((1, MLA_HEADS, tm, HEAD_PAD), lambda b, i: (b, 0, i, 0)),
            pl.BlockSpec((1, tm, MLA_HEADS * V_HEAD), lambda b, i: (b, i, 0)),
        ],
        out_shape=[
            jax.ShapeDtypeStruct((B, MLA_HEADS, L, HEAD_PAD), BF16),
            jax.ShapeDtypeStruct((B, MLA_HEADS, L, HEAD_PAD), BF16),
            jax.ShapeDtypeStruct((B, L, MLA_HEADS * V_HEAD), BF16),
        ],
        compiler_params=_cparams(2),
        name="mla_prep",
    )(m, P["ctab"][:L], P["stab"][:L], *g, *w)


def _attn_kernel(q_ref, k_ref, v_ref, o_ref):
    v = v_ref[0]
    outs = []
    for h in range(2):
        s = lax.dot_general(q_ref[0, h], k_ref[0, h], (((1,), (1,)), ((), ())), preferred_element_type=F32)
        p = jnp.exp(s - jnp.max(s, axis=-1, keepdims=True))
        l = jnp.sum(p, axis=-1, keepdims=True)
        outs.append(jnp.dot(p.astype(BF16), v, preferred_element_type=F32) / l)
    lane = lax.broadcasted_iota(jnp.int32, outs[0].shape, 1)
    o_ref[0] = jnp.where(lane < V_HEAD, outs[0], outs[1]).astype(BF16)


def _attention(q, k, v, tq):
    B, H, L, _ = q.shape
    return pl.pallas_call(
        _attn_kernel,
        grid=(B, H // 2, L // tq),
        in_specs=[
            pl.BlockSpec((1, 2, tq, HEAD_PAD), lambda b, h, i: (b, h, i, 0)),
            pl.BlockSpec((1, 2, L, HEAD_PAD), lambda b, h, i: (b, h, 0, 0)),
            pl.BlockSpec((1, L, 2 * V_HEAD), lambda b, h, i: (b, 0, h)),
        ],
        out_specs=pl.BlockSpec((1, tq, 2 * V_HEAD), lambda b, h, i: (b, i, h)),
        out_shape=jax.ShapeDtypeStruct((B, L, H * V_HEAD), BF16),
        compiler_params=_cparams(3),
        name="mla_attention",
    )(q, k, v)


def _merge_kernel(x_ref, yh_ref, ym_ref, g_ref, who_ref, wmo_ref, wo_ref, nf_ref, rw_ref, rb_ref,
                  x1_ref, h2_ref, lg_ref):
    g = g_ref[...].astype(F32)
    a = jnp.dot(yh_ref[...], who_ref[...], preferred_element_type=F32)
    b = jnp.dot(ym_ref[...], wmo_ref[...], preferred_element_type=F32)
    merged = jax.nn.sigmoid(g[:, :D_MODEL]) * a + jax.nn.sigmoid(g[:, D_MODEL:]) * b
    x1 = x_ref[...] + jnp.dot(merged.astype(BF16), wo_ref[...], preferred_element_type=F32)
    x1_ref[...] = x1
    h2 = x1 * lax.rsqrt(jnp.mean(x1 * x1, axis=-1, keepdims=True) + EPS) * nf_ref[...]
    h2_ref[...] = h2.astype(BF16)
    lg_ref[...] = jnp.dot(h2, rw_ref[...], precision=lax.Precision.HIGHEST, preferred_element_type=F32) + rb_ref[...]


def _merge(x, yh, ym, g, P, tm):
    T = x.shape[0]
    w = [P[n] for n in ("w_hy_o", "w_mla_o", "w_out", "norm_ffn", "router_w", "router_b")]
    return pl.pallas_call(
        _merge_kernel,
        grid=(T // tm,),
        in_specs=[
            pl.BlockSpec((tm, D_MODEL), lambda i: (i, 0)),
            pl.BlockSpec((tm, HY_WIDTH), lambda i: (i, 0)),
            pl.BlockSpec((tm, MLA_HEADS * V_HEAD), lambda i: (i, 0)),
            pl.BlockSpec((tm, 2 * D_MODEL), lambda i: (i, 0)),
        ] + [_full(a.shape) for a in w],
        out_specs=[
            pl.BlockSpec((tm, D_MODEL), lambda i: (i, 0)),
            pl.BlockSpec((tm, D_MODEL), lambda i: (i, 0)),
            pl.BlockSpec((tm, LANE), lambda i: (i, 0)),
        ],
        out_shape=[
            jax.ShapeDtypeStruct((T, D_MODEL), F32),
            jax.ShapeDtypeStruct((T, D_MODEL), BF16),
            jax.ShapeDtypeStruct((T, LANE), F32),
        ],
        compiler_params=_cparams(1),
        name="merge_router",
    )(x, yh, ym, g, *w)


def _expert_kernel(be_ref, na_ref, x_ref, wgu_ref, bgu_ref, wd_ref, bd_ref, y_ref):
    i = pl.program_id(0)

    @pl.when(i < na_ref[0])
    def _():
        gu = jnp.dot(x_ref[...], wgu_ref[0], preferred_element_type=F32) + bgu_ref[0]
        gate = jnp.minimum(gu[:, :D_FF], SWIGLU_LIMIT)
        up = jnp.clip(gu[:, D_FF:], -SWIGLU_LIMIT, SWIGLU_LIMIT)
        act = (up + 1.0) * (gate * jax.nn.sigmoid(SWIGLU_ALPHA * gate))
        y_ref[...] = jnp.dot(act.astype(BF16), wd_ref[0], preferred_element_type=F32) + bd_ref[0]

    @pl.when(i >= na_ref[0])
    def _():
        y_ref[...] = jnp.zeros_like(y_ref)


def _experts(block_e, n_active, xs, P):
    n_rows = xs.shape[0]
    nb = n_rows // MOE_BLOCK
    grid_spec = pltpu.PrefetchScalarGridSpec(
        num_scalar_prefetch=2,
        grid=(nb,),
        in_specs=[
            pl.BlockSpec((MOE_BLOCK, D_MODEL), lambda i, be, na: (i, 0)),
            pl.BlockSpec((1, D_MODEL, 2 * D_FF), lambda i, be, na: (be[i], 0, 0)),
            pl.BlockSpec((1, 1, 2 * D_FF), lambda i, be, na: (be[i], 0, 0)),
            pl.BlockSpec((1, D_FF, D_MODEL), lambda i, be, na: (be[i], 0, 0)),
            pl.BlockSpec((1, 1, D_MODEL), lambda i, be, na: (be[i], 0, 0)),
        ],
        out_specs=pl.BlockSpec((MOE_BLOCK, D_MODEL), lambda i, be, na: (i, 0)),
    )
    return pl.pallas_call(
        _expert_kernel,
        grid_spec=grid_spec,
        out_shape=jax.ShapeDtypeStruct((n_rows, D_MODEL), F32),
        compiler_params=_cparams(1),
        name="moe_experts",
    )(block_e, n_active, xs, P["w_gu"], P["b_gu"], P["w_down"], P["b_down"])


def _combine_kernel(x1_ref, yk_ref, gt_ref, o_ref):
    gt = gt_ref[...]
    acc = yk_ref[0] * gt[:, 0:1]
    for k in range(1, TOP_K):
        acc = acc + yk_ref[k] * gt[:, k:k + 1]
    o_ref[...] = x1_ref[...] + acc


def _combine(x1, yk, gates, tm):
    T = x1.shape[0]
    return pl.pallas_call(
        _combine_kernel,
        grid=(T // tm,),
        in_specs=[
            pl.BlockSpec((tm, D_MODEL), lambda i: (i, 0)),
            pl.BlockSpec((TOP_K, tm, D_MODEL), lambda i: (0, i, 0)),
            pl.BlockSpec((tm, TOP_K), lambda i: (i, 0)),
        ],
        out_specs=pl.BlockSpec((tm, D_MODEL), lambda i: (i, 0)),
        out_shape=jax.ShapeDtypeStruct((T, D_MODEL), F32),
        compiler_params=_cparams(1),
        name="moe_combine",
    )(x1, yk, gates)


def _route(logits):
    T = logits.shape[0]
    top_v, top_e = lax.top_k(logits[:, :N_EXPERTS], TOP_K)
    gates = jax.nn.softmax(top_v, axis=-1)
    flat_e = top_e.reshape(-1).astype(jnp.int32)
    onehot = (flat_e[:, None] == jnp.arange(N_EXPERTS, dtype=jnp.int32)[None, :]).astype(jnp.int32)
    csum = jnp.cumsum(onehot, axis=0)
    rank = jnp.take_along_axis(csum, flat_e[:, None], axis=1)[:, 0] - 1
    counts = csum[-1]
    padded = (counts + MOE_BLOCK - 1) // MOE_BLOCK * MOE_BLOCK
    pad_end = jnp.cumsum(padded)
    pad_start = pad_end - padded
    dest = pad_start[flat_e] + rank
    n_assign = T * TOP_K
    n_blocks = -(-(n_assign + N_EXPERTS * (MOE_BLOCK - 1)) // MOE_BLOCK)
    block_start = jnp.arange(n_blocks, dtype=jnp.int32) * MOE_BLOCK
    block_e = jnp.minimum(jnp.searchsorted(pad_end, block_start, side="right"), N_EXPERTS - 1).astype(jnp.int32)
    n_active = (pad_end[-1] // MOE_BLOCK).astype(jnp.int32).reshape(1)
    return dest.astype(jnp.int32), gates, block_e, n_active, n_blocks * MOE_BLOCK


def _moe(x1, h2, logits, P, tm):
    T = x1.shape[0]
    dest, gates, block_e, n_active, n_rows = _route(logits)
    tok = jnp.arange(T * TOP_K, dtype=jnp.int32) // TOP_K
    row_tok = jnp.zeros((n_rows,), jnp.int32).at[dest].set(tok)
    xs = h2[row_tok]
    ys = _experts(block_e, n_active, xs, P)
    yk = ys[dest.reshape(T, TOP_K).T]
    return _combine(x1, yk, gates, tm)


def _prep_params(norm_mix, w_in, q_a_norm, w_uq, kv_a_norm, w_ukv, q_norm, k_norm, w_hy_o, w_mla_o, w_out,
                 norm_ffn, router_w, router_b, w_gu, b_gu, w_down, b_down, L):
    P = {}
    P["norm_mix"] = norm_mix.astype(F32).reshape(1, D_MODEL)
    P["w_hy"] = w_in[:, :HY_END].astype(BF16)
    P["w_m"] = jnp.pad(w_in[:, HY_END:KR_END], ((0, 0), (0, MLA_IN - (KR_END - HY_END)))).astype(BF16)
    P["w_g"] = w_in[:, KR_END:].astype(BF16)

    half = QK_ROPE // 2
    swap = jnp.concatenate([jnp.arange(QK_NOPE), QK_NOPE + half + jnp.arange(half), QK_NOPE + jnp.arange(half)])
    rope_lane = (jnp.arange(QK_HEAD) >= QK_NOPE)

    def slots(w):
        return jnp.pad(w, ((0, 0), (0, 0), (0, HEAD_PAD - QK_HEAD))).reshape(w.shape[0], MLA_HEADS * HEAD_PAD)

    wq = w_uq.astype(F32).reshape(Q_LORA, MLA_HEADS, QK_HEAD)
    P["wqa"] = slots(wq).astype(BF16)
    P["wqb"] = slots(jnp.where(rope_lane, wq[:, :, swap], 0.0)).astype(BF16)
    wkv = w_ukv.astype(F32).reshape(KV_LORA, MLA_HEADS, QK_NOPE + V_HEAD)
    P["wka"] = slots(jnp.pad(wkv[:, :, :QK_NOPE], ((0, 0), (0, 0), (0, QK_ROPE)))).astype(BF16)
    P["wv"] = wkv[:, :, QK_NOPE:].reshape(KV_LORA, MLA_HEADS * V_HEAD).astype(BF16)
    eye = jnp.eye(LANE, QK_ROPE, dtype=F32)
    e_a = jnp.pad(eye, ((0, 0), (QK_NOPE, 0)))
    e_b = e_a[:, swap] * rope_lane
    P["ea"] = slots(jnp.broadcast_to(e_a[:, None, :], (LANE, MLA_HEADS, QK_HEAD))).astype(BF16)
    P["eb"] = slots(jnp.broadcast_to(e_b[:, None, :], (LANE, MLA_HEADS, QK_HEAD))).astype(BF16)

    sign = jnp.concatenate([jnp.zeros(QK_NOPE), -jnp.ones(half), jnp.ones(half)]).astype(F32)

    def lane_pad(g):
        return jnp.pad(g, (0, HEAD_PAD - QK_HEAD)).reshape(1, HEAD_PAD)

    for name, g in (("gq", q_norm.astype(F32)), ("gk", k_norm.astype(F32))):
        P[name] = lane_pad(g)
        P[name + "s"] = lane_pad(g[swap] * sign)
    P["gqa"] = q_a_norm.astype(F32).reshape(1, Q_LORA)
    P["gkva"] = kv_a_norm.astype(F32).reshape(1, KV_LORA)
    inv = 1.0 / (ROPE_THETA ** (jnp.arange(0, QK_ROPE, 2, dtype=F32) / QK_ROPE))
    ang = jnp.arange(L, dtype=F32)[:, None] * inv[None, :]
    cos, sin = jnp.cos(ang), jnp.sin(ang)
    P["ctab"] = jnp.concatenate([jnp.ones((L, QK_NOPE), F32), cos, cos, jnp.zeros((L, HEAD_PAD - QK_HEAD), F32)], axis=1)
    P["stab"] = jnp.concatenate([jnp.zeros((L, QK_NOPE), F32), sin, sin, jnp.zeros((L, HEAD_PAD - QK_HEAD), F32)], axis=1)

    P["w_hy_o"] = w_hy_o.astype(BF16)
    P["w_mla_o"] = w_mla_o.astype(BF16)
    P["w_out"] = w_out.astype(BF16)
    P["norm_ffn"] = norm_ffn.astype(F32).reshape(1, D_MODEL)
    P["router_w"] = jnp.pad(router_w.astype(F32), ((0, 0), (0, LANE - N_EXPERTS)))
    P["router_b"] = jnp.pad(router_b.astype(F32), (0, LANE - N_EXPERTS)).reshape(1, LANE)
    P["w_gu"] = w_gu.astype(BF16)
    P["b_gu"] = b_gu.astype(F32).reshape(N_EXPERTS, 1, 2 * D_FF)
    P["w_down"] = w_down.astype(BF16)
    P["b_down"] = b_down.astype(F32).reshape(N_EXPERTS, 1, D_MODEL)
    return P


def _tile(n, want):
    t = min(n, want)
    assert n % t == 0
    return t


def _layer(xs, norm_mix, w_in, hy_conv_w, hy_conv_b, hy_fw1, hy_fb1, hy_fw2, hy_fb2, hy_fw3, hy_freq, hy_decay, hy_skip,
           q_a_norm, w_uq, kv_a_norm, w_ukv, q_norm, k_norm, w_hy_o, w_mla_o, w_out,
           norm_ffn, router_w, router_b, w_gu, b_gu, w_down, b_down):
    L = xs[0].shape[1]
    assert all(x.shape[1] == L and x.shape[2] == D_MODEL for x in xs) and L % MXU == 0
    P = _prep_params(norm_mix, w_in, q_a_norm, w_uq, kv_a_norm, w_ukv, q_norm, k_norm, w_hy_o, w_mla_o, w_out,
                     norm_ffn, router_w, router_b, w_gu, b_gu, w_down, b_down, L)
    kf = _hyena_filters(L, hy_fw1, hy_fb1, hy_fw2, hy_fb2, hy_fw3, hy_freq, hy_decay)
    tm = _tile(L, 512)
    proj = []
    for x in xs:
        B = x.shape[0]
        proj.append(_inproj(x.reshape(B * L, D_MODEL), P["norm_mix"], P["w_hy"], P["w_m"], P["w_g"], tm))
    u_ts = [u.reshape(x.shape[0], L, HY_END).transpose(2, 0, 1) for x, (u, _, _) in zip(xs, proj)]
    y_ts = _hyena(u_ts, kf, hy_conv_w, hy_conv_b, hy_skip, L)
    outs = []
    for x, (_, m, g), y_t in zip(xs, proj, y_ts):
        B = x.shape[0]
        T = B * L
        q, k, v = _mla_prep(m.reshape(B, L, MLA_IN), P, tm)
        y_mla = _attention(q, k, v, _tile(L, 256)).reshape(T, MLA_HEADS * V_HEAD)
        y_hy = y_t.transpose(1, 2, 0).reshape(T, HY_WIDTH)
        x1, h2, logits = _merge(x.reshape(T, D_MODEL), y_hy, y_mla, g, P, tm)
        outs.append(_moe(x1, h2, logits, P, tm).reshape(B, L, D_MODEL))
    return outs


def kernel(x_prompt, x_sample, norm_mix, w_in, hy_conv_w, hy_conv_b, hy_fw1, hy_fb1, hy_fw2, hy_fb2, hy_fw3, hy_freq, hy_decay, hy_skip, q_a_norm, w_uq, kv_a_norm, w_ukv, q_norm, k_norm, w_hy_o, w_mla_o, w_out, norm_ffn, router_w, router_b, w_gu, b_gu, w_down, b_down):
    depth = norm_mix.shape[0]
    xs = [x_prompt, x_sample]
    weights = (norm_mix, w_in, hy_conv_w, hy_conv_b, hy_fw1, hy_fb1, hy_fw2, hy_fb2, hy_fw3, hy_freq, hy_decay, hy_skip,
               q_a_norm, w_uq, kv_a_norm, w_ukv, q_norm, k_norm, w_hy_o, w_mla_o, w_out,
               norm_ffn, router_w, router_b, w_gu, b_gu, w_down, b_down)
    for l in range(depth):
        xs = _layer(xs, *[w[l] for w in weights])
    return (xs[0], xs[1])
```

```python
import functools
import math

import jax
import jax.numpy as jnp
from jax import lax
from jax.experimental import pallas as pl
from jax.experimental.pallas import tpu as pltpu
from jax.experimental.pallas import tpu_sc as plsc

F32 = jnp.float32
BF16 = jnp.bfloat16

D_MODEL = 1024
HY_WIDTH = 512
HY_ORDER = 2
HY_EMB_BANDS = 16
HY_FILTER_DIM = 64
MLA_HEADS = 8
V_HEAD = 64
QK_NOPE = 64
QK_ROPE = 32
QK_HEAD = QK_NOPE + QK_ROPE
Q_LORA = 256
KV_LORA = 128
ROPE_THETA = 10000.0
HY_END = 3 * HY_WIDTH
Q_END = HY_END + Q_LORA
KV_END = Q_END + KV_LORA
KR_END = KV_END + QK_ROPE
G1_END = KR_END + D_MODEL
N_EXPERTS = 32
TOP_K = 4
D_FF = D_MODEL
SWIGLU_LIMIT = 7.0
SWIGLU_ALPHA = 1.702
EPS = 1e-6

LANE = 128
MXU = 256
HEAD_PAD = LANE
MLA_IN = 512
CONV_CB = 8
MOE_BLOCK = 256
PACKED = D_MODEL // 2
SC_CHUNK = 64
VMEM_LIMIT = 52 * 1024 * 1024


def _cparams(n_axes):
    return pltpu.CompilerParams(dimension_semantics=("arbitrary",) * n_axes, vmem_limit_bytes=VMEM_LIMIT)


def _full(shape):
    n = len(shape)
    return pl.BlockSpec(shape, lambda *_: (0,) * n)


def _inproj_kernel(x_ref, g_ref, why_ref, wm_ref, wg_ref, u_ref, m_ref, gate_ref):
    x = x_ref[...]
    r = lax.rsqrt(jnp.mean(x * x, axis=-1, keepdims=True) + EPS)
    h = (x * r * g_ref[...]).astype(BF16)
    u_ref[...] = jnp.dot(h, why_ref[...], preferred_element_type=F32).astype(BF16)
    m_ref[...] = jnp.dot(h, wm_ref[...], preferred_element_type=F32).astype(BF16)
    gate_ref[...] = jnp.dot(h, wg_ref[...], preferred_element_type=F32).astype(BF16)


def _inproj(x, g, w_hy, w_m, w_g, tm):
    T = x.shape[0]
    return pl.pallas_call(
        _inproj_kernel,
        grid=(T // tm,),
        in_specs=[
            pl.BlockSpec((tm, D_MODEL), lambda i: (i, 0)),
            _full((1, D_MODEL)),
            _full(w_hy.shape),
            _full(w_m.shape),
            _full(w_g.shape),
        ],
        out_specs=[
            pl.BlockSpec((tm, HY_END), lambda i: (i, 0)),
            pl.BlockSpec((tm, MLA_IN), lambda i: (i, 0)),
            pl.BlockSpec((tm, 2 * D_MODEL), lambda i: (i, 0)),
        ],
        out_shape=[
            jax.ShapeDtypeStruct((T, HY_END), BF16),
            jax.ShapeDtypeStruct((T, MLA_IN), BF16),
            jax.ShapeDtypeStruct((T, 2 * D_MODEL), BF16),
        ],
        compiler_params=_cparams(1),
        name="inproj",
    )(x, g, w_hy, w_m, w_g)


def _filter_kernel(bands_ref, w1t_ref, w1c_ref, w1s_ref, b1_ref, w2_ref, b2_ref, fr_ref, w3f_ref, w3b_ref, dec_ref,
                   out_ref, *, L):
    hi = lax.Precision.HIGHEST
    n = 2 * L
    j = lax.broadcasted_iota(jnp.int32, (1, n), 1)
    pos = jnp.abs(j - L).astype(F32)
    t01 = pos / max(L - 1, 1)
    ang = ((2.0 * math.pi / L) * pos) * bands_ref[...]
    pre = (w1t_ref[...] * t01
           + jnp.dot(w1c_ref[...], jnp.cos(ang), precision=hi, preferred_element_type=F32)
           - jnp.dot(w1s_ref[...], jnp.sin(ang), precision=hi, preferred_element_type=F32)
           + b1_ref[...])
    z = jnp.sin(fr_ref[0] * pre)
    z = jnp.sin(fr_ref[1] * (jnp.dot(w2_ref[...], z, precision=hi, preferred_element_type=F32) + b2_ref[...]))
    win = jnp.exp(-t01 * jnp.abs(dec_ref[...]))
    h_b = jnp.dot(w3b_ref[...], z[:, :L], precision=hi, preferred_element_type=F32) * win[:, :L]
    h_f = jnp.dot(w3f_ref[...], z[:, L:], precision=hi, preferred_element_type=F32) * win[:, L:]
    out_ref[:, :L] = jnp.where(j[:, :L] == 0, 0.0, h_b)
    out_ref[:, L:] = h_f


def _hyena_filters(L, fw1, fb1, fw2, fb2, fw3, freq, decay):
    rows = HY_ORDER * HY_WIDTH
    rb = 256
    bands = jnp.linspace(1e-4, HY_EMB_BANDS - 1, HY_EMB_BANDS, dtype=F32).reshape(HY_EMB_BANDS, 1)
    w1 = fw1.astype(F32).T
    w3 = fw3.astype(F32).T
    fd = HY_FILTER_DIM
    return pl.pallas_call(
        functools.partial(_filter_kernel, L=L),
        grid=(rows // rb,),
        in_specs=[
            _full((HY_EMB_BANDS, 1)),
            _full((fd, 1)),
            _full((fd, HY_EMB_BANDS)),
            _full((fd, HY_EMB_BANDS)),
            _full((fd, 1)),
            _full((fd, fd)),
            _full((fd, 1)),
            _full((2, fd, 1)),
            pl.BlockSpec((rb, fd), lambda i: (i, 0)),
            pl.BlockSpec((rb, fd), lambda i: (i, 0)),
            pl.BlockSpec((rb, 1), lambda i: (i, 0)),
        ],
        out_specs=pl.BlockSpec((rb, 2 * L), lambda i: (i, 0)),
        out_shape=jax.ShapeDtypeStruct((rows, 2 * L), F32),
        compiler_params=_cparams(1),
        name="hyena_filters",
    )(bands, w1[:, :1], w1[:, 1:1 + HY_EMB_BANDS], w1[:, 1 + HY_EMB_BANDS:], fb1.astype(F32).reshape(fd, 1),
      fw2.astype(F32).T, fb2.astype(F32).reshape(fd, 1), freq.astype(F32).reshape(2, fd, 1),
      w3[:rows], w3[rows:], decay.astype(F32).reshape(rows, 1))


def _shift_conv(u, w0, w1, w2, b):
    L = u.shape[1]
    lane = lax.broadcasted_iota(jnp.int32, u.shape, 1)
    prev = jnp.where(lane == 0, 0.0, pltpu.roll(u, 1, 1))
    nxt = jnp.where(lane == L - 1, 0.0, pltpu.roll(u, L - 1, 1))
    return prev * w0 + u * w1 + nxt * w2 + b


def _toeplitz_conv(z, tz_ref, L):
    B = z.shape[0]
    nj = L // MXU
    zb = z.astype(BF16)
    tiles = [zb[:, MXU * J:MXU * (J + 1)] for J in range(nj)]
    acc = [None] * nj
    for d in range(-(nj - 1), nj):
        js = [J for J in range(nj) if 0 <= J + d < nj]
        lhs = tiles[js[0]] if len(js) == 1 else jnp.concatenate([tiles[J] for J in js], axis=0)
        res = jnp.dot(lhs, tz_ref[:, L + MXU * d:L + MXU * (d + 1)], preferred_element_type=F32)
        for n, J in enumerate(js):
            part = res[n * B:(n + 1) * B]
            acc[J + d] = part if acc[J + d] is None else acc[J + d] + part
    return acc[0] if nj == 1 else jnp.concatenate(acc, axis=1)


def _hyena_kernel(cw_ref, cb_ref, skip_ref, kf_ref, *refs, L, n_groups):
    in_refs = refs[:3 * n_groups]
    out_refs = refs[3 * n_groups:4 * n_groups]
    tz_ref = refs[4 * n_groups]
    c0 = pl.program_id(0) * CONV_CB

    def channel(c, carry):
        ch = c0 + c
        for o in range(HY_ORDER):
            row = jnp.broadcast_to(kf_ref[o, pl.ds(c, 1), :], (MXU, 2 * L))
            tz_ref[o] = pltpu.roll(row, 0, 1, stride=1, stride_axis=0).astype(BF16)
        for g in range(n_groups):
            x1_ref, x2_ref, v_ref = in_refs[3 * g:3 * g + 3]
            sec = []
            for s, ref in enumerate((x1_ref, x2_ref, v_ref)):
                k = s * HY_WIDTH + ch
                sec.append(_shift_conv(ref[c].astype(F32), cw_ref[0, k], cw_ref[1, k], cw_ref[2, k], cb_ref[k]))
            x1, x2, z = sec
            for o, gate in enumerate((x1, x2)):
                y = _toeplitz_conv(z, tz_ref.at[o], L) + z * skip_ref[o, ch]
                z = gate * y
            out_refs[g][c] = z.astype(BF16)
        return carry

    lax.fori_loop(0, CONV_CB, channel, 0)


def _hyena(u_ts, kf, conv_w, conv_b, skip, L):
    n_groups = len(u_ts)
    nblk = HY_WIDTH // CONV_CB
    smem = pl.BlockSpec(memory_space=pltpu.SMEM)
    in_specs = [smem, smem, smem, pl.BlockSpec((HY_ORDER, CONV_CB, 2 * L), lambda i: (0, i, 0))]
    args = [conv_w.astype(F32), conv_b.astype(F32), skip.astype(F32), kf.reshape(HY_ORDER, HY_WIDTH, 2 * L)]
    out_specs, out_shape = [], []
    for u in u_ts:
        B = u.shape[1]
        for s in range(3):
            in_specs.append(pl.BlockSpec((CONV_CB, B, L), lambda i, s=s: (i + s * nblk, 0, 0)))
            args.append(u)
        out_specs.append(pl.BlockSpec((CONV_CB, B, L), lambda i: (i, 0, 0)))
        out_shape.append(jax.ShapeDtypeStruct((HY_WIDTH, B, L), BF16))
    return pl.pallas_call(
        functools.partial(_hyena_kernel, L=L, n_groups=n_groups),
        grid=(nblk,),
        in_specs=in_specs,
        out_specs=out_specs,
        out_shape=out_shape,
        scratch_shapes=[pltpu.VMEM((HY_ORDER, MXU, 2 * L), BF16)],
        compiler_params=_cparams(1),
        name="hyena_conv",
    )(*args)


def _rms(x, n):
    return lax.rsqrt(jnp.sum(x * x, axis=-1, keepdims=True) * (1.0 / n) + EPS)


def _mla_prep_kernel(m_ref, ct_ref, st_ref, gqa_ref, gkva_ref, gq_ref, gqs_ref, gk_ref, gks_ref,
                     wqa_ref, wqb_ref, wka_ref, ea_ref, eb_ref, wv_ref, q_ref, k_ref, v_ref):
    m = m_ref[0].astype(F32)
    cq = m[:, :Q_LORA]
    ckv = m[:, Q_LORA:Q_LORA + KV_LORA]
    kr = m_ref[0][:, Q_LORA + KV_LORA:Q_LORA + KV_LORA + LANE]
    cqn = (cq * _rms(cq, Q_LORA) * gqa_ref[...]).astype(BF16)
    ckvn = (ckv * _rms(ckv, KV_LORA) * gkva_ref[...]).astype(BF16)
    qa = jnp.dot(cqn, wqa_ref[...], preferred_element_type=F32)
    qb = jnp.dot(cqn, wqb_ref[...], preferred_element_type=F32)
    ka = jnp.dot(ckvn, wka_ref[...], preferred_element_type=F32) + jnp.dot(kr, ea_ref[...], preferred_element_type=F32)
    kb = jnp.dot(kr, eb_ref[...], preferred_element_type=F32)
    v_ref[0] = jnp.dot(ckvn, wv_ref[...], preferred_element_type=F32).astype(BF16)
    ct, st = ct_ref[...], st_ref[...]
    scale = QK_HEAD ** -0.5
    qc, qs = gq_ref[...] * ct * scale, gqs_ref[...] * st * scale
    kc, ks = gk_ref[...] * ct, gks_ref[...] * st
    for h in range(MLA_HEADS):
        sl = slice(h * HEAD_PAD, (h + 1) * HEAD_PAD)
        q_ref[0, h] = (_rms(qa[:, sl], QK_HEAD) * (qa[:, sl] * qc + qb[:, sl] * qs)).astype(BF16)
        k_ref[0, h] = (_rms(ka[:, sl], QK_HEAD) * (ka[:, sl] * kc + kb[:, sl] * ks)).astype(BF16)


def _mla_prep(m, P, tm):
    B, L, _ = m.shape
    w = [P[n] for n in ("wqa", "wqb", "wka", "ea", "eb", "wv")]
    g = [P[n] for n in ("gqa", "gkva", "gq", "gqs", "gk", "gks")]
    hp = MLA_HEADS * HEAD_PAD
    return pl.pallas_call(
        _mla_prep_kernel,
        grid=(B, L // tm),
        in_specs=[pl.BlockSpec((1, tm, MLA_IN), lambda b, i: (b, i, 0)),
                  pl.BlockSpec((tm, HEAD_PAD), lambda b, i: (i, 0)),
                  pl.BlockSpec((tm, HEAD_PAD), lambda b, i: (i, 0))]
        + [_full(a.shape) for a in g] + [_full(a.shape) for a in w],
        out_specs=[
            pl.BlockSpec((1, MLA_HEADS, tm, HEAD_PAD), lambda b, i: (b, 0, i, 0)),
            pl.BlockSpec((1, MLA_HEADS, tm, HEAD_PAD), lambda b, i: (b, 0, i, 0)),
            pl.BlockSpec((1, tm, MLA_HEADS * V_HEAD), lambda b, i: (b, i, 0)),
        ],
        out_shape=[
            jax.ShapeDtypeStruct((B, MLA_HEADS, L, HEAD_PAD), BF16),
            jax.ShapeDtypeStruct((B, MLA_HEADS, L, HEAD_PAD), BF16),
            jax.ShapeDtypeStruct((B, L, MLA_HEADS * V_HEAD), BF16),
        ],
        compiler_params=_cparams(2),
        name="mla_prep",
    )(m, P["ctab"][:L], P["stab"][:L], *g, *w)


def _attn_kernel(q_ref, k_ref, v_ref, o_ref):
    v = v_ref[0]
    outs = []
    for h in range(2):
        s = lax.dot_general(q_ref[0, h], k_ref[0, h], (((1,), (1,)), ((), ())), preferred_element_type=F32)
        p = jnp.exp(s - jnp.max(s, axis=-1, keepdims=True))
        l = jnp.sum(p, axis=-1, keepdims=True)
        outs.append(jnp.dot(p.astype(BF16), v, preferred_element_type=F32) / l)
    lane = lax.broadcasted_iota(jnp.int32, outs[0].shape, 1)
    o_ref[0] = jnp.where(lane < V_HEAD, outs[0], outs[1]).astype(BF16)


def _attention(q, k, v, tq):
    B, H, L, _ = q.shape
    return pl.pallas_call(
        _attn_kernel,
        grid=(B, H // 2, L // tq),
        in_specs=[
            pl.BlockSpec((1, 2, tq, HEAD_PAD), lambda b, h, i: (b, h, i, 0)),
            pl.BlockSpec((1, 2, L, HEAD_PAD), lambda b, h, i: (b, h, 0, 0)),
            pl.BlockSpec((1, L, 2 * V_HEAD), lambda b, h, i: (b, 0, h)),
        ],
        out_specs=pl.BlockSpec((1, tq, 2 * V_HEAD), lambda b, h, i: (b, i, h)),
        out_shape=jax.ShapeDtypeStruct((B, L, H * V_HEAD), BF16),
        compiler_params=_cparams(3),
        name="mla_attention",
    )(q, k, v)


def _pack_rows(a):
    bits = lax.bitcast_convert_type(a.astype(BF16).astype(F32), jnp.uint32)
    return (bits[:, :PACKED] >> 16) | (bits[:, PACKED:] & jnp.uint32(0xFFFF0000))


def _unpack_rows(p):
    lo = lax.bitcast_convert_type(p << 16, F32)
    hi = lax.bitcast_convert_type(p & jnp.uint32(0xFFFF0000), F32)
    return jnp.concatenate([lo, hi], axis=1)


def _merge_kernel(x_ref, yh_ref, ym_ref, g_ref, who_ref, wmo_ref, wo_ref, nf_ref, rw_ref, rb_ref,
                  x1_ref, h2_ref, lg_ref):
    g = g_ref[...].astype(F32)
    a = jnp.dot(yh_ref[...], who_ref[...], preferred_element_type=F32)
    b = jnp.dot(ym_ref[...], wmo_ref[...], preferred_element_type=F32)
    merged = jax.nn.sigmoid(g[:, :D_MODEL]) * a + jax.nn.sigmoid(g[:, D_MODEL:]) * b
    x1 = x_ref[...] + jnp.dot(merged.astype(BF16), wo_ref[...], preferred_element_type=F32)
    x1_ref[...] = x1
    h2 = x1 * lax.rsqrt(jnp.mean(x1 * x1, axis=-1, keepdims=True) + EPS) * nf_ref[...]
    h2_ref[...] = _pack_rows(h2)
    lg_ref[...] = jnp.dot(h2, rw_ref[...], precision=lax.Precision.HIGHEST, preferred_element_type=F32) + rb_ref[...]


def _merge(x, yh, ym, g, P, tm):
    T = x.shape[0]
    w = [P[n] for n in ("w_hy_o", "w_mla_o", "w_out", "norm_ffn", "router_w", "router_b")]
    return pl.pallas_call(
        _merge_kernel,
        grid=(T // tm,),
        in_specs=[
            pl.BlockSpec((tm, D_MODEL), lambda i: (i, 0)),
            pl.BlockSpec((tm, HY_WIDTH), lambda i: (i, 0)),
            pl.BlockSpec((tm, MLA_HEADS * V_HEAD), lambda i: (i, 0)),
            pl.BlockSpec((tm, 2 * D_MODEL), lambda i: (i, 0)),
        ] + [_full(a.shape) for a in w],
        out_specs=[
            pl.BlockSpec((tm, D_MODEL), lambda i: (i, 0)),
            pl.BlockSpec((tm, PACKED), lambda i: (i, 0)),
            pl.BlockSpec((tm, LANE), lambda i: (i, 0)),
        ],
        out_shape=[
            jax.ShapeDtypeStruct((T, D_MODEL), F32),
            jax.ShapeDtypeStruct((T, PACKED), jnp.uint32),
            jax.ShapeDtypeStruct((T, LANE), F32),
        ],
        compiler_params=_cparams(1),
        name="merge_router",
    )(x, yh, ym, g, *w)


def _expert_kernel(be_ref, na_ref, x_ref, wgu_ref, bgu_ref, wd_ref, bd_ref, y_ref):
    i = pl.program_id(0)

    @pl.when(i < na_ref[0])
    def _():
        x = _unpack_rows(x_ref[...]).astype(BF16)
        gu = jnp.dot(x, wgu_ref[0], preferred_element_type=F32) + bgu_ref[0]
        gate = jnp.minimum(gu[:, :D_FF], SWIGLU_LIMIT)
        up = jnp.clip(gu[:, D_FF:], -SWIGLU_LIMIT, SWIGLU_LIMIT)
        act = (up + 1.0) * (gate * jax.nn.sigmoid(SWIGLU_ALPHA * gate))
        y_ref[...] = _pack_rows(jnp.dot(act.astype(BF16), wd_ref[0], preferred_element_type=F32) + bd_ref[0])

    @pl.when(i >= na_ref[0])
    def _():
        y_ref[...] = jnp.zeros_like(y_ref)


def _experts(block_e, n_active, xs, P):
    n_rows = xs.shape[0]
    nb = n_rows // MOE_BLOCK
    grid_spec = pltpu.PrefetchScalarGridSpec(
        num_scalar_prefetch=2,
        grid=(nb,),
        in_specs=[
            pl.BlockSpec((MOE_BLOCK, PACKED), lambda i, be, na: (i, 0)),
            pl.BlockSpec((1, D_MODEL, 2 * D_FF), lambda i, be, na: (be[i], 0, 0)),
            pl.BlockSpec((1, 1, 2 * D_FF), lambda i, be, na: (be[i], 0, 0)),
            pl.BlockSpec((1, D_FF, D_MODEL), lambda i, be, na: (be[i], 0, 0)),
            pl.BlockSpec((1, 1, D_MODEL), lambda i, be, na: (be[i], 0, 0)),
        ],
        out_specs=pl.BlockSpec((MOE_BLOCK, PACKED), lambda i, be, na: (i, 0)),
    )
    return pl.pallas_call(
        _expert_kernel,
        grid_spec=grid_spec,
        out_shape=jax.ShapeDtypeStruct((n_rows, PACKED), jnp.uint32),
        compiler_params=_cparams(1),
        name="moe_experts",
    )(block_e, n_active, xs, P["w_gu"], P["b_gu"], P["w_down"], P["b_down"])


def _combine_kernel(x1_ref, yk_ref, gt_ref, o_ref):
    gt = gt_ref[...]
    acc = _unpack_rows(yk_ref[0]) * gt[:, 0:1]
    for k in range(1, TOP_K):
        acc = acc + _unpack_rows(yk_ref[k]) * gt[:, k:k + 1]
    o_ref[...] = x1_ref[...] + acc


def _combine(x1, yk, gates, tm):
    T = x1.shape[0]
    return pl.pallas_call(
        _combine_kernel,
        grid=(T // tm,),
        in_specs=[
            pl.BlockSpec((tm, D_MODEL), lambda i: (i, 0)),
            pl.BlockSpec((TOP_K, tm, PACKED), lambda i: (0, i, 0)),
            pl.BlockSpec((tm, LANE), lambda i: (i, 0)),
        ],
        out_specs=pl.BlockSpec((tm, D_MODEL), lambda i: (i, 0)),
        out_shape=jax.ShapeDtypeStruct((T, D_MODEL), F32),
        compiler_params=_cparams(1),
        name="moe_combine",
    )(x1, yk, gates)


def _route_kernel(lg_ref, dest_ref, gate_ref, cnt_ref, run_ref, start_ref, *, tm):
    ph, i = pl.program_id(0), pl.program_id(1)
    lane = lax.broadcasted_iota(jnp.int32, (tm, LANE), 1)
    work = jnp.where(lane < N_EXPERTS, lg_ref[...], -jnp.inf)
    hots, vals = [], []
    for _ in range(TOP_K):
        m = jnp.max(work, axis=-1, keepdims=True)
        first = jnp.min(jnp.where(work == m, lane, LANE), axis=-1, keepdims=True)
        hot = lane == first
        hots.append(hot)
        vals.append(m)
        work = jnp.where(hot, -jnp.inf, work)
    chosen = jnp.zeros((tm, LANE), F32)
    for hot in hots:
        chosen = jnp.where(hot, 1.0, chosen)
    tile_cnt = jnp.sum(chosen, axis=0, keepdims=True)

    @pl.when(jnp.logical_and(ph == 0, i == 0))
    def _():
        run_ref[...] = jnp.zeros_like(run_ref)

    @pl.when(ph == 0)
    def _():
        run_ref[...] += tile_cnt

    @pl.when(jnp.logical_and(ph == 1, i == 0))
    def _():
        cnt = run_ref[...]
        cnt_ref[...] = cnt
        padded = jnp.floor((cnt + (MOE_BLOCK - 1)) * (1.0 / MOE_BLOCK)) * MOE_BLOCK
        scan = jnp.broadcast_to(padded, (8, LANE))
        lane8 = lax.broadcasted_iota(jnp.int32, (8, LANE), 1)
        s = 1
        while s < LANE:
            scan = scan + jnp.where(lane8 >= s, pltpu.roll(scan, s, 1), 0.0)
            s *= 2
        start_ref[...] = scan[0:1] - padded
        run_ref[...] = jnp.zeros_like(run_ref)

    @pl.when(ph == 1)
    def _():
        ex = [jnp.exp(v - vals[0]) for v in vals]
        inv = 1.0 / (ex[0] + ex[1] + ex[2] + ex[3])
        gates = jnp.zeros((tm, LANE), F32)
        for k in range(TOP_K):
            gates = jnp.where(lane == k, ex[k] * inv, gates)
        gate_ref[...] = gates
        r = lax.broadcasted_iota(jnp.int32, (tm, tm), 0)
        c = lax.broadcasted_iota(jnp.int32, (tm, tm), 1)
        earlier = jnp.where(c < r, 1.0, 0.0).astype(BF16)
        before = jnp.dot(earlier, chosen.astype(BF16), preferred_element_type=F32) + (run_ref[...] + start_ref[...])
        dest = jnp.zeros((tm, LANE), F32)
        for k, hot in enumerate(hots):
            dest = jnp.where(lane == k, jnp.sum(jnp.where(hot, before, 0.0), axis=-1, keepdims=True), dest)
        dest_ref[...] = dest.astype(jnp.int32)
        run_ref[...] += tile_cnt


def _route(logits, tm):
    T = logits.shape[0]
    dest, gates, counts = pl.pallas_call(
        functools.partial(_route_kernel, tm=tm),
        grid=(2, T // tm),
        in_specs=[pl.BlockSpec((tm, LANE), lambda p, i: (i, 0))],
        out_specs=[
            pl.BlockSpec((tm, LANE), lambda p, i: (p * i, 0)),
            pl.BlockSpec((tm, LANE), lambda p, i: (p * i, 0)),
            pl.BlockSpec((1, LANE), lambda p, i: (0, 0)),
        ],
        out_shape=[
            jax.ShapeDtypeStruct((T, LANE), jnp.int32),
            jax.ShapeDtypeStruct((T, LANE), F32),
            jax.ShapeDtypeStruct((1, LANE), F32),
        ],
        scratch_shapes=[pltpu.VMEM((1, LANE), F32), pltpu.VMEM((1, LANE), F32)],
        compiler_params=_cparams(2),
        name="moe_route",
    )(logits)
    counts = counts[0, :N_EXPERTS].astype(jnp.int32)
    pad_end = jnp.cumsum((counts + MOE_BLOCK - 1) // MOE_BLOCK * MOE_BLOCK)
    n_blocks = -(-(T * TOP_K + N_EXPERTS * (MOE_BLOCK - 1)) // MOE_BLOCK)
    block_start = jnp.arange(n_blocks, dtype=jnp.int32) * MOE_BLOCK
    block_e = jnp.minimum(jnp.searchsorted(pad_end, block_start, side="right"), N_EXPERTS - 1).astype(jnp.int32)
    n_active = (pad_end[-1] // MOE_BLOCK).astype(jnp.int32).reshape(1)
    dest_flat = dest[:, :TOP_K].T.reshape(TOP_K * T)
    return dest_flat, gates, block_e, n_active, n_blocks * MOE_BLOCK


def _sc_workers():
    info = plsc.get_sparse_core_info()
    return info.num_cores, info.num_cores * info.num_subcores


def _sc_scatter_rows(rows, dest_flat, n_rows):
    T, W = rows.shape
    n_cores, n_workers = _sc_workers()
    per_w = T // n_workers
    assert per_w * n_workers == T and per_w % SC_CHUNK == 0

    def body(rows_hbm, dest_hbm, out_hbm, buf, *idx):
        wid = lax.axis_index("s") * n_cores + lax.axis_index("c")

        @pl.loop(0, per_w // SC_CHUNK)
        def _(j):
            off = pl.multiple_of(wid * per_w + j * SC_CHUNK, SC_CHUNK)
            pltpu.sync_copy(rows_hbm.at[pl.ds(off, SC_CHUNK)], buf)
            for k in range(TOP_K):
                pltpu.sync_copy(dest_hbm.at[pl.ds(k * T + off, SC_CHUNK)], idx[k])
            for k in range(TOP_K):
                pltpu.sync_copy(buf, out_hbm.at[idx[k]])

    return pl.kernel(
        body,
        out_type=jax.ShapeDtypeStruct((n_rows, W), rows.dtype),
        mesh=plsc.VectorSubcoreMesh(core_axis_name="c", subcore_axis_name="s"),
        scratch_types=[pltpu.VMEM((SC_CHUNK, W), rows.dtype)] + [pltpu.VMEM((SC_CHUNK,), jnp.int32)] * TOP_K,
        name="moe_dispatch_sc",
    )(rows, dest_flat)


def _sc_gather_rows(table, dest_flat):
    N = dest_flat.shape[0]
    W = table.shape[1]
    n_cores, n_workers = _sc_workers()
    per_w = N // n_workers
    assert per_w * n_workers == N and per_w % SC_CHUNK == 0

    def body(table_hbm, dest_hbm, out_hbm, buf, idx):
        wid = lax.axis_index("s") * n_cores + lax.axis_index("c")

        @pl.loop(0, per_w // SC_CHUNK)
        def _(j):
            off = pl.multiple_of(wid * per_w + j * SC_CHUNK, SC_CHUNK)
            pltpu.sync_copy(dest_hbm.at[pl.ds(off, SC_CHUNK)], idx)
            pltpu.sync_copy(table_hbm.at[idx], buf)
            pltpu.sync_copy(buf, out_hbm.at[pl.ds(off, SC_CHUNK)])

    return pl.kernel(
        body,
        out_type=jax.ShapeDtypeStruct((N, W), table.dtype),
        mesh=plsc.VectorSubcoreMesh(core_axis_name="c", subcore_axis_name="s"),
        scratch_types=[pltpu.VMEM((SC_CHUNK, W), table.dtype), pltpu.VMEM((SC_CHUNK,), jnp.int32)],
        name="moe_gather_sc",
    )(table, dest_flat)


def _moe(x1, h2p, logits, P, tm):
    T = x1.shape[0]
    dest_flat, gates, block_e, n_active, n_rows = _route(logits, tm)
    xs = _sc_scatter_rows(h2p, dest_flat, n_rows)
    ys = _experts(block_e, n_active, xs, P)
    yk = _sc_gather_rows(ys, dest_flat).reshape(TOP_K, T, PACKED)
    return _combine(x1, yk, gates, tm)


def _prep_params(norm_mix, w_in, q_a_norm, w_uq, kv_a_norm, w_ukv, q_norm, k_norm, w_hy_o, w_mla_o, w_out,
                 norm_ffn, router_w, router_b, w_gu, b_gu, w_down, b_down, L):
    P = {}
    P["norm_mix"] = norm_mix.astype(F32).reshape(1, D_MODEL)
    P["w_hy"] = w_in[:, :HY_END].astype(BF16)
    P["w_m"] = jnp.pad(w_in[:, HY_END:KR_END], ((0, 0), (0, MLA_IN - (KR_END - HY_END)))).astype(BF16)
    P["w_g"] = w_in[:, KR_END:].astype(BF16)

    half = QK_ROPE // 2
    swap = jnp.concatenate([jnp.arange(QK_NOPE), QK_NOPE + half + jnp.arange(half), QK_NOPE + jnp.arange(half)])
    rope_lane = (jnp.arange(QK_HEAD) >= QK_NOPE)

    def slots(w):
        return jnp.pad(w, ((0, 0), (0, 0), (0, HEAD_PAD - QK_HEAD))).reshape(w.shape[0], MLA_HEADS * HEAD_PAD)

    wq = w_uq.astype(F32).reshape(Q_LORA, MLA_HEADS, QK_HEAD)
    P["wqa"] = slots(wq).astype(BF16)
    P["wqb"] = slots(jnp.where(rope_lane, wq[:, :, swap], 0.0)).astype(BF16)
    wkv = w_ukv.astype(F32).reshape(KV_LORA, MLA_HEADS, QK_NOPE + V_HEAD)
    P["wka"] = slots(jnp.pad(wkv[:, :, :QK_NOPE], ((0, 0), (0, 0), (0, QK_ROPE)))).astype(BF16)
    P["wv"] = wkv[:, :, QK_NOPE:].reshape(KV_LORA, MLA_HEADS * V_HEAD).astype(BF16)
    eye = jnp.eye(LANE, QK_ROPE, dtype=F32)
    e_a = jnp.pad(eye, ((0, 0), (QK_NOPE, 0)))
    e_b = e_a[:, swap] * rope_lane
    P["ea"] = slots(jnp.broadcast_to(e_a[:, None, :], (LANE, MLA_HEADS, QK_HEAD))).astype(BF16)
    P["eb"] = slots(jnp.broadcast_to(e_b[:, None, :], (LANE, MLA_HEADS, QK_HEAD))).astype(BF16)

    sign = jnp.concatenate([jnp.zeros(QK_NOPE), -jnp.ones(half), jnp.ones(half)]).astype(F32)

    def lane_pad(g):
        return jnp.pad(g, (0, HEAD_PAD - QK_HEAD)).reshape(1, HEAD_PAD)

    for name, g in (("gq", q_norm.astype(F32)), ("gk", k_norm.astype(F32))):
        P[name] = lane_pad(g)
        P[name + "s"] = lane_pad(g[swap] * sign)
    P["gqa"] = q_a_norm.astype(F32).reshape(1, Q_LORA)
    P["gkva"] = kv_a_norm.astype(F32).reshape(1, KV_LORA)
    inv = 1.0 / (ROPE_THETA ** (jnp.arange(0, QK_ROPE, 2, dtype=F32) / QK_ROPE))
    ang = jnp.arange(L, dtype=F32)[:, None] * inv[None, :]
    cos, sin = jnp.cos(ang), jnp.sin(ang)
    P["ctab"] = jnp.concatenate([jnp.ones((L, QK_NOPE), F32), cos, cos, jnp.zeros((L, HEAD_PAD - QK_HEAD), F32)], axis=1)
    P["stab"] = jnp.concatenate([jnp.zeros((L, QK_NOPE), F32), sin, sin, jnp.zeros((L, HEAD_PAD - QK_HEAD), F32)], axis=1)

    P["w_hy_o"] = w_hy_o.astype(BF16)
    P["w_mla_o"] = w_mla_o.astype(BF16)
    P["w_out"] = w_out.astype(BF16)
    P["norm_ffn"] = norm_ffn.astype(F32).reshape(1, D_MODEL)
    P["router_w"] = jnp.pad(router_w.astype(F32), ((0, 0), (0, LANE - N_EXPERTS)))
    P["router_b"] = jnp.pad(router_b.astype(F32), (0, LANE - N_EXPERTS)).reshape(1, LANE)
    P["w_gu"] = w_gu.astype(BF16)
    P["b_gu"] = b_gu.astype(F32).reshape(N_EXPERTS, 1, 2 * D_FF)
    P["w_down"] = w_down.astype(BF16)
    P["b_down"] = b_down.astype(F32).reshape(N_EXPERTS, 1, D_MODEL)
    return P


def _tile(n, want):
    t = min(n, want)
    assert n % t == 0
    return t


def _layer(xs, norm_mix, w_in, hy_conv_w, hy_conv_b, hy_fw1, hy_fb1, hy_fw2, hy_fb2, hy_fw3, hy_freq, hy_decay, hy_skip,
           q_a_norm, w_uq, kv_a_norm, w_ukv, q_norm, k_norm, w_hy_o, w_mla_o, w_out,
           norm_ffn, router_w, router_b, w_gu, b_gu, w_down, b_down):
    L = xs[0].shape[1]
    assert all(x.shape[1] == L and x.shape[2] == D_MODEL for x in xs) and L % MXU == 0
    P = _prep_params(norm_mix, w_in, q_a_norm, w_uq, kv_a_norm, w_ukv, q_norm, k_norm, w_hy_o, w_mla_o, w_out,
                     norm_ffn, router_w, router_b, w_gu, b_gu, w_down, b_down, L)
    kf = _hyena_filters(L, hy_fw1, hy_fb1, hy_fw2, hy_fb2, hy_fw3, hy_freq, hy_decay)
    tm = _tile(L, 512)
    proj = []
    for x in xs:
        B = x.shape[0]
        proj.append(_inproj(x.reshape(B * L, D_MODEL), P["norm_mix"], P["w_hy"], P["w_m"], P["w_g"], tm))
    u_ts = [u.reshape(x.shape[0], L, HY_END).transpose(2, 0, 1) for x, (u, _, _) in zip(xs, proj)]
    y_ts = _hyena(u_ts, kf, hy_conv_w, hy_conv_b, hy_skip, L)
    outs = []
    for x, (_, m, g), y_t in zip(xs, proj, y_ts):
        B = x.shape[0]
        T = B * L
        q, k, v = _mla_prep(m.reshape(B, L, MLA_IN), P, tm)
        y_mla = _attention(q, k, v, _tile(L, 256)).reshape(T, MLA_HEADS * V_HEAD)
        y_hy = y_t.transpose(1, 2, 0).reshape(T, HY_WIDTH)
        x1, h2, logits = _merge(x.reshape(T, D_MODEL), y_hy, y_mla, g, P, tm)
        outs.append(_moe(x1, h2, logits, P, tm).reshape(B, L, D_MODEL))
    return outs


def kernel(x_prompt, x_sample, norm_mix, w_in, hy_conv_w, hy_conv_b, hy_fw1, hy_fb1, hy_fw2, hy_fb2, hy_fw3, hy_freq, hy_decay, hy_skip, q_a_norm, w_uq, kv_a_norm, w_ukv, q_norm, k_norm, w_hy_o, w_mla_o, w_out, norm_ffn, router_w, router_b, w_gu, b_gu, w_down, b_down):
    depth = norm_mix.shape[0]
    xs = [x_prompt, x_sample]
    weights = (norm_mix, w_in, hy_conv_w, hy_conv_b, hy_fw1, hy_fb1, hy_fw2, hy_fb2, hy_fw3, hy_freq, hy_decay, hy_skip,
               q_a_norm, w_uq, kv_a_norm, w_ukv, q_norm, k_norm, w_hy_o, w_mla_o, w_out,
               norm_ffn, router_w, router_b, w_gu, b_gu, w_down, b_down)
    for l in range(depth):
        xs = _layer(xs, *[w[l] for w in weights])
    return (xs[0], xs[1])
```

```python
import functools
import math

import jax
import jax.numpy as jnp
from jax import lax
from jax.experimental import pallas as pl
from jax.experimental.pallas import tpu as pltpu
from jax.experimental.pallas import tpu_sc as plsc

F32 = jnp.float32
BF16 = jnp.bfloat16

D_MODEL = 1024
HY_WIDTH = 512
HY_ORDER = 2
HY_EMB_BANDS = 16
HY_FILTER_DIM = 64
MLA_HEADS = 8
V_HEAD = 64
QK_NOPE = 64
QK_ROPE = 32
QK_HEAD = QK_NOPE + QK_ROPE
Q_LORA = 256
KV_LORA = 128
ROPE_THETA = 10000.0
HY_END = 3 * HY_WIDTH
Q_END = HY_END + Q_LORA
KV_END = Q_END + KV_LORA
KR_END = KV_END + QK_ROPE
G1_END = KR_END + D_MODEL
N_EXPERTS = 32
TOP_K = 4
D_FF = D_MODEL
SWIGLU_LIMIT = 7.0
SWIGLU_ALPHA = 1.702
EPS = 1e-6

LANE = 128
MXU = 256
HEAD_PAD = LANE
MLA_IN = 512
CONV_CB = 8
MOE_BLOCK = 512
PACKED = D_MODEL // 2
SC_CHUNK = 64
VMEM_LIMIT = 52 * 1024 * 1024


def _cparams(n_axes):
    return pltpu.CompilerParams(dimension_semantics=("arbitrary",) * n_axes, vmem_limit_bytes=VMEM_LIMIT)


def _full(shape):
    n = len(shape)
    return pl.BlockSpec(shape, lambda *_: (0,) * n)


def _inproj_kernel(x_ref, g_ref, why_ref, wm_ref, wg_ref, u_ref, m_ref, gate_ref):
    x = x_ref[...]
    r = lax.rsqrt(jnp.mean(x * x, axis=-1, keepdims=True) + EPS)
    h = (x * r * g_ref[...]).astype(BF16)
    u_ref[...] = lax.dot_general(why_ref[...], h, (((1,), (1,)), ((), ())), preferred_element_type=F32)
    m_ref[...] = jnp.dot(h, wm_ref[...], preferred_element_type=F32).astype(BF16)
    gate_ref[...] = jnp.dot(h, wg_ref[...], preferred_element_type=F32).astype(BF16)


def _inproj(x, g, w_hy_t, w_m, w_g, tm):
    B, L, _ = x.shape
    return pl.pallas_call(
        _inproj_kernel,
        grid=(B, L // tm),
        in_specs=[
            pl.BlockSpec((None, tm, D_MODEL), lambda b, i: (b, i, 0)),
            _full((1, D_MODEL)),
            _full(w_hy_t.shape),
            _full(w_m.shape),
            _full(w_g.shape),
        ],
        out_specs=[
            pl.BlockSpec((None, HY_END, tm), lambda b, i: (b, 0, i)),
            pl.BlockSpec((None, tm, MLA_IN), lambda b, i: (b, i, 0)),
            pl.BlockSpec((None, tm, 2 * D_MODEL), lambda b, i: (b, i, 0)),
        ],
        out_shape=[
            jax.ShapeDtypeStruct((B, HY_END, L), F32),
            jax.ShapeDtypeStruct((B, L, MLA_IN), BF16),
            jax.ShapeDtypeStruct((B, L, 2 * D_MODEL), BF16),
        ],
        compiler_params=_cparams(2),
        name="inproj",
    )(x, g, w_hy_t, w_m, w_g)


def _filter_kernel(bands_ref, w1t_ref, w1c_ref, w1s_ref, b1_ref, w2_ref, b2_ref, fr_ref, w3f_ref, w3b_ref, dec_ref,
                   out_ref, *, L):
    hi = lax.Precision.HIGHEST
    n = 2 * L
    j = lax.broadcasted_iota(jnp.int32, (1, n), 1)
    pos = jnp.abs(j - L).astype(F32)
    t01 = pos / max(L - 1, 1)
    ang = ((2.0 * math.pi / L) * pos) * bands_ref[...]
    pre = (w1t_ref[...] * t01
           + jnp.dot(w1c_ref[...], jnp.cos(ang), precision=hi, preferred_element_type=F32)
           - jnp.dot(w1s_ref[...], jnp.sin(ang), precision=hi, preferred_element_type=F32)
           + b1_ref[...])
    z = jnp.sin(fr_ref[0] * pre)
    z = jnp.sin(fr_ref[1] * (jnp.dot(w2_ref[...], z, precision=hi, preferred_element_type=F32) + b2_ref[...]))
    win = jnp.exp(-t01 * jnp.abs(dec_ref[...]))
    h_b = jnp.dot(w3b_ref[...], z[:, :L], precision=hi, preferred_element_type=F32) * win[:, :L]
    h_f = jnp.dot(w3f_ref[...], z[:, L:], precision=hi, preferred_element_type=F32) * win[:, L:]
    out_ref[:, :L] = jnp.where(j[:, :L] == 0, 0.0, h_b)
    out_ref[:, L:] = h_f


def _hyena_filters(L, fw1, fb1, fw2, fb2, fw3, freq, decay):
    rows = HY_ORDER * HY_WIDTH
    rb = 256
    bands = jnp.linspace(1e-4, HY_EMB_BANDS - 1, HY_EMB_BANDS, dtype=F32).reshape(HY_EMB_BANDS, 1)
    w1 = fw1.astype(F32).T
    w3 = fw3.astype(F32).T
    fd = HY_FILTER_DIM
    return pl.pallas_call(
        functools.partial(_filter_kernel, L=L),
        grid=(rows // rb,),
        in_specs=[
            _full((HY_EMB_BANDS, 1)),
            _full((fd, 1)),
            _full((fd, HY_EMB_BANDS)),
            _full((fd, HY_EMB_BANDS)),
            _full((fd, 1)),
            _full((fd, fd)),
            _full((fd, 1)),
            _full((2, fd, 1)),
            pl.BlockSpec((rb, fd), lambda i: (i, 0)),
            pl.BlockSpec((rb, fd), lambda i: (i, 0)),
            pl.BlockSpec((rb, 1), lambda i: (i, 0)),
        ],
        out_specs=pl.BlockSpec((rb, 2 * L), lambda i: (i, 0)),
        out_shape=jax.ShapeDtypeStruct((rows, 2 * L), F32),
        compiler_params=_cparams(1),
        name="hyena_filters",
    )(bands, w1[:, :1], w1[:, 1:1 + HY_EMB_BANDS], w1[:, 1 + HY_EMB_BANDS:], fb1.astype(F32).reshape(fd, 1),
      fw2.astype(F32).T, fb2.astype(F32).reshape(fd, 1), freq.astype(F32).reshape(2, fd, 1),
      w3[:rows], w3[rows:], decay.astype(F32).reshape(rows, 1))


def _shift_conv(u, w0, w1, w2, b):
    L = u.shape[1]
    lane = lax.broadcasted_iota(jnp.int32, u.shape, 1)
    prev = jnp.where(lane == 0, 0.0, pltpu.roll(u, 1, 1))
    nxt = jnp.where(lane == L - 1, 0.0, pltpu.roll(u, L - 1, 1))
    return prev * w0 + u * w1 + nxt * w2 + b


def _toeplitz_conv(z, tz_ref, L):
    B = z.shape[0]
    nj = L // MXU
    zb = z.astype(BF16)
    tiles = [zb[:, MXU * J:MXU * (J + 1)] for J in range(nj)]
    acc = [None] * nj
    for d in range(-(nj - 1), nj):
        js = [J for J in range(nj) if 0 <= J + d < nj]
        lhs = tiles[js[0]] if len(js) == 1 else jnp.concatenate([tiles[J] for J in js], axis=0)
        res = jnp.dot(lhs, tz_ref[:, L + MXU * d:L + MXU * (d + 1)], preferred_element_type=F32)
        for n, J in enumerate(js):
            part = res[n * B:(n + 1) * B]
            acc[J + d] = part if acc[J + d] is None else acc[J + d] + part
    return acc[0] if nj == 1 else jnp.concatenate(acc, axis=1)


def _hyena_kernel(cw_ref, cb_ref, skip_ref, kf_ref, *refs, L, n_groups):
    in_refs = refs[:3 * n_groups]
    out_refs = refs[3 * n_groups:4 * n_groups]
    tz_ref = refs[4 * n_groups]
    c0 = pl.program_id(0) * CONV_CB

    def channel(c, carry):
        ch = c0 + c
        for o in range(HY_ORDER):
            row = jnp.broadcast_to(kf_ref[o, pl.ds(c, 1), :], (MXU, 2 * L))
            tz_ref[o] = pltpu.roll(row, 0, 1, stride=1, stride_axis=0).astype(BF16)
        for g in range(n_groups):
            x1_ref, x2_ref, v_ref = in_refs[3 * g:3 * g + 3]
            sec = []
            for s, ref in enumerate((x1_ref, x2_ref, v_ref)):
                k = s * HY_WIDTH + ch
                sec.append(_shift_conv(ref[:, c, :], cw_ref[0, k], cw_ref[1, k], cw_ref[2, k], cb_ref[k]))
            x1, x2, z = sec
            for o, gate in enumerate((x1, x2)):
                y = _toeplitz_conv(z, tz_ref.at[o], L) + z * skip_ref[o, ch]
                z = gate * y
            out_refs[g][:, c, :] = z
        return carry

    lax.fori_loop(0, CONV_CB, channel, 0)


def _hyena(u_ts, kf, conv_w, conv_b, skip, L):
    n_groups = len(u_ts)
    nblk = HY_WIDTH // CONV_CB
    smem = pl.BlockSpec(memory_space=pltpu.SMEM)
    in_specs = [smem, smem, smem, pl.BlockSpec((HY_ORDER, CONV_CB, 2 * L), lambda i: (0, i, 0))]
    args = [conv_w.astype(F32), conv_b.astype(F32), skip.astype(F32), kf.reshape(HY_ORDER, HY_WIDTH, 2 * L)]
    out_specs, out_shape = [], []
    for u in u_ts:
        B = u.shape[0]
        for s in range(3):
            in_specs.append(pl.BlockSpec((B, CONV_CB, L), lambda i, s=s: (0, i + s * nblk, 0)))
            args.append(u)
        out_specs.append(pl.BlockSpec((B, CONV_CB, L), lambda i: (0, i, 0)))
        out_shape.append(jax.ShapeDtypeStruct((B, HY_WIDTH, L), F32))
    return pl.pallas_call(
        functools.partial(_hyena_kernel, L=L, n_groups=n_groups),
        grid=(nblk,),
        in_specs=in_specs,
        out_specs=out_specs,
        out_shape=out_shape,
        scratch_shapes=[pltpu.VMEM((HY_ORDER, MXU, 2 * L), BF16)],
        compiler_params=_cparams(1),
        name="hyena_conv",
    )(*args)


def _rms(x, n):
    return lax.rsqrt(jnp.sum(x * x, axis=-1, keepdims=True) * (1.0 / n) + EPS)


def _mla_prep_kernel(m_ref, ct_ref, st_ref, gqa_ref, gkva_ref, gq_ref, gqs_ref, gk_ref, gks_ref,
                     wqa_ref, wqb_ref, wka_ref, ea_ref, eb_ref, wv_ref, q_ref, k_ref, v_ref):
    m = m_ref[0].astype(F32)
    cq = m[:, :Q_LORA]
    ckv = m[:, Q_LORA:Q_LORA + KV_LORA]
    kr = m_ref[0][:, Q_LORA + KV_LORA:Q_LORA + KV_LORA + LANE]
    cqn = (cq * _rms(cq, Q_LORA) * gqa_ref[...]).astype(BF16)
    ckvn = (ckv * _rms(ckv, KV_LORA) * gkva_ref[...]).astype(BF16)
    qa = jnp.dot(cqn, wqa_ref[...], preferred_element_type=F32)
    qb = jnp.dot(cqn, wqb_ref[...], preferred_element_type=F32)
    ka = jnp.dot(ckvn, wka_ref[...], preferred_element_type=F32) + jnp.dot(kr, ea_ref[...], preferred_element_type=F32)
    kb = jnp.dot(kr, eb_ref[...], preferred_element_type=F32)
    v_ref[0] = jnp.dot(ckvn, wv_ref[...], preferred_element_type=F32).astype(BF16)
    ct, st = ct_ref[...], st_ref[...]
    scale = QK_HEAD ** -0.5 * math.log2(math.e)
    qc, qs = gq_ref[...] * ct * scale, gqs_ref[...] * st * scale
    kc, ks = gk_ref[...] * ct, gks_ref[...] * st
    for h in range(MLA_HEADS):
        sl = slice(h * HEAD_PAD, (h + 1) * HEAD_PAD)
        q_ref[0, h] = (_rms(qa[:, sl], QK_HEAD) * (qa[:, sl] * qc + qb[:, sl] * qs)).astype(BF16)
        k_ref[0, h] = (_rms(ka[:, sl], QK_HEAD) * (ka[:, sl] * kc + kb[:, sl] * ks)).astype(BF16)


def _mla_prep(m, P, tm):
    B, L, _ = m.shape
    w = [P[n] for n in ("wqa", "wqb", "wka", "ea", "eb", "wv")]
    g = [P[n] for n in ("gqa", "gkva", "gq", "gqs", "gk", "gks")]
    hp = MLA_HEADS * HEAD_PAD
    return pl.pallas_call(
        _mla_prep_kernel,
        grid=(B, L // tm),
        in_specs=[pl.BlockSpec((1, tm, MLA_IN), lambda b, i: (b, i, 0)),
                  pl.BlockSpec((tm, HEAD_PAD), lambda b, i: (i, 0)),
                  pl.BlockSpec((tm, HEAD_PAD), lambda b, i: (i, 0))]
        + [_full(a.shape) for a in g] + [_full(a.shape) for a in w],
        out_specs=[
            pl.BlockSpec((1, MLA_HEADS, tm, HEAD_PAD), lambda b, i: (b, 0, i, 0)),
            pl.BlockSpec((1, MLA_HEADS, tm, HEAD_PAD), lambda b, i: (b, 0, i, 0)),
            pl.BlockSpec((1, tm, MLA_HEADS * V_HEAD), lambda b, i: (b, i, 0)),
        ],
        out_shape=[
            jax.ShapeDtypeStruct((B, MLA_HEADS, L, HEAD_PAD), BF16),
            jax.ShapeDtypeStruct((B, MLA_HEADS, L, HEAD_PAD), BF16),
            jax.ShapeDtypeStruct((B, L, MLA_HEADS * V_HEAD), BF16),
        ],
        compiler_params=_cparams(2),
        name="mla_prep",
    )(m, P["ctab"][:L], P["stab"][:L], *g, *w)


def _attn_kernel(q_ref, k_ref, v_ref, o_ref):
    L = v_ref.shape[1]
    one_col = jnp.where(lax.broadcasted_iota(jnp.int32, (L, LANE), 1) == 0, 1.0, 0.0).astype(BF16)
    for pair in range(MLA_HEADS // 2):
        lanes = pl.ds(pair * LANE, LANE)
        v_ext = jnp.concatenate([v_ref[0, :, lanes], one_col], axis=1)
        outs = []
        for h in (2 * pair, 2 * pair + 1):
            s = lax.dot_general(q_ref[0, h], k_ref[0, h], (((1,), (1,)), ((), ())), preferred_element_type=F32)
            p = jnp.exp2(s - jnp.max(s, axis=-1, keepdims=True)).astype(BF16)
            pv = jnp.dot(p, v_ext, preferred_element_type=F32)
            outs.append(pv[:, :LANE] / pv[:, LANE:LANE + 1])
        lane = lax.broadcasted_iota(jnp.int32, outs[0].shape, 1)
        o_ref[0, :, lanes] = jnp.where(lane < V_HEAD, outs[0], outs[1]).astype(BF16)


def _attention(q, k, v, tq):
    B, H, L, _ = q.shape
    return pl.pallas_call(
        _attn_kernel,
        grid=(B, L // tq),
        in_specs=[
            pl.BlockSpec((1, H, tq, HEAD_PAD), lambda b, i: (b, 0, i, 0)),
            pl.BlockSpec((1, H, L, HEAD_PAD), lambda b, i: (b, 0, 0, 0)),
            pl.BlockSpec((1, L, H * V_HEAD), lambda b, i: (b, 0, 0)),
        ],
        out_specs=pl.BlockSpec((1, tq, H * V_HEAD), lambda b, i: (b, i, 0)),
        out_shape=jax.ShapeDtypeStruct((B, L, H * V_HEAD), BF16),
        compiler_params=_cparams(2),
        name="mla_attention",
    )(q, k, v)


def _pack_rows(a):
    bits = lax.bitcast_convert_type(a.astype(BF16).astype(F32), jnp.uint32)
    return (bits[:, :PACKED] >> 16) | (bits[:, PACKED:] & jnp.uint32(0xFFFF0000))


def _unpack_rows(p):
    lo = lax.bitcast_convert_type(p << 16, F32)
    hi = lax.bitcast_convert_type(p & jnp.uint32(0xFFFF0000), F32)
    return jnp.concatenate([lo, hi], axis=1)


def _merge_kernel(x_ref, yh_ref, ym_ref, g_ref, who_ref, wmo_ref, wo_ref, nf_ref, rwh_ref, rwl_ref, rb_ref,
                  x1_ref, h2_ref, lg_ref):
    tm = x_ref.shape[0]
    sub = tm // 2
    for r in range(0, tm, sub):
        rows = pl.ds(r, sub)
        g = g_ref[rows, :].astype(F32)
        a = lax.dot_general(yh_ref[:, rows].astype(BF16), who_ref[...], (((0,), (0,)), ((), ())),
                            preferred_element_type=F32)
        b = jnp.dot(ym_ref[rows, :], wmo_ref[...], preferred_element_type=F32)
        merged = jax.nn.sigmoid(g[:, :D_MODEL]) * a + jax.nn.sigmoid(g[:, D_MODEL:]) * b
        x1 = x_ref[rows, :] + jnp.dot(merged.astype(BF16), wo_ref[...], preferred_element_type=F32)
        x1_ref[rows, :] = x1
        h2 = x1 * lax.rsqrt(jnp.mean(x1 * x1, axis=-1, keepdims=True) + EPS) * nf_ref[...]
        h2_ref[rows, :] = _pack_rows(h2)
        h_hi = h2.astype(BF16)
        h_lo = (h2 - h_hi.astype(F32)).astype(BF16)
        lg_ref[rows, :] = (jnp.dot(h_hi, rwh_ref[...], preferred_element_type=F32)
                           + jnp.dot(h_lo, rwh_ref[...], preferred_element_type=F32)
                           + jnp.dot(h_hi, rwl_ref[...], preferred_element_type=F32) + rb_ref[...])


def _merge(x, yh_t, ym, g, P, tm):
    B, L, _ = x.shape
    nt = L // tm
    w = [P[n] for n in ("w_hy_o", "w_mla_o", "w_out", "norm_ffn", "router_w_hi", "router_w_lo", "router_b")]
    return pl.pallas_call(
        _merge_kernel,
        grid=(B, nt),
        in_specs=[
            pl.BlockSpec((None, tm, D_MODEL), lambda b, i: (b, i, 0)),
            pl.BlockSpec((None, HY_WIDTH, tm), lambda b, i: (b, 0, i)),
            pl.BlockSpec((None, tm, MLA_HEADS * V_HEAD), lambda b, i: (b, i, 0)),
            pl.BlockSpec((None, tm, 2 * D_MODEL), lambda b, i: (b, i, 0)),
        ] + [_full(a.shape) for a in w],
        out_specs=[
            pl.BlockSpec((tm, D_MODEL), lambda b, i: (b * nt + i, 0)),
            pl.BlockSpec((tm, PACKED), lambda b, i: (b * nt + i, 0)),
            pl.BlockSpec((tm, LANE), lambda b, i: (b * nt + i, 0)),
        ],
        out_shape=[
            jax.ShapeDtypeStruct((B * L, D_MODEL), F32),
            jax.ShapeDtypeStruct((B * L, PACKED), jnp.uint32),
            jax.ShapeDtypeStruct((B * L, LANE), F32),
        ],
        compiler_params=_cparams(2),
        name="merge_router",
    )(x, yh_t, ym, g, *w)


def _expert_kernel(be_ref, na_ref, x_ref, wgu_ref, bgu_ref, wd_ref, bd_ref, y_ref):
    i = pl.program_id(0)

    @pl.when(i < na_ref[0])
    def _():
        x = _unpack_rows(x_ref[...]).astype(BF16)
        gu = jnp.dot(x, wgu_ref[0], preferred_element_type=F32) + bgu_ref[0]
        gate = jnp.minimum(gu[:, :D_FF], SWIGLU_LIMIT)
        up = jnp.clip(gu[:, D_FF:], -SWIGLU_LIMIT, SWIGLU_LIMIT)
        act = (up + 1.0) * (gate * jax.nn.sigmoid(SWIGLU_ALPHA * gate))
        y_ref[...] = _pack_rows(jnp.dot(act.astype(BF16), wd_ref[0], preferred_element_type=F32) + bd_ref[0])

    @pl.when(i >= na_ref[0])
    def _():
        y_ref[...] = jnp.zeros_like(y_ref)


def _experts(block_e, n_active, xs, P):
    n_rows = xs.shape[0]
    nb = n_rows // MOE_BLOCK
    grid_spec = pltpu.PrefetchScalarGridSpec(
        num_scalar_prefetch=2,
        grid=(nb,),
        in_specs=[
            pl.BlockSpec((MOE_BLOCK, PACKED), lambda i, be, na: (i, 0)),
            pl.BlockSpec((1, D_MODEL, 2 * D_FF), lambda i, be, na: (be[i], 0, 0)),
            pl.BlockSpec((1, 1, 2 * D_FF), lambda i, be, na: (be[i], 0, 0)),
            pl.BlockSpec((1, D_FF, D_MODEL), lambda i, be, na: (be[i], 0, 0)),
            pl.BlockSpec((1, 1, D_MODEL), lambda i, be, na: (be[i], 0, 0)),
        ],
        out_specs=pl.BlockSpec((MOE_BLOCK, PACKED), lambda i, be, na: (i, 0)),
    )
    return pl.pallas_call(
        _expert_kernel,
        grid_spec=grid_spec,
        out_shape=jax.ShapeDtypeStruct((n_rows, PACKED), jnp.uint32),
        compiler_params=_cparams(1),
        name="moe_experts",
    )(block_e, n_active, xs, P["w_gu"], P["b_gu"], P["w_down"], P["b_down"])


def _combine_kernel(x1_ref, yk_ref, gt_ref, o_ref):
    gt = gt_ref[...]
    acc = _unpack_rows(yk_ref[0]) * gt[:, 0:1]
    for k in range(1, TOP_K):
        acc = acc + _unpack_rows(yk_ref[k]) * gt[:, k:k + 1]
    o_ref[...] = x1_ref[...] + acc


def _combine(x1, yk, gates, tm):
    T = x1.shape[0]
    return pl.pallas_call(
        _combine_kernel,
        grid=(T // tm,),
        in_specs=[
            pl.BlockSpec((tm, D_MODEL), lambda i: (i, 0)),
            pl.BlockSpec((TOP_K, tm, PACKED), lambda i: (0, i, 0)),
            pl.BlockSpec((tm, LANE), lambda i: (i, 0)),
        ],
        out_specs=pl.BlockSpec((tm, D_MODEL), lambda i: (i, 0)),
        out_shape=jax.ShapeDtypeStruct((T, D_MODEL), F32),
        compiler_params=_cparams(1),
        name="moe_combine",
    )(x1, yk, gates)


def _route_kernel(lg_ref, dest_ref, gate_ref, cnt_ref, run_ref, start_ref, *, tm):
    ph, i = pl.program_id(0), pl.program_id(1)
    lane = lax.broadcasted_iota(jnp.int32, (tm, LANE), 1)
    work = jnp.where(lane < N_EXPERTS, lg_ref[...], -jnp.inf)
    hots, vals = [], []
    for _ in range(TOP_K):
        m = jnp.max(work, axis=-1, keepdims=True)
        first = jnp.min(jnp.where(work == m, lane, LANE), axis=-1, keepdims=True)
        hot = lane == first
        hots.append(hot)
        vals.append(m)
        work = jnp.where(hot, -jnp.inf, work)
    chosen = jnp.zeros((tm, LANE), F32)
    for hot in hots:
        chosen = jnp.where(hot, 1.0, chosen)
    tile_cnt = jnp.sum(chosen, axis=0, keepdims=True)

    @pl.when(jnp.logical_and(ph == 0, i == 0))
    def _():
        run_ref[...] = jnp.zeros_like(run_ref)

    @pl.when(ph == 0)
    def _():
        run_ref[...] += tile_cnt

    @pl.when(jnp.logical_and(ph == 1, i == 0))
    def _():
        cnt = run_ref[...]
        cnt_ref[...] = cnt
        padded = jnp.floor((cnt + (MOE_BLOCK - 1)) * (1.0 / MOE_BLOCK)) * MOE_BLOCK
        scan = jnp.broadcast_to(padded, (8, LANE))
        lane8 = lax.broadcasted_iota(jnp.int32, (8, LANE), 1)
        s = 1
        while s < LANE:
            scan = scan + jnp.where(lane8 >= s, pltpu.roll(scan, s, 1), 0.0)
            s *= 2
        start_ref[...] = scan[0:1] - padded
        run_ref[...] = jnp.zeros_like(run_ref)

    @pl.when(ph == 1)
    def _():
        ex = [jnp.exp(v - vals[0]) for v in vals]
        inv = 1.0 / (ex[0] + ex[1] + ex[2] + ex[3])
        gates = jnp.zeros((tm, LANE), F32)
        for k in range(TOP_K):
            gates = jnp.where(lane == k, ex[k] * inv, gates)
        gate_ref[...] = gates
        r = lax.broadcasted_iota(jnp.int32, (tm, tm), 0)
        c = lax.broadcasted_iota(jnp.int32, (tm, tm), 1)
        earlier = jnp.where(c < r, 1.0, 0.0).astype(BF16)
        before = jnp.dot(earlier, chosen.astype(BF16), preferred_element_type=F32) + (run_ref[...] + start_ref[...])
        dest = jnp.zeros((tm, LANE), F32)
        for k, hot in enumerate(hots):
            dest = jnp.where(lane == k, jnp.sum(jnp.where(hot, before, 0.0), axis=-1, keepdims=True), dest)
        dest_ref[...] = dest.astype(jnp.int32)
        run_ref[...] += tile_cnt


def _route(logits, tm):
    T = logits.shape[0]
    dest, gates, counts = pl.pallas_call(
        functools.partial(_route_kernel, tm=tm),
        grid=(2, T // tm),
        in_specs=[pl.BlockSpec((tm, LANE), lambda p, i: (i, 0))],
        out_specs=[
            pl.BlockSpec((tm, LANE), lambda p, i: (p * i, 0)),
            pl.BlockSpec((tm, LANE), lambda p, i: (p * i, 0)),
            pl.BlockSpec((1, LANE), lambda p, i: (0, 0)),
        ],
        out_shape=[
            jax.ShapeDtypeStruct((T, LANE), jnp.int32),
            jax.ShapeDtypeStruct((T, LANE), F32),
            jax.ShapeDtypeStruct((1, LANE), F32),
        ],
        scratch_shapes=[pltpu.VMEM((1, LANE), F32), pltpu.VMEM((1, LANE), F32)],
        compiler_params=_cparams(2),
        name="moe_route",
    )(logits)
    counts = counts[0, :N_EXPERTS].astype(jnp.int32)
    pad_end = jnp.cumsum((counts + MOE_BLOCK - 1) // MOE_BLOCK * MOE_BLOCK)
    n_blocks = -(-(T * TOP_K + N_EXPERTS * (MOE_BLOCK - 1)) // MOE_BLOCK)
    block_start = jnp.arange(n_blocks, dtype=jnp.int32) * MOE_BLOCK
    block_e = jnp.minimum(jnp.searchsorted(pad_end, block_start, side="right"), N_EXPERTS - 1).astype(jnp.int32)
    n_active = (pad_end[-1] // MOE_BLOCK).astype(jnp.int32).reshape(1)
    dest_flat = dest[:, :TOP_K].T.reshape(TOP_K * T)
    return dest_flat, gates, block_e, n_active, n_blocks * MOE_BLOCK


def _sc_workers():
    info = plsc.get_sparse_core_info()
    return info.num_cores, info.num_cores * info.num_subcores


def _sc_scatter_rows(rows, dest_flat, n_rows):
    T, W = rows.shape
    n_cores, n_workers = _sc_workers()
    per_w = T // n_workers
    assert per_w * n_workers == T and per_w % SC_CHUNK == 0

    def body(rows_hbm, dest_hbm, out_hbm, buf, *idx):
        wid = lax.axis_index("s") * n_cores + lax.axis_index("c")

        @pl.loop(0, per_w // SC_CHUNK)
        def _(j):
            off = pl.multiple_of(wid * per_w + j * SC_CHUNK, SC_CHUNK)
            pltpu.sync_copy(rows_hbm.at[pl.ds(off, SC_CHUNK)], buf)
            for k in range(TOP_K):
                pltpu.sync_copy(dest_hbm.at[pl.ds(k * T + off, SC_CHUNK)], idx[k])
            for k in range(TOP_K):
                pltpu.sync_copy(buf, out_hbm.at[idx[k]])

    return pl.kernel(
        body,
        out_type=jax.ShapeDtypeStruct((n_rows, W), rows.dtype),
        mesh=plsc.VectorSubcoreMesh(core_axis_name="c", subcore_axis_name="s"),
        scratch_types=[pltpu.VMEM((SC_CHUNK, W), rows.dtype)] + [pltpu.VMEM((SC_CHUNK,), jnp.int32)] * TOP_K,
        name="moe_dispatch_sc",
    )(rows, dest_flat)


def _sc_gather_rows(table, dest_flat):
    N = dest_flat.shape[0]
    W = table.shape[1]
    n_cores, n_workers = _sc_workers()
    per_w = N // n_workers
    assert per_w * n_workers == N and per_w % SC_CHUNK == 0

    def body(table_hbm, dest_hbm, out_hbm, buf, idx):
        wid = lax.axis_index("s") * n_cores + lax.axis_index("c")

        @pl.loop(0, per_w // SC_CHUNK)
        def _(j):
            off = pl.multiple_of(wid * per_w + j * SC_CHUNK, SC_CHUNK)
            pltpu.sync_copy(dest_hbm.at[pl.ds(off, SC_CHUNK)], idx)
            pltpu.sync_copy(table_hbm.at[idx], buf)
            pltpu.sync_copy(buf, out_hbm.at[pl.ds(off, SC_CHUNK)])

    return pl.kernel(
        body,
        out_type=jax.ShapeDtypeStruct((N, W), table.dtype),
        mesh=plsc.VectorSubcoreMesh(core_axis_name="c", subcore_axis_name="s"),
        scratch_types=[pltpu.VMEM((SC_CHUNK, W), table.dtype), pltpu.VMEM((SC_CHUNK,), jnp.int32)],
        name="moe_gather_sc",
    )(table, dest_flat)


def _moe(x1, h2p, logits, P, tm):
    T = x1.shape[0]
    dest_flat, gates, block_e, n_active, n_rows = _route(logits, tm)
    xs = _sc_scatter_rows(h2p, dest_flat, n_rows)
    ys = _experts(block_e, n_active, xs, P)
    yk = _sc_gather_rows(ys, dest_flat).reshape(TOP_K, T, PACKED)
    return _combine(x1, yk, gates, tm)


def _prep_params(norm_mix, w_in, q_a_norm, w_uq, kv_a_norm, w_ukv, q_norm, k_norm, w_hy_o, w_mla_o, w_out,
                 norm_ffn, router_w, router_b, w_gu, b_gu, w_down, b_down, L):
    P = {}
    P["norm_mix"] = norm_mix.astype(F32).reshape(1, D_MODEL)
    P["w_hy_t"] = w_in[:, :HY_END].T.astype(BF16)
    P["w_m"] = jnp.pad(w_in[:, HY_END:KR_END], ((0, 0), (0, MLA_IN - (KR_END - HY_END)))).astype(BF16)
    P["w_g"] = w_in[:, KR_END:].astype(BF16)

    half = QK_ROPE // 2
    swap = jnp.concatenate([jnp.arange(QK_NOPE), QK_NOPE + half + jnp.arange(half), QK_NOPE + jnp.arange(half)])
    rope_lane = (jnp.arange(QK_HEAD) >= QK_NOPE)

    def slots(w):
        return jnp.pad(w, ((0, 0), (0, 0), (0, HEAD_PAD - QK_HEAD))).reshape(w.shape[0], MLA_HEADS * HEAD_PAD)

    wq = w_uq.astype(F32).reshape(Q_LORA, MLA_HEADS, QK_HEAD)
    P["wqa"] = slots(wq).astype(BF16)
    P["wqb"] = slots(jnp.where(rope_lane, wq[:, :, swap], 0.0)).astype(BF16)
    wkv = w_ukv.astype(F32).reshape(KV_LORA, MLA_HEADS, QK_NOPE + V_HEAD)
    P["wka"] = slots(jnp.pad(wkv[:, :, :QK_NOPE], ((0, 0), (0, 0), (0, QK_ROPE)))).astype(BF16)
    P["wv"] = wkv[:, :, QK_NOPE:].reshape(KV_LORA, MLA_HEADS * V_HEAD).astype(BF16)
    eye = jnp.eye(LANE, QK_ROPE, dtype=F32)
    e_a = jnp.pad(eye, ((0, 0), (QK_NOPE, 0)))
    e_b = e_a[:, swap] * rope_lane
    P["ea"] = slots(jnp.broadcast_to(e_a[:, None, :], (LANE, MLA_HEADS, QK_HEAD))).astype(BF16)
    P["eb"] = slots(jnp.broadcast_to(e_b[:, None, :], (LANE, MLA_HEADS, QK_HEAD))).astype(BF16)

    sign = jnp.concatenate([jnp.zeros(QK_NOPE), -jnp.ones(half), jnp.ones(half)]).astype(F32)

    def lane_pad(g):
        return jnp.pad(g, (0, HEAD_PAD - QK_HEAD)).reshape(1, HEAD_PAD)

    for name, g in (("gq", q_norm.astype(F32)), ("gk", k_norm.astype(F32))):
        P[name] = lane_pad(g)
        P[name + "s"] = lane_pad(g[swap] * sign)
    P["gqa"] = q_a_norm.astype(F32).reshape(1, Q_LORA)
    P["gkva"] = kv_a_norm.astype(F32).reshape(1, KV_LORA)
    inv = 1.0 / (ROPE_THETA ** (jnp.arange(0, QK_ROPE, 2, dtype=F32) / QK_ROPE))
    ang = jnp.arange(L, dtype=F32)[:, None] * inv[None, :]
    cos, sin = jnp.cos(ang), jnp.sin(ang)
    P["ctab"] = jnp.concatenate([jnp.ones((L, QK_NOPE), F32), cos, cos, jnp.zeros((L, HEAD_PAD - QK_HEAD), F32)], axis=1)
    P["stab"] = jnp.concatenate([jnp.zeros((L, QK_NOPE), F32), sin, sin, jnp.zeros((L, HEAD_PAD - QK_HEAD), F32)], axis=1)

    P["w_hy_o"] = w_hy_o.astype(BF16)
    P["w_mla_o"] = w_mla_o.astype(BF16)
    P["w_out"] = w_out.astype(BF16)
    P["norm_ffn"] = norm_ffn.astype(F32).reshape(1, D_MODEL)
    rw = jnp.pad(router_w.astype(F32), ((0, 0), (0, LANE - N_EXPERTS)))
    P["router_w_hi"] = rw.astype(BF16)
    P["router_w_lo"] = (rw - P["router_w_hi"].astype(F32)).astype(BF16)
    P["router_b"] = jnp.pad(router_b.astype(F32), (0, LANE - N_EXPERTS)).reshape(1, LANE)
    P["w_gu"] = w_gu.astype(BF16)
    P["b_gu"] = b_gu.astype(F32).reshape(N_EXPERTS, 1, 2 * D_FF)
    P["w_down"] = w_down.astype(BF16)
    P["b_down"] = b_down.astype(F32).reshape(N_EXPERTS, 1, D_MODEL)
    return P


def _tile(n, want):
    t = min(n, want)
    assert n % t == 0
    return t


def _layer(xs, norm_mix, w_in, hy_conv_w, hy_conv_b, hy_fw1, hy_fb1, hy_fw2, hy_fb2, hy_fw3, hy_freq, hy_decay, hy_skip,
           q_a_norm, w_uq, kv_a_norm, w_ukv, q_norm, k_norm, w_hy_o, w_mla_o, w_out,
           norm_ffn, router_w, router_b, w_gu, b_gu, w_down, b_down):
    L = xs[0].shape[1]
    assert all(x.shape[1] == L and x.shape[2] == D_MODEL for x in xs) and L % MXU == 0
    P = _prep_params(norm_mix, w_in, q_a_norm, w_uq, kv_a_norm, w_ukv, q_norm, k_norm, w_hy_o, w_mla_o, w_out,
                     norm_ffn, router_w, router_b, w_gu, b_gu, w_down, b_down, L)
    kf = _hyena_filters(L, hy_fw1, hy_fb1, hy_fw2, hy_fb2, hy_fw3, hy_freq, hy_decay)
    tm = _tile(L, 512)
    proj = [_inproj(x, P["norm_mix"], P["w_hy_t"], P["w_m"], P["w_g"], tm) for x in xs]
    y_ts = _hyena([u_t for u_t, _, _ in proj], kf, hy_conv_w, hy_conv_b, hy_skip, L)
    outs = []
    for x, (_, m, g), y_t in zip(xs, proj, y_ts):
        B = x.shape[0]
        q, k, v = _mla_prep(m, P, tm)
        y_mla = _attention(q, k, v, _tile(L, 256))
        x1, h2, logits = _merge(x, y_t, y_mla, g, P, tm)
        outs.append(_moe(x1, h2, logits, P, tm).reshape(B, L, D_MODEL))
    return outs


def kernel(x_prompt, x_sample, norm_mix, w_in, hy_conv_w, hy_conv_b, hy_fw1, hy_fb1, hy_fw2, hy_fb2, hy_fw3, hy_freq, hy_decay, hy_skip, q_a_norm, w_uq, kv_a_norm, w_ukv, q_norm, k_norm, w_hy_o, w_mla_o, w_out, norm_ffn, router_w, router_b, w_gu, b_gu, w_down, b_down):
    depth = norm_mix.shape[0]
    xs = [x_prompt, x_sample]
    weights = (norm_mix, w_in, hy_conv_w, hy_conv_b, hy_fw1, hy_fb1, hy_fw2, hy_fb2, hy_fw3, hy_freq, hy_decay, hy_skip,
               q_a_norm, w_uq, kv_a_norm, w_ukv, q_norm, k_norm, w_hy_o, w_mla_o, w_out,
               norm_ffn, router_w, router_b, w_gu, b_gu, w_down, b_down)
    for l in range(depth):
        xs = _layer(xs, *[w[l] for w in weights])
    return (xs[0], xs[1])
```

```python
import functools
import math

import jax
import jax.numpy as jnp
from jax import lax
from jax.experimental import pallas as pl
from jax.experimental.pallas import tpu as pltpu
from jax.experimental.pallas import tpu_sc as plsc

F32 = jnp.float32
BF16 = jnp.bfloat16

D_MODEL = 1024
HY_WIDTH = 512
HY_ORDER = 2
HY_EMB_BANDS = 16
HY_FILTER_DIM = 64
MLA_HEADS = 8
V_HEAD = 64
QK_NOPE = 64
QK_ROPE = 32
QK_HEAD = QK_NOPE + QK_ROPE
Q_LORA = 256
KV_LORA = 128
ROPE_THETA = 10000.0
HY_END = 3 * HY_WIDTH
Q_END = HY_END + Q_LORA
KV_END = Q_END + KV_LORA
KR_END = KV_END + QK_ROPE
G1_END = KR_END + D_MODEL
N_EXPERTS = 32
TOP_K = 4
D_FF = D_MODEL
SWIGLU_LIMIT = 7.0
SWIGLU_ALPHA = 1.702
EPS = 1e-6

LANE = 128
MXU = 256
HEAD_PAD = LANE
MLA_IN = 512
CONV_CB = 8
MOE_BLOCK = 512
PACKED = D_MODEL // 2
SC_CHUNK = 64
ROUTE_PARTS = 4
VMEM_LIMIT = 52 * 1024 * 1024


def _cparams(n_axes):
    return pltpu.CompilerParams(dimension_semantics=("arbitrary",) * n_axes, vmem_limit_bytes=VMEM_LIMIT)


def _full(shape):
    n = len(shape)
    return pl.BlockSpec(shape, lambda *_: (0,) * n)


def _inproj_kernel(x_ref, g_ref, why_ref, wm_ref, wg_ref, u_ref, m_ref, gate_ref):
    x = x_ref[...]
    r = lax.rsqrt(jnp.mean(x * x, axis=-1, keepdims=True) + EPS)
    h = (x * r * g_ref[...]).astype(BF16)
    u_ref[...] = lax.dot_general(why_ref[...], h, (((1,), (1,)), ((), ())), preferred_element_type=F32)
    m_ref[...] = jnp.dot(h, wm_ref[...], preferred_element_type=F32).astype(BF16)
    gate_ref[...] = jnp.dot(h, wg_ref[...], preferred_element_type=F32).astype(BF16)


def _inproj(x, g, w_hy_t, w_m, w_g, tm):
    B, L, _ = x.shape
    return pl.pallas_call(
        _inproj_kernel,
        grid=(B, L // tm),
        in_specs=[
            pl.BlockSpec((None, tm, D_MODEL), lambda b, i: (b, i, 0)),
            _full((1, D_MODEL)),
            _full(w_hy_t.shape),
            _full(w_m.shape),
            _full(w_g.shape),
        ],
        out_specs=[
            pl.BlockSpec((None, HY_END, tm), lambda b, i: (b, 0, i)),
            pl.BlockSpec((None, tm, MLA_IN), lambda b, i: (b, i, 0)),
            pl.BlockSpec((None, tm, 2 * D_MODEL), lambda b, i: (b, i, 0)),
        ],
        out_shape=[
            jax.ShapeDtypeStruct((B, HY_END, L), F32),
            jax.ShapeDtypeStruct((B, L, MLA_IN), BF16),
            jax.ShapeDtypeStruct((B, L, 2 * D_MODEL), BF16),
        ],
        compiler_params=_cparams(2),
        name="inproj",
    )(x, g, w_hy_t, w_m, w_g)


def _filter_kernel(bands_ref, w1t_ref, w1c_ref, w1s_ref, b1_ref, w2_ref, b2_ref, fr_ref, w3f_ref, w3b_ref, dec_ref,
                   out_ref, *, L):
    hi = lax.Precision.HIGHEST
    n = 2 * L
    j = lax.broadcasted_iota(jnp.int32, (1, n), 1)
    pos = jnp.abs(j - L).astype(F32)
    t01 = pos / max(L - 1, 1)
    ang = ((2.0 * math.pi / L) * pos) * bands_ref[...]
    pre = (w1t_ref[...] * t01
           + jnp.dot(w1c_ref[...], jnp.cos(ang), precision=hi, preferred_element_type=F32)
           - jnp.dot(w1s_ref[...], jnp.sin(ang), precision=hi, preferred_element_type=F32)
           + b1_ref[...])
    z = jnp.sin(fr_ref[0] * pre)
    z = jnp.sin(fr_ref[1] * (jnp.dot(w2_ref[...], z, precision=hi, preferred_element_type=F32) + b2_ref[...]))
    win = jnp.exp(-t01 * jnp.abs(dec_ref[...]))
    h_b = jnp.dot(w3b_ref[...], z[:, :L], precision=hi, preferred_element_type=F32) * win[:, :L]
    h_f = jnp.dot(w3f_ref[...], z[:, L:], precision=hi, preferred_element_type=F32) * win[:, L:]
    out_ref[:, :L] = jnp.where(j[:, :L] == 0, 0.0, h_b)
    out_ref[:, L:] = h_f


def _hyena_filters(L, fw1, fb1, fw2, fb2, fw3, freq, decay):
    rows = HY_ORDER * HY_WIDTH
    rb = 256
    bands = jnp.linspace(1e-4, HY_EMB_BANDS - 1, HY_EMB_BANDS, dtype=F32).reshape(HY_EMB_BANDS, 1)
    w1 = fw1.astype(F32).T
    w3 = fw3.astype(F32).T
    fd = HY_FILTER_DIM
    return pl.pallas_call(
        functools.partial(_filter_kernel, L=L),
        grid=(rows // rb,),
        in_specs=[
            _full((HY_EMB_BANDS, 1)),
            _full((fd, 1)),
            _full((fd, HY_EMB_BANDS)),
            _full((fd, HY_EMB_BANDS)),
            _full((fd, 1)),
            _full((fd, fd)),
            _full((fd, 1)),
            _full((2, fd, 1)),
            pl.BlockSpec((rb, fd), lambda i: (i, 0)),
            pl.BlockSpec((rb, fd), lambda i: (i, 0)),
            pl.BlockSpec((rb, 1), lambda i: (i, 0)),
        ],
        out_specs=pl.BlockSpec((rb, 2 * L), lambda i: (i, 0)),
        out_shape=jax.ShapeDtypeStruct((rows, 2 * L), F32),
        compiler_params=_cparams(1),
        name="hyena_filters",
    )(bands, w1[:, :1], w1[:, 1:1 + HY_EMB_BANDS], w1[:, 1 + HY_EMB_BANDS:], fb1.astype(F32).reshape(fd, 1),
      fw2.astype(F32).T, fb2.astype(F32).reshape(fd, 1), freq.astype(F32).reshape(2, fd, 1),
      w3[:rows], w3[rows:], decay.astype(F32).reshape(rows, 1))


def _shift_conv(u, w0, w1, w2, b):
    L = u.shape[1]
    lane = lax.broadcasted_iota(jnp.int32, u.shape, 1)
    prev = jnp.where(lane == 0, 0.0, pltpu.roll(u, 1, 1))
    nxt = jnp.where(lane == L - 1, 0.0, pltpu.roll(u, L - 1, 1))
    return prev * w0 + u * w1 + nxt * w2 + b


def _toeplitz_conv(z, tz_ref, L):
    B = z.shape[0]
    nj = L // MXU
    zb = z.astype(BF16)
    tiles = [zb[:, MXU * J:MXU * (J + 1)] for J in range(nj)]
    acc = [None] * nj
    for d in range(-(nj - 1), nj):
        js = [J for J in range(nj) if 0 <= J + d < nj]
        lhs = tiles[js[0]] if len(js) == 1 else jnp.concatenate([tiles[J] for J in js], axis=0)
        res = jnp.dot(lhs, tz_ref[:, L + MXU * d:L + MXU * (d + 1)], preferred_element_type=F32)
        for n, J in enumerate(js):
            part = res[n * B:(n + 1) * B]
            acc[J + d] = part if acc[J + d] is None else acc[J + d] + part
    return acc[0] if nj == 1 else jnp.concatenate(acc, axis=1)


def _hyena_kernel(cw_ref, cb_ref, skip_ref, kf_ref, *refs, L, n_groups):
    in_refs = refs[:3 * n_groups]
    out_refs = refs[3 * n_groups:4 * n_groups]
    tz_ref = refs[4 * n_groups]
    c0 = pl.program_id(0) * CONV_CB

    def channel(c, carry):
        ch = c0 + c
        for o in range(HY_ORDER):
            row = jnp.broadcast_to(kf_ref[o, pl.ds(c, 1), :], (MXU, 2 * L))
            tz_ref[o] = pltpu.roll(row, 0, 1, stride=1, stride_axis=0).astype(BF16)
        for g in range(n_groups):
            x1_ref, x2_ref, v_ref = in_refs[3 * g:3 * g + 3]
            sec = []
            for s, ref in enumerate((x1_ref, x2_ref, v_ref)):
                k = s * HY_WIDTH + ch
                sec.append(_shift_conv(ref[:, c, :], cw_ref[0, k], cw_ref[1, k], cw_ref[2, k], cb_ref[k]))
            x1, x2, z = sec
            for o, gate in enumerate((x1, x2)):
                y = _toeplitz_conv(z, tz_ref.at[o], L) + z * skip_ref[o, ch]
                z = gate * y
            out_refs[g][:, c, :] = z
        return carry

    lax.fori_loop(0, CONV_CB, channel, 0)


def _hyena(u_ts, kf, conv_w, conv_b, skip, L):
    n_groups = len(u_ts)
    nblk = HY_WIDTH // CONV_CB
    smem = pl.BlockSpec(memory_space=pltpu.SMEM)
    in_specs = [smem, smem, smem, pl.BlockSpec((HY_ORDER, CONV_CB, 2 * L), lambda i: (0, i, 0))]
    args = [conv_w.astype(F32), conv_b.astype(F32), skip.astype(F32), kf.reshape(HY_ORDER, HY_WIDTH, 2 * L)]
    out_specs, out_shape = [], []
    for u in u_ts:
        B = u.shape[0]
        for s in range(3):
            in_specs.append(pl.BlockSpec((B, CONV_CB, L), lambda i, s=s: (0, i + s * nblk, 0)))
            args.append(u)
        out_specs.append(pl.BlockSpec((B, CONV_CB, L), lambda i: (0, i, 0)))
        out_shape.append(jax.ShapeDtypeStruct((B, HY_WIDTH, L), F32))
    return pl.pallas_call(
        functools.partial(_hyena_kernel, L=L, n_groups=n_groups),
        grid=(nblk,),
        in_specs=in_specs,
        out_specs=out_specs,
        out_shape=out_shape,
        scratch_shapes=[pltpu.VMEM((HY_ORDER, MXU, 2 * L), BF16)],
        compiler_params=_cparams(1),
        name="hyena_conv",
    )(*args)


def _rms(x, n):
    return lax.rsqrt(jnp.sum(x * x, axis=-1, keepdims=True) * (1.0 / n) + EPS)


def _mla_prep_kernel(m_ref, ct_ref, st_ref, gqa_ref, gkva_ref, gq_ref, gqs_ref, gk_ref, gks_ref,
                     wqa_ref, wqb_ref, wka_ref, ea_ref, eb_ref, wv_ref, q_ref, k_ref, v_ref):
    m = m_ref[0].astype(F32)
    cq = m[:, :Q_LORA]
    ckv = m[:, Q_LORA:Q_LORA + KV_LORA]
    kr = m_ref[0][:, Q_LORA + KV_LORA:Q_LORA + KV_LORA + LANE]
    cqn = (cq * _rms(cq, Q_LORA) * gqa_ref[...]).astype(BF16)
    ckvn = (ckv * _rms(ckv, KV_LORA) * gkva_ref[...]).astype(BF16)
    qa = jnp.dot(cqn, wqa_ref[...], preferred_element_type=F32)
    qb = jnp.dot(cqn, wqb_ref[...], preferred_element_type=F32)
    ka = jnp.dot(ckvn, wka_ref[...], preferred_element_type=F32) + jnp.dot(kr, ea_ref[...], preferred_element_type=F32)
    kb = jnp.dot(kr, eb_ref[...], preferred_element_type=F32)
    v_ref[0] = jnp.dot(ckvn, wv_ref[...], preferred_element_type=F32).astype(BF16)
    ct, st = ct_ref[...], st_ref[...]
    scale = QK_HEAD ** -0.5 * math.log2(math.e)
    qc, qs = gq_ref[...] * ct * scale, gqs_ref[...] * st * scale
    kc, ks = gk_ref[...] * ct, gks_ref[...] * st
    for h in range(MLA_HEADS):
        sl = slice(h * HEAD_PAD, (h + 1) * HEAD_PAD)
        q_ref[0, h] = (_rms(qa[:, sl], QK_HEAD) * (qa[:, sl] * qc + qb[:, sl] * qs)).astype(BF16)
        k_ref[0, h] = (_rms(ka[:, sl], QK_HEAD) * (ka[:, sl] * kc + kb[:, sl] * ks)).astype(BF16)


def _mla_prep(m, P, tm):
    B, L, _ = m.shape
    w = [P[n] for n in ("wqa", "wqb", "wka", "ea", "eb", "wv")]
    g = [P[n] for n in ("gqa", "gkva", "gq", "gqs", "gk", "gks")]
    hp = MLA_HEADS * HEAD_PAD
    return pl.pallas_call(
        _mla_prep_kernel,
        grid=(B, L // tm),
        in_specs=[pl.BlockSpec((1, tm, MLA_IN), lambda b, i: (b, i, 0)),
                  pl.BlockSpec((tm, HEAD_PAD), lambda b, i: (i, 0)),
                  pl.BlockSpec((tm, HEAD_PAD), lambda b, i: (i, 0))]
        + [_full(a.shape) for a in g] + [_full(a.shape) for a in w],
        out_specs=[
            pl.BlockSpec((1, MLA_HEADS, tm, HEAD_PAD), lambda b, i: (b, 0, i, 0)),
            pl.BlockSpec((1, MLA_HEADS, tm, HEAD_PAD), lambda b, i: (b, 0, i, 0)),
            pl.BlockSpec((1, tm, MLA_HEADS * V_HEAD), lambda b, i: (b, i, 0)),
        ],
        out_shape=[
            jax.ShapeDtypeStruct((B, MLA_HEADS, L, HEAD_PAD), BF16),
            jax.ShapeDtypeStruct((B, MLA_HEADS, L, HEAD_PAD), BF16),
            jax.ShapeDtypeStruct((B, L, MLA_HEADS * V_HEAD), BF16),
        ],
        compiler_params=_cparams(2),
        name="mla_prep",
    )(m, P["ctab"][:L], P["stab"][:L], *g, *w)


def _attn_kernel(q_ref, k_ref, v_ref, o_ref):
    L = v_ref.shape[1]
    one_col = jnp.where(lax.broadcasted_iota(jnp.int32, (L, LANE), 1) == 0, 1.0, 0.0).astype(BF16)
    for pair in range(MLA_HEADS // 2):
        lanes = pl.ds(pair * LANE, LANE)
        v_ext = jnp.concatenate([v_ref[0, :, lanes], one_col], axis=1)
        outs = []
        for h in (2 * pair, 2 * pair + 1):
            s = lax.dot_general(q_ref[0, h], k_ref[0, h], (((1,), (1,)), ((), ())), preferred_element_type=F32)
            p = jnp.exp2(s - jnp.max(s, axis=-1, keepdims=True)).astype(BF16)
            pv = jnp.dot(p, v_ext, preferred_element_type=F32)
            outs.append(pv[:, :LANE] / pv[:, LANE:LANE + 1])
        lane = lax.broadcasted_iota(jnp.int32, outs[0].shape, 1)
        o_ref[0, :, lanes] = jnp.where(lane < V_HEAD, outs[0], outs[1]).astype(BF16)


def _attention(q, k, v, tq):
    B, H, L, _ = q.shape
    return pl.pallas_call(
        _attn_kernel,
        grid=(B, L // tq),
        in_specs=[
            pl.BlockSpec((1, H, tq, HEAD_PAD), lambda b, i: (b, 0, i, 0)),
            pl.BlockSpec((1, H, L, HEAD_PAD), lambda b, i: (b, 0, 0, 0)),
            pl.BlockSpec((1, L, H * V_HEAD), lambda b, i: (b, 0, 0)),
        ],
        out_specs=pl.BlockSpec((1, tq, H * V_HEAD), lambda b, i: (b, i, 0)),
        out_shape=jax.ShapeDtypeStruct((B, L, H * V_HEAD), BF16),
        compiler_params=_cparams(2),
        name="mla_attention",
    )(q, k, v)


def _pack_rows(a):
    bits = lax.bitcast_convert_type(a.astype(BF16).astype(F32), jnp.uint32)
    return (bits[:, :PACKED] >> 16) | (bits[:, PACKED:] & jnp.uint32(0xFFFF0000))


def _unpack_rows(p):
    lo = lax.bitcast_convert_type(p << 16, F32)
    hi = lax.bitcast_convert_type(p & jnp.uint32(0xFFFF0000), F32)
    return jnp.concatenate([lo, hi], axis=1)


def _merge_kernel(x_ref, yh_ref, ym_ref, g_ref, who_ref, wmo_ref, wo_ref, nf_ref, rwh_ref, rwl_ref, rb_ref,
                  x1_ref, h2_ref, lg_ref):
    tm = x_ref.shape[0]
    sub = tm // 2
    for r in range(0, tm, sub):
        rows = pl.ds(r, sub)
        g = g_ref[rows, :].astype(F32)
        a = lax.dot_general(yh_ref[:, rows].astype(BF16), who_ref[...], (((0,), (0,)), ((), ())),
                            preferred_element_type=F32)
        b = jnp.dot(ym_ref[rows, :], wmo_ref[...], preferred_element_type=F32)
        merged = jax.nn.sigmoid(g[:, :D_MODEL]) * a + jax.nn.sigmoid(g[:, D_MODEL:]) * b
        x1 = x_ref[rows, :] + jnp.dot(merged.astype(BF16), wo_ref[...], preferred_element_type=F32)
        x1_ref[rows, :] = x1
        h2 = x1 * lax.rsqrt(jnp.mean(x1 * x1, axis=-1, keepdims=True) + EPS) * nf_ref[...]
        h2_ref[rows, :] = _pack_rows(h2)
        h_hi = h2.astype(BF16)
        h_lo = (h2 - h_hi.astype(F32)).astype(BF16)
        lg_ref[rows, :] = (jnp.dot(h_hi, rwh_ref[...], preferred_element_type=F32)
                           + jnp.dot(h_lo, rwh_ref[...], preferred_element_type=F32)
                           + jnp.dot(h_hi, rwl_ref[...], preferred_element_type=F32) + rb_ref[...])


def _merge(x, yh_t, ym, g, P, tm):
    B, L, _ = x.shape
    nt = L // tm
    w = [P[n] for n in ("w_hy_o", "w_mla_o", "w_out", "norm_ffn", "router_w_hi", "router_w_lo", "router_b")]
    return pl.pallas_call(
        _merge_kernel,
        grid=(B, nt),
        in_specs=[
            pl.BlockSpec((None, tm, D_MODEL), lambda b, i: (b, i, 0)),
            pl.BlockSpec((None, HY_WIDTH, tm), lambda b, i: (b, 0, i)),
            pl.BlockSpec((None, tm, MLA_HEADS * V_HEAD), lambda b, i: (b, i, 0)),
            pl.BlockSpec((None, tm, 2 * D_MODEL), lambda b, i: (b, i, 0)),
        ] + [_full(a.shape) for a in w],
        out_specs=[
            pl.BlockSpec((tm, D_MODEL), lambda b, i: (b * nt + i, 0)),
            pl.BlockSpec((tm, PACKED), lambda b, i: (b * nt + i, 0)),
            pl.BlockSpec((tm, LANE), lambda b, i: (b * nt + i, 0)),
        ],
        out_shape=[
            jax.ShapeDtypeStruct((B * L, D_MODEL), F32),
            jax.ShapeDtypeStruct((B * L, PACKED), jnp.uint32),
            jax.ShapeDtypeStruct((B * L, LANE), F32),
        ],
        compiler_params=_cparams(2),
        name="merge_router",
    )(x, yh_t, ym, g, *w)


def _expert_kernel(be_ref, na_ref, x_ref, wgu_ref, bgu_ref, wd_ref, bd_ref, y_ref, wgu_bf, wd_bf):
    i = pl.program_id(0)
    active = i < na_ref[0]
    new_expert = jnp.logical_or(i == 0, be_ref[i] != be_ref[jnp.maximum(i - 1, 0)])

    @pl.when(jnp.logical_and(active, new_expert))
    def _():
        wgu_bf[...] = wgu_ref[0].astype(BF16)
        wd_bf[...] = wd_ref[0].astype(BF16)

    @pl.when(active)
    def _():
        x = _unpack_rows(x_ref[...]).astype(BF16)
        gu = jnp.dot(x, wgu_bf[...], preferred_element_type=F32) + bgu_ref[0]
        gate = jnp.minimum(gu[:, :D_FF], SWIGLU_LIMIT)
        up = jnp.clip(gu[:, D_FF:], -SWIGLU_LIMIT, SWIGLU_LIMIT)
        act = (up + 1.0) * (gate * jax.nn.sigmoid(SWIGLU_ALPHA * gate))
        y_ref[...] = _pack_rows(jnp.dot(act.astype(BF16), wd_bf[...], preferred_element_type=F32) + bd_ref[0])

    @pl.when(i >= na_ref[0])
    def _():
        y_ref[...] = jnp.zeros_like(y_ref)


def _experts(block_e, n_active, xs, P):
    n_rows = xs.shape[0]
    nb = n_rows // MOE_BLOCK
    grid_spec = pltpu.PrefetchScalarGridSpec(
        num_scalar_prefetch=2,
        grid=(nb,),
        in_specs=[
            pl.BlockSpec((MOE_BLOCK, PACKED), lambda i, be, na: (i, 0)),
            pl.BlockSpec((1, D_MODEL, 2 * D_FF), lambda i, be, na: (be[i], 0, 0)),
            pl.BlockSpec((1, 1, 2 * D_FF), lambda i, be, na: (be[i], 0, 0)),
            pl.BlockSpec((1, D_FF, D_MODEL), lambda i, be, na: (be[i], 0, 0)),
            pl.BlockSpec((1, 1, D_MODEL), lambda i, be, na: (be[i], 0, 0)),
        ],
        out_specs=pl.BlockSpec((MOE_BLOCK, PACKED), lambda i, be, na: (i, 0)),
        scratch_shapes=[pltpu.VMEM((D_MODEL, 2 * D_FF), BF16), pltpu.VMEM((D_FF, D_MODEL), BF16)],
    )
    return pl.pallas_call(
        _expert_kernel,
        grid_spec=grid_spec,
        out_shape=jax.ShapeDtypeStruct((n_rows, PACKED), jnp.uint32),
        compiler_params=_cparams(1),
        name="moe_experts",
    )(block_e, n_active, xs, P["w_gu"], P["b_gu"], P["w_down"], P["b_down"])


def _combine_kernel(x1_ref, yk_ref, gt_ref, o_ref):
    gt = gt_ref[...]
    acc = _unpack_rows(yk_ref[0]) * gt[:, 0:1]
    for k in range(1, TOP_K):
        acc = acc + _unpack_rows(yk_ref[k]) * gt[:, k:k + 1]
    o_ref[...] = x1_ref[...] + acc


def _combine(x1, yk, gates, tm):
    T = x1.shape[0]
    return pl.pallas_call(
        _combine_kernel,
        grid=(T // tm,),
        in_specs=[
            pl.BlockSpec((tm, D_MODEL), lambda i: (i, 0)),
            pl.BlockSpec((TOP_K, tm, PACKED), lambda i: (0, i, 0)),
            pl.BlockSpec((tm, LANE), lambda i: (i, 0)),
        ],
        out_specs=pl.BlockSpec((tm, D_MODEL), lambda i: (i, 0)),
        out_shape=jax.ShapeDtypeStruct((T, D_MODEL), F32),
        compiler_params=_cparams(1),
        name="moe_combine",
    )(x1, yk, gates)


def _route_kernel(lg_ref, dest_ref, gate_ref, cnt_ref, run_ref, start_ref, *, tm):
    ph, i = pl.program_id(0), pl.program_id(1)
    sub = tm // ROUTE_PARTS
    lane = lax.broadcasted_iota(jnp.int32, (sub, LANE), 1)
    parts = []
    for r in range(0, tm, sub):
        work = jnp.where(lane < N_EXPERTS, lg_ref[pl.ds(r, sub), :], -jnp.inf)
        hots, vals = [], []
        for _ in range(TOP_K):
            m = jnp.max(work, axis=-1, keepdims=True)
            first = jnp.min(jnp.where(work == m, lane, LANE), axis=-1, keepdims=True)
            hot = lane == first
            hots.append(hot)
            vals.append(m)
            work = jnp.where(hot, -jnp.inf, work)
        picked = jnp.zeros((sub, LANE), F32)
        for hot in hots:
            picked = jnp.where(hot, 1.0, picked)
        parts.append((hots, vals, picked))
    chosen = jnp.concatenate([p[2] for p in parts], axis=0)
    tile_cnt = jnp.sum(chosen, axis=0, keepdims=True)

    @pl.when(jnp.logical_and(ph == 0, i == 0))
    def _():
        run_ref[...] = jnp.zeros_like(run_ref)

    @pl.when(ph == 0)
    def _():
        run_ref[...] += tile_cnt

    @pl.when(jnp.logical_and(ph == 1, i == 0))
    def _():
        cnt = run_ref[...]
        cnt_ref[...] = cnt
        padded = jnp.floor((cnt + (MOE_BLOCK - 1)) * (1.0 / MOE_BLOCK)) * MOE_BLOCK
        scan = jnp.broadcast_to(padded, (8, LANE))
        lane8 = lax.broadcasted_iota(jnp.int32, (8, LANE), 1)
        s = 1
        while s < LANE:
            scan = scan + jnp.where(lane8 >= s, pltpu.roll(scan, s, 1), 0.0)
            s *= 2
        start_ref[...] = scan[0:1] - padded
        run_ref[...] = jnp.zeros_like(run_ref)

    @pl.when(ph == 1)
    def _():
        r = lax.broadcasted_iota(jnp.int32, (tm, tm), 0)
        c = lax.broadcasted_iota(jnp.int32, (tm, tm), 1)
        earlier = jnp.where(c < r, 1.0, 0.0).astype(BF16)
        before = jnp.dot(earlier, chosen.astype(BF16), preferred_element_type=F32) + (run_ref[...] + start_ref[...])
        for n, (hots, vals, _) in enumerate(parts):
            rows = pl.ds(n * sub, sub)
            ex = [jnp.exp(v - vals[0]) for v in vals]
            inv = 1.0 / (ex[0] + ex[1] + ex[2] + ex[3])
            gates = jnp.zeros((sub, LANE), F32)
            dest = jnp.zeros((sub, LANE), F32)
            mine = before[n * sub:(n + 1) * sub]
            for k, hot in enumerate(hots):
                gates = jnp.where(lane == k, ex[k] * inv, gates)
                dest = jnp.where(lane == k, jnp.sum(jnp.where(hot, mine, 0.0), axis=-1, keepdims=True), dest)
            gate_ref[rows, :] = gates
            dest_ref[rows, :] = dest.astype(jnp.int32)
        run_ref[...] += tile_cnt


def _route(logits, tm):
    T = logits.shape[0]
    dest, gates, counts = pl.pallas_call(
        functools.partial(_route_kernel, tm=tm),
        grid=(2, T // tm),
        in_specs=[pl.BlockSpec((tm, LANE), lambda p, i: (i, 0))],
        out_specs=[
            pl.BlockSpec((tm, LANE), lambda p, i: (p * i, 0)),
            pl.BlockSpec((tm, LANE), lambda p, i: (p * i, 0)),
            pl.BlockSpec((1, LANE), lambda p, i: (0, 0)),
        ],
        out_shape=[
            jax.ShapeDtypeStruct((T, LANE), jnp.int32),
            jax.ShapeDtypeStruct((T, LANE), F32),
            jax.ShapeDtypeStruct((1, LANE), F32),
        ],
        scratch_shapes=[pltpu.VMEM((1, LANE), F32), pltpu.VMEM((1, LANE), F32)],
        compiler_params=_cparams(2),
        name="moe_route",
    )(logits)
    counts = counts[0, :N_EXPERTS].astype(jnp.int32)
    pad_end = jnp.cumsum((counts + MOE_BLOCK - 1) // MOE_BLOCK * MOE_BLOCK)
    n_blocks = -(-(T * TOP_K + N_EXPERTS * (MOE_BLOCK - 1)) // MOE_BLOCK)
    block_start = jnp.arange(n_blocks, dtype=jnp.int32) * MOE_BLOCK
    block_e = jnp.sum((block_start[:, None] >= pad_end[None, :]).astype(jnp.int32), axis=1)
    block_e = jnp.minimum(block_e, N_EXPERTS - 1)
    n_active = (pad_end[-1] // MOE_BLOCK).astype(jnp.int32).reshape(1)
    dest_flat = dest[:, :TOP_K].T.reshape(TOP_K * T)
    return dest_flat, gates, block_e, n_active, n_blocks * MOE_BLOCK


def _sc_workers():
    info = plsc.get_sparse_core_info()
    return info.num_cores, info.num_cores * info.num_subcores


def _sc_scatter_rows(rows, dest_flat, n_rows):
    T, W = rows.shape
    n_cores, n_workers = _sc_workers()
    per_w = T // n_workers
    assert per_w * n_workers == T and per_w % SC_CHUNK == 0

    def body(rows_hbm, dest_hbm, out_hbm, buf, *idx):
        wid = lax.axis_index("s") * n_cores + lax.axis_index("c")

        @pl.loop(0, per_w // SC_CHUNK)
        def _(j):
            off = pl.multiple_of(wid * per_w + j * SC_CHUNK, SC_CHUNK)
            pltpu.sync_copy(rows_hbm.at[pl.ds(off, SC_CHUNK)], buf)
            for k in range(TOP_K):
                pltpu.sync_copy(dest_hbm.at[pl.ds(k * T + off, SC_CHUNK)], idx[k])
            for k in range(TOP_K):
                pltpu.sync_copy(buf, out_hbm.at[idx[k]])

    return pl.kernel(
        body,
        out_type=jax.ShapeDtypeStruct((n_rows, W), rows.dtype),
        mesh=plsc.VectorSubcoreMesh(core_axis_name="c", subcore_axis_name="s"),
        scratch_types=[pltpu.VMEM((SC_CHUNK, W), rows.dtype)] + [pltpu.VMEM((SC_CHUNK,), jnp.int32)] * TOP_K,
        name="moe_dispatch_sc",
    )(rows, dest_flat)


def _sc_gather_rows(table, dest_flat):
    N = dest_flat.shape[0]
    W = table.shape[1]
    n_cores, n_workers = _sc_workers()
    per_w = N // n_workers
    assert per_w * n_workers == N and per_w % SC_CHUNK == 0

    def body(table_hbm, dest_hbm, out_hbm, buf, idx):
        wid = lax.axis_index("s") * n_cores + lax.axis_index("c")

        @pl.loop(0, per_w // SC_CHUNK)
        def _(j):
            off = pl.multiple_of(wid * per_w + j * SC_CHUNK, SC_CHUNK)
            pltpu.sync_copy(dest_hbm.at[pl.ds(off, SC_CHUNK)], idx)
            pltpu.sync_copy(table_hbm.at[idx], buf)
            pltpu.sync_copy(buf, out_hbm.at[pl.ds(off, SC_CHUNK)])

    return pl.kernel(
        body,
        out_type=jax.ShapeDtypeStruct((N, W), table.dtype),
        mesh=plsc.VectorSubcoreMesh(core_axis_name="c", subcore_axis_name="s"),
        scratch_types=[pltpu.VMEM((SC_CHUNK, W), table.dtype), pltpu.VMEM((SC_CHUNK,), jnp.int32)],
        name="moe_gather_sc",
    )(table, dest_flat)


def _moe(x1, h2p, logits, P, tm):
    T = x1.shape[0]
    dest_flat, gates, block_e, n_active, n_rows = _route(logits, tm)
    xs = _sc_scatter_rows(h2p, dest_flat, n_rows)
    ys = _experts(block_e, n_active, xs, P)
    yk = _sc_gather_rows(ys, dest_flat).reshape(TOP_K, T, PACKED)
    return _combine(x1, yk, gates, tm)


def _prep_params(norm_mix, w_in, q_a_norm, w_uq, kv_a_norm, w_ukv, q_norm, k_norm, w_hy_o, w_mla_o, w_out,
                 norm_ffn, router_w, router_b, w_gu, b_gu, w_down, b_down, L):
    P = {}
    P["norm_mix"] = norm_mix.astype(F32).reshape(1, D_MODEL)
    P["w_hy_t"] = w_in[:, :HY_END].T.astype(BF16)
    P["w_m"] = jnp.pad(w_in[:, HY_END:KR_END], ((0, 0), (0, MLA_IN - (KR_END - HY_END)))).astype(BF16)
    P["w_g"] = w_in[:, KR_END:].astype(BF16)

    half = QK_ROPE // 2
    swap = jnp.concatenate([jnp.arange(QK_NOPE), QK_NOPE + half + jnp.arange(half), QK_NOPE + jnp.arange(half)])
    rope_lane = (jnp.arange(QK_HEAD) >= QK_NOPE)

    def slots(w):
        return jnp.pad(w, ((0, 0), (0, 0), (0, HEAD_PAD - QK_HEAD))).reshape(w.shape[0], MLA_HEADS * HEAD_PAD)

    wq = w_uq.astype(F32).reshape(Q_LORA, MLA_HEADS, QK_HEAD)
    P["wqa"] = slots(wq).astype(BF16)
    P["wqb"] = slots(jnp.where(rope_lane, wq[:, :, swap], 0.0)).astype(BF16)
    wkv = w_ukv.astype(F32).reshape(KV_LORA, MLA_HEADS, QK_NOPE + V_HEAD)
    P["wka"] = slots(jnp.pad(wkv[:, :, :QK_NOPE], ((0, 0), (0, 0), (0, QK_ROPE)))).astype(BF16)
    P["wv"] = wkv[:, :, QK_NOPE:].reshape(KV_LORA, MLA_HEADS * V_HEAD).astype(BF16)
    eye = jnp.eye(LANE, QK_ROPE, dtype=F32)
    e_a = jnp.pad(eye, ((0, 0), (QK_NOPE, 0)))
    e_b = e_a[:, swap] * rope_lane
    P["ea"] = slots(jnp.broadcast_to(e_a[:, None, :], (LANE, MLA_HEADS, QK_HEAD))).astype(BF16)
    P["eb"] = slots(jnp.broadcast_to(e_b[:, None, :], (LANE, MLA_HEADS, QK_HEAD))).astype(BF16)

    sign = jnp.concatenate([jnp.zeros(QK_NOPE), -jnp.ones(half), jnp.ones(half)]).astype(F32)

    def lane_pad(g):
        return jnp.pad(g, (0, HEAD_PAD - QK_HEAD)).reshape(1, HEAD_PAD)

    for name, g in (("gq", q_norm.astype(F32)), ("gk", k_norm.astype(F32))):
        P[name] = lane_pad(g)
        P[name + "s"] = lane_pad(g[swap] * sign)
    P["gqa"] = q_a_norm.astype(F32).reshape(1, Q_LORA)
    P["gkva"] = kv_a_norm.astype(F32).reshape(1, KV_LORA)
    inv = 1.0 / (ROPE_THETA ** (jnp.arange(0, QK_ROPE, 2, dtype=F32) / QK_ROPE))
    ang = jnp.arange(L, dtype=F32)[:, None] * inv[None, :]
    cos, sin = jnp.cos(ang), jnp.sin(ang)
    P["ctab"] = jnp.concatenate([jnp.ones((L, QK_NOPE), F32), cos, cos, jnp.zeros((L, HEAD_PAD - QK_HEAD), F32)], axis=1)
    P["stab"] = jnp.concatenate([jnp.zeros((L, QK_NOPE), F32), sin, sin, jnp.zeros((L, HEAD_PAD - QK_HEAD), F32)], axis=1)

    P["w_hy_o"] = w_hy_o.astype(BF16)
    P["w_mla_o"] = w_mla_o.astype(BF16)
    P["w_out"] = w_out.astype(BF16)
    P["norm_ffn"] = norm_ffn.astype(F32).reshape(1, D_MODEL)
    rw = jnp.pad(router_w.astype(F32), ((0, 0), (0, LANE - N_EXPERTS)))
    P["router_w_hi"] = rw.astype(BF16)
    P["router_w_lo"] = (rw - P["router_w_hi"].astype(F32)).astype(BF16)
    P["router_b"] = jnp.pad(router_b.astype(F32), (0, LANE - N_EXPERTS)).reshape(1, LANE)
    P["w_gu"] = w_gu.astype(F32)
    P["b_gu"] = b_gu.astype(F32).reshape(N_EXPERTS, 1, 2 * D_FF)
    P["w_down"] = w_down.astype(F32)
    P["b_down"] = b_down.astype(F32).reshape(N_EXPERTS, 1, D_MODEL)
    return P


def _tile(n, want):
    t = min(n, want)
    assert n % t == 0
    return t


def _layer(xs, norm_mix, w_in, hy_conv_w, hy_conv_b, hy_fw1, hy_fb1, hy_fw2, hy_fb2, hy_fw3, hy_freq, hy_decay, hy_skip,
           q_a_norm, w_uq, kv_a_norm, w_ukv, q_norm, k_norm, w_hy_o, w_mla_o, w_out,
           norm_ffn, router_w, router_b, w_gu, b_gu, w_down, b_down):
    L = xs[0].shape[1]
    assert all(x.shape[1] == L and x.shape[2] == D_MODEL for x in xs) and L % MXU == 0
    P = _prep_params(norm_mix, w_in, q_a_norm, w_uq, kv_a_norm, w_ukv, q_norm, k_norm, w_hy_o, w_mla_o, w_out,
                     norm_ffn, router_w, router_b, w_gu, b_gu, w_down, b_down, L)
    kf = _hyena_filters(L, hy_fw1, hy_fb1, hy_fw2, hy_fb2, hy_fw3, hy_freq, hy_decay)
    tm = _tile(L, 512)
    proj = [_inproj(x, P["norm_mix"], P["w_hy_t"], P["w_m"], P["w_g"], tm) for x in xs]
    y_ts = _hyena([u_t for u_t, _, _ in proj], kf, hy_conv_w, hy_conv_b, hy_skip, L)
    mixed = []
    for x, (_, m, g), y_t in zip(xs, proj, y_ts):
        q, k, v = _mla_prep(m, P, tm)
        y_mla = _attention(q, k, v, _tile(L, 256))
        mixed.append(_merge(x, y_t, y_mla, g, P, tm))
    outs = [None] * len(xs)
    for n in sorted(range(len(xs)), key=lambda n: -xs[n].shape[0]):
        x1, h2, logits = mixed[n]
        outs[n] = _moe(x1, h2, logits, P, tm).reshape(xs[n].shape)
    return outs


def kernel(x_prompt, x_sample, norm_mix, w_in, hy_conv_w, hy_conv_b, hy_fw1, hy_fb1, hy_fw2, hy_fb2, hy_fw3, hy_freq, hy_decay, hy_skip, q_a_norm, w_uq, kv_a_norm, w_ukv, q_norm, k_norm, w_hy_o, w_mla_o, w_out, norm_ffn, router_w, router_b, w_gu, b_gu, w_down, b_down):
    depth = norm_mix.shape[0]
    xs = [x_prompt, x_sample]
    weights = (norm_mix, w_in, hy_conv_w, hy_conv_b, hy_fw1, hy_fb1, hy_fw2, hy_fb2, hy_fw3, hy_freq, hy_decay, hy_skip,
               q_a_norm, w_uq, kv_a_norm, w_ukv, q_norm, k_norm, w_hy_o, w_mla_o, w_out,
               norm_ffn, router_w, router_b, w_gu, b_gu, w_down, b_down)
    for l in range(depth):
        xs = _layer(xs, *[w[l] for w in weights])
    return (xs[0], xs[1])
```

```python
import functools
import math

import jax
import jax.numpy as jnp
from jax import lax
from jax.experimental import pallas as pl
from jax.experimental.pallas import tpu as pltpu
from jax.experimental.pallas import tpu_sc as plsc

F32 = jnp.float32
BF16 = jnp.bfloat16

D_MODEL = 1024
HY_WIDTH = 512
HY_ORDER = 2
HY_EMB_BANDS = 16
HY_FILTER_DIM = 64
MLA_HEADS = 8
V_HEAD = 64
QK_NOPE = 64
QK_ROPE = 32
QK_HEAD = QK_NOPE + QK_ROPE
Q_LORA = 256
KV_LORA = 128
ROPE_THETA = 10000.0
HY_END = 3 * HY_WIDTH
Q_END = HY_END + Q_LORA
KV_END = Q_END + KV_LORA
KR_END = KV_END + QK_ROPE
G1_END = KR_END + D_MODEL
N_EXPERTS = 32
TOP_K = 4
D_FF = D_MODEL
SWIGLU_LIMIT = 7.0
SWIGLU_ALPHA = 1.702
EPS = 1e-6

LANE = 128
MXU = 256
HEAD_PAD = LANE
MLA_IN = 512
CONV_CB = 8
CONV_UNROLL = 4
MOE_BLOCK = 512
PACKED = D_MODEL // 2
SC_CHUNK = 64
VMEM_LIMIT = 52 * 1024 * 1024


def _cparams(n_axes):
    return pltpu.CompilerParams(dimension_semantics=("arbitrary",) * n_axes, vmem_limit_bytes=VMEM_LIMIT)


def _full(shape):
    n = len(shape)
    return pl.BlockSpec(shape, lambda *_: (0,) * n)


def _inproj_kernel(x_ref, g_ref, why_ref, wm_ref, wg_ref, u_ref, m_ref, gate_ref):
    x = x_ref[...]
    r = lax.rsqrt(jnp.mean(x * x, axis=-1, keepdims=True) + EPS)
    h = (x * r * g_ref[...]).astype(BF16)
    u_ref[...] = lax.dot_general(why_ref[...], h, (((1,), (1,)), ((), ())), preferred_element_type=F32)
    m_ref[...] = jnp.dot(h, wm_ref[...], preferred_element_type=F32).astype(BF16)
    gate_ref[...] = jnp.dot(h, wg_ref[...], preferred_element_type=F32).astype(BF16)


def _inproj(x, g, w_hy_t, w_m, w_g, tm):
    B, L, _ = x.shape
    return pl.pallas_call(
        _inproj_kernel,
        grid=(B, L // tm),
        in_specs=[
            pl.BlockSpec((None, tm, D_MODEL), lambda b, i: (b, i, 0)),
            _full((1, D_MODEL)),
            _full(w_hy_t.shape),
            _full(w_m.shape),
            _full(w_g.shape),
        ],
        out_specs=[
            pl.BlockSpec((None, HY_END, tm), lambda b, i: (b, 0, i)),
            pl.BlockSpec((None, tm, MLA_IN), lambda b, i: (b, i, 0)),
            pl.BlockSpec((None, tm, 2 * D_MODEL), lambda b, i: (b, i, 0)),
        ],
        out_shape=[
            jax.ShapeDtypeStruct((B, HY_END, L), F32),
            jax.ShapeDtypeStruct((B, L, MLA_IN), BF16),
            jax.ShapeDtypeStruct((B, L, 2 * D_MODEL), BF16),
        ],
        compiler_params=_cparams(2),
        name="inproj",
    )(x, g, w_hy_t, w_m, w_g)


def _filter_kernel(bands_ref, w1t_ref, w1c_ref, w1s_ref, b1_ref, w2_ref, b2_ref, fr_ref, w3f_ref, w3b_ref, dec_ref,
                   out_ref, *, L):
    hi = lax.Precision.HIGHEST
    n = 2 * L
    j = lax.broadcasted_iota(jnp.int32, (1, n), 1)
    pos = jnp.abs(j - L).astype(F32)
    t01 = pos / max(L - 1, 1)
    ang = ((2.0 * math.pi / L) * pos) * bands_ref[...]
    pre = (w1t_ref[...] * t01
           + jnp.dot(w1c_ref[...], jnp.cos(ang), precision=hi, preferred_element_type=F32)
           - jnp.dot(w1s_ref[...], jnp.sin(ang), precision=hi, preferred_element_type=F32)
           + b1_ref[...])
    z = jnp.sin(fr_ref[0] * pre)
    z = jnp.sin(fr_ref[1] * (jnp.dot(w2_ref[...], z, precision=hi, preferred_element_type=F32) + b2_ref[...]))
    win = jnp.exp(-t01 * jnp.abs(dec_ref[...]))
    h_b = jnp.dot(w3b_ref[...], z[:, :L], precision=hi, preferred_element_type=F32) * win[:, :L]
    h_f = jnp.dot(w3f_ref[...], z[:, L:], precision=hi, preferred_element_type=F32) * win[:, L:]
    out_ref[:, :L] = jnp.where(j[:, :L] == 0, 0.0, h_b)
    out_ref[:, L:] = h_f


def _hyena_filters(L, fw1, fb1, fw2, fb2, fw3, freq, decay):
    rows = HY_ORDER * HY_WIDTH
    rb = 256
    bands = jnp.linspace(1e-4, HY_EMB_BANDS - 1, HY_EMB_BANDS, dtype=F32).reshape(HY_EMB_BANDS, 1)
    w1 = fw1.astype(F32).T
    w3 = fw3.astype(F32).T
    fd = HY_FILTER_DIM
    return pl.pallas_call(
        functools.partial(_filter_kernel, L=L),
        grid=(rows // rb,),
        in_specs=[
            _full((HY_EMB_BANDS, 1)),
            _full((fd, 1)),
            _full((fd, HY_EMB_BANDS)),
            _full((fd, HY_EMB_BANDS)),
            _full((fd, 1)),
            _full((fd, fd)),
            _full((fd, 1)),
            _full((2, fd, 1)),
            pl.BlockSpec((rb, fd), lambda i: (i, 0)),
            pl.BlockSpec((rb, fd), lambda i: (i, 0)),
            pl.BlockSpec((rb, 1), lambda i: (i, 0)),
        ],
        out_specs=pl.BlockSpec((rb, 2 * L), lambda i: (i, 0)),
        out_shape=jax.ShapeDtypeStruct((rows, 2 * L), F32),
        compiler_params=_cparams(1),
        name="hyena_filters",
    )(bands, w1[:, :1], w1[:, 1:1 + HY_EMB_BANDS], w1[:, 1 + HY_EMB_BANDS:], fb1.astype(F32).reshape(fd, 1),
      fw2.astype(F32).T, fb2.astype(F32).reshape(fd, 1), freq.astype(F32).reshape(2, fd, 1),
      w3[:rows], w3[rows:], decay.astype(F32).reshape(rows, 1))


def _shift_conv(u, w0, w1, w2, b):
    L = u.shape[1]
    lane = lax.broadcasted_iota(jnp.int32, (1, L), 1)
    w0_row = jnp.where(lane == 0, 0.0, w0)
    w2_row = jnp.where(lane == L - 1, 0.0, w2)
    return pltpu.roll(u, 1, 1) * w0_row + u * w1 + pltpu.roll(u, L - 1, 1) * w2_row + b


def _toeplitz_conv(z, tz_ref, L):
    B = z.shape[0]
    nj = L // MXU
    zb = z.astype(BF16)
    tiles = [zb[:, MXU * J:MXU * (J + 1)] for J in range(nj)]
    acc = [None] * nj
    for d in range(-(nj - 1), nj):
        js = [J for J in range(nj) if 0 <= J + d < nj]
        lhs = tiles[js[0]] if len(js) == 1 else jnp.concatenate([tiles[J] for J in js], axis=0)
        res = jnp.dot(lhs, tz_ref[:, L + MXU * d:L + MXU * (d + 1)], preferred_element_type=F32)
        for n, J in enumerate(js):
            part = res[n * B:(n + 1) * B]
            acc[J + d] = part if acc[J + d] is None else acc[J + d] + part
    return acc[0] if nj == 1 else jnp.concatenate(acc, axis=1)


def _hyena_kernel(cw_ref, cb_ref, skip_ref, kf_ref, *refs, L, n_groups):
    in_refs = refs[:3 * n_groups]
    out_refs = refs[3 * n_groups:4 * n_groups]
    tz_ref = refs[4 * n_groups]
    c0 = pl.program_id(0) * CONV_CB

    def channels(p, carry):
        for u in range(CONV_UNROLL):
            c = p * CONV_UNROLL + u
            ch = c0 + c
            for o in range(HY_ORDER):
                row = jnp.broadcast_to(kf_ref[o, pl.ds(c, 1), :], (MXU, 2 * L))
                tz_ref[u, o] = pltpu.roll(row, 0, 1, stride=1, stride_axis=0).astype(BF16)
            for g in range(n_groups):
                stage = refs[4 * n_groups + 1 + g]
                sec = []
                for s, ref in enumerate(in_refs[3 * g:3 * g + 3]):
                    stage[u, s] = ref[:, c, :]
                    k = s * HY_WIDTH + ch
                    sec.append(_shift_conv(stage[u, s], cw_ref[0, k], cw_ref[1, k], cw_ref[2, k], cb_ref[k]))
                x1, x2, z = sec
                for o, gate in enumerate((x1, x2)):
                    y = _toeplitz_conv(z, tz_ref.at[u, o], L) + z * skip_ref[o, ch]
                    z = gate * y
                out_refs[g][:, c, :] = z
        return carry

    lax.fori_loop(0, CONV_CB // CONV_UNROLL, channels, 0)


def _hyena(u_ts, kf, conv_w, conv_b, skip, L):
    n_groups = len(u_ts)
    nblk = HY_WIDTH // CONV_CB
    smem = pl.BlockSpec(memory_space=pltpu.SMEM)
    in_specs = [smem, smem, smem, pl.BlockSpec((HY_ORDER, CONV_CB, 2 * L), lambda i: (0, i, 0))]
    args = [conv_w.astype(F32), conv_b.astype(F32), skip.astype(F32), kf.reshape(HY_ORDER, HY_WIDTH, 2 * L)]
    out_specs, out_shape = [], []
    for u in u_ts:
        B = u.shape[0]
        for s in range(3):
            in_specs.append(pl.BlockSpec((B, CONV_CB, L), lambda i, s=s: (0, i + s * nblk, 0)))
            args.append(u)
        out_specs.append(pl.BlockSpec((B, CONV_CB, L), lambda i: (0, i, 0)))
        out_shape.append(jax.ShapeDtypeStruct((B, HY_WIDTH, L), F32))
    return pl.pallas_call(
        functools.partial(_hyena_kernel, L=L, n_groups=n_groups),
        grid=(nblk,),
        in_specs=in_specs,
        out_specs=out_specs,
        out_shape=out_shape,
        scratch_shapes=[pltpu.VMEM((CONV_UNROLL, HY_ORDER, MXU, 2 * L), BF16)]
        + [pltpu.VMEM((CONV_UNROLL, 3, u.shape[0], L), F32) for u in u_ts],
        compiler_params=_cparams(1),
        name="hyena_conv",
    )(*args)


def _rms(x, n):
    return lax.rsqrt(jnp.sum(x * x, axis=-1, keepdims=True) * (1.0 / n) + EPS)


def _mla_prep_kernel(m_ref, ct_ref, st_ref, gqa_ref, gkva_ref, gq_ref, gqs_ref, gk_ref, gks_ref,
                     wqa_ref, wqb_ref, wka_ref, ea_ref, eb_ref, wv_ref, q_ref, k_ref, v_ref):
    m = m_ref[0].astype(F32)
    cq = m[:, :Q_LORA]
    ckv = m[:, Q_LORA:Q_LORA + KV_LORA]
    kr = m_ref[0][:, Q_LORA + KV_LORA:Q_LORA + KV_LORA + LANE]
    cqn = (cq * _rms(cq, Q_LORA) * gqa_ref[...]).astype(BF16)
    ckvn = (ckv * _rms(ckv, KV_LORA) * gkva_ref[...]).astype(BF16)
    qa = jnp.dot(cqn, wqa_ref[...], preferred_element_type=F32)
    qb = jnp.dot(cqn, wqb_ref[...], preferred_element_type=F32)
    ka = jnp.dot(ckvn, wka_ref[...], preferred_element_type=F32) + jnp.dot(kr, ea_ref[...], preferred_element_type=F32)
    kb = jnp.dot(kr, eb_ref[...], preferred_element_type=F32)
    v_ref[0] = jnp.dot(ckvn, wv_ref[...], preferred_element_type=F32).astype(BF16)
    ct, st = ct_ref[...], st_ref[...]
    scale = QK_HEAD ** -0.5 * math.log2(math.e)
    qc, qs = gq_ref[...] * ct * scale, gqs_ref[...] * st * scale
    kc, ks = gk_ref[...] * ct, gks_ref[...] * st
    for h in range(MLA_HEADS):
        sl = slice(h * HEAD_PAD, (h + 1) * HEAD_PAD)
        q_ref[0, h] = (_rms(qa[:, sl], QK_HEAD) * (qa[:, sl] * qc + qb[:, sl] * qs)).astype(BF16)
        k_ref[0, h] = (_rms(ka[:, sl], QK_HEAD) * (ka[:, sl] * kc + kb[:, sl] * ks)).astype(BF16)


def _mla_prep(m, P, tm):
    B, L, _ = m.shape
    w = [P[n] for n in ("wqa", "wqb", "wka", "ea", "eb", "wv")]
    g = [P[n] for n in ("gqa", "gkva", "gq", "gqs", "gk", "gks")]
    hp = MLA_HEADS * HEAD_PAD
    return pl.pallas_call(
        _mla_prep_kernel,
        grid=(B, L // tm),
        in_specs=[pl.BlockSpec((1, tm, MLA_IN), lambda b, i: (b, i, 0)),
                  pl.BlockSpec((tm, HEAD_PAD), lambda b, i: (i, 0)),
                  pl.BlockSpec((tm, HEAD_PAD), lambda b, i: (i, 0))]
        + [_full(a.shape) for a in g] + [_full(a.shape) for a in w],
        out_specs=[
            pl.BlockSpec((1, MLA_HEADS, tm, HEAD_PAD), lambda b, i: (b, 0, i, 0)),
            pl.BlockSpec((1, MLA_HEADS, tm, HEAD_PAD), lambda b, i: (b, 0, i, 0)),
            pl.BlockSpec((1, tm, MLA_HEADS * V_HEAD), lambda b, i: (b, i, 0)),
        ],
        out_shape=[
            jax.ShapeDtypeStruct((B, MLA_HEADS, L, HEAD_PAD), BF16),
            jax.ShapeDtypeStruct((B, MLA_HEADS, L, HEAD_PAD), BF16),
            jax.ShapeDtypeStruct((B, L, MLA_HEADS * V_HEAD), BF16),
        ],
        compiler_params=_cparams(2),
        name="mla_prep",
    )(m, P["ctab"][:L], P["stab"][:L], *g, *w)


def _attn_kernel(q_ref, k_ref, v_ref, o_ref):
    L = v_ref.shape[1]
    one_col = jnp.where(lax.broadcasted_iota(jnp.int32, (L, LANE), 1) == 0, 1.0, 0.0).astype(BF16)
    for pair in range(MLA_HEADS // 2):
        lanes = pl.ds(pair * LANE, LANE)
        v_ext = jnp.concatenate([v_ref[0, :, lanes], one_col], axis=1)
        outs = []
        for h in (2 * pair, 2 * pair + 1):
            s = lax.dot_general(q_ref[0, h], k_ref[0, h], (((1,), (1,)), ((), ())), preferred_element_type=F32)
            p = jnp.exp2(s - jnp.max(s, axis=-1, keepdims=True)).astype(BF16)
            pv = jnp.dot(p, v_ext, preferred_element_type=F32)
            outs.append(pv[:, :LANE] / pv[:, LANE:LANE + 1])
        lane = lax.broadcasted_iota(jnp.int32, outs[0].shape, 1)
        o_ref[0, :, lanes] = jnp.where(lane < V_HEAD, outs[0], outs[1]).astype(BF16)


def _attention(q, k, v, tq):
    B, H, L, _ = q.shape
    return pl.pallas_call(
        _attn_kernel,
        grid=(B, L // tq),
        in_specs=[
            pl.BlockSpec((1, H, tq, HEAD_PAD), lambda b, i: (b, 0, i, 0)),
            pl.BlockSpec((1, H, L, HEAD_PAD), lambda b, i: (b, 0, 0, 0)),
            pl.BlockSpec((1, L, H * V_HEAD), lambda b, i: (b, 0, 0)),
        ],
        out_specs=pl.BlockSpec((1, tq, H * V_HEAD), lambda b, i: (b, i, 0)),
        out_shape=jax.ShapeDtypeStruct((B, L, H * V_HEAD), BF16),
        compiler_params=_cparams(2),
        name="mla_attention",
    )(q, k, v)


def _pack_rows(a):
    bits = lax.bitcast_convert_type(a.astype(BF16).astype(F32), jnp.uint32)
    return (bits[:, :PACKED] >> 16) | (bits[:, PACKED:] & jnp.uint32(0xFFFF0000))


def _unpack_rows(p):
    lo = lax.bitcast_convert_type(p << 16, F32)
    hi = lax.bitcast_convert_type(p & jnp.uint32(0xFFFF0000), F32)
    return jnp.concatenate([lo, hi], axis=1)


def _merge_kernel(x_ref, yh_ref, ym_ref, g_ref, who_ref, wmo_ref, wo_ref, nf_ref, rwh_ref, rwl_ref, rb_ref,
                  x1_ref, h2_ref, ids_ref, gate_ref, cnt_ref):
    tm = x_ref.shape[0]
    sub = tm // 2

    @pl.when(jnp.logical_and(pl.program_id(0) == 0, pl.program_id(1) == 0))
    def _():
        cnt_ref[...] = jnp.zeros_like(cnt_ref)

    lane = lax.broadcasted_iota(jnp.int32, (sub, LANE), 1)
    for r in range(0, tm, sub):
        rows = pl.ds(r, sub)
        g = g_ref[rows, :].astype(F32)
        a = lax.dot_general(yh_ref[:, rows].astype(BF16), who_ref[...], (((0,), (0,)), ((), ())),
                            preferred_element_type=F32)
        b = jnp.dot(ym_ref[rows, :], wmo_ref[...], preferred_element_type=F32)
        merged = jax.nn.sigmoid(g[:, :D_MODEL]) * a + jax.nn.sigmoid(g[:, D_MODEL:]) * b
        x1 = x_ref[rows, :] + jnp.dot(merged.astype(BF16), wo_ref[...], preferred_element_type=F32)
        x1_ref[rows, :] = x1
        h2 = x1 * lax.rsqrt(jnp.mean(x1 * x1, axis=-1, keepdims=True) + EPS) * nf_ref[...]
        h2_ref[rows, :] = _pack_rows(h2)
        h_hi = h2.astype(BF16)
        h_lo = (h2 - h_hi.astype(F32)).astype(BF16)
        logits = (jnp.dot(h_hi, rwh_ref[...], preferred_element_type=F32)
                  + jnp.dot(h_lo, rwh_ref[...], preferred_element_type=F32)
                  + jnp.dot(h_hi, rwl_ref[...], preferred_element_type=F32) + rb_ref[...])
        work = jnp.where(lane < N_EXPERTS, logits, -jnp.inf)
        ids = jnp.zeros((sub, LANE), jnp.int32)
        picked = jnp.zeros((sub, LANE), F32)
        vals = []
        for k in range(TOP_K):
            m = jnp.max(work, axis=-1, keepdims=True)
            first = jnp.min(jnp.where(work == m, lane, LANE), axis=-1, keepdims=True)
            hot = lane == first
            vals.append(m)
            ids = jnp.where(lane == k, first, ids)
            picked = jnp.where(hot, 1.0, picked)
            work = jnp.where(hot, -jnp.inf, work)
        ex = [jnp.exp(v - vals[0]) for v in vals]
        inv = 1.0 / (ex[0] + ex[1] + ex[2] + ex[3])
        gates = jnp.zeros((sub, LANE), F32)
        for k in range(TOP_K):
            gates = jnp.where(lane == k, ex[k] * inv, gates)
        ids_ref[rows, :] = ids
        gate_ref[rows, :] = gates
        cnt_ref[...] += jnp.sum(picked, axis=0, keepdims=True)


def _merge(x, yh_t, ym, g, P, tm):
    B, L, _ = x.shape
    nt = L // tm
    w = [P[n] for n in ("w_hy_o", "w_mla_o", "w_out", "norm_ffn", "router_w_hi", "router_w_lo", "router_b")]
    return pl.pallas_call(
        _merge_kernel,
        grid=(B, nt),
        in_specs=[
            pl.BlockSpec((None, tm, D_MODEL), lambda b, i: (b, i, 0)),
            pl.BlockSpec((None, HY_WIDTH, tm), lambda b, i: (b, 0, i)),
            pl.BlockSpec((None, tm, MLA_HEADS * V_HEAD), lambda b, i: (b, i, 0)),
            pl.BlockSpec((None, tm, 2 * D_MODEL), lambda b, i: (b, i, 0)),
        ] + [_full(a.shape) for a in w],
        out_specs=[
            pl.BlockSpec((tm, D_MODEL), lambda b, i: (b * nt + i, 0)),
            pl.BlockSpec((tm, PACKED), lambda b, i: (b * nt + i, 0)),
            pl.BlockSpec((tm, LANE), lambda b, i: (b * nt + i, 0)),
            pl.BlockSpec((tm, LANE), lambda b, i: (b * nt + i, 0)),
            pl.BlockSpec((1, LANE), lambda b, i: (0, 0)),
        ],
        out_shape=[
            jax.ShapeDtypeStruct((B * L, D_MODEL), F32),
            jax.ShapeDtypeStruct((B * L, PACKED), jnp.uint32),
            jax.ShapeDtypeStruct((B * L, LANE), jnp.int32),
            jax.ShapeDtypeStruct((B * L, LANE), F32),
            jax.ShapeDtypeStruct((1, LANE), F32),
        ],
        compiler_params=_cparams(2),
        name="merge_router",
    )(x, yh_t, ym, g, *w)


def _expert_kernel(be_ref, na_ref, x_ref, wgu_ref, bgu_ref, wd_ref, bd_ref, y_ref, wgu_bf, wd_bf):
    i = pl.program_id(0)
    active = i < na_ref[0]
    new_expert = jnp.logical_or(i == 0, be_ref[i] != be_ref[jnp.maximum(i - 1, 0)])

    @pl.when(jnp.logical_and(active, new_expert))
    def _():
        wgu_bf[...] = wgu_ref[0].astype(BF16)
        wd_bf[...] = wd_ref[0].astype(BF16)

    @pl.when(active)
    def _():
        x = _unpack_rows(x_ref[...]).astype(BF16)
        gu = jnp.dot(x, wgu_bf[...], preferred_element_type=F32) + bgu_ref[0]
        gate = jnp.minimum(gu[:, :D_FF], SWIGLU_LIMIT)
        up = jnp.clip(gu[:, D_FF:], -SWIGLU_LIMIT, SWIGLU_LIMIT)
        act = (up + 1.0) * (gate * jax.nn.sigmoid(SWIGLU_ALPHA * gate))
        y_ref[...] = _pack_rows(jnp.dot(act.astype(BF16), wd_bf[...], preferred_element_type=F32) + bd_ref[0])

    @pl.when(i >= na_ref[0])
    def _():
        y_ref[...] = jnp.zeros_like(y_ref)


def _experts(block_e, n_active, xs, P):
    n_rows = xs.shape[0]
    nb = n_rows // MOE_BLOCK
    grid_spec = pltpu.PrefetchScalarGridSpec(
        num_scalar_prefetch=2,
        grid=(nb,),
        in_specs=[
            pl.BlockSpec((MOE_BLOCK, PACKED), lambda i, be, na: (i, 0)),
            pl.BlockSpec((1, D_MODEL, 2 * D_FF), lambda i, be, na: (be[i], 0, 0)),
            pl.BlockSpec((1, 1, 2 * D_FF), lambda i, be, na: (be[i], 0, 0)),
            pl.BlockSpec((1, D_FF, D_MODEL), lambda i, be, na: (be[i], 0, 0)),
            pl.BlockSpec((1, 1, D_MODEL), lambda i, be, na: (be[i], 0, 0)),
        ],
        out_specs=pl.BlockSpec((MOE_BLOCK, PACKED), lambda i, be, na: (i, 0)),
        scratch_shapes=[pltpu.VMEM((D_MODEL, 2 * D_FF), BF16), pltpu.VMEM((D_FF, D_MODEL), BF16)],
    )
    return pl.pallas_call(
        _expert_kernel,
        grid_spec=grid_spec,
        out_shape=jax.ShapeDtypeStruct((n_rows, PACKED), jnp.uint32),
        compiler_params=_cparams(1),
        name="moe_experts",
    )(block_e, n_active, xs, P["w_gu"], P["b_gu"], P["w_down"], P["b_down"])


def _combine_kernel(x1_ref, yk_ref, gt_ref, o_ref):
    gt = gt_ref[...]
    acc = _unpack_rows(yk_ref[0]) * gt[:, 0:1]
    for k in range(1, TOP_K):
        acc = acc + _unpack_rows(yk_ref[k]) * gt[:, k:k + 1]
    o_ref[...] = x1_ref[...] + acc


def _combine(x1, yk, gates, tm):
    T = x1.shape[0]
    return pl.pallas_call(
        _combine_kernel,
        grid=(T // tm,),
        in_specs=[
            pl.BlockSpec((tm, D_MODEL), lambda i: (i, 0)),
            pl.BlockSpec((TOP_K, tm, PACKED), lambda i: (0, i, 0)),
            pl.BlockSpec((tm, LANE), lambda i: (i, 0)),
        ],
        out_specs=pl.BlockSpec((tm, D_MODEL), lambda i: (i, 0)),
        out_shape=jax.ShapeDtypeStruct((T, D_MODEL), F32),
        compiler_params=_cparams(1),
        name="moe_combine",
    )(x1, yk, gates)


def _route_kernel(ids_ref, cnt_ref, dest_ref, run_ref, start_ref, *, tm):
    @pl.when(pl.program_id(0) == 0)
    def _():
        cnt = cnt_ref[...]
        padded = jnp.floor((cnt + (MOE_BLOCK - 1)) * (1.0 / MOE_BLOCK)) * MOE_BLOCK
        scan = jnp.broadcast_to(padded, (8, LANE))
        lane8 = lax.broadcasted_iota(jnp.int32, (8, LANE), 1)
        s = 1
        while s < LANE:
            scan = scan + jnp.where(lane8 >= s, pltpu.roll(scan, s, 1), 0.0)
            s *= 2
        start_ref[...] = scan[0:1] - padded
        run_ref[...] = jnp.zeros_like(run_ref)

    lane = lax.broadcasted_iota(jnp.int32, (tm, LANE), 1)
    ids = ids_ref[...]
    hots = [lane == ids[:, k:k + 1] for k in range(TOP_K)]
    chosen = jnp.zeros((tm, LANE), F32)
    for hot in hots:
        chosen = jnp.where(hot, 1.0, chosen)
    r = lax.broadcasted_iota(jnp.int32, (tm, tm), 0)
    c = lax.broadcasted_iota(jnp.int32, (tm, tm), 1)
    earlier = jnp.where(c < r, 1.0, 0.0).astype(BF16)
    before = jnp.dot(earlier, chosen.astype(BF16), preferred_element_type=F32) + (run_ref[...] + start_ref[...])
    dest = jnp.zeros((tm, LANE), F32)
    for k, hot in enumerate(hots):
        dest = jnp.where(lane == k, jnp.sum(jnp.where(hot, before, 0.0), axis=-1, keepdims=True), dest)
    dest_ref[...] = dest.astype(jnp.int32)
    run_ref[...] += jnp.sum(chosen, axis=0, keepdims=True)


def _route(ids, counts, tm):
    T = ids.shape[0]
    dest = pl.pallas_call(
        functools.partial(_route_kernel, tm=tm),
        grid=(T // tm,),
        in_specs=[pl.BlockSpec((tm, LANE), lambda i: (i, 0)), _full((1, LANE))],
        out_specs=pl.BlockSpec((tm, LANE), lambda i: (i, 0)),
        out_shape=jax.ShapeDtypeStruct((T, LANE), jnp.int32),
        scratch_shapes=[pltpu.VMEM((1, LANE), F32), pltpu.VMEM((1, LANE), F32)],
        compiler_params=_cparams(1),
        name="moe_route",
    )(ids, counts)
    counts = counts[0, :N_EXPERTS].astype(jnp.int32)
    pad_end = jnp.cumsum((counts + MOE_BLOCK - 1) // MOE_BLOCK * MOE_BLOCK)
    n_blocks = -(-(T * TOP_K + N_EXPERTS * (MOE_BLOCK - 1)) // MOE_BLOCK)
    block_start = jnp.arange(n_blocks, dtype=jnp.int32) * MOE_BLOCK
    block_e = jnp.sum((block_start[:, None] >= pad_end[None, :]).astype(jnp.int32), axis=1)
    block_e = jnp.minimum(block_e, N_EXPERTS - 1)
    n_active = (pad_end[-1] // MOE_BLOCK).astype(jnp.int32).reshape(1)
    dest_flat = dest[:, :TOP_K].T.reshape(TOP_K * T)
    return dest_flat, block_e, n_active, n_blocks * MOE_BLOCK


def _sc_workers():
    info = plsc.get_sparse_core_info()
    return info.num_cores, info.num_cores * info.num_subcores


def _sc_scatter_rows(rows, dest_flat, n_rows):
    T, W = rows.shape
    n_cores, n_workers = _sc_workers()
    per_w = T // n_workers
    assert per_w * n_workers == T and per_w % SC_CHUNK == 0

    def body(rows_hbm, dest_hbm, out_hbm, buf, *idx):
        wid = lax.axis_index("s") * n_cores + lax.axis_index("c")

        @pl.loop(0, per_w // SC_CHUNK)
        def _(j):
            off = pl.multiple_of(wid * per_w + j * SC_CHUNK, SC_CHUNK)
            pltpu.sync_copy(rows_hbm.at[pl.ds(off, SC_CHUNK)], buf)
            for k in range(TOP_K):
                pltpu.sync_copy(dest_hbm.at[pl.ds(k * T + off, SC_CHUNK)], idx[k])
            for k in range(TOP_K):
                pltpu.sync_copy(buf, out_hbm.at[idx[k]])

    return pl.kernel(
        body,
        out_type=jax.ShapeDtypeStruct((n_rows, W), rows.dtype),
        mesh=plsc.VectorSubcoreMesh(core_axis_name="c", subcore_axis_name="s"),
        scratch_types=[pltpu.VMEM((SC_CHUNK, W), rows.dtype)] + [pltpu.VMEM((SC_CHUNK,), jnp.int32)] * TOP_K,
        name="moe_dispatch_sc",
    )(rows, dest_flat)


def _sc_gather_rows(table, dest_flat):
    N = dest_flat.shape[0]
    W = table.shape[1]
    n_cores, n_workers = _sc_workers()
    per_w = N // n_workers
    assert per_w * n_workers == N and per_w % SC_CHUNK == 0

    def body(table_hbm, dest_hbm, out_hbm, buf, idx):
        wid = lax.axis_index("s") * n_cores + lax.axis_index("c")

        @pl.loop(0, per_w // SC_CHUNK)
        def _(j):
            off = pl.multiple_of(wid * per_w + j * SC_CHUNK, SC_CHUNK)
            pltpu.sync_copy(dest_hbm.at[pl.ds(off, SC_CHUNK)], idx)
            pltpu.sync_copy(table_hbm.at[idx], buf)
            pltpu.sync_copy(buf, out_hbm.at[pl.ds(off, SC_CHUNK)])

    return pl.kernel(
        body,
        out_type=jax.ShapeDtypeStruct((N, W), table.dtype),
        mesh=plsc.VectorSubcoreMesh(core_axis_name="c", subcore_axis_name="s"),
        scratch_types=[pltpu.VMEM((SC_CHUNK, W), table.dtype), pltpu.VMEM((SC_CHUNK,), jnp.int32)],
        name="moe_gather_sc",
    )(table, dest_flat)


def _moe(groups, P, tm):
    staged = []
    for x1, h2p, ids, gates, counts in groups:
        dest_flat, block_e, n_active, n_rows = _route(ids, counts, tm)
        staged.append((dest_flat, block_e, n_active, _sc_scatter_rows(h2p, dest_flat, n_rows)))
    ys = []
    for dest_flat, block_e, n_active, xs in staged:
        if ys:
            xs, ys[-1] = lax.optimization_barrier((xs, ys[-1]))
        ys.append(_experts(block_e, n_active, xs, P))
    outs = []
    for (x1, _, _, gates, _), (dest_flat, _, _, _), y in zip(groups, staged, ys):
        yk = _sc_gather_rows(y, dest_flat).reshape(TOP_K, x1.shape[0], PACKED)
        outs.append(_combine(x1, yk, gates, tm))
    return outs


def _prep_params(norm_mix, w_in, q_a_norm, w_uq, kv_a_norm, w_ukv, q_norm, k_norm, w_hy_o, w_mla_o, w_out,
                 norm_ffn, router_w, router_b, w_gu, b_gu, w_down, b_down, L):
    P = {}
    P["norm_mix"] = norm_mix.astype(F32).reshape(1, D_MODEL)
    P["w_hy_t"] = w_in[:, :HY_END].T.astype(BF16)
    P["w_m"] = jnp.pad(w_in[:, HY_END:KR_END], ((0, 0), (0, MLA_IN - (KR_END - HY_END)))).astype(BF16)
    P["w_g"] = w_in[:, KR_END:].astype(BF16)

    half = QK_ROPE // 2
    swap = jnp.concatenate([jnp.arange(QK_NOPE), QK_NOPE + half + jnp.arange(half), QK_NOPE + jnp.arange(half)])
    rope_lane = (jnp.arange(QK_HEAD) >= QK_NOPE)

    def slots(w):
        return jnp.pad(w, ((0, 0), (0, 0), (0, HEAD_PAD - QK_HEAD))).reshape(w.shape[0], MLA_HEADS * HEAD_PAD)

    wq = w_uq.astype(F32).reshape(Q_LORA, MLA_HEADS, QK_HEAD)
    P["wqa"] = slots(wq).astype(BF16)
    P["wqb"] = slots(jnp.where(rope_lane, wq[:, :, swap], 0.0)).astype(BF16)
    wkv = w_ukv.astype(F32).reshape(KV_LORA, MLA_HEADS, QK_NOPE + V_HEAD)
    P["wka"] = slots(jnp.pad(wkv[:, :, :QK_NOPE], ((0, 0), (0, 0), (0, QK_ROPE)))).astype(BF16)
    P["wv"] = wkv[:, :, QK_NOPE:].reshape(KV_LORA, MLA_HEADS * V_HEAD).astype(BF16)
    eye = jnp.eye(LANE, QK_ROPE, dtype=F32)
    e_a = jnp.pad(eye, ((0, 0), (QK_NOPE, 0)))
    e_b = e_a[:, swap] * rope_lane
    P["ea"] = slots(jnp.broadcast_to(e_a[:, None, :], (LANE, MLA_HEADS, QK_HEAD))).astype(BF16)
    P["eb"] = slots(jnp.broadcast_to(e_b[:, None, :], (LANE, MLA_HEADS, QK_HEAD))).astype(BF16)

    sign = jnp.concatenate([jnp.zeros(QK_NOPE), -jnp.ones(half), jnp.ones(half)]).astype(F32)

    def lane_pad(g):
        return jnp.pad(g, (0, HEAD_PAD - QK_HEAD)).reshape(1, HEAD_PAD)

    for name, g in (("gq", q_norm.astype(F32)), ("gk", k_norm.astype(F32))):
        P[name] = lane_pad(g)
        P[name + "s"] = lane_pad(g[swap] * sign)
    P["gqa"] = q_a_norm.astype(F32).reshape(1, Q_LORA)
    P["gkva"] = kv_a_norm.astype(F32).reshape(1, KV_LORA)
    inv = 1.0 / (ROPE_THETA ** (jnp.arange(0, QK_ROPE, 2, dtype=F32) / QK_ROPE))
    ang = jnp.arange(L, dtype=F32)[:, None] * inv[None, :]
    cos, sin = jnp.cos(ang), jnp.sin(ang)
    P["ctab"] = jnp.concatenate([jnp.ones((L, QK_NOPE), F32), cos, cos, jnp.zeros((L, HEAD_PAD - QK_HEAD), F32)], axis=1)
    P["stab"] = jnp.concatenate([jnp.zeros((L, QK_NOPE), F32), sin, sin, jnp.zeros((L, HEAD_PAD - QK_HEAD), F32)], axis=1)

    P["w_hy_o"] = w_hy_o.astype(BF16)
    P["w_mla_o"] = w_mla_o.astype(BF16)
    P["w_out"] = w_out.astype(BF16)
    P["norm_ffn"] = norm_ffn.astype(F32).reshape(1, D_MODEL)
    rw = jnp.pad(router_w.astype(F32), ((0, 0), (0, LANE - N_EXPERTS)))
    P["router_w_hi"] = rw.astype(BF16)
    P["router_w_lo"] = (rw - P["router_w_hi"].astype(F32)).astype(BF16)
    P["router_b"] = jnp.pad(router_b.astype(F32), (0, LANE - N_EXPERTS)).reshape(1, LANE)
    P["w_gu"] = w_gu.astype(F32)
    P["b_gu"] = b_gu.astype(F32).reshape(N_EXPERTS, 1, 2 * D_FF)
    P["w_down"] = w_down.astype(F32)
    P["b_down"] = b_down.astype(F32).reshape(N_EXPERTS, 1, D_MODEL)
    return P


def _tile(n, want):
    t = min(n, want)
    assert n % t == 0
    return t


def _layer(xs, norm_mix, w_in, hy_conv_w, hy_conv_b, hy_fw1, hy_fb1, hy_fw2, hy_fb2, hy_fw3, hy_freq, hy_decay, hy_skip,
           q_a_norm, w_uq, kv_a_norm, w_ukv, q_norm, k_norm, w_hy_o, w_mla_o, w_out,
           norm_ffn, router_w, router_b, w_gu, b_gu, w_down, b_down):
    L = xs[0].shape[1]
    assert all(x.shape[1] == L and x.shape[2] == D_MODEL for x in xs) and L % MXU == 0
    P = _prep_params(norm_mix, w_in, q_a_norm, w_uq, kv_a_norm, w_ukv, q_norm, k_norm, w_hy_o, w_mla_o, w_out,
                     norm_ffn, router_w, router_b, w_gu, b_gu, w_down, b_down, L)
    kf = _hyena_filters(L, hy_fw1, hy_fb1, hy_fw2, hy_fb2, hy_fw3, hy_freq, hy_decay)
    tm = _tile(L, 512)
    proj = [_inproj(x, P["norm_mix"], P["w_hy_t"], P["w_m"], P["w_g"], tm) for x in xs]
    y_ts = _hyena([u_t for u_t, _, _ in proj], kf, hy_conv_w, hy_conv_b, hy_skip, L)
    mixed = []
    for x, (_, m, g), y_t in zip(xs, proj, y_ts):
        q, k, v = _mla_prep(m, P, tm)
        y_mla = _attention(q, k, v, _tile(L, 512))
        mixed.append(_merge(x, y_t, y_mla, g, P, tm))
    order = sorted(range(len(xs)), key=lambda n: -xs[n].shape[0])
    moe_out = _moe([mixed[n] for n in order], P, tm)
    outs = [None] * len(xs)
    for n, y in zip(order, moe_out):
        outs[n] = y.reshape(xs[n].shape)
    return outs


def kernel(x_prompt, x_sample, norm_mix, w_in, hy_conv_w, hy_conv_b, hy_fw1, hy_fb1, hy_fw2, hy_fb2, hy_fw3, hy_freq, hy_decay, hy_skip, q_a_norm, w_uq, kv_a_norm, w_ukv, q_norm, k_norm, w_hy_o, w_mla_o, w_out, norm_ffn, router_w, router_b, w_gu, b_gu, w_down, b_down):
    depth = norm_mix.shape[0]
    xs = [x_prompt, x_sample]
    weights = (norm_mix, w_in, hy_conv_w, hy_conv_b, hy_fw1, hy_fb1, hy_fw2, hy_fb2, hy_fw3, hy_freq, hy_decay, hy_skip,
               q_a_norm, w_uq, kv_a_norm, w_ukv, q_norm, k_norm, w_hy_o, w_mla_o, w_out,
               norm_ffn, router_w, router_b, w_gu, b_gu, w_down, b_down)
    for l in range(depth):
        xs = _layer(xs, *[w[l] for w in weights])
    return (xs[0], xs[1])
```

```python
import functools
import math

import jax
import jax.numpy as jnp
from jax import lax
from jax.experimental import pallas as pl
from jax.experimental.pallas import tpu as pltpu
from jax.experimental.pallas import tpu_sc as plsc

F32 = jnp.float32
BF16 = jnp.bfloat16

D_MODEL = 1024
HY_WIDTH = 512
HY_ORDER = 2
HY_EMB_BANDS = 16
HY_FILTER_DIM = 64
MLA_HEADS = 8
V_HEAD = 64
QK_NOPE = 64
QK_ROPE = 32
QK_HEAD = QK_NOPE + QK_ROPE
Q_LORA = 256
KV_LORA = 128
ROPE_THETA = 10000.0
HY_END = 3 * HY_WIDTH
Q_END = HY_END + Q_LORA
KV_END = Q_END + KV_LORA
KR_END = KV_END + QK_ROPE
G1_END = KR_END + D_MODEL
N_EXPERTS = 32
TOP_K = 4
D_FF = D_MODEL
SWIGLU_LIMIT = 7.0
SWIGLU_ALPHA = 1.702
EPS = 1e-6

LANE = 128
MXU = 256
HEAD_PAD = LANE
MLA_IN = 512
CONV_CB = 8
CONV_UNROLL = 4
MOE_BLOCK = 512
PACKED = D_MODEL // 2
SC_CHUNK = 64
VMEM_LIMIT = 52 * 1024 * 1024


def _cparams(n_axes):
    return pltpu.CompilerParams(dimension_semantics=("arbitrary",) * n_axes, vmem_limit_bytes=VMEM_LIMIT)


def _full(shape):
    n = len(shape)
    return pl.BlockSpec(shape, lambda *_: (0,) * n)


def _inproj_kernel(x_ref, g_ref, why_ref, wm_ref, wg_ref, u_ref, m_ref, gate_ref):
    x = x_ref[...]
    r = lax.rsqrt(jnp.mean(x * x, axis=-1, keepdims=True) + EPS)
    h = (x * r * g_ref[...]).astype(BF16)
    u_ref[...] = lax.dot_general(why_ref[...], h, (((1,), (1,)), ((), ())), preferred_element_type=F32)
    m_ref[...] = jnp.dot(h, wm_ref[...], preferred_element_type=F32).astype(BF16)
    gate_ref[...] = jnp.dot(h, wg_ref[...], preferred_element_type=F32).astype(BF16)


def _inproj(x, g, w_hy_t, w_m, w_g, tm):
    B, L, _ = x.shape
    return pl.pallas_call(
        _inproj_kernel,
        grid=(B, L // tm),
        in_specs=[
            pl.BlockSpec((None, tm, D_MODEL), lambda b, i: (b, i, 0)),
            _full((1, D_MODEL)),
            _full(w_hy_t.shape),
            _full(w_m.shape),
            _full(w_g.shape),
        ],
        out_specs=[
            pl.BlockSpec((None, HY_END, tm), lambda b, i: (b, 0, i)),
            pl.BlockSpec((None, tm, MLA_IN), lambda b, i: (b, i, 0)),
            pl.BlockSpec((None, tm, 2 * D_MODEL), lambda b, i: (b, i, 0)),
        ],
        out_shape=[
            jax.ShapeDtypeStruct((B, HY_END, L), F32),
            jax.ShapeDtypeStruct((B, L, MLA_IN), BF16),
            jax.ShapeDtypeStruct((B, L, 2 * D_MODEL), BF16),
        ],
        compiler_params=_cparams(2),
        name="inproj",
    )(x, g, w_hy_t, w_m, w_g)


def _filter_kernel(bands_ref, w1t_ref, w1c_ref, w1s_ref, b1_ref, w2_ref, b2_ref, fr_ref, w3f_ref, w3b_ref, dec_ref,
                   out_ref, *, L):
    hi = lax.Precision.HIGHEST
    n = 2 * L
    j = lax.broadcasted_iota(jnp.int32, (1, n), 1)
    pos = jnp.abs(j - L).astype(F32)
    t01 = pos / max(L - 1, 1)
    ang = ((2.0 * math.pi / L) * pos) * bands_ref[...]
    pre = (w1t_ref[...] * t01
           + jnp.dot(w1c_ref[...], jnp.cos(ang), precision=hi, preferred_element_type=F32)
           - jnp.dot(w1s_ref[...], jnp.sin(ang), precision=hi, preferred_element_type=F32)
           + b1_ref[...])
    z = jnp.sin(fr_ref[0] * pre)
    z = jnp.sin(fr_ref[1] * (jnp.dot(w2_ref[...], z, precision=hi, preferred_element_type=F32) + b2_ref[...]))
    win = jnp.exp(-t01 * jnp.abs(dec_ref[...]))
    h_b = jnp.dot(w3b_ref[...], z[:, :L], precision=hi, preferred_element_type=F32) * win[:, :L]
    h_f = jnp.dot(w3f_ref[...], z[:, L:], precision=hi, preferred_element_type=F32) * win[:, L:]
    out_ref[:, :L] = jnp.where(j[:, :L] == 0, 0.0, h_b)
    out_ref[:, L:] = h_f


def _hyena_filters(L, fw1, fb1, fw2, fb2, fw3, freq, decay):
    rows = HY_ORDER * HY_WIDTH
    rb = 256
    bands = jnp.linspace(1e-4, HY_EMB_BANDS - 1, HY_EMB_BANDS, dtype=F32).reshape(HY_EMB_BANDS, 1)
    w1 = fw1.astype(F32).T
    w3 = fw3.astype(F32).T
    fd = HY_FILTER_DIM
    return pl.pallas_call(
        functools.partial(_filter_kernel, L=L),
        grid=(rows // rb,),
        in_specs=[
            _full((HY_EMB_BANDS, 1)),
            _full((fd, 1)),
            _full((fd, HY_EMB_BANDS)),
            _full((fd, HY_EMB_BANDS)),
            _full((fd, 1)),
            _full((fd, fd)),
            _full((fd, 1)),
            _full((2, fd, 1)),
            pl.BlockSpec((rb, fd), lambda i: (i, 0)),
            pl.BlockSpec((rb, fd), lambda i: (i, 0)),
            pl.BlockSpec((rb, 1), lambda i: (i, 0)),
        ],
        out_specs=pl.BlockSpec((rb, 2 * L), lambda i: (i, 0)),
        out_shape=jax.ShapeDtypeStruct((rows, 2 * L), F32),
        compiler_params=_cparams(1),
        name="hyena_filters",
    )(bands, w1[:, :1], w1[:, 1:1 + HY_EMB_BANDS], w1[:, 1 + HY_EMB_BANDS:], fb1.astype(F32).reshape(fd, 1),
      fw2.astype(F32).T, fb2.astype(F32).reshape(fd, 1), freq.astype(F32).reshape(2, fd, 1),
      w3[:rows], w3[rows:], decay.astype(F32).reshape(rows, 1))


def _shift_conv(u, w0, w1, w2, b):
    L = u.shape[1]
    lane = lax.broadcasted_iota(jnp.int32, (1, L), 1)
    w0_row = jnp.where(lane == 0, 0.0, w0)
    w2_row = jnp.where(lane == L - 1, 0.0, w2)
    return pltpu.roll(u, 1, 1) * w0_row + u * w1 + pltpu.roll(u, L - 1, 1) * w2_row + b


def _toeplitz_conv(z, tz_ref, L):
    B = z.shape[0]
    nj = L // MXU
    zb = z.astype(BF16)
    tiles = [zb[:, MXU * J:MXU * (J + 1)] for J in range(nj)]
    acc = [None] * nj
    for d in range(-(nj - 1), nj):
        js = [J for J in range(nj) if 0 <= J + d < nj]
        lhs = tiles[js[0]] if len(js) == 1 else jnp.concatenate([tiles[J] for J in js], axis=0)
        res = jnp.dot(lhs, tz_ref[:, L + MXU * d:L + MXU * (d + 1)], preferred_element_type=F32)
        for n, J in enumerate(js):
            part = res[n * B:(n + 1) * B]
            acc[J + d] = part if acc[J + d] is None else acc[J + d] + part
    return acc[0] if nj == 1 else jnp.concatenate(acc, axis=1)


def _hyena_kernel(cw_ref, cb_ref, skip_ref, kf_ref, *refs, L, n_groups):
    in_refs = refs[:3 * n_groups]
    out_refs = refs[3 * n_groups:4 * n_groups]
    tz_ref = refs[4 * n_groups]
    c0 = pl.program_id(0) * CONV_CB

    def channels(p, carry):
        for u in range(CONV_UNROLL):
            c = p * CONV_UNROLL + u
            ch = c0 + c
            for o in range(HY_ORDER):
                row = jnp.broadcast_to(kf_ref[o, pl.ds(c, 1), :], (MXU, 2 * L))
                tz_ref[u, o] = pltpu.roll(row, 0, 1, stride=1, stride_axis=0).astype(BF16)
            for g in range(n_groups):
                stage = refs[4 * n_groups + 1 + g]
                sec = []
                for s, ref in enumerate(in_refs[3 * g:3 * g + 3]):
                    stage[u, s] = ref[:, c, :]
                    k = s * HY_WIDTH + ch
                    sec.append(_shift_conv(stage[u, s], cw_ref[0, k], cw_ref[1, k], cw_ref[2, k], cb_ref[k]))
                x1, x2, z = sec
                for o, gate in enumerate((x1, x2)):
                    y = _toeplitz_conv(z, tz_ref.at[u, o], L) + z * skip_ref[o, ch]
                    z = gate * y
                out_refs[g][:, c, :] = z
        return carry

    lax.fori_loop(0, CONV_CB // CONV_UNROLL, channels, 0)


def _hyena(u_ts, kf, conv_w, conv_b, skip, L):
    n_groups = len(u_ts)
    nblk = HY_WIDTH // CONV_CB
    smem = pl.BlockSpec(memory_space=pltpu.SMEM)
    in_specs = [smem, smem, smem, pl.BlockSpec((HY_ORDER, CONV_CB, 2 * L), lambda i: (0, i, 0))]
    args = [conv_w.astype(F32), conv_b.astype(F32), skip.astype(F32), kf.reshape(HY_ORDER, HY_WIDTH, 2 * L)]
    out_specs, out_shape = [], []
    for u in u_ts:
        B = u.shape[0]
        for s in range(3):
            in_specs.append(pl.BlockSpec((B, CONV_CB, L), lambda i, s=s: (0, i + s * nblk, 0)))
            args.append(u)
        out_specs.append(pl.BlockSpec((B, CONV_CB, L), lambda i: (0, i, 0)))
        out_shape.append(jax.ShapeDtypeStruct((B, HY_WIDTH, L), F32))
    return pl.pallas_call(
        functools.partial(_hyena_kernel, L=L, n_groups=n_groups),
        grid=(nblk,),
        in_specs=in_specs,
        out_specs=out_specs,
        out_shape=out_shape,
        scratch_shapes=[pltpu.VMEM((CONV_UNROLL, HY_ORDER, MXU, 2 * L), BF16)]
        + [pltpu.VMEM((CONV_UNROLL, 3, u.shape[0], L), F32) for u in u_ts],
        compiler_params=_cparams(1),
        name="hyena_conv",
    )(*args)


def _rms(x, n):
    return lax.rsqrt(jnp.sum(x * x, axis=-1, keepdims=True) * (1.0 / n) + EPS)


def _mla_prep_kernel(m_ref, ct_ref, st_ref, gqa_ref, gkva_ref, gq_ref, gqs_ref, gk_ref, gks_ref,
                     wqa_ref, wqb_ref, wka_ref, ea_ref, eb_ref, wv_ref, q_ref, k_ref, v_ref):
    m = m_ref[0].astype(F32)
    cq = m[:, :Q_LORA]
    ckv = m[:, Q_LORA:Q_LORA + KV_LORA]
    kr = m_ref[0][:, Q_LORA + KV_LORA:Q_LORA + KV_LORA + LANE]
    cqn = (cq * _rms(cq, Q_LORA) * gqa_ref[...]).astype(BF16)
    ckvn = (ckv * _rms(ckv, KV_LORA) * gkva_ref[...]).astype(BF16)
    qa = jnp.dot(cqn, wqa_ref[...], preferred_element_type=F32)
    qb = jnp.dot(cqn, wqb_ref[...], preferred_element_type=F32)
    ka = jnp.dot(ckvn, wka_ref[...], preferred_element_type=F32) + jnp.dot(kr, ea_ref[...], preferred_element_type=F32)
    kb = jnp.dot(kr, eb_ref[...], preferred_element_type=F32)
    v_ref[0] = jnp.dot(ckvn, wv_ref[...], preferred_element_type=F32).astype(BF16)
    ct, st = ct_ref[...], st_ref[...]
    scale = QK_HEAD ** -0.5 * math.log2(math.e)
    qc, qs = gq_ref[...] * ct * scale, gqs_ref[...] * st * scale
    kc, ks = gk_ref[...] * ct, gks_ref[...] * st
    for h in range(MLA_HEADS):
        sl = slice(h * HEAD_PAD, (h + 1) * HEAD_PAD)
        q_ref[0, h] = (_rms(qa[:, sl], QK_HEAD) * (qa[:, sl] * qc + qb[:, sl] * qs)).astype(BF16)
        k_ref[0, h] = (_rms(ka[:, sl], QK_HEAD) * (ka[:, sl] * kc + kb[:, sl] * ks)).astype(BF16)


def _mla_prep(m, P, tm):
    B, L, _ = m.shape
    w = [P[n] for n in ("wqa", "wqb", "wka", "ea", "eb", "wv")]
    g = [P[n] for n in ("gqa", "gkva", "gq", "gqs", "gk", "gks")]
    hp = MLA_HEADS * HEAD_PAD
    return pl.pallas_call(
        _mla_prep_kernel,
        grid=(B, L // tm),
        in_specs=[pl.BlockSpec((1, tm, MLA_IN), lambda b, i: (b, i, 0)),
                  pl.BlockSpec((tm, HEAD_PAD), lambda b, i: (i, 0)),
                  pl.BlockSpec((tm, HEAD_PAD), lambda b, i: (i, 0))]
        + [_full(a.shape) for a in g] + [_full(a.shape) for a in w],
        out_specs=[
            pl.BlockSpec((1, MLA_HEADS, tm, HEAD_PAD), lambda b, i: (b, 0, i, 0)),
            pl.BlockSpec((1, MLA_HEADS, tm, HEAD_PAD), lambda b, i: (b, 0, i, 0)),
            pl.BlockSpec((1, tm, MLA_HEADS * V_HEAD), lambda b, i: (b, i, 0)),
        ],
        out_shape=[
            jax.ShapeDtypeStruct((B, MLA_HEADS, L, HEAD_PAD), BF16),
            jax.ShapeDtypeStruct((B, MLA_HEADS, L, HEAD_PAD), BF16),
            jax.ShapeDtypeStruct((B, L, MLA_HEADS * V_HEAD), BF16),
        ],
        compiler_params=_cparams(2),
        name="mla_prep",
    )(m, P["ctab"][:L], P["stab"][:L], *g, *w)


def _attn_kernel(q_ref, k_ref, v_ref, o_ref):
    L = v_ref.shape[1]
    one_col = jnp.where(lax.broadcasted_iota(jnp.int32, (L, LANE), 1) == 0, 1.0, 0.0).astype(BF16)
    for pair in range(MLA_HEADS // 2):
        lanes = pl.ds(pair * LANE, LANE)
        v_ext = jnp.concatenate([v_ref[0, :, lanes], one_col], axis=1)
        outs = []
        for h in (2 * pair, 2 * pair + 1):
            s = lax.dot_general(q_ref[0, h], k_ref[0, h], (((1,), (1,)), ((), ())), preferred_element_type=F32)
            p = jnp.exp2(s - jnp.max(s, axis=-1, keepdims=True)).astype(BF16)
            pv = jnp.dot(p, v_ext, preferred_element_type=F32)
            outs.append(pv[:, :LANE] / pv[:, LANE:LANE + 1])
        lane = lax.broadcasted_iota(jnp.int32, outs[0].shape, 1)
        o_ref[0, :, lanes] = jnp.where(lane < V_HEAD, outs[0], outs[1]).astype(BF16)


def _attention(q, k, v, tq):
    B, H, L, _ = q.shape
    return pl.pallas_call(
        _attn_kernel,
        grid=(B, L // tq),
        in_specs=[
            pl.BlockSpec((1, H, tq, HEAD_PAD), lambda b, i: (b, 0, i, 0)),
            pl.BlockSpec((1, H, L, HEAD_PAD), lambda b, i: (b, 0, 0, 0)),
            pl.BlockSpec((1, L, H * V_HEAD), lambda b, i: (b, 0, 0)),
        ],
        out_specs=pl.BlockSpec((1, tq, H * V_HEAD), lambda b, i: (b, i, 0)),
        out_shape=jax.ShapeDtypeStruct((B, L, H * V_HEAD), BF16),
        compiler_params=_cparams(2),
        name="mla_attention",
    )(q, k, v)


def _pack_rows(a):
    bits = lax.bitcast_convert_type(a.astype(BF16).astype(F32), jnp.uint32)
    return (bits[:, :PACKED] >> 16) | (bits[:, PACKED:] & jnp.uint32(0xFFFF0000))


def _unpack_rows(p):
    lo = lax.bitcast_convert_type(p << 16, F32)
    hi = lax.bitcast_convert_type(p & jnp.uint32(0xFFFF0000), F32)
    return jnp.concatenate([lo, hi], axis=1)


def _merge_kernel(x_ref, yh_ref, ym_ref, g_ref, who_ref, wmo_ref, wo_ref, nf_ref, rwh_ref, rwl_ref, rb_ref,
                  x1_ref, h2_ref, ids_ref, gate_ref, cnt_ref):
    tm = x_ref.shape[0]
    sub = tm // 2

    @pl.when(jnp.logical_and(pl.program_id(0) == 0, pl.program_id(1) == 0))
    def _():
        cnt_ref[...] = jnp.zeros_like(cnt_ref)

    lane = lax.broadcasted_iota(jnp.int32, (sub, LANE), 1)
    for r in range(0, tm, sub):
        rows = pl.ds(r, sub)
        g = g_ref[rows, :].astype(F32)
        a = lax.dot_general(yh_ref[:, rows].astype(BF16), who_ref[...], (((0,), (0,)), ((), ())),
                            preferred_element_type=F32)
        b = jnp.dot(ym_ref[rows, :], wmo_ref[...], preferred_element_type=F32)
        merged = jax.nn.sigmoid(g[:, :D_MODEL]) * a + jax.nn.sigmoid(g[:, D_MODEL:]) * b
        x1 = x_ref[rows, :] + jnp.dot(merged.astype(BF16), wo_ref[...], preferred_element_type=F32)
        x1_ref[rows, :] = x1
        h2 = x1 * lax.rsqrt(jnp.mean(x1 * x1, axis=-1, keepdims=True) + EPS) * nf_ref[...]
        h2_ref[rows, :] = _pack_rows(h2)
        h_hi = h2.astype(BF16)
        h_lo = (h2 - h_hi.astype(F32)).astype(BF16)
        logits = (jnp.dot(h_hi, rwh_ref[...], preferred_element_type=F32)
                  + jnp.dot(h_lo, rwh_ref[...], preferred_element_type=F32)
                  + jnp.dot(h_hi, rwl_ref[...], preferred_element_type=F32) + rb_ref[...])
        work = jnp.where(lane < N_EXPERTS, logits, -jnp.inf)
        ids = jnp.zeros((sub, LANE), jnp.int32)
        picked = jnp.zeros((sub, LANE), F32)
        vals = []
        for k in range(TOP_K):
            m = jnp.max(work, axis=-1, keepdims=True)
            first = jnp.min(jnp.where(work == m, lane, LANE), axis=-1, keepdims=True)
            hot = lane == first
            vals.append(m)
            ids = jnp.where(lane == k, first, ids)
            picked = jnp.where(hot, 1.0, picked)
            work = jnp.where(hot, -jnp.inf, work)
        ex = [jnp.exp(v - vals[0]) for v in vals]
        inv = 1.0 / (ex[0] + ex[1] + ex[2] + ex[3])
        gates = jnp.zeros((sub, LANE), F32)
        for k in range(TOP_K):
            gates = jnp.where(lane == k, ex[k] * inv, gates)
        ids_ref[rows, :] = ids
        gate_ref[rows, :] = gates
        cnt_ref[...] += jnp.sum(picked, axis=0, keepdims=True)


def _merge(x, yh_t, ym, g, P, tm):
    B, L, _ = x.shape
    nt = L // tm
    w = [P[n] for n in ("w_hy_o", "w_mla_o", "w_out", "norm_ffn", "router_w_hi", "router_w_lo", "router_b")]
    return pl.pallas_call(
        _merge_kernel,
        grid=(B, nt),
        in_specs=[
            pl.BlockSpec((None, tm, D_MODEL), lambda b, i: (b, i, 0)),
            pl.BlockSpec((None, HY_WIDTH, tm), lambda b, i: (b, 0, i)),
            pl.BlockSpec((None, tm, MLA_HEADS * V_HEAD), lambda b, i: (b, i, 0)),
            pl.BlockSpec((None, tm, 2 * D_MODEL), lambda b, i: (b, i, 0)),
        ] + [_full(a.shape) for a in w],
        out_specs=[
            pl.BlockSpec((tm, D_MODEL), lambda b, i: (b * nt + i, 0)),
            pl.BlockSpec((tm, PACKED), lambda b, i: (b * nt + i, 0)),
            pl.BlockSpec((tm, LANE), lambda b, i: (b * nt + i, 0)),
            pl.BlockSpec((tm, LANE), lambda b, i: (b * nt + i, 0)),
            pl.BlockSpec((1, LANE), lambda b, i: (0, 0)),
        ],
        out_shape=[
            jax.ShapeDtypeStruct((B * L, D_MODEL), F32),
            jax.ShapeDtypeStruct((B * L, PACKED), jnp.uint32),
            jax.ShapeDtypeStruct((B * L, LANE), jnp.int32),
            jax.ShapeDtypeStruct((B * L, LANE), F32),
            jax.ShapeDtypeStruct((1, LANE), F32),
        ],
        compiler_params=_cparams(2),
        name="merge_router",
    )(x, yh_t, ym, g, *w)


def _expert_kernel(be_ref, na_ref, x_ref, wgu_ref, bgu_ref, wd_ref, bd_ref, y_ref):
    i = pl.program_id(0)

    @pl.when(i < na_ref[0])
    def _():
        x = _unpack_rows(x_ref[...])
        gu = jnp.dot(x, wgu_ref[0], preferred_element_type=F32) + bgu_ref[0]
        gate = jnp.minimum(gu[:, :D_FF], SWIGLU_LIMIT)
        up = jnp.clip(gu[:, D_FF:], -SWIGLU_LIMIT, SWIGLU_LIMIT)
        act = (up + 1.0) * (gate * jax.nn.sigmoid(SWIGLU_ALPHA * gate))
        y_ref[...] = _pack_rows(jnp.dot(act, wd_ref[0], preferred_element_type=F32) + bd_ref[0])

    @pl.when(i >= na_ref[0])
    def _():
        y_ref[...] = jnp.zeros_like(y_ref)


def _experts(block_e, n_active, xs, P):
    n_rows = xs.shape[0]
    nb = n_rows // MOE_BLOCK
    grid_spec = pltpu.PrefetchScalarGridSpec(
        num_scalar_prefetch=2,
        grid=(nb,),
        in_specs=[
            pl.BlockSpec((MOE_BLOCK, PACKED), lambda i, be, na: (i, 0)),
            pl.BlockSpec((1, D_MODEL, 2 * D_FF), lambda i, be, na: (be[i], 0, 0)),
            pl.BlockSpec((1, 1, 2 * D_FF), lambda i, be, na: (be[i], 0, 0)),
            pl.BlockSpec((1, D_FF, D_MODEL), lambda i, be, na: (be[i], 0, 0)),
            pl.BlockSpec((1, 1, D_MODEL), lambda i, be, na: (be[i], 0, 0)),
        ],
        out_specs=pl.BlockSpec((MOE_BLOCK, PACKED), lambda i, be, na: (i, 0)),
    )
    return pl.pallas_call(
        _expert_kernel,
        grid_spec=grid_spec,
        out_shape=jax.ShapeDtypeStruct((n_rows, PACKED), jnp.uint32),
        compiler_params=_cparams(1),
        name="moe_experts",
    )(block_e, n_active, xs, P["w_gu"], P["b_gu"], P["w_down"], P["b_down"])


def _combine_kernel(x1_ref, yk_ref, gt_ref, o_ref):
    gt = gt_ref[...]
    acc = _unpack_rows(yk_ref[0]) * gt[:, 0:1]
    for k in range(1, TOP_K):
        acc = acc + _unpack_rows(yk_ref[k]) * gt[:, k:k + 1]
    o_ref[...] = x1_ref[...] + acc


def _combine(x1, yk, gates, tm):
    T = x1.shape[0]
    return pl.pallas_call(
        _combine_kernel,
        grid=(T // tm,),
        in_specs=[
            pl.BlockSpec((tm, D_MODEL), lambda i: (i, 0)),
            pl.BlockSpec((TOP_K, tm, PACKED), lambda i: (0, i, 0)),
            pl.BlockSpec((tm, LANE), lambda i: (i, 0)),
        ],
        out_specs=pl.BlockSpec((tm, D_MODEL), lambda i: (i, 0)),
        out_shape=jax.ShapeDtypeStruct((T, D_MODEL), F32),
        compiler_params=_cparams(1),
        name="moe_combine",
    )(x1, yk, gates)


def _route_kernel(ids_ref, cnt_ref, dest_ref, run_ref, start_ref, *, tm):
    @pl.when(pl.program_id(0) == 0)
    def _():
        cnt = cnt_ref[...]
        padded = jnp.floor((cnt + (MOE_BLOCK - 1)) * (1.0 / MOE_BLOCK)) * MOE_BLOCK
        scan = jnp.broadcast_to(padded, (8, LANE))
        lane8 = lax.broadcasted_iota(jnp.int32, (8, LANE), 1)
        s = 1
        while s < LANE:
            scan = scan + jnp.where(lane8 >= s, pltpu.roll(scan, s, 1), 0.0)
            s *= 2
        start_ref[...] = scan[0:1] - padded
        run_ref[...] = jnp.zeros_like(run_ref)

    lane = lax.broadcasted_iota(jnp.int32, (tm, LANE), 1)
    ids = ids_ref[...]
    hots = [lane == ids[:, k:k + 1] for k in range(TOP_K)]
    chosen = jnp.zeros((tm, LANE), F32)
    for hot in hots:
        chosen = jnp.where(hot, 1.0, chosen)
    r = lax.broadcasted_iota(jnp.int32, (tm, tm), 0)
    c = lax.broadcasted_iota(jnp.int32, (tm, tm), 1)
    earlier = jnp.where(c < r, 1.0, 0.0).astype(BF16)
    before = jnp.dot(earlier, chosen.astype(BF16), preferred_element_type=F32) + (run_ref[...] + start_ref[...])
    dest = jnp.zeros((tm, LANE), F32)
    for k, hot in enumerate(hots):
        dest = jnp.where(lane == k, jnp.sum(jnp.where(hot, before, 0.0), axis=-1, keepdims=True), dest)
    dest_ref[...] = dest.astype(jnp.int32)
    run_ref[...] += jnp.sum(chosen, axis=0, keepdims=True)


def _route(ids, counts, tm):
    T = ids.shape[0]
    dest = pl.pallas_call(
        functools.partial(_route_kernel, tm=tm),
        grid=(T // tm,),
        in_specs=[pl.BlockSpec((tm, LANE), lambda i: (i, 0)), _full((1, LANE))],
        out_specs=pl.BlockSpec((tm, LANE), lambda i: (i, 0)),
        out_shape=jax.ShapeDtypeStruct((T, LANE), jnp.int32),
        scratch_shapes=[pltpu.VMEM((1, LANE), F32), pltpu.VMEM((1, LANE), F32)],
        compiler_params=_cparams(1),
        name="moe_route",
    )(ids, counts)
    counts = counts[0, :N_EXPERTS].astype(jnp.int32)
    pad_end = jnp.cumsum((counts + MOE_BLOCK - 1) // MOE_BLOCK * MOE_BLOCK)
    n_blocks = -(-(T * TOP_K + N_EXPERTS * (MOE_BLOCK - 1)) // MOE_BLOCK)
    block_start = jnp.arange(n_blocks, dtype=jnp.int32) * MOE_BLOCK
    block_e = jnp.sum((block_start[:, None] >= pad_end[None, :]).astype(jnp.int32), axis=1)
    block_e = jnp.minimum(block_e, N_EXPERTS - 1)
    n_active = (pad_end[-1] // MOE_BLOCK).astype(jnp.int32).reshape(1)
    dest_flat = dest[:, :TOP_K].T.reshape(TOP_K * T)
    return dest_flat, block_e, n_active, n_blocks * MOE_BLOCK


def _sc_workers():
    info = plsc.get_sparse_core_info()
    return info.num_cores, info.num_cores * info.num_subcores


def _sc_scatter_rows(rows, dest_flat, n_rows):
    T, W = rows.shape
    n_cores, n_workers = _sc_workers()
    per_w = T // n_workers
    assert per_w * n_workers == T and per_w % SC_CHUNK == 0

    def body(rows_hbm, dest_hbm, out_hbm, buf, *idx):
        wid = lax.axis_index("s") * n_cores + lax.axis_index("c")

        @pl.loop(0, per_w // SC_CHUNK)
        def _(j):
            off = pl.multiple_of(wid * per_w + j * SC_CHUNK, SC_CHUNK)
            pltpu.sync_copy(rows_hbm.at[pl.ds(off, SC_CHUNK)], buf)
            for k in range(TOP_K):
                pltpu.sync_copy(dest_hbm.at[pl.ds(k * T + off, SC_CHUNK)], idx[k])
            for k in range(TOP_K):
                pltpu.sync_copy(buf, out_hbm.at[idx[k]])

    return pl.kernel(
        body,
        out_type=jax.ShapeDtypeStruct((n_rows, W), rows.dtype),
        mesh=plsc.VectorSubcoreMesh(core_axis_name="c", subcore_axis_name="s"),
        scratch_types=[pltpu.VMEM((SC_CHUNK, W), rows.dtype)] + [pltpu.VMEM((SC_CHUNK,), jnp.int32)] * TOP_K,
        name="moe_dispatch_sc",
    )(rows, dest_flat)


def _sc_gather_rows(table, dest_flat):
    N = dest_flat.shape[0]
    W = table.shape[1]
    n_cores, n_workers = _sc_workers()
    per_w = N // n_workers
    assert per_w * n_workers == N and per_w % SC_CHUNK == 0

    def body(table_hbm, dest_hbm, out_hbm, buf, idx):
        wid = lax.axis_index("s") * n_cores + lax.axis_index("c")

        @pl.loop(0, per_w // SC_CHUNK)
        def _(j):
            off = pl.multiple_of(wid * per_w + j * SC_CHUNK, SC_CHUNK)
            pltpu.sync_copy(dest_hbm.at[pl.ds(off, SC_CHUNK)], idx)
            pltpu.sync_copy(table_hbm.at[idx], buf)
            pltpu.sync_copy(buf, out_hbm.at[pl.ds(off, SC_CHUNK)])

    return pl.kernel(
        body,
        out_type=jax.ShapeDtypeStruct((N, W), table.dtype),
        mesh=plsc.VectorSubcoreMesh(core_axis_name="c", subcore_axis_name="s"),
        scratch_types=[pltpu.VMEM((SC_CHUNK, W), table.dtype), pltpu.VMEM((SC_CHUNK,), jnp.int32)],
        name="moe_gather_sc",
    )(table, dest_flat)


def _moe(groups, P, tm):
    staged = []
    for x1, h2p, ids, gates, counts in groups:
        dest_flat, block_e, n_active, n_rows = _route(ids, counts, tm)
        staged.append((dest_flat, block_e, n_active, _sc_scatter_rows(h2p, dest_flat, n_rows)))
    ys = []
    for dest_flat, block_e, n_active, xs in staged:
        if ys:
            xs, ys[-1] = lax.optimization_barrier((xs, ys[-1]))
        ys.append(_experts(block_e, n_active, xs, P))
    outs = []
    for (x1, _, _, gates, _), (dest_flat, _, _, _), y in zip(groups, staged, ys):
        yk = _sc_gather_rows(y, dest_flat).reshape(TOP_K, x1.shape[0], PACKED)
        outs.append(_combine(x1, yk, gates, tm))
    return outs


def _prep_params(norm_mix, w_in, q_a_norm, w_uq, kv_a_norm, w_ukv, q_norm, k_norm, w_hy_o, w_mla_o, w_out,
                 norm_ffn, router_w, router_b, w_gu, b_gu, w_down, b_down, L):
    P = {}
    P["norm_mix"] = norm_mix.astype(F32).reshape(1, D_MODEL)
    P["w_hy_t"] = w_in[:, :HY_END].T.astype(BF16)
    P["w_m"] = jnp.pad(w_in[:, HY_END:KR_END], ((0, 0), (0, MLA_IN - (KR_END - HY_END)))).astype(BF16)
    P["w_g"] = w_in[:, KR_END:].astype(BF16)

    half = QK_ROPE // 2
    swap = jnp.concatenate([jnp.arange(QK_NOPE), QK_NOPE + half + jnp.arange(half), QK_NOPE + jnp.arange(half)])
    rope_lane = (jnp.arange(QK_HEAD) >= QK_NOPE)

    def slots(w):
        return jnp.pad(w, ((0, 0), (0, 0), (0, HEAD_PAD - QK_HEAD))).reshape(w.shape[0], MLA_HEADS * HEAD_PAD)

    wq = w_uq.astype(F32).reshape(Q_LORA, MLA_HEADS, QK_HEAD)
    P["wqa"] = slots(wq).astype(BF16)
    P["wqb"] = slots(jnp.where(rope_lane, wq[:, :, swap], 0.0)).astype(BF16)
    wkv = w_ukv.astype(F32).reshape(KV_LORA, MLA_HEADS, QK_NOPE + V_HEAD)
    P["wka"] = slots(jnp.pad(wkv[:, :, :QK_NOPE], ((0, 0), (0, 0), (0, QK_ROPE)))).astype(BF16)
    P["wv"] = wkv[:, :, QK_NOPE:].reshape(KV_LORA, MLA_HEADS * V_HEAD).astype(BF16)
    eye = jnp.eye(LANE, QK_ROPE, dtype=F32)
    e_a = jnp.pad(eye, ((0, 0), (QK_NOPE, 0)))
    e_b = e_a[:, swap] * rope_lane
    P["ea"] = slots(jnp.broadcast_to(e_a[:, None, :], (LANE, MLA_HEADS, QK_HEAD))).astype(BF16)
    P["eb"] = slots(jnp.broadcast_to(e_b[:, None, :], (LANE, MLA_HEADS, QK_HEAD))).astype(BF16)

    sign = jnp.concatenate([jnp.zeros(QK_NOPE), -jnp.ones(half), jnp.ones(half)]).astype(F32)

    def lane_pad(g):
        return jnp.pad(g, (0, HEAD_PAD - QK_HEAD)).reshape(1, HEAD_PAD)

    for name, g in (("gq", q_norm.astype(F32)), ("gk", k_norm.astype(F32))):
        P[name] = lane_pad(g)
        P[name + "s"] = lane_pad(g[swap] * sign)
    P["gqa"] = q_a_norm.astype(F32).reshape(1, Q_LORA)
    P["gkva"] = kv_a_norm.astype(F32).reshape(1, KV_LORA)
    inv = 1.0 / (ROPE_THETA ** (jnp.arange(0, QK_ROPE, 2, dtype=F32) / QK_ROPE))
    ang = jnp.arange(L, dtype=F32)[:, None] * inv[None, :]
    cos, sin = jnp.cos(ang), jnp.sin(ang)
    P["ctab"] = jnp.concatenate([jnp.ones((L, QK_NOPE), F32), cos, cos, jnp.zeros((L, HEAD_PAD - QK_HEAD), F32)], axis=1)
    P["stab"] = jnp.concatenate([jnp.zeros((L, QK_NOPE), F32), sin, sin, jnp.zeros((L, HEAD_PAD - QK_HEAD), F32)], axis=1)

    P["w_hy_o"] = w_hy_o.astype(BF16)
    P["w_mla_o"] = w_mla_o.astype(BF16)
    P["w_out"] = w_out.astype(BF16)
    P["norm_ffn"] = norm_ffn.astype(F32).reshape(1, D_MODEL)
    rw = jnp.pad(router_w.astype(F32), ((0, 0), (0, LANE - N_EXPERTS)))
    P["router_w_hi"] = rw.astype(BF16)
    P["router_w_lo"] = (rw - P["router_w_hi"].astype(F32)).astype(BF16)
    P["router_b"] = jnp.pad(router_b.astype(F32), (0, LANE - N_EXPERTS)).reshape(1, LANE)
    P["w_gu"] = w_gu.astype(F32)
    P["b_gu"] = b_gu.astype(F32).reshape(N_EXPERTS, 1, 2 * D_FF)
    P["w_down"] = w_down.astype(F32)
    P["b_down"] = b_down.astype(F32).reshape(N_EXPERTS, 1, D_MODEL)
    return P


def _tile(n, want):
    t = min(n, want)
    assert n % t == 0
    return t


def _layer(xs, norm_mix, w_in, hy_conv_w, hy_conv_b, hy_fw1, hy_fb1, hy_fw2, hy_fb2, hy_fw3, hy_freq, hy_decay, hy_skip,
           q_a_norm, w_uq, kv_a_norm, w_ukv, q_norm, k_norm, w_hy_o, w_mla_o, w_out,
           norm_ffn, router_w, router_b, w_gu, b_gu, w_down, b_down):
    L = xs[0].shape[1]
    assert all(x.shape[1] == L and x.shape[2] == D_MODEL for x in xs) and L % MXU == 0
    P = _prep_params(norm_mix, w_in, q_a_norm, w_uq, kv_a_norm, w_ukv, q_norm, k_norm, w_hy_o, w_mla_o, w_out,
                     norm_ffn, router_w, router_b, w_gu, b_gu, w_down, b_down, L)
    kf = _hyena_filters(L, hy_fw1, hy_fb1, hy_fw2, hy_fb2, hy_fw3, hy_freq, hy_decay)
    tm = _tile(L, 512)
    proj = [_inproj(x, P["norm_mix"], P["w_hy_t"], P["w_m"], P["w_g"], tm) for x in xs]
    y_ts = _hyena([u_t for u_t, _, _ in proj], kf, hy_conv_w, hy_conv_b, hy_skip, L)
    mixed = []
    for x, (_, m, g), y_t in zip(xs, proj, y_ts):
        q, k, v = _mla_prep(m, P, tm)
        y_mla = _attention(q, k, v, _tile(L, 512))
        mixed.append(_merge(x, y_t, y_mla, g, P, tm))
    order = sorted(range(len(xs)), key=lambda n: -xs[n].shape[0])
    moe_out = _moe([mixed[n] for n in order], P, tm)
    outs = [None] * len(xs)
    for n, y in zip(order, moe_out):
        outs[n] = y.reshape(xs[n].shape)
    return outs


def kernel(x_prompt, x_sample, norm_mix, w_in, hy_conv_w, hy_conv_b, hy_fw1, hy_fb1, hy_fw2, hy_fb2, hy_fw3, hy_freq, hy_decay, hy_skip, q_a_norm, w_uq, kv_a_norm, w_ukv, q_norm, k_norm, w_hy_o, w_mla_o, w_out, norm_ffn, router_w, router_b, w_gu, b_gu, w_down, b_down):
    depth = norm_mix.shape[0]
    xs = [x_prompt, x_sample]
    weights = (norm_mix, w_in, hy_conv_w, hy_conv_b, hy_fw1, hy_fb1, hy_fw2, hy_fb2, hy_fw3, hy_freq, hy_decay, hy_skip,
               q_a_norm, w_uq, kv_a_norm, w_ukv, q_norm, k_norm, w_hy_o, w_mla_o, w_out,
               norm_ffn, router_w, router_b, w_gu, b_gu, w_down, b_down)
    for l in range(depth):
        xs = _layer(xs, *[w[l] for w in weights])
    return (xs[0], xs[1])
```

```python
import functools
import math

import jax
import jax.numpy as jnp
from jax import lax
from jax.experimental import pallas as pl
from jax.experimental.pallas import tpu as pltpu
from jax.experimental.pallas import tpu_sc as plsc

F32 = jnp.float32
BF16 = jnp.bfloat16

D_MODEL = 1024
HY_WIDTH = 512
HY_ORDER = 2
HY_EMB_BANDS = 16
HY_FILTER_DIM = 64
MLA_HEADS = 8
V_HEAD = 64
QK_NOPE = 64
QK_ROPE = 32
QK_HEAD = QK_NOPE + QK_ROPE
Q_LORA = 256
KV_LORA = 128
ROPE_THETA = 10000.0
HY_END = 3 * HY_WIDTH
Q_END = HY_END + Q_LORA
KV_END = Q_END + KV_LORA
KR_END = KV_END + QK_ROPE
G1_END = KR_END + D_MODEL
N_EXPERTS = 32
TOP_K = 4
D_FF = D_MODEL
SWIGLU_LIMIT = 7.0
SWIGLU_ALPHA = 1.702
EPS = 1e-6

LANE = 128
MXU = 256
HEAD_PAD = LANE
MLA_IN = 512
CONV_CB = 8
CONV_UNROLL = 4
MOE_BLOCK = 512
PACKED = D_MODEL // 2
SC_CHUNK = 64
VMEM_LIMIT = 52 * 1024 * 1024


def _cparams(n_axes):
    return pltpu.CompilerParams(dimension_semantics=("arbitrary",) * n_axes, vmem_limit_bytes=VMEM_LIMIT)


def _full(shape):
    n = len(shape)
    return pl.BlockSpec(shape, lambda *_: (0,) * n)


def _inproj_kernel(x_ref, g_ref, why_ref, wm_ref, wg_ref, u_ref, m_ref, gate_ref):
    x = x_ref[...]
    r = lax.rsqrt(jnp.mean(x * x, axis=-1, keepdims=True) + EPS)
    h = (x * r * g_ref[...]).astype(BF16)
    u_ref[...] = lax.dot_general(why_ref[...], h, (((1,), (1,)), ((), ())), preferred_element_type=F32)
    m_ref[...] = jnp.dot(h, wm_ref[...], preferred_element_type=F32).astype(BF16)
    gate_ref[...] = jnp.dot(h, wg_ref[...], preferred_element_type=F32).astype(BF16)


def _inproj(x, g, w_hy_t, w_m, w_g, tm):
    B, L, _ = x.shape
    return pl.pallas_call(
        _inproj_kernel,
        grid=(B, L // tm),
        in_specs=[
            pl.BlockSpec((None, tm, D_MODEL), lambda b, i: (b, i, 0)),
            _full((1, D_MODEL)),
            _full(w_hy_t.shape),
            _full(w_m.shape),
            _full(w_g.shape),
        ],
        out_specs=[
            pl.BlockSpec((None, HY_END, tm), lambda b, i: (b, 0, i)),
            pl.BlockSpec((None, tm, MLA_IN), lambda b, i: (b, i, 0)),
            pl.BlockSpec((None, tm, 2 * D_MODEL), lambda b, i: (b, i, 0)),
        ],
        out_shape=[
            jax.ShapeDtypeStruct((B, HY_END, L), F32),
            jax.ShapeDtypeStruct((B, L, MLA_IN), BF16),
            jax.ShapeDtypeStruct((B, L, 2 * D_MODEL), BF16),
        ],
        compiler_params=_cparams(2),
        name="inproj",
    )(x, g, w_hy_t, w_m, w_g)


def _filter_kernel(bands_ref, w1t_ref, w1c_ref, w1s_ref, b1_ref, w2_ref, b2_ref, fr_ref, w3f_ref, w3b_ref, dec_ref,
                   out_ref, *, L):
    hi = lax.Precision.HIGHEST
    n = 2 * L
    j = lax.broadcasted_iota(jnp.int32, (1, n), 1)
    pos = jnp.abs(j - L).astype(F32)
    t01 = pos / max(L - 1, 1)
    ang = ((2.0 * math.pi / L) * pos) * bands_ref[...]
    pre = (w1t_ref[...] * t01
           + jnp.dot(w1c_ref[...], jnp.cos(ang), precision=hi, preferred_element_type=F32)
           - jnp.dot(w1s_ref[...], jnp.sin(ang), precision=hi, preferred_element_type=F32)
           + b1_ref[...])
    z = jnp.sin(fr_ref[0] * pre)
    z = jnp.sin(fr_ref[1] * (jnp.dot(w2_ref[...], z, precision=hi, preferred_element_type=F32) + b2_ref[...]))
    win = jnp.exp(-t01 * jnp.abs(dec_ref[...]))
    h_b = jnp.dot(w3b_ref[...], z[:, :L], precision=hi, preferred_element_type=F32) * win[:, :L]
    h_f = jnp.dot(w3f_ref[...], z[:, L:], precision=hi, preferred_element_type=F32) * win[:, L:]
    out_ref[:, :L] = jnp.where(j[:, :L] == 0, 0.0, h_b)
    out_ref[:, L:] = h_f


def _hyena_filters(L, fw1, fb1, fw2, fb2, fw3, freq, decay):
    rows = HY_ORDER * HY_WIDTH
    rb = 256
    bands = jnp.linspace(1e-4, HY_EMB_BANDS - 1, HY_EMB_BANDS, dtype=F32).reshape(HY_EMB_BANDS, 1)
    w1 = fw1.astype(F32).T
    w3 = fw3.astype(F32).T
    fd = HY_FILTER_DIM
    return pl.pallas_call(
        functools.partial(_filter_kernel, L=L),
        grid=(rows // rb,),
        in_specs=[
            _full((HY_EMB_BANDS, 1)),
            _full((fd, 1)),
            _full((fd, HY_EMB_BANDS)),
            _full((fd, HY_EMB_BANDS)),
            _full((fd, 1)),
            _full((fd, fd)),
            _full((fd, 1)),
            _full((2, fd, 1)),
            pl.BlockSpec((rb, fd), lambda i: (i, 0)),
            pl.BlockSpec((rb, fd), lambda i: (i, 0)),
            pl.BlockSpec((rb, 1), lambda i: (i, 0)),
        ],
        out_specs=pl.BlockSpec((rb, 2 * L), lambda i: (i, 0)),
        out_shape=jax.ShapeDtypeStruct((rows, 2 * L), F32),
        compiler_params=_cparams(1),
        name="hyena_filters",
    )(bands, w1[:, :1], w1[:, 1:1 + HY_EMB_BANDS], w1[:, 1 + HY_EMB_BANDS:], fb1.astype(F32).reshape(fd, 1),
      fw2.astype(F32).T, fb2.astype(F32).reshape(fd, 1), freq.astype(F32).reshape(2, fd, 1),
      w3[:rows], w3[rows:], decay.astype(F32).reshape(rows, 1))


def _shift_conv(u, w0, w1, w2, b):
    L = u.shape[1]
    lane = lax.broadcasted_iota(jnp.int32, (1, L), 1)
    w0_row = jnp.where(lane == 0, 0.0, w0)
    w2_row = jnp.where(lane == L - 1, 0.0, w2)
    return pltpu.roll(u, 1, 1) * w0_row + u * w1 + pltpu.roll(u, L - 1, 1) * w2_row + b


def _toeplitz_conv(z, tz_ref, L):
    B = z.shape[0]
    nj = L // MXU
    zb = z.astype(BF16)
    tiles = [zb[:, MXU * J:MXU * (J + 1)] for J in range(nj)]
    acc = [None] * nj
    for d in range(-(nj - 1), nj):
        js = [J for J in range(nj) if 0 <= J + d < nj]
        lhs = tiles[js[0]] if len(js) == 1 else jnp.concatenate([tiles[J] for J in js], axis=0)
        res = jnp.dot(lhs, tz_ref[:, L + MXU * d:L + MXU * (d + 1)], preferred_element_type=F32)
        for n, J in enumerate(js):
            part = res[n * B:(n + 1) * B]
            acc[J + d] = part if acc[J + d] is None else acc[J + d] + part
    return acc[0] if nj == 1 else jnp.concatenate(acc, axis=1)


def _hyena_kernel(cw_ref, cb_ref, skip_ref, kf_ref, *refs, L, n_groups):
    in_refs = refs[:3 * n_groups]
    out_refs = refs[3 * n_groups:4 * n_groups]
    tz_ref = refs[4 * n_groups]
    c0 = pl.program_id(0) * CONV_CB

    def channels(p, carry):
        for u in range(CONV_UNROLL):
            c = p * CONV_UNROLL + u
            ch = c0 + c
            for o in range(HY_ORDER):
                row = jnp.broadcast_to(kf_ref[o, pl.ds(c, 1), :], (MXU, 2 * L))
                tz_ref[u, o] = pltpu.roll(row, 0, 1, stride=1, stride_axis=0).astype(BF16)
            for g in range(n_groups):
                stage = refs[4 * n_groups + 1 + g]
                sec = []
                for s, ref in enumerate(in_refs[3 * g:3 * g + 3]):
                    stage[u, s] = ref[:, c, :]
                    k = s * HY_WIDTH + ch
                    sec.append(_shift_conv(stage[u, s], cw_ref[0, k], cw_ref[1, k], cw_ref[2, k], cb_ref[k]))
                x1, x2, z = sec
                for o, gate in enumerate((x1, x2)):
                    y = _toeplitz_conv(z, tz_ref.at[u, o], L) + z * skip_ref[o, ch]
                    z = gate * y
                out_refs[g][:, c, :] = z
        return carry

    lax.fori_loop(0, CONV_CB // CONV_UNROLL, channels, 0)


def _hyena(u_ts, kf, conv_w, conv_b, skip, L):
    n_groups = len(u_ts)
    nblk = HY_WIDTH // CONV_CB
    smem = pl.BlockSpec(memory_space=pltpu.SMEM)
    in_specs = [smem, smem, smem, pl.BlockSpec((HY_ORDER, CONV_CB, 2 * L), lambda i: (0, i, 0))]
    args = [conv_w.astype(F32), conv_b.astype(F32), skip.astype(F32), kf.reshape(HY_ORDER, HY_WIDTH, 2 * L)]
    out_specs, out_shape = [], []
    for u in u_ts:
        B = u.shape[0]
        for s in range(3):
            in_specs.append(pl.BlockSpec((B, CONV_CB, L), lambda i, s=s: (0, i + s * nblk, 0)))
            args.append(u)
        out_specs.append(pl.BlockSpec((B, CONV_CB, L), lambda i: (0, i, 0)))
        out_shape.append(jax.ShapeDtypeStruct((B, HY_WIDTH, L), F32))
    return pl.pallas_call(
        functools.partial(_hyena_kernel, L=L, n_groups=n_groups),
        grid=(nblk,),
        in_specs=in_specs,
        out_specs=out_specs,
        out_shape=out_shape,
        scratch_shapes=[pltpu.VMEM((CONV_UNROLL, HY_ORDER, MXU, 2 * L), BF16)]
        + [pltpu.VMEM((CONV_UNROLL, 3, u.shape[0], L), F32) for u in u_ts],
        compiler_params=_cparams(1),
        name="hyena_conv",
    )(*args)


def _rms(x, n):
    return lax.rsqrt(jnp.sum(x * x, axis=-1, keepdims=True) * (1.0 / n) + EPS)


def _mla_prep_kernel(m_ref, ct_ref, st_ref, gqa_ref, gkva_ref, gq_ref, gqs_ref, gk_ref, gks_ref,
                     wqa_ref, wqb_ref, wka_ref, ea_ref, eb_ref, wv_ref, q_ref, k_ref, v_ref):
    m = m_ref[0].astype(F32)
    cq = m[:, :Q_LORA]
    ckv = m[:, Q_LORA:Q_LORA + KV_LORA]
    kr = m_ref[0][:, Q_LORA + KV_LORA:Q_LORA + KV_LORA + LANE]
    cqn = (cq * _rms(cq, Q_LORA) * gqa_ref[...]).astype(BF16)
    ckvn = (ckv * _rms(ckv, KV_LORA) * gkva_ref[...]).astype(BF16)
    qa = jnp.dot(cqn, wqa_ref[...], preferred_element_type=F32)
    qb = jnp.dot(cqn, wqb_ref[...], preferred_element_type=F32)
    ka = jnp.dot(ckvn, wka_ref[...], preferred_element_type=F32) + jnp.dot(kr, ea_ref[...], preferred_element_type=F32)
    kb = jnp.dot(kr, eb_ref[...], preferred_element_type=F32)
    v_ref[0] = jnp.dot(ckvn, wv_ref[...], preferred_element_type=F32).astype(BF16)
    ct, st = ct_ref[...], st_ref[...]
    scale = QK_HEAD ** -0.5 * math.log2(math.e)
    qc, qs = gq_ref[...] * ct * scale, gqs_ref[...] * st * scale
    kc, ks = gk_ref[...] * ct, gks_ref[...] * st
    for h in range(MLA_HEADS):
        sl = slice(h * HEAD_PAD, (h + 1) * HEAD_PAD)
        q_ref[0, h] = (_rms(qa[:, sl], QK_HEAD) * (qa[:, sl] * qc + qb[:, sl] * qs)).astype(BF16)
        k_ref[0, h] = (_rms(ka[:, sl], QK_HEAD) * (ka[:, sl] * kc + kb[:, sl] * ks)).astype(BF16)


def _mla_prep(m, P, tm):
    B, L, _ = m.shape
    w = [P[n] for n in ("wqa", "wqb", "wka", "ea", "eb", "wv")]
    g = [P[n] for n in ("gqa", "gkva", "gq", "gqs", "gk", "gks")]
    hp = MLA_HEADS * HEAD_PAD
    return pl.pallas_call(
        _mla_prep_kernel,
        grid=(B, L // tm),
        in_specs=[pl.BlockSpec((1, tm, MLA_IN), lambda b, i: (b, i, 0)),
                  pl.BlockSpec((tm, HEAD_PAD), lambda b, i: (i, 0)),
                  pl.BlockSpec((tm, HEAD_PAD), lambda b, i: (i, 0))]
        + [_full(a.shape) for a in g] + [_full(a.shape) for a in w],
        out_specs=[
            pl.BlockSpec((1, MLA_HEADS, tm, HEAD_PAD), lambda b, i: (b, 0, i, 0)),
            pl.BlockSpec((1, MLA_HEADS, tm, HEAD_PAD), lambda b, i: (b, 0, i, 0)),
            pl.BlockSpec((1, tm, MLA_HEADS * V_HEAD), lambda b, i: (b, i, 0)),
        ],
        out_shape=[
            jax.ShapeDtypeStruct((B, MLA_HEADS, L, HEAD_PAD), BF16),
            jax.ShapeDtypeStruct((B, MLA_HEADS, L, HEAD_PAD), BF16),
            jax.ShapeDtypeStruct((B, L, MLA_HEADS * V_HEAD), BF16),
        ],
        compiler_params=_cparams(2),
        name="mla_prep",
    )(m, P["ctab"][:L], P["stab"][:L], *g, *w)


def _attn_kernel(q_ref, k_ref, v_ref, o_ref):
    L = v_ref.shape[1]
    one_col = jnp.where(lax.broadcasted_iota(jnp.int32, (L, LANE), 1) == 0, 1.0, 0.0).astype(BF16)
    for pair in range(MLA_HEADS // 2):
        lanes = pl.ds(pair * LANE, LANE)
        v_ext = jnp.concatenate([v_ref[0, :, lanes], one_col], axis=1)
        outs = []
        for h in (2 * pair, 2 * pair + 1):
            s = lax.dot_general(q_ref[0, h], k_ref[0, h], (((1,), (1,)), ((), ())), preferred_element_type=F32)
            p = jnp.exp2(s - jnp.max(s, axis=-1, keepdims=True)).astype(BF16)
            pv = jnp.dot(p, v_ext, preferred_element_type=F32)
            outs.append(pv[:, :LANE] / pv[:, LANE:LANE + 1])
        lane = lax.broadcasted_iota(jnp.int32, outs[0].shape, 1)
        o_ref[0, :, lanes] = jnp.where(lane < V_HEAD, outs[0], outs[1]).astype(BF16)


def _attention(q, k, v, tq):
    B, H, L, _ = q.shape
    return pl.pallas_call(
        _attn_kernel,
        grid=(B, L // tq),
        in_specs=[
            pl.BlockSpec((1, H, tq, HEAD_PAD), lambda b, i: (b, 0, i, 0)),
            pl.BlockSpec((1, H, L, HEAD_PAD), lambda b, i: (b, 0, 0, 0)),
            pl.BlockSpec((1, L, H * V_HEAD), lambda b, i: (b, 0, 0)),
        ],
        out_specs=pl.BlockSpec((1, tq, H * V_HEAD), lambda b, i: (b, i, 0)),
        out_shape=jax.ShapeDtypeStruct((B, L, H * V_HEAD), BF16),
        compiler_params=_cparams(2),
        name="mla_attention",
    )(q, k, v)


def _pack_rows(a):
    bits = lax.bitcast_convert_type(a.astype(BF16).astype(F32), jnp.uint32)
    return (bits[:, :PACKED] >> 16) | (bits[:, PACKED:] & jnp.uint32(0xFFFF0000))


def _unpack_rows(p):
    lo = lax.bitcast_convert_type(p << 16, F32)
    hi = lax.bitcast_convert_type(p & jnp.uint32(0xFFFF0000), F32)
    return jnp.concatenate([lo, hi], axis=1)


def _merge_kernel(x_ref, yh_ref, ym_ref, g_ref, who_ref, wmo_ref, wo_ref, nf_ref, rwh_ref, rwl_ref, rb_ref,
                  x1_ref, h2_ref, ids_ref, gate_ref, cnt_ref):
    tm = x_ref.shape[0]
    sub = tm // 2

    @pl.when(jnp.logical_and(pl.program_id(0) == 0, pl.program_id(1) == 0))
    def _():
        cnt_ref[...] = jnp.zeros_like(cnt_ref)

    eid = lax.broadcasted_iota(jnp.int32, (N_EXPERTS, sub), 0)
    for r in range(0, tm, sub):
        rows = pl.ds(r, sub)
        g = g_ref[rows, :].astype(F32)
        a = lax.dot_general(yh_ref[:, rows].astype(BF16), who_ref[...], (((0,), (0,)), ((), ())),
                            preferred_element_type=F32)
        b = jnp.dot(ym_ref[rows, :], wmo_ref[...], preferred_element_type=F32)
        merged = jax.nn.sigmoid(g[:, :D_MODEL]) * a + jax.nn.sigmoid(g[:, D_MODEL:]) * b
        x1 = x_ref[rows, :] + jnp.dot(merged.astype(BF16), wo_ref[...], preferred_element_type=F32)
        x1_ref[rows, :] = x1
        h2 = x1 * lax.rsqrt(jnp.mean(x1 * x1, axis=-1, keepdims=True) + EPS) * nf_ref[...]
        h2_ref[rows, :] = _pack_rows(h2)
        h_hi = h2.astype(BF16)
        h_lo = (h2 - h_hi.astype(F32)).astype(BF16)
        logits = (jnp.dot(h_hi, rwh_ref[...], preferred_element_type=F32)
                  + jnp.dot(h_lo, rwh_ref[...], preferred_element_type=F32)
                  + jnp.dot(h_hi, rwl_ref[...], preferred_element_type=F32) + rb_ref[...])
        work = logits.T[:N_EXPERTS]
        picked = jnp.zeros((N_EXPERTS, sub), F32)
        vals, firsts = [], []
        for k in range(TOP_K):
            m = jnp.max(work, axis=0, keepdims=True)
            first = jnp.min(jnp.where(work == m, eid, N_EXPERTS), axis=0, keepdims=True)
            hot = eid == first
            vals.append(m)
            firsts.append(first)
            picked = jnp.where(hot, 1.0, picked)
            work = jnp.where(hot, -jnp.inf, work)
        ex = [jnp.exp(v - vals[0]) for v in vals]
        inv = 1.0 / (ex[0] + ex[1] + ex[2] + ex[3])
        pad = jnp.zeros((LANE - TOP_K, sub), F32)
        ids_t = jnp.concatenate([f.astype(F32) for f in firsts] + [pad], axis=0)
        gates_t = jnp.concatenate([e * inv for e in ex] + [pad], axis=0)
        ids_ref[rows, :] = ids_t.T.astype(jnp.int32)
        gate_ref[rows, :] = gates_t.T
        cnt_ref[...] += jnp.sum(picked, axis=1, keepdims=True)


def _merge(x, yh_t, ym, g, P, tm):
    B, L, _ = x.shape
    nt = L // tm
    w = [P[n] for n in ("w_hy_o", "w_mla_o", "w_out", "norm_ffn", "router_w_hi", "router_w_lo", "router_b")]
    return pl.pallas_call(
        _merge_kernel,
        grid=(B, nt),
        in_specs=[
            pl.BlockSpec((None, tm, D_MODEL), lambda b, i: (b, i, 0)),
            pl.BlockSpec((None, HY_WIDTH, tm), lambda b, i: (b, 0, i)),
            pl.BlockSpec((None, tm, MLA_HEADS * V_HEAD), lambda b, i: (b, i, 0)),
            pl.BlockSpec((None, tm, 2 * D_MODEL), lambda b, i: (b, i, 0)),
        ] + [_full(a.shape) for a in w],
        out_specs=[
            pl.BlockSpec((tm, D_MODEL), lambda b, i: (b * nt + i, 0)),
            pl.BlockSpec((tm, PACKED), lambda b, i: (b * nt + i, 0)),
            pl.BlockSpec((tm, LANE), lambda b, i: (b * nt + i, 0)),
            pl.BlockSpec((tm, LANE), lambda b, i: (b * nt + i, 0)),
            pl.BlockSpec((N_EXPERTS, 1), lambda b, i: (0, 0)),
        ],
        out_shape=[
            jax.ShapeDtypeStruct((B * L, D_MODEL), F32),
            jax.ShapeDtypeStruct((B * L, PACKED), jnp.uint32),
            jax.ShapeDtypeStruct((B * L, LANE), jnp.int32),
            jax.ShapeDtypeStruct((B * L, LANE), F32),
            jax.ShapeDtypeStruct((N_EXPERTS, 1), F32),
        ],
        compiler_params=_cparams(2),
        name="merge_router",
    )(x, yh_t, ym, g, *w)


def _expert_kernel(be_ref, na_ref, x_ref, wgu_ref, bgu_ref, wd_ref, bd_ref, y_ref):
    i = pl.program_id(0)

    @pl.when(i < na_ref[0])
    def _():
        x = _unpack_rows(x_ref[...])
        gu = jnp.dot(x, wgu_ref[0], preferred_element_type=F32) + bgu_ref[0]
        gate = jnp.minimum(gu[:, :D_FF], SWIGLU_LIMIT)
        up = jnp.clip(gu[:, D_FF:], -SWIGLU_LIMIT, SWIGLU_LIMIT)
        act = (up + 1.0) * (gate * jax.nn.sigmoid(SWIGLU_ALPHA * gate))
        y_ref[...] = _pack_rows(jnp.dot(act, wd_ref[0], preferred_element_type=F32) + bd_ref[0])

    @pl.when(i >= na_ref[0])
    def _():
        y_ref[...] = jnp.zeros_like(y_ref)


def _experts(block_e, n_active, xs, P):
    n_rows = xs.shape[0]
    nb = n_rows // MOE_BLOCK
    grid_spec = pltpu.PrefetchScalarGridSpec(
        num_scalar_prefetch=2,
        grid=(nb,),
        in_specs=[
            pl.BlockSpec((MOE_BLOCK, PACKED), lambda i, be, na: (i, 0)),
            pl.BlockSpec((1, D_MODEL, 2 * D_FF), lambda i, be, na: (be[i], 0, 0)),
            pl.BlockSpec((1, 1, 2 * D_FF), lambda i, be, na: (be[i], 0, 0)),
            pl.BlockSpec((1, D_FF, D_MODEL), lambda i, be, na: (be[i], 0, 0)),
            pl.BlockSpec((1, 1, D_MODEL), lambda i, be, na: (be[i], 0, 0)),
        ],
        out_specs=pl.BlockSpec((MOE_BLOCK, PACKED), lambda i, be, na: (i, 0)),
    )
    return pl.pallas_call(
        _expert_kernel,
        grid_spec=grid_spec,
        out_shape=jax.ShapeDtypeStruct((n_rows, PACKED), jnp.uint32),
        compiler_params=_cparams(1),
        name="moe_experts",
    )(block_e, n_active, xs, P["w_gu"], P["b_gu"], P["w_down"], P["b_down"])


def _combine_kernel(x1_ref, yk_ref, gt_ref, o_ref):
    gt = gt_ref[...]
    acc = _unpack_rows(yk_ref[0]) * gt[:, 0:1]
    for k in range(1, TOP_K):
        acc = acc + _unpack_rows(yk_ref[k]) * gt[:, k:k + 1]
    o_ref[...] = x1_ref[...] + acc


def _combine(x1, yk, gates, tm):
    T = x1.shape[0]
    return pl.pallas_call(
        _combine_kernel,
        grid=(T // tm,),
        in_specs=[
            pl.BlockSpec((tm, D_MODEL), lambda i: (i, 0)),
            pl.BlockSpec((TOP_K, tm, PACKED), lambda i: (0, i, 0)),
            pl.BlockSpec((tm, LANE), lambda i: (i, 0)),
        ],
        out_specs=pl.BlockSpec((tm, D_MODEL), lambda i: (i, 0)),
        out_shape=jax.ShapeDtypeStruct((T, D_MODEL), F32),
        compiler_params=_cparams(1),
        name="moe_combine",
    )(x1, yk, gates)


def _route_kernel(ids_ref, cnt_ref, dest_ref, run_ref, start_ref, *, tm):
    @pl.when(pl.program_id(0) == 0)
    def _():
        cnt = cnt_ref[...]
        padded = jnp.floor((cnt + (MOE_BLOCK - 1)) * (1.0 / MOE_BLOCK)) * MOE_BLOCK
        scan = jnp.broadcast_to(padded, (8, LANE))
        lane8 = lax.broadcasted_iota(jnp.int32, (8, LANE), 1)
        s = 1
        while s < LANE:
            scan = scan + jnp.where(lane8 >= s, pltpu.roll(scan, s, 1), 0.0)
            s *= 2
        start_ref[...] = scan[0:1] - padded
        run_ref[...] = jnp.zeros_like(run_ref)

    lane = lax.broadcasted_iota(jnp.int32, (tm, LANE), 1)
    ids = ids_ref[...]
    hots = [lane == ids[:, k:k + 1] for k in range(TOP_K)]
    chosen = jnp.zeros((tm, LANE), F32)
    for hot in hots:
        chosen = jnp.where(hot, 1.0, chosen)
    r = lax.broadcasted_iota(jnp.int32, (tm, tm), 0)
    c = lax.broadcasted_iota(jnp.int32, (tm, tm), 1)
    earlier = jnp.where(c < r, 1.0, 0.0).astype(BF16)
    before = jnp.dot(earlier, chosen.astype(BF16), preferred_element_type=F32) + (run_ref[...] + start_ref[...])
    dest = jnp.zeros((tm, LANE), F32)
    for k, hot in enumerate(hots):
        dest = jnp.where(lane == k, jnp.sum(jnp.where(hot, before, 0.0), axis=-1, keepdims=True), dest)
    dest_ref[...] = dest.astype(jnp.int32)
    run_ref[...] += jnp.sum(chosen, axis=0, keepdims=True)


def _route(ids, counts, tm):
    T = ids.shape[0]
    dest = pl.pallas_call(
        functools.partial(_route_kernel, tm=tm),
        grid=(T // tm,),
        in_specs=[pl.BlockSpec((tm, LANE), lambda i: (i, 0)), _full((1, LANE))],
        out_specs=pl.BlockSpec((tm, LANE), lambda i: (i, 0)),
        out_shape=jax.ShapeDtypeStruct((T, LANE), jnp.int32),
        scratch_shapes=[pltpu.VMEM((1, LANE), F32), pltpu.VMEM((1, LANE), F32)],
        compiler_params=_cparams(1),
        name="moe_route",
    )(ids, jnp.pad(counts[:, 0], (0, LANE - N_EXPERTS)).reshape(1, LANE))
    counts = counts[:, 0].astype(jnp.int32)
    pad_end = jnp.cumsum((counts + MOE_BLOCK - 1) // MOE_BLOCK * MOE_BLOCK)
    n_blocks = -(-(T * TOP_K + N_EXPERTS * (MOE_BLOCK - 1)) // MOE_BLOCK)
    block_start = jnp.arange(n_blocks, dtype=jnp.int32) * MOE_BLOCK
    block_e = jnp.sum((block_start[:, None] >= pad_end[None, :]).astype(jnp.int32), axis=1)
    block_e = jnp.minimum(block_e, N_EXPERTS - 1)
    n_active = (pad_end[-1] // MOE_BLOCK).astype(jnp.int32).reshape(1)
    dest_flat = dest[:, :TOP_K].T.reshape(TOP_K * T)
    return dest_flat, block_e, n_active, n_blocks * MOE_BLOCK


def _sc_workers():
    info = plsc.get_sparse_core_info()
    return info.num_cores, info.num_cores * info.num_subcores


def _sc_scatter_rows(rows, dest_flat, n_rows):
    T, W = rows.shape
    n_cores, n_workers = _sc_workers()
    per_w = T // n_workers
    assert per_w * n_workers == T and per_w % SC_CHUNK == 0

    def body(rows_hbm, dest_hbm, out_hbm, buf, *idx):
        wid = lax.axis_index("s") * n_cores + lax.axis_index("c")

        @pl.loop(0, per_w // SC_CHUNK)
        def _(j):
            off = pl.multiple_of(wid * per_w + j * SC_CHUNK, SC_CHUNK)
            pltpu.sync_copy(rows_hbm.at[pl.ds(off, SC_CHUNK)], buf)
            for k in range(TOP_K):
                pltpu.sync_copy(dest_hbm.at[pl.ds(k * T + off, SC_CHUNK)], idx[k])
            for k in range(TOP_K):
                pltpu.sync_copy(buf, out_hbm.at[idx[k]])

    return pl.kernel(
        body,
        out_type=jax.ShapeDtypeStruct((n_rows, W), rows.dtype),
        mesh=plsc.VectorSubcoreMesh(core_axis_name="c", subcore_axis_name="s"),
        scratch_types=[pltpu.VMEM((SC_CHUNK, W), rows.dtype)] + [pltpu.VMEM((SC_CHUNK,), jnp.int32)] * TOP_K,
        name="moe_dispatch_sc",
    )(rows, dest_flat)


def _sc_gather_rows(table, dest_flat):
    N = dest_flat.shape[0]
    W = table.shape[1]
    n_cores, n_workers = _sc_workers()
    per_w = N // n_workers
    assert per_w * n_workers == N and per_w % SC_CHUNK == 0

    def body(table_hbm, dest_hbm, out_hbm, buf, idx):
        wid = lax.axis_index("s") * n_cores + lax.axis_index("c")

        @pl.loop(0, per_w // SC_CHUNK)
        def _(j):
            off = pl.multiple_of(wid * per_w + j * SC_CHUNK, SC_CHUNK)
            pltpu.sync_copy(dest_hbm.at[pl.ds(off, SC_CHUNK)], idx)
            pltpu.sync_copy(table_hbm.at[idx], buf)
            pltpu.sync_copy(buf, out_hbm.at[pl.ds(off, SC_CHUNK)])

    return pl.kernel(
        body,
        out_type=jax.ShapeDtypeStruct((N, W), table.dtype),
        mesh=plsc.VectorSubcoreMesh(core_axis_name="c", subcore_axis_name="s"),
        scratch_types=[pltpu.VMEM((SC_CHUNK, W), table.dtype), pltpu.VMEM((SC_CHUNK,), jnp.int32)],
        name="moe_gather_sc",
    )(table, dest_flat)


def _moe(groups, P, tm):
    staged = []
    for x1, h2p, ids, gates, counts in groups:
        dest_flat, block_e, n_active, n_rows = _route(ids, counts, tm)
        staged.append((dest_flat, block_e, n_active, _sc_scatter_rows(h2p, dest_flat, n_rows)))
    ys = []
    for dest_flat, block_e, n_active, xs in staged:
        if ys:
            xs, ys[-1] = lax.optimization_barrier((xs, ys[-1]))
        ys.append(_experts(block_e, n_active, xs, P))
    outs = []
    for (x1, _, _, gates, _), (dest_flat, _, _, _), y in zip(groups, staged, ys):
        yk = _sc_gather_rows(y, dest_flat).reshape(TOP_K, x1.shape[0], PACKED)
        outs.append(_combine(x1, yk, gates, tm))
    return outs


def _prep_params(norm_mix, w_in, q_a_norm, w_uq, kv_a_norm, w_ukv, q_norm, k_norm, w_hy_o, w_mla_o, w_out,
                 norm_ffn, router_w, router_b, w_gu, b_gu, w_down, b_down, L):
    P = {}
    P["norm_mix"] = norm_mix.astype(F32).reshape(1, D_MODEL)
    P["w_hy_t"] = w_in[:, :HY_END].T.astype(BF16)
    P["w_m"] = jnp.pad(w_in[:, HY_END:KR_END], ((0, 0), (0, MLA_IN - (KR_END - HY_END)))).astype(BF16)
    P["w_g"] = w_in[:, KR_END:].astype(BF16)

    half = QK_ROPE // 2
    swap = jnp.concatenate([jnp.arange(QK_NOPE), QK_NOPE + half + jnp.arange(half), QK_NOPE + jnp.arange(half)])
    rope_lane = (jnp.arange(QK_HEAD) >= QK_NOPE)

    def slots(w):
        return jnp.pad(w, ((0, 0), (0, 0), (0, HEAD_PAD - QK_HEAD))).reshape(w.shape[0], MLA_HEADS * HEAD_PAD)

    wq = w_uq.astype(F32).reshape(Q_LORA, MLA_HEADS, QK_HEAD)
    P["wqa"] = slots(wq).astype(BF16)
    P["wqb"] = slots(jnp.where(rope_lane, wq[:, :, swap], 0.0)).astype(BF16)
    wkv = w_ukv.astype(F32).reshape(KV_LORA, MLA_HEADS, QK_NOPE + V_HEAD)
    P["wka"] = slots(jnp.pad(wkv[:, :, :QK_NOPE], ((0, 0), (0, 0), (0, QK_ROPE)))).astype(BF16)
    P["wv"] = wkv[:, :, QK_NOPE:].reshape(KV_LORA, MLA_HEADS * V_HEAD).astype(BF16)
    eye = jnp.eye(LANE, QK_ROPE, dtype=F32)
    e_a = jnp.pad(eye, ((0, 0), (QK_NOPE, 0)))
    e_b = e_a[:, swap] * rope_lane
    P["ea"] = slots(jnp.broadcast_to(e_a[:, None, :], (LANE, MLA_HEADS, QK_HEAD))).astype(BF16)
    P["eb"] = slots(jnp.broadcast_to(e_b[:, None, :], (LANE, MLA_HEADS, QK_HEAD))).astype(BF16)

    sign = jnp.concatenate([jnp.zeros(QK_NOPE), -jnp.ones(half), jnp.ones(half)]).astype(F32)

    def lane_pad(g):
        return jnp.pad(g, (0, HEAD_PAD - QK_HEAD)).reshape(1, HEAD_PAD)

    for name, g in (("gq", q_norm.astype(F32)), ("gk", k_norm.astype(F32))):
        P[name] = lane_pad(g)
        P[name + "s"] = lane_pad(g[swap] * sign)
    P["gqa"] = q_a_norm.astype(F32).reshape(1, Q_LORA)
    P["gkva"] = kv_a_norm.astype(F32).reshape(1, KV_LORA)
    inv = 1.0 / (ROPE_THETA ** (jnp.arange(0, QK_ROPE, 2, dtype=F32) / QK_ROPE))
    ang = jnp.arange(L, dtype=F32)[:, None] * inv[None, :]
    cos, sin = jnp.cos(ang), jnp.sin(ang)
    P["ctab"] = jnp.concatenate([jnp.ones((L, QK_NOPE), F32), cos, cos, jnp.zeros((L, HEAD_PAD - QK_HEAD), F32)], axis=1)
    P["stab"] = jnp.concatenate([jnp.zeros((L, QK_NOPE), F32), sin, sin, jnp.zeros((L, HEAD_PAD - QK_HEAD), F32)], axis=1)

    P["w_hy_o"] = w_hy_o.astype(BF16)
    P["w_mla_o"] = w_mla_o.astype(BF16)
    P["w_out"] = w_out.astype(BF16)
    P["norm_ffn"] = norm_ffn.astype(F32).reshape(1, D_MODEL)
    rw = jnp.pad(router_w.astype(F32), ((0, 0), (0, LANE - N_EXPERTS)))
    P["router_w_hi"] = rw.astype(BF16)
    P["router_w_lo"] = (rw - P["router_w_hi"].astype(F32)).astype(BF16)
    P["router_b"] = jnp.pad(router_b.astype(F32), (0, LANE - N_EXPERTS)).reshape(1, LANE)
    P["w_gu"] = w_gu.astype(F32)
    P["b_gu"] = b_gu.astype(F32).reshape(N_EXPERTS, 1, 2 * D_FF)
    P["w_down"] = w_down.astype(F32)
    P["b_down"] = b_down.astype(F32).reshape(N_EXPERTS, 1, D_MODEL)
    return P


def _tile(n, want):
    t = min(n, want)
    assert n % t == 0
    return t


def _layer(xs, norm_mix, w_in, hy_conv_w, hy_conv_b, hy_fw1, hy_fb1, hy_fw2, hy_fb2, hy_fw3, hy_freq, hy_decay, hy_skip,
           q_a_norm, w_uq, kv_a_norm, w_ukv, q_norm, k_norm, w_hy_o, w_mla_o, w_out,
           norm_ffn, router_w, router_b, w_gu, b_gu, w_down, b_down):
    L = xs[0].shape[1]
    assert all(x.shape[1] == L and x.shape[2] == D_MODEL for x in xs) and L % MXU == 0
    P = _prep_params(norm_mix, w_in, q_a_norm, w_uq, kv_a_norm, w_ukv, q_norm, k_norm, w_hy_o, w_mla_o, w_out,
                     norm_ffn, router_w, router_b, w_gu, b_gu, w_down, b_down, L)
    kf = _hyena_filters(L, hy_fw1, hy_fb1, hy_fw2, hy_fb2, hy_fw3, hy_freq, hy_decay)
    tm = _tile(L, 512)
    proj = [_inproj(x, P["norm_mix"], P["w_hy_t"], P["w_m"], P["w_g"], tm) for x in xs]
    y_ts = _hyena([u_t for u_t, _, _ in proj], kf, hy_conv_w, hy_conv_b, hy_skip, L)
    mixed = []
    for x, (_, m, g), y_t in zip(xs, proj, y_ts):
        q, k, v = _mla_prep(m, P, tm)
        y_mla = _attention(q, k, v, _tile(L, 512))
        mixed.append(_merge(x, y_t, y_mla, g, P, tm))
    order = sorted(range(len(xs)), key=lambda n: -xs[n].shape[0])
    moe_out = _moe([mixed[n] for n in order], P, tm)
    outs = [None] * len(xs)
    for n, y in zip(order, moe_out):
        outs[n] = y.reshape(xs[n].shape)
    return outs


def kernel(x_prompt, x_sample, norm_mix, w_in, hy_conv_w, hy_conv_b, hy_fw1, hy_fb1, hy_fw2, hy_fb2, hy_fw3, hy_freq, hy_decay, hy_skip, q_a_norm, w_uq, kv_a_norm, w_ukv, q_norm, k_norm, w_hy_o, w_mla_o, w_out, norm_ffn, router_w, router_b, w_gu, b_gu, w_down, b_down):
    depth = norm_mix.shape[0]
    xs = [x_prompt, x_sample]
    weights = (norm_mix, w_in, hy_conv_w, hy_conv_b, hy_fw1, hy_fb1, hy_fw2, hy_fb2, hy_fw3, hy_freq, hy_decay, hy_skip,
               q_a_norm, w_uq, kv_a_norm, w_ukv, q_norm, k_norm, w_hy_o, w_mla_o, w_out,
               norm_ffn, router_w, router_b, w_gu, b_gu, w_down, b_down)
    for l in range(depth):
        xs = _layer(xs, *[w[l] for w in weights])
    return (xs[0], xs[1])
```

```python
import functools
import math

import jax
import jax.numpy as jnp
from jax import lax
from jax.experimental import pallas as pl
from jax.experimental.pallas import tpu as pltpu
from jax.experimental.pallas import tpu_sc as plsc

F32 = jnp.float32
BF16 = jnp.bfloat16

D_MODEL = 1024
HY_WIDTH = 512
HY_ORDER = 2
HY_EMB_BANDS = 16
HY_FILTER_DIM = 64
MLA_HEADS = 8
V_HEAD = 64
QK_NOPE = 64
QK_ROPE = 32
QK_HEAD = QK_NOPE + QK_ROPE
Q_LORA = 256
KV_LORA = 128
ROPE_THETA = 10000.0
HY_END = 3 * HY_WIDTH
Q_END = HY_END + Q_LORA
KV_END = Q_END + KV_LORA
KR_END = KV_END + QK_ROPE
G1_END = KR_END + D_MODEL
N_EXPERTS = 32
TOP_K = 4
D_FF = D_MODEL
SWIGLU_LIMIT = 7.0
SWIGLU_ALPHA = 1.702
EPS = 1e-6

LANE = 128
MXU = 256
HEAD_PAD = LANE
MLA_IN = 512
CONV_CB = 8
CONV_UNROLL = 4
MOE_BLOCK = 512
PACKED = D_MODEL // 2
SC_CHUNK = 64
VMEM_LIMIT = 52 * 1024 * 1024


def _cparams(n_axes):
    return pltpu.CompilerParams(dimension_semantics=("arbitrary",) * n_axes, vmem_limit_bytes=VMEM_LIMIT)


def _full(shape):
    n = len(shape)
    return pl.BlockSpec(shape, lambda *_: (0,) * n)


def _inproj_kernel(x_ref, g_ref, why_ref, wm_ref, wg_ref, u_ref, m_ref, gate_ref):
    x = x_ref[...]
    r = lax.rsqrt(jnp.mean(x * x, axis=-1, keepdims=True) + EPS)
    h = (x * r * g_ref[...]).astype(BF16)
    u_ref[...] = lax.dot_general(why_ref[...], h, (((1,), (1,)), ((), ())), preferred_element_type=F32)
    m_ref[...] = jnp.dot(h, wm_ref[...], preferred_element_type=F32).astype(BF16)
    gate_ref[...] = jnp.dot(h, wg_ref[...], preferred_element_type=F32).astype(BF16)


def _inproj(x, g, w_hy_t, w_m, w_g, tm):
    B, L, _ = x.shape
    return pl.pallas_call(
        _inproj_kernel,
        grid=(B, L // tm),
        in_specs=[
            pl.BlockSpec((None, tm, D_MODEL), lambda b, i: (b, i, 0)),
            _full((1, D_MODEL)),
            _full(w_hy_t.shape),
            _full(w_m.shape),
            _full(w_g.shape),
        ],
        out_specs=[
            pl.BlockSpec((None, HY_END, tm), lambda b, i: (b, 0, i)),
            pl.BlockSpec((None, tm, MLA_IN), lambda b, i: (b, i, 0)),
            pl.BlockSpec((None, tm, 2 * D_MODEL), lambda b, i: (b, i, 0)),
        ],
        out_shape=[
            jax.ShapeDtypeStruct((B, HY_END, L), F32),
            jax.ShapeDtypeStruct((B, L, MLA_IN), BF16),
            jax.ShapeDtypeStruct((B, L, 2 * D_MODEL), BF16),
        ],
        compiler_params=_cparams(2),
        name="inproj",
    )(x, g, w_hy_t, w_m, w_g)


def _filter_kernel(bands_ref, w1t_ref, w1c_ref, w1s_ref, b1_ref, w2_ref, b2_ref, fr_ref, w3f_ref, w3b_ref, dec_ref,
                   out_ref, *, L):
    hi = lax.Precision.HIGHEST
    n = 2 * L
    j = lax.broadcasted_iota(jnp.int32, (1, n), 1)
    pos = jnp.abs(j - L).astype(F32)
    t01 = pos / max(L - 1, 1)
    ang = ((2.0 * math.pi / L) * pos) * bands_ref[...]
    pre = (w1t_ref[...] * t01
           + jnp.dot(w1c_ref[...], jnp.cos(ang), precision=hi, preferred_element_type=F32)
           - jnp.dot(w1s_ref[...], jnp.sin(ang), precision=hi, preferred_element_type=F32)
           + b1_ref[...])
    z = jnp.sin(fr_ref[0] * pre)
    z = jnp.sin(fr_ref[1] * (jnp.dot(w2_ref[...], z, precision=hi, preferred_element_type=F32) + b2_ref[...]))
    win = jnp.exp(-t01 * jnp.abs(dec_ref[...]))
    h_b = jnp.dot(w3b_ref[...], z[:, :L], precision=hi, preferred_element_type=F32) * win[:, :L]
    h_f = jnp.dot(w3f_ref[...], z[:, L:], precision=hi, preferred_element_type=F32) * win[:, L:]
    out_ref[:, :L] = jnp.where(j[:, :L] == 0, 0.0, h_b)
    out_ref[:, L:] = h_f


def _hyena_filters(L, fw1, fb1, fw2, fb2, fw3, freq, decay):
    rows = HY_ORDER * HY_WIDTH
    rb = 256
    bands = jnp.linspace(1e-4, HY_EMB_BANDS - 1, HY_EMB_BANDS, dtype=F32).reshape(HY_EMB_BANDS, 1)
    w1 = fw1.astype(F32).T
    w3 = fw3.astype(F32).T
    fd = HY_FILTER_DIM
    return pl.pallas_call(
        functools.partial(_filter_kernel, L=L),
        grid=(rows // rb,),
        in_specs=[
            _full((HY_EMB_BANDS, 1)),
            _full((fd, 1)),
            _full((fd, HY_EMB_BANDS)),
            _full((fd, HY_EMB_BANDS)),
            _full((fd, 1)),
            _full((fd, fd)),
            _full((fd, 1)),
            _full((2, fd, 1)),
            pl.BlockSpec((rb, fd), lambda i: (i, 0)),
            pl.BlockSpec((rb, fd), lambda i: (i, 0)),
            pl.BlockSpec((rb, 1), lambda i: (i, 0)),
        ],
        out_specs=pl.BlockSpec((rb, 2 * L), lambda i: (i, 0)),
        out_shape=jax.ShapeDtypeStruct((rows, 2 * L), F32),
        compiler_params=_cparams(1),
        name="hyena_filters",
    )(bands, w1[:, :1], w1[:, 1:1 + HY_EMB_BANDS], w1[:, 1 + HY_EMB_BANDS:], fb1.astype(F32).reshape(fd, 1),
      fw2.astype(F32).T, fb2.astype(F32).reshape(fd, 1), freq.astype(F32).reshape(2, fd, 1),
      w3[:rows], w3[rows:], decay.astype(F32).reshape(rows, 1))


def _shift_conv(u, w0, w1, w2, b):
    L = u.shape[1]
    lane = lax.broadcasted_iota(jnp.int32, (1, L), 1)
    w0_row = jnp.where(lane == 0, 0.0, w0)
    w2_row = jnp.where(lane == L - 1, 0.0, w2)
    return pltpu.roll(u, 1, 1) * w0_row + u * w1 + pltpu.roll(u, L - 1, 1) * w2_row + b


def _toeplitz_conv(z, tz_ref, L):
    B = z.shape[0]
    nj = L // MXU
    zb = z.astype(BF16)
    tiles = [zb[:, MXU * J:MXU * (J + 1)] for J in range(nj)]
    acc = [None] * nj
    for d in range(-(nj - 1), nj):
        js = [J for J in range(nj) if 0 <= J + d < nj]
        lhs = tiles[js[0]] if len(js) == 1 else jnp.concatenate([tiles[J] for J in js], axis=0)
        res = jnp.dot(lhs, tz_ref[:, L + MXU * d:L + MXU * (d + 1)], preferred_element_type=F32)
        for n, J in enumerate(js):
            part = res[n * B:(n + 1) * B]
            acc[J + d] = part if acc[J + d] is None else acc[J + d] + part
    return acc[0] if nj == 1 else jnp.concatenate(acc, axis=1)


def _hyena_kernel(cw_ref, cb_ref, skip_ref, kf_ref, *refs, L, n_groups):
    in_refs = refs[:3 * n_groups]
    out_refs = refs[3 * n_groups:4 * n_groups]
    tz_ref = refs[4 * n_groups]
    c0 = pl.program_id(0) * CONV_CB

    def channels(p, carry):
        for u in range(CONV_UNROLL):
            c = p * CONV_UNROLL + u
            ch = c0 + c
            for o in range(HY_ORDER):
                row = jnp.broadcast_to(kf_ref[o, pl.ds(c, 1), :], (MXU, 2 * L))
                tz_ref[u, o] = pltpu.roll(row, 0, 1, stride=1, stride_axis=0).astype(BF16)
            for g in range(n_groups):
                stage = refs[4 * n_groups + 1 + g]
                sec = []
                for s, ref in enumerate(in_refs[3 * g:3 * g + 3]):
                    stage[u, s] = ref[:, c, :]
                    k = s * HY_WIDTH + ch
                    sec.append(_shift_conv(stage[u, s], cw_ref[0, k], cw_ref[1, k], cw_ref[2, k], cb_ref[k]))
                x1, x2, z = sec
                for o, gate in enumerate((x1, x2)):
                    y = _toeplitz_conv(z, tz_ref.at[u, o], L) + z * skip_ref[o, ch]
                    z = gate * y
                out_refs[g][:, c, :] = z
        return carry

    lax.fori_loop(0, CONV_CB // CONV_UNROLL, channels, 0)


def _hyena(u_ts, kf, conv_w, conv_b, skip, L):
    n_groups = len(u_ts)
    nblk = HY_WIDTH // CONV_CB
    smem = pl.BlockSpec(memory_space=pltpu.SMEM)
    in_specs = [smem, smem, smem, pl.BlockSpec((HY_ORDER, CONV_CB, 2 * L), lambda i: (0, i, 0))]
    args = [conv_w.astype(F32), conv_b.astype(F32), skip.astype(F32), kf.reshape(HY_ORDER, HY_WIDTH, 2 * L)]
    out_specs, out_shape = [], []
    for u in u_ts:
        B = u.shape[0]
        for s in range(3):
            in_specs.append(pl.BlockSpec((B, CONV_CB, L), lambda i, s=s: (0, i + s * nblk, 0)))
            args.append(u)
        out_specs.append(pl.BlockSpec((B, CONV_CB, L), lambda i: (0, i, 0)))
        out_shape.append(jax.ShapeDtypeStruct((B, HY_WIDTH, L), F32))
    return pl.pallas_call(
        functools.partial(_hyena_kernel, L=L, n_groups=n_groups),
        grid=(nblk,),
        in_specs=in_specs,
        out_specs=out_specs,
        out_shape=out_shape,
        scratch_shapes=[pltpu.VMEM((CONV_UNROLL, HY_ORDER, MXU, 2 * L), BF16)]
        + [pltpu.VMEM((CONV_UNROLL, 3, u.shape[0], L), F32) for u in u_ts],
        compiler_params=_cparams(1),
        name="hyena_conv",
    )(*args)


def _rms(x, n):
    return lax.rsqrt(jnp.sum(x * x, axis=-1, keepdims=True) * (1.0 / n) + EPS)


def _mla_prep_kernel(m_ref, ct_ref, st_ref, gqa_ref, gkva_ref, gq_ref, gqs_ref, gk_ref, gks_ref,
                     wqa_ref, wqb_ref, wka_ref, ea_ref, eb_ref, wv_ref, q_ref, k_ref, v_ref):
    m = m_ref[0].astype(F32)
    cq = m[:, :Q_LORA]
    ckv = m[:, Q_LORA:Q_LORA + KV_LORA]
    kr = m_ref[0][:, Q_LORA + KV_LORA:Q_LORA + KV_LORA + LANE]
    cqn = (cq * _rms(cq, Q_LORA) * gqa_ref[...]).astype(BF16)
    ckvn = (ckv * _rms(ckv, KV_LORA) * gkva_ref[...]).astype(BF16)
    qa = jnp.dot(cqn, wqa_ref[...], preferred_element_type=F32)
    qb = jnp.dot(cqn, wqb_ref[...], preferred_element_type=F32)
    ka = jnp.dot(ckvn, wka_ref[...], preferred_element_type=F32) + jnp.dot(kr, ea_ref[...], preferred_element_type=F32)
    kb = jnp.dot(kr, eb_ref[...], preferred_element_type=F32)
    v_ref[0] = jnp.dot(ckvn, wv_ref[...], preferred_element_type=F32).astype(BF16)
    ct, st = ct_ref[...], st_ref[...]
    scale = QK_HEAD ** -0.5 * math.log2(math.e)
    qc, qs = gq_ref[...] * ct * scale, gqs_ref[...] * st * scale
    kc, ks = gk_ref[...] * ct, gks_ref[...] * st
    for h in range(MLA_HEADS):
        sl = slice(h * HEAD_PAD, (h + 1) * HEAD_PAD)
        q_ref[0, h] = (_rms(qa[:, sl], QK_HEAD) * (qa[:, sl] * qc + qb[:, sl] * qs)).astype(BF16)
        k_ref[0, h] = (_rms(ka[:, sl], QK_HEAD) * (ka[:, sl] * kc + kb[:, sl] * ks)).astype(BF16)


def _mla_prep(m, P, tm):
    B, L, _ = m.shape
    w = [P[n] for n in ("wqa", "wqb", "wka", "ea", "eb", "wv")]
    g = [P[n] for n in ("gqa", "gkva", "gq", "gqs", "gk", "gks")]
    hp = MLA_HEADS * HEAD_PAD
    return pl.pallas_call(
        _mla_prep_kernel,
        grid=(B, L // tm),
        in_specs=[pl.BlockSpec((1, tm, MLA_IN), lambda b, i: (b, i, 0)),
                  pl.BlockSpec((tm, HEAD_PAD), lambda b, i: (i, 0)),
                  pl.BlockSpec((tm, HEAD_PAD), lambda b, i: (i, 0))]
        + [_full(a.shape) for a in g] + [_full(a.shape) for a in w],
        out_specs=[
            pl.BlockSpec((1, MLA_HEADS, tm, HEAD_PAD), lambda b, i: (b, 0, i, 0)),
            pl.BlockSpec((1, MLA_HEADS, tm, HEAD_PAD), lambda b, i: (b, 0, i, 0)),
            pl.BlockSpec((1, tm, MLA_HEADS * V_HEAD), lambda b, i: (b, i, 0)),
        ],
        out_shape=[
            jax.ShapeDtypeStruct((B, MLA_HEADS, L, HEAD_PAD), BF16),
            jax.ShapeDtypeStruct((B, MLA_HEADS, L, HEAD_PAD), BF16),
            jax.ShapeDtypeStruct((B, L, MLA_HEADS * V_HEAD), BF16),
        ],
        compiler_params=_cparams(2),
        name="mla_prep",
    )(m, P["ctab"][:L], P["stab"][:L], *g, *w)


def _attn_kernel(q_ref, k_ref, v_ref, o_ref):
    L = v_ref.shape[1]
    one_col = jnp.where(lax.broadcasted_iota(jnp.int32, (L, LANE), 1) == 0, 1.0, 0.0).astype(BF16)
    for pair in range(MLA_HEADS // 2):
        lanes = pl.ds(pair * LANE, LANE)
        v_ext = jnp.concatenate([v_ref[0, :, lanes], one_col], axis=1)
        outs = []
        for h in (2 * pair, 2 * pair + 1):
            s = lax.dot_general(q_ref[0, h], k_ref[0, h], (((1,), (1,)), ((), ())), preferred_element_type=F32)
            p = jnp.exp2(s - jnp.max(s, axis=-1, keepdims=True)).astype(BF16)
            pv = jnp.dot(p, v_ext, preferred_element_type=F32)
            outs.append(pv[:, :LANE] / pv[:, LANE:LANE + 1])
        lane = lax.broadcasted_iota(jnp.int32, outs[0].shape, 1)
        o_ref[0, :, lanes] = jnp.where(lane < V_HEAD, outs[0], outs[1]).astype(BF16)


def _attention(q, k, v, tq):
    B, H, L, _ = q.shape
    return pl.pallas_call(
        _attn_kernel,
        grid=(B, L // tq),
        in_specs=[
            pl.BlockSpec((1, H, tq, HEAD_PAD), lambda b, i: (b, 0, i, 0)),
            pl.BlockSpec((1, H, L, HEAD_PAD), lambda b, i: (b, 0, 0, 0)),
            pl.BlockSpec((1, L, H * V_HEAD), lambda b, i: (b, 0, 0)),
        ],
        out_specs=pl.BlockSpec((1, tq, H * V_HEAD), lambda b, i: (b, i, 0)),
        out_shape=jax.ShapeDtypeStruct((B, L, H * V_HEAD), BF16),
        compiler_params=_cparams(2),
        name="mla_attention",
    )(q, k, v)


def _pack_rows(a):
    bits = lax.bitcast_convert_type(a.astype(BF16).astype(F32), jnp.uint32)
    return (bits[:, :PACKED] >> 16) | (bits[:, PACKED:] & jnp.uint32(0xFFFF0000))


def _unpack_rows(p):
    lo = lax.bitcast_convert_type(p << 16, F32)
    hi = lax.bitcast_convert_type(p & jnp.uint32(0xFFFF0000), F32)
    return jnp.concatenate([lo, hi], axis=1)


def _merge_kernel(x_ref, yh_ref, ym_ref, g_ref, who_ref, wmo_ref, wo_ref, nf_ref, rwh_ref, rwl_ref, rb_ref,
                  x1_ref, h2_ref, ids_ref, gate_ref, cnt_ref):
    tm = x_ref.shape[0]
    sub = tm // 2

    @pl.when(jnp.logical_and(pl.program_id(0) == 0, pl.program_id(1) == 0))
    def _():
        cnt_ref[...] = jnp.zeros_like(cnt_ref)

    eid = lax.broadcasted_iota(jnp.int32, (N_EXPERTS, sub), 0)
    for r in range(0, tm, sub):
        rows = pl.ds(r, sub)
        g = g_ref[rows, :].astype(F32)
        a = lax.dot_general(yh_ref[:, rows].astype(BF16), who_ref[...], (((0,), (0,)), ((), ())),
                            preferred_element_type=F32)
        b = jnp.dot(ym_ref[rows, :], wmo_ref[...], preferred_element_type=F32)
        merged = jax.nn.sigmoid(g[:, :D_MODEL]) * a + jax.nn.sigmoid(g[:, D_MODEL:]) * b
        x1 = x_ref[rows, :] + jnp.dot(merged.astype(BF16), wo_ref[...], preferred_element_type=F32)
        x1_ref[rows, :] = x1
        h2 = x1 * lax.rsqrt(jnp.mean(x1 * x1, axis=-1, keepdims=True) + EPS) * nf_ref[...]
        h2_ref[rows, :] = _pack_rows(h2)
        h_hi = h2.astype(BF16)
        h_lo = (h2 - h_hi.astype(F32)).astype(BF16)
        logits = (jnp.dot(h_hi, rwh_ref[...], preferred_element_type=F32)
                  + jnp.dot(h_lo, rwh_ref[...], preferred_element_type=F32)
                  + jnp.dot(h_hi, rwl_ref[...], preferred_element_type=F32) + rb_ref[...])
        work = logits.T[:N_EXPERTS]
        picked = jnp.zeros((N_EXPERTS, sub), F32)
        vals, firsts = [], []
        for k in range(TOP_K):
            m = jnp.max(work, axis=0, keepdims=True)
            first = jnp.min(jnp.where(work == m, eid, N_EXPERTS), axis=0, keepdims=True)
            hot = eid == first
            vals.append(m)
            firsts.append(first)
            picked = jnp.where(hot, 1.0, picked)
            work = jnp.where(hot, -jnp.inf, work)
        ex = [jnp.exp(v - vals[0]) for v in vals]
        inv = 1.0 / (ex[0] + ex[1] + ex[2] + ex[3])
        pad = jnp.zeros((LANE - TOP_K, sub), F32)
        ids_t = jnp.concatenate([f.astype(F32) for f in firsts] + [pad], axis=0)
        gates_t = jnp.concatenate([e * inv for e in ex] + [pad], axis=0)
        ids_ref[rows, :] = ids_t.T.astype(jnp.int32)
        gate_ref[rows, :] = gates_t.T
        cnt_ref[...] += jnp.sum(picked, axis=1, keepdims=True)


def _merge(x, yh_t, ym, g, P, tm):
    B, L, _ = x.shape
    nt = L // tm
    w = [P[n] for n in ("w_hy_o", "w_mla_o", "w_out", "norm_ffn", "router_w_hi", "router_w_lo", "router_b")]
    return pl.pallas_call(
        _merge_kernel,
        grid=(B, nt),
        in_specs=[
            pl.BlockSpec((None, tm, D_MODEL), lambda b, i: (b, i, 0)),
            pl.BlockSpec((None, HY_WIDTH, tm), lambda b, i: (b, 0, i)),
            pl.BlockSpec((None, tm, MLA_HEADS * V_HEAD), lambda b, i: (b, i, 0)),
            pl.BlockSpec((None, tm, 2 * D_MODEL), lambda b, i: (b, i, 0)),
        ] + [_full(a.shape) for a in w],
        out_specs=[
            pl.BlockSpec((tm, D_MODEL), lambda b, i: (b * nt + i, 0)),
            pl.BlockSpec((tm, PACKED), lambda b, i: (b * nt + i, 0)),
            pl.BlockSpec((tm, LANE), lambda b, i: (b * nt + i, 0)),
            pl.BlockSpec((tm, LANE), lambda b, i: (b * nt + i, 0)),
            pl.BlockSpec((N_EXPERTS, 1), lambda b, i: (0, 0)),
        ],
        out_shape=[
            jax.ShapeDtypeStruct((B * L, D_MODEL), F32),
            jax.ShapeDtypeStruct((B * L, PACKED), jnp.uint32),
            jax.ShapeDtypeStruct((B * L, LANE), jnp.int32),
            jax.ShapeDtypeStruct((B * L, LANE), F32),
            jax.ShapeDtypeStruct((N_EXPERTS, 1), F32),
        ],
        compiler_params=_cparams(2),
        name="merge_router",
    )(x, yh_t, ym, g, *w)


def _expert_kernel(be_ref, na_ref, first_ref, slot_ref, nxt_ref, x_ref, wgu_hbm, bgu_ref, wd_hbm, bd_ref, y_ref,
                   wgu_buf, wd_buf, sems):
    i = pl.program_id(0)
    active = i < na_ref[0]
    slot = slot_ref[i]

    def weight_copies(expert, s):
        return (pltpu.make_async_copy(wgu_hbm.at[expert], wgu_buf.at[s], sems.at[0, s]),
                pltpu.make_async_copy(wd_hbm.at[expert], wd_buf.at[s], sems.at[1, s]))

    @pl.when(i == 0)
    def _():
        for cp in weight_copies(be_ref[0], 0):
            cp.start()

    @pl.when(jnp.logical_and(active, first_ref[i] == 1))
    def _():
        for cp in weight_copies(be_ref[i], slot):
            cp.wait()

        @pl.when(nxt_ref[i] >= 0)
        def _():
            for cp in weight_copies(nxt_ref[i], 1 - slot):
                cp.start()

    @pl.when(active)
    def _():
        x = _unpack_rows(x_ref[...])
        gu = jnp.dot(x, wgu_buf[slot], preferred_element_type=F32) + bgu_ref[0]
        gate = jnp.minimum(gu[:, :D_FF], SWIGLU_LIMIT)
        up = jnp.clip(gu[:, D_FF:], -SWIGLU_LIMIT, SWIGLU_LIMIT)
        act = (up + 1.0) * (gate * jax.nn.sigmoid(SWIGLU_ALPHA * gate))
        y_ref[...] = _pack_rows(jnp.dot(act, wd_buf[slot], preferred_element_type=F32) + bd_ref[0])

    @pl.when(i >= na_ref[0])
    def _():
        y_ref[...] = jnp.zeros_like(y_ref)


def _experts(plan, xs, P):
    n_rows = xs.shape[0]
    nb = n_rows // MOE_BLOCK
    grid_spec = pltpu.PrefetchScalarGridSpec(
        num_scalar_prefetch=5,
        grid=(nb,),
        in_specs=[
            pl.BlockSpec((MOE_BLOCK, PACKED), lambda i, be, *_: (i, 0)),
            pl.BlockSpec(memory_space=pl.ANY),
            pl.BlockSpec((1, 1, 2 * D_FF), lambda i, be, *_: (be[i], 0, 0)),
            pl.BlockSpec(memory_space=pl.ANY),
            pl.BlockSpec((1, 1, D_MODEL), lambda i, be, *_: (be[i], 0, 0)),
        ],
        out_specs=pl.BlockSpec((MOE_BLOCK, PACKED), lambda i, be, *_: (i, 0)),
        scratch_shapes=[
            pltpu.VMEM((2, D_MODEL, 2 * D_FF), F32),
            pltpu.VMEM((2, D_FF, D_MODEL), F32),
            pltpu.SemaphoreType.DMA((2, 2)),
        ],
    )
    return pl.pallas_call(
        _expert_kernel,
        grid_spec=grid_spec,
        out_shape=jax.ShapeDtypeStruct((n_rows, PACKED), jnp.uint32),
        compiler_params=_cparams(1),
        name="moe_experts",
    )(*plan, xs, P["w_gu"], P["b_gu"], P["w_down"], P["b_down"])


def _combine_kernel(x1_ref, yk_ref, gt_ref, o_ref):
    gt = gt_ref[...]
    acc = _unpack_rows(yk_ref[0]) * gt[:, 0:1]
    for k in range(1, TOP_K):
        acc = acc + _unpack_rows(yk_ref[k]) * gt[:, k:k + 1]
    o_ref[...] = x1_ref[...] + acc


def _combine(x1, yk, gates, tm):
    T = x1.shape[0]
    return pl.pallas_call(
        _combine_kernel,
        grid=(T // tm,),
        in_specs=[
            pl.BlockSpec((tm, D_MODEL), lambda i: (i, 0)),
            pl.BlockSpec((TOP_K, tm, PACKED), lambda i: (0, i, 0)),
            pl.BlockSpec((tm, LANE), lambda i: (i, 0)),
        ],
        out_specs=pl.BlockSpec((tm, D_MODEL), lambda i: (i, 0)),
        out_shape=jax.ShapeDtypeStruct((T, D_MODEL), F32),
        compiler_params=_cparams(1),
        name="moe_combine",
    )(x1, yk, gates)


def _route_kernel(ids_ref, cnt_ref, dest_ref, run_ref, start_ref, *, tm):
    @pl.when(pl.program_id(0) == 0)
    def _():
        cnt = cnt_ref[...]
        padded = jnp.floor((cnt + (MOE_BLOCK - 1)) * (1.0 / MOE_BLOCK)) * MOE_BLOCK
        scan = jnp.broadcast_to(padded, (8, LANE))
        lane8 = lax.broadcasted_iota(jnp.int32, (8, LANE), 1)
        s = 1
        while s < LANE:
            scan = scan + jnp.where(lane8 >= s, pltpu.roll(scan, s, 1), 0.0)
            s *= 2
        start_ref[...] = scan[0:1] - padded
        run_ref[...] = jnp.zeros_like(run_ref)

    lane = lax.broadcasted_iota(jnp.int32, (tm, LANE), 1)
    ids = ids_ref[...]
    hots = [lane == ids[:, k:k + 1] for k in range(TOP_K)]
    chosen = jnp.zeros((tm, LANE), F32)
    for hot in hots:
        chosen = jnp.where(hot, 1.0, chosen)
    r = lax.broadcasted_iota(jnp.int32, (tm, tm), 0)
    c = lax.broadcasted_iota(jnp.int32, (tm, tm), 1)
    earlier = jnp.where(c < r, 1.0, 0.0).astype(BF16)
    before = jnp.dot(earlier, chosen.astype(BF16), preferred_element_type=F32) + (run_ref[...] + start_ref[...])
    dest = jnp.zeros((tm, LANE), F32)
    for k, hot in enumerate(hots):
        dest = jnp.where(lane == k, jnp.sum(jnp.where(hot, before, 0.0), axis=-1, keepdims=True), dest)
    dest_ref[...] = dest.astype(jnp.int32)
    run_ref[...] += jnp.sum(chosen, axis=0, keepdims=True)


def _route(ids, counts, tm):
    T = ids.shape[0]
    dest = pl.pallas_call(
        functools.partial(_route_kernel, tm=tm),
        grid=(T // tm,),
        in_specs=[pl.BlockSpec((tm, LANE), lambda i: (i, 0)), _full((1, LANE))],
        out_specs=pl.BlockSpec((tm, LANE), lambda i: (i, 0)),
        out_shape=jax.ShapeDtypeStruct((T, LANE), jnp.int32),
        scratch_shapes=[pltpu.VMEM((1, LANE), F32), pltpu.VMEM((1, LANE), F32)],
        compiler_params=_cparams(1),
        name="moe_route",
    )(ids, jnp.pad(counts[:, 0], (0, LANE - N_EXPERTS)).reshape(1, LANE))
    counts = counts[:, 0].astype(jnp.int32)
    pad_end = jnp.cumsum((counts + MOE_BLOCK - 1) // MOE_BLOCK * MOE_BLOCK)
    n_blocks = -(-(T * TOP_K + N_EXPERTS * (MOE_BLOCK - 1)) // MOE_BLOCK)
    block_start = jnp.arange(n_blocks, dtype=jnp.int32) * MOE_BLOCK
    block_e = jnp.sum((block_start[:, None] >= pad_end[None, :]).astype(jnp.int32), axis=1)
    block_e = jnp.minimum(block_e, N_EXPERTS - 1)
    n_active = (pad_end[-1] // MOE_BLOCK).astype(jnp.int32)
    blk = jnp.arange(n_blocks, dtype=jnp.int32)
    first = jnp.logical_and(blk < n_active, jnp.logical_or(blk == 0, block_e != jnp.roll(block_e, 1)))
    slot = (jnp.cumsum(first.astype(jnp.int32)) - 1) % 2
    run_end = pad_end[block_e] // MOE_BLOCK
    nxt = jnp.where(run_end < n_active, block_e[jnp.minimum(run_end, n_blocks - 1)], -1)
    plan = (block_e.astype(jnp.int32), n_active.reshape(1), first.astype(jnp.int32), slot.astype(jnp.int32),
            nxt.astype(jnp.int32))
    dest_flat = dest[:, :TOP_K].T.reshape(TOP_K * T)
    return dest_flat, plan, n_blocks * MOE_BLOCK


def _sc_workers():
    info = plsc.get_sparse_core_info()
    return info.num_cores, info.num_cores * info.num_subcores


def _sc_scatter_rows(rows, dest_flat, n_rows):
    T, W = rows.shape
    n_cores, n_workers = _sc_workers()
    per_w = T // n_workers
    assert per_w * n_workers == T and per_w % SC_CHUNK == 0

    def body(rows_hbm, dest_hbm, out_hbm, buf, *idx):
        wid = lax.axis_index("s") * n_cores + lax.axis_index("c")

        @pl.loop(0, per_w // SC_CHUNK)
        def _(j):
            off = pl.multiple_of(wid * per_w + j * SC_CHUNK, SC_CHUNK)
            pltpu.sync_copy(rows_hbm.at[pl.ds(off, SC_CHUNK)], buf)
            for k in range(TOP_K):
                pltpu.sync_copy(dest_hbm.at[pl.ds(k * T + off, SC_CHUNK)], idx[k])
            for k in range(TOP_K):
                pltpu.sync_copy(buf, out_hbm.at[idx[k]])

    return pl.kernel(
        body,
        out_type=jax.ShapeDtypeStruct((n_rows, W), rows.dtype),
        mesh=plsc.VectorSubcoreMesh(core_axis_name="c", subcore_axis_name="s"),
        scratch_types=[pltpu.VMEM((SC_CHUNK, W), rows.dtype)] + [pltpu.VMEM((SC_CHUNK,), jnp.int32)] * TOP_K,
        name="moe_dispatch_sc",
    )(rows, dest_flat)


def _sc_gather_rows(table, dest_flat):
    N = dest_flat.shape[0]
    W = table.shape[1]
    n_cores, n_workers = _sc_workers()
    per_w = N // n_workers
    assert per_w * n_workers == N and per_w % SC_CHUNK == 0

    def body(table_hbm, dest_hbm, out_hbm, buf, idx):
        wid = lax.axis_index("s") * n_cores + lax.axis_index("c")

        @pl.loop(0, per_w // SC_CHUNK)
        def _(j):
            off = pl.multiple_of(wid * per_w + j * SC_CHUNK, SC_CHUNK)
            pltpu.sync_copy(dest_hbm.at[pl.ds(off, SC_CHUNK)], idx)
            pltpu.sync_copy(table_hbm.at[idx], buf)
            pltpu.sync_copy(buf, out_hbm.at[pl.ds(off, SC_CHUNK)])

    return pl.kernel(
        body,
        out_type=jax.ShapeDtypeStruct((N, W), table.dtype),
        mesh=plsc.VectorSubcoreMesh(core_axis_name="c", subcore_axis_name="s"),
        scratch_types=[pltpu.VMEM((SC_CHUNK, W), table.dtype), pltpu.VMEM((SC_CHUNK,), jnp.int32)],
        name="moe_gather_sc",
    )(table, dest_flat)


def _moe(groups, P, tm):
    staged = []
    for x1, h2p, ids, gates, counts in groups:
        dest_flat, plan, n_rows = _route(ids, counts, tm)
        staged.append((dest_flat, plan, _sc_scatter_rows(h2p, dest_flat, n_rows)))
    ys = []
    for dest_flat, plan, xs in staged:
        if ys:
            xs, ys[-1] = lax.optimization_barrier((xs, ys[-1]))
        ys.append(_experts(plan, xs, P))
    outs = []
    for (x1, _, _, gates, _), (dest_flat, _, _), y in zip(groups, staged, ys):
        yk = _sc_gather_rows(y, dest_flat).reshape(TOP_K, x1.shape[0], PACKED)
        outs.append(_combine(x1, yk, gates, tm))
    return outs


def _prep_params(norm_mix, w_in, q_a_norm, w_uq, kv_a_norm, w_ukv, q_norm, k_norm, w_hy_o, w_mla_o, w_out,
                 norm_ffn, router_w, router_b, w_gu, b_gu, w_down, b_down, L):
    P = {}
    P["norm_mix"] = norm_mix.astype(F32).reshape(1, D_MODEL)
    P["w_hy_t"] = w_in[:, :HY_END].T.astype(BF16)
    P["w_m"] = jnp.pad(w_in[:, HY_END:KR_END], ((0, 0), (0, MLA_IN - (KR_END - HY_END)))).astype(BF16)
    P["w_g"] = w_in[:, KR_END:].astype(BF16)

    half = QK_ROPE // 2
    swap = jnp.concatenate([jnp.arange(QK_NOPE), QK_NOPE + half + jnp.arange(half), QK_NOPE + jnp.arange(half)])
    rope_lane = (jnp.arange(QK_HEAD) >= QK_NOPE)

    def slots(w):
        return jnp.pad(w, ((0, 0), (0, 0), (0, HEAD_PAD - QK_HEAD))).reshape(w.shape[0], MLA_HEADS * HEAD_PAD)

    wq = w_uq.astype(F32).reshape(Q_LORA, MLA_HEADS, QK_HEAD)
    P["wqa"] = slots(wq).astype(BF16)
    P["wqb"] = slots(jnp.where(rope_lane, wq[:, :, swap], 0.0)).astype(BF16)
    wkv = w_ukv.astype(F32).reshape(KV_LORA, MLA_HEADS, QK_NOPE + V_HEAD)
    P["wka"] = slots(jnp.pad(wkv[:, :, :QK_NOPE], ((0, 0), (0, 0), (0, QK_ROPE)))).astype(BF16)
    P["wv"] = wkv[:, :, QK_NOPE:].reshape(KV_LORA, MLA_HEADS * V_HEAD).astype(BF16)
    eye = jnp.eye(LANE, QK_ROPE, dtype=F32)
    e_a = jnp.pad(eye, ((0, 0), (QK_NOPE, 0)))
    e_b = e_a[:, swap] * rope_lane
    P["ea"] = slots(jnp.broadcast_to(e_a[:, None, :], (LANE, MLA_HEADS, QK_HEAD))).astype(BF16)
    P["eb"] = slots(jnp.broadcast_to(e_b[:, None, :], (LANE, MLA_HEADS, QK_HEAD))).astype(BF16)

    sign = jnp.concatenate([jnp.zeros(QK_NOPE), -jnp.ones(half), jnp.ones(half)]).astype(F32)

    def lane_pad(g):
        return jnp.pad(g, (0, HEAD_PAD - QK_HEAD)).reshape(1, HEAD_PAD)

    for name, g in (("gq", q_norm.astype(F32)), ("gk", k_norm.astype(F32))):
        P[name] = lane_pad(g)
        P[name + "s"] = lane_pad(g[swap] * sign)
    P["gqa"] = q_a_norm.astype(F32).reshape(1, Q_LORA)
    P["gkva"] = kv_a_norm.astype(F32).reshape(1, KV_LORA)
    inv = 1.0 / (ROPE_THETA ** (jnp.arange(0, QK_ROPE, 2, dtype=F32) / QK_ROPE))
    ang = jnp.arange(L, dtype=F32)[:, None] * inv[None, :]
    cos, sin = jnp.cos(ang), jnp.sin(ang)
    P["ctab"] = jnp.concatenate([jnp.ones((L, QK_NOPE), F32), cos, cos, jnp.zeros((L, HEAD_PAD - QK_HEAD), F32)], axis=1)
    P["stab"] = jnp.concatenate([jnp.zeros((L, QK_NOPE), F32), sin, sin, jnp.zeros((L, HEAD_PAD - QK_HEAD), F32)], axis=1)

    P["w_hy_o"] = w_hy_o.astype(BF16)
    P["w_mla_o"] = w_mla_o.astype(BF16)
    P["w_out"] = w_out.astype(BF16)
    P["norm_ffn"] = norm_ffn.astype(F32).reshape(1, D_MODEL)
    rw = jnp.pad(router_w.astype(F32), ((0, 0), (0, LANE - N_EXPERTS)))
    P["router_w_hi"] = rw.astype(BF16)
    P["router_w_lo"] = (rw - P["router_w_hi"].astype(F32)).astype(BF16)
    P["router_b"] = jnp.pad(router_b.astype(F32), (0, LANE - N_EXPERTS)).reshape(1, LANE)
    P["w_gu"] = w_gu.astype(F32)
    P["b_gu"] = b_gu.astype(F32).reshape(N_EXPERTS, 1, 2 * D_FF)
    P["w_down"] = w_down.astype(F32)
    P["b_down"] = b_down.astype(F32).reshape(N_EXPERTS, 1, D_MODEL)
    return P


def _tile(n, want):
    t = min(n, want)
    assert n % t == 0
    return t


def _layer(xs, norm_mix, w_in, hy_conv_w, hy_conv_b, hy_fw1, hy_fb1, hy_fw2, hy_fb2, hy_fw3, hy_freq, hy_decay, hy_skip,
           q_a_norm, w_uq, kv_a_norm, w_ukv, q_norm, k_norm, w_hy_o, w_mla_o, w_out,
           norm_ffn, router_w, router_b, w_gu, b_gu, w_down, b_down):
    L = xs[0].shape[1]
    assert all(x.shape[1] == L and x.shape[2] == D_MODEL for x in xs) and L % MXU == 0
    P = _prep_params(norm_mix, w_in, q_a_norm, w_uq, kv_a_norm, w_ukv, q_norm, k_norm, w_hy_o, w_mla_o, w_out,
                     norm_ffn, router_w, router_b, w_gu, b_gu, w_down, b_down, L)
    kf = _hyena_filters(L, hy_fw1, hy_fb1, hy_fw2, hy_fb2, hy_fw3, hy_freq, hy_decay)
    tm = _tile(L, 512)
    proj = [_inproj(x, P["norm_mix"], P["w_hy_t"], P["w_m"], P["w_g"], tm) for x in xs]
    y_ts = _hyena([u_t for u_t, _, _ in proj], kf, hy_conv_w, hy_conv_b, hy_skip, L)
    mixed = []
    for x, (_, m, g), y_t in zip(xs, proj, y_ts):
        q, k, v = _mla_prep(m, P, tm)
        y_mla = _attention(q, k, v, _tile(L, 512))
        mixed.append(_merge(x, y_t, y_mla, g, P, tm))
    order = sorted(range(len(xs)), key=lambda n: -xs[n].shape[0])
    moe_out = _moe([mixed[n] for n in order], P, tm)
    outs = [None] * len(xs)
    for n, y in zip(order, moe_out):
        outs[n] = y.reshape(xs[n].shape)
    return outs


def kernel(x_prompt, x_sample, norm_mix, w_in, hy_conv_w, hy_conv_b, hy_fw1, hy_fb1, hy_fw2, hy_fb2, hy_fw3, hy_freq, hy_decay, hy_skip, q_a_norm, w_uq, kv_a_norm, w_ukv, q_norm, k_norm, w_hy_o, w_mla_o, w_out, norm_ffn, router_w, router_b, w_gu, b_gu, w_down, b_down):
    depth = norm_mix.shape[0]
    xs = [x_prompt, x_sample]
    weights = (norm_mix, w_in, hy_conv_w, hy_conv_b, hy_fw1, hy_fb1, hy_fw2, hy_fb2, hy_fw3, hy_freq, hy_decay, hy_skip,
               q_a_norm, w_uq, kv_a_norm, w_ukv, q_norm, k_norm, w_hy_o, w_mla_o, w_out,
               norm_ffn, router_w, router_b, w_gu, b_gu, w_down, b_down)
    for l in range(depth):
        xs = _layer(xs, *[w[l] for w in weights])
    return (xs[0], xs[1])
```

```python
import functools
import math

import jax
import jax.numpy as jnp
from jax import lax
from jax.experimental import pallas as pl
from jax.experimental.pallas import tpu as pltpu
from jax.experimental.pallas import tpu_sc as plsc

F32 = jnp.float32
BF16 = jnp.bfloat16

D_MODEL = 1024
HY_WIDTH = 512
HY_ORDER = 2
HY_EMB_BANDS = 16
HY_FILTER_DIM = 64
MLA_HEADS = 8
V_HEAD = 64
QK_NOPE = 64
QK_ROPE = 32
QK_HEAD = QK_NOPE + QK_ROPE
Q_LORA = 256
KV_LORA = 128
ROPE_THETA = 10000.0
HY_END = 3 * HY_WIDTH
Q_END = HY_END + Q_LORA
KV_END = Q_END + KV_LORA
KR_END = KV_END + QK_ROPE
G1_END = KR_END + D_MODEL
N_EXPERTS = 32
TOP_K = 4
D_FF = D_MODEL
SWIGLU_LIMIT = 7.0
SWIGLU_ALPHA = 1.702
EPS = 1e-6

LANE = 128
MXU = 256
HEAD_PAD = LANE
MLA_IN = 512
CONV_CB = 8
CONV_UNROLL = 4
MOE_BLOCK = 512
PACKED = D_MODEL // 2
SC_CHUNK = 64
VMEM_LIMIT = 52 * 1024 * 1024


def _cparams(n_axes):
    return pltpu.CompilerParams(dimension_semantics=("arbitrary",) * n_axes, vmem_limit_bytes=VMEM_LIMIT)


def _full(shape):
    n = len(shape)
    return pl.BlockSpec(shape, lambda *_: (0,) * n)


def _inproj_kernel(x_ref, g_ref, why_ref, wm_ref, wg_ref, u_ref, m_ref, gate_ref):
    x = x_ref[...]
    r = lax.rsqrt(jnp.mean(x * x, axis=-1, keepdims=True) + EPS)
    h = (x * r * g_ref[...]).astype(BF16)
    u_ref[...] = lax.dot_general(why_ref[...], h, (((1,), (1,)), ((), ())), preferred_element_type=F32)
    m_ref[...] = jnp.dot(h, wm_ref[...], preferred_element_type=F32).astype(BF16)
    gate_ref[...] = jnp.dot(h, wg_ref[...], preferred_element_type=F32).astype(BF16)


def _inproj(x, g, w_hy_t, w_m, w_g, tm):
    B, L, _ = x.shape
    return pl.pallas_call(
        _inproj_kernel,
        grid=(B, L // tm),
        in_specs=[
            pl.BlockSpec((None, tm, D_MODEL), lambda b, i: (b, i, 0)),
            _full((1, D_MODEL)),
            _full(w_hy_t.shape),
            _full(w_m.shape),
            _full(w_g.shape),
        ],
        out_specs=[
            pl.BlockSpec((None, HY_END, tm), lambda b, i: (b, 0, i)),
            pl.BlockSpec((None, tm, MLA_IN), lambda b, i: (b, i, 0)),
            pl.BlockSpec((None, tm, 2 * D_MODEL), lambda b, i: (b, i, 0)),
        ],
        out_shape=[
            jax.ShapeDtypeStruct((B, HY_END, L), F32),
            jax.ShapeDtypeStruct((B, L, MLA_IN), BF16),
            jax.ShapeDtypeStruct((B, L, 2 * D_MODEL), BF16),
        ],
        compiler_params=_cparams(2),
        name="inproj",
    )(x, g, w_hy_t, w_m, w_g)


def _filter_kernel(bands_ref, w1t_ref, w1c_ref, w1s_ref, b1_ref, w2_ref, b2_ref, fr_ref, w3f_ref, w3b_ref, dec_ref,
                   out_ref, *, L):
    hi = lax.Precision.HIGHEST
    n = 2 * L
    j = lax.broadcasted_iota(jnp.int32, (1, n), 1)
    pos = jnp.abs(j - L).astype(F32)
    t01 = pos / max(L - 1, 1)
    ang = ((2.0 * math.pi / L) * pos) * bands_ref[...]
    pre = (w1t_ref[...] * t01
           + jnp.dot(w1c_ref[...], jnp.cos(ang), precision=hi, preferred_element_type=F32)
           - jnp.dot(w1s_ref[...], jnp.sin(ang), precision=hi, preferred_element_type=F32)
           + b1_ref[...])
    z = jnp.sin(fr_ref[0] * pre)
    z = jnp.sin(fr_ref[1] * (jnp.dot(w2_ref[...], z, precision=hi, preferred_element_type=F32) + b2_ref[...]))
    win = jnp.exp(-t01 * jnp.abs(dec_ref[...]))
    h_b = jnp.dot(w3b_ref[...], z[:, :L], precision=hi, preferred_element_type=F32) * win[:, :L]
    h_f = jnp.dot(w3f_ref[...], z[:, L:], precision=hi, preferred_element_type=F32) * win[:, L:]
    out_ref[:, :L] = jnp.where(j[:, :L] == 0, 0.0, h_b)
    out_ref[:, L:] = h_f


def _hyena_filters(L, fw1, fb1, fw2, fb2, fw3, freq, decay):
    rows = HY_ORDER * HY_WIDTH
    rb = 256
    bands = jnp.linspace(1e-4, HY_EMB_BANDS - 1, HY_EMB_BANDS, dtype=F32).reshape(HY_EMB_BANDS, 1)
    w1 = fw1.astype(F32).T
    w3 = fw3.astype(F32).T
    fd = HY_FILTER_DIM
    return pl.pallas_call(
        functools.partial(_filter_kernel, L=L),
        grid=(rows // rb,),
        in_specs=[
            _full((HY_EMB_BANDS, 1)),
            _full((fd, 1)),
            _full((fd, HY_EMB_BANDS)),
            _full((fd, HY_EMB_BANDS)),
            _full((fd, 1)),
            _full((fd, fd)),
            _full((fd, 1)),
            _full((2, fd, 1)),
            pl.BlockSpec((rb, fd), lambda i: (i, 0)),
            pl.BlockSpec((rb, fd), lambda i: (i, 0)),
            pl.BlockSpec((rb, 1), lambda i: (i, 0)),
        ],
        out_specs=pl.BlockSpec((rb, 2 * L), lambda i: (i, 0)),
        out_shape=jax.ShapeDtypeStruct((rows, 2 * L), F32),
        compiler_params=_cparams(1),
        name="hyena_filters",
    )(bands, w1[:, :1], w1[:, 1:1 + HY_EMB_BANDS], w1[:, 1 + HY_EMB_BANDS:], fb1.astype(F32).reshape(fd, 1),
      fw2.astype(F32).T, fb2.astype(F32).reshape(fd, 1), freq.astype(F32).reshape(2, fd, 1),
      w3[:rows], w3[rows:], decay.astype(F32).reshape(rows, 1))


def _shift_conv(u, w0, w1, w2, b):
    L = u.shape[1]
    lane = lax.broadcasted_iota(jnp.int32, (1, L), 1)
    w0_row = jnp.where(lane == 0, 0.0, w0)
    w2_row = jnp.where(lane == L - 1, 0.0, w2)
    return pltpu.roll(u, 1, 1) * w0_row + u * w1 + pltpu.roll(u, L - 1, 1) * w2_row + b


def _toeplitz_conv(z, tz_ref, L):
    B = z.shape[0]
    nj = L // MXU
    zb = z.astype(BF16)
    tiles = [zb[:, MXU * J:MXU * (J + 1)] for J in range(nj)]
    acc = [None] * nj
    for d in range(-(nj - 1), nj):
        js = [J for J in range(nj) if 0 <= J + d < nj]
        lhs = tiles[js[0]] if len(js) == 1 else jnp.concatenate([tiles[J] for J in js], axis=0)
        res = jnp.dot(lhs, tz_ref[:, L + MXU * d:L + MXU * (d + 1)], preferred_element_type=F32)
        for n, J in enumerate(js):
            part = res[n * B:(n + 1) * B]
            acc[J + d] = part if acc[J + d] is None else acc[J + d] + part
    return acc[0] if nj == 1 else jnp.concatenate(acc, axis=1)


def _hyena_kernel(cw_ref, cb_ref, skip_ref, kf_ref, *refs, L, n_groups):
    in_refs = refs[:3 * n_groups]
    out_refs = refs[3 * n_groups:4 * n_groups]
    tz_ref = refs[4 * n_groups]
    c0 = pl.program_id(0) * CONV_CB

    def channels(p, carry):
        for u in range(CONV_UNROLL):
            c = p * CONV_UNROLL + u
            ch = c0 + c
            for o in range(HY_ORDER):
                row = jnp.broadcast_to(kf_ref[o, pl.ds(c, 1), :], (MXU, 2 * L))
                tz_ref[u, o] = pltpu.roll(row, 0, 1, stride=1, stride_axis=0).astype(BF16)
            for g in range(n_groups):
                stage = refs[4 * n_groups + 1 + g]
                sec = []
                for s, ref in enumerate(in_refs[3 * g:3 * g + 3]):
                    stage[u, s] = ref[:, c, :]
                    k = s * HY_WIDTH + ch
                    sec.append(_shift_conv(stage[u, s], cw_ref[0, k], cw_ref[1, k], cw_ref[2, k], cb_ref[k]))
                x1, x2, z = sec
                for o, gate in enumerate((x1, x2)):
                    y = _toeplitz_conv(z, tz_ref.at[u, o], L) + z * skip_ref[o, ch]
                    z = gate * y
                out_refs[g][:, c, :] = z
        return carry

    lax.fori_loop(0, CONV_CB // CONV_UNROLL, channels, 0)


def _hyena(u_ts, kf, conv_w, conv_b, skip, L):
    n_groups = len(u_ts)
    nblk = HY_WIDTH // CONV_CB
    smem = pl.BlockSpec(memory_space=pltpu.SMEM)
    in_specs = [smem, smem, smem, pl.BlockSpec((HY_ORDER, CONV_CB, 2 * L), lambda i: (0, i, 0))]
    args = [conv_w.astype(F32), conv_b.astype(F32), skip.astype(F32), kf.reshape(HY_ORDER, HY_WIDTH, 2 * L)]
    out_specs, out_shape = [], []
    for u in u_ts:
        B = u.shape[0]
        for s in range(3):
            in_specs.append(pl.BlockSpec((B, CONV_CB, L), lambda i, s=s: (0, i + s * nblk, 0)))
            args.append(u)
        out_specs.append(pl.BlockSpec((B, CONV_CB, L), lambda i: (0, i, 0)))
        out_shape.append(jax.ShapeDtypeStruct((B, HY_WIDTH, L), F32))
    return pl.pallas_call(
        functools.partial(_hyena_kernel, L=L, n_groups=n_groups),
        grid=(nblk,),
        in_specs=in_specs,
        out_specs=out_specs,
        out_shape=out_shape,
        scratch_shapes=[pltpu.VMEM((CONV_UNROLL, HY_ORDER, MXU, 2 * L), BF16)]
        + [pltpu.VMEM((CONV_UNROLL, 3, u.shape[0], L), F32) for u in u_ts],
        compiler_params=_cparams(1),
        name="hyena_conv",
    )(*args)


def _rms(x, n):
    return lax.rsqrt(jnp.sum(x * x, axis=-1, keepdims=True) * (1.0 / n) + EPS)


def _mla_prep_kernel(m_ref, ct_ref, st_ref, gqa_ref, gkva_ref, gq_ref, gqs_ref, gk_ref, gks_ref,
                     wqa_ref, wqb_ref, wka_ref, ea_ref, eb_ref, wv_ref, q_ref, k_ref, v_ref):
    m = m_ref[0].astype(F32)
    cq = m[:, :Q_LORA]
    ckv = m[:, Q_LORA:Q_LORA + KV_LORA]
    kr = m_ref[0][:, Q_LORA + KV_LORA:Q_LORA + KV_LORA + LANE]
    cqn = (cq * _rms(cq, Q_LORA) * gqa_ref[...]).astype(BF16)
    ckvn = (ckv * _rms(ckv, KV_LORA) * gkva_ref[...]).astype(BF16)
    qa = jnp.dot(cqn, wqa_ref[...], preferred_element_type=F32)
    qb = jnp.dot(cqn, wqb_ref[...], preferred_element_type=F32)
    ka = jnp.dot(ckvn, wka_ref[...], preferred_element_type=F32) + jnp.dot(kr, ea_ref[...], preferred_element_type=F32)
    kb = jnp.dot(kr, eb_ref[...], preferred_element_type=F32)
    v_ref[0] = jnp.dot(ckvn, wv_ref[...], preferred_element_type=F32).astype(BF16)
    ct, st = ct_ref[...], st_ref[...]
    scale = QK_HEAD ** -0.5 * math.log2(math.e)
    qc, qs = gq_ref[...] * ct * scale, gqs_ref[...] * st * scale
    kc, ks = gk_ref[...] * ct, gks_ref[...] * st
    for h in range(MLA_HEADS):
        sl = slice(h * HEAD_PAD, (h + 1) * HEAD_PAD)
        q_ref[0, h] = (_rms(qa[:, sl], QK_HEAD) * (qa[:, sl] * qc + qb[:, sl] * qs)).astype(BF16)
        k_ref[0, h] = (_rms(ka[:, sl], QK_HEAD) * (ka[:, sl] * kc + kb[:, sl] * ks)).astype(BF16)


def _mla_prep(m, P, tm):
    B, L, _ = m.shape
    w = [P[n] for n in ("wqa", "wqb", "wka", "ea", "eb", "wv")]
    g = [P[n] for n in ("gqa", "gkva", "gq", "gqs", "gk", "gks")]
    hp = MLA_HEADS * HEAD_PAD
    return pl.pallas_call(
        _mla_prep_kernel,
        grid=(B, L // tm),
        in_specs=[pl.BlockSpec((1, tm, MLA_IN), lambda b, i: (b, i, 0)),
                  pl.BlockSpec((tm, HEAD_PAD), lambda b, i: (i, 0)),
                  pl.BlockSpec((tm, HEAD_PAD), lambda b, i: (i, 0))]
        + [_full(a.shape) for a in g] + [_full(a.shape) for a in w],
        out_specs=[
            pl.BlockSpec((1, MLA_HEADS, tm, HEAD_PAD), lambda b, i: (b, 0, i, 0)),
            pl.BlockSpec((1, MLA_HEADS, tm, HEAD_PAD), lambda b, i: (b, 0, i, 0)),
            pl.BlockSpec((1, tm, MLA_HEADS * V_HEAD), lambda b, i: (b, i, 0)),
        ],
        out_shape=[
            jax.ShapeDtypeStruct((B, MLA_HEADS, L, HEAD_PAD), BF16),
            jax.ShapeDtypeStruct((B, MLA_HEADS, L, HEAD_PAD), BF16),
            jax.ShapeDtypeStruct((B, L, MLA_HEADS * V_HEAD), BF16),
        ],
        compiler_params=_cparams(2),
        name="mla_prep",
    )(m, P["ctab"][:L], P["stab"][:L], *g, *w)


def _attn_kernel(q_ref, k_ref, v_ref, o_ref):
    L = v_ref.shape[1]
    one_col = jnp.where(lax.broadcasted_iota(jnp.int32, (L, LANE), 1) == 0, 1.0, 0.0).astype(BF16)
    for pair in range(MLA_HEADS // 2):
        lanes = pl.ds(pair * LANE, LANE)
        v_ext = jnp.concatenate([v_ref[0, :, lanes], one_col], axis=1)
        outs = []
        for h in (2 * pair, 2 * pair + 1):
            s = lax.dot_general(q_ref[0, h], k_ref[0, h], (((1,), (1,)), ((), ())), preferred_element_type=F32)
            p = jnp.exp2(s - jnp.max(s, axis=-1, keepdims=True)).astype(BF16)
            pv = jnp.dot(p, v_ext, preferred_element_type=F32)
            outs.append(pv[:, :LANE] / pv[:, LANE:LANE + 1])
        lane = lax.broadcasted_iota(jnp.int32, outs[0].shape, 1)
        o_ref[0, :, lanes] = jnp.where(lane < V_HEAD, outs[0], outs[1]).astype(BF16)


def _attention(q, k, v, tq):
    B, H, L, _ = q.shape
    return pl.pallas_call(
        _attn_kernel,
        grid=(B, L // tq),
        in_specs=[
            pl.BlockSpec((1, H, tq, HEAD_PAD), lambda b, i: (b, 0, i, 0)),
            pl.BlockSpec((1, H, L, HEAD_PAD), lambda b, i: (b, 0, 0, 0)),
            pl.BlockSpec((1, L, H * V_HEAD), lambda b, i: (b, 0, 0)),
        ],
        out_specs=pl.BlockSpec((1, tq, H * V_HEAD), lambda b, i: (b, i, 0)),
        out_shape=jax.ShapeDtypeStruct((B, L, H * V_HEAD), BF16),
        compiler_params=_cparams(2),
        name="mla_attention",
    )(q, k, v)


def _pack_rows(a):
    bits = lax.bitcast_convert_type(a.astype(BF16).astype(F32), jnp.uint32)
    return (bits[:, :PACKED] >> 16) | (bits[:, PACKED:] & jnp.uint32(0xFFFF0000))


def _unpack_rows(p):
    lo = lax.bitcast_convert_type(p << 16, F32)
    hi = lax.bitcast_convert_type(p & jnp.uint32(0xFFFF0000), F32)
    return jnp.concatenate([lo, hi], axis=1)


def _merge_kernel(x_ref, yh_ref, ym_ref, g_ref, who_ref, wmo_ref, wo_ref, nf_ref, rwh_ref, rwl_ref, rb_ref,
                  x1_ref, h2_ref, ids_ref, gate_ref, cnt_ref):
    tm = x_ref.shape[0]
    sub = tm // 2

    @pl.when(jnp.logical_and(pl.program_id(0) == 0, pl.program_id(1) == 0))
    def _():
        cnt_ref[...] = jnp.zeros_like(cnt_ref)

    eid = lax.broadcasted_iota(jnp.int32, (N_EXPERTS, sub), 0)
    for r in range(0, tm, sub):
        rows = pl.ds(r, sub)
        g = g_ref[rows, :].astype(F32)
        a = lax.dot_general(yh_ref[:, rows].astype(BF16), who_ref[...], (((0,), (0,)), ((), ())),
                            preferred_element_type=F32)
        b = jnp.dot(ym_ref[rows, :], wmo_ref[...], preferred_element_type=F32)
        merged = jax.nn.sigmoid(g[:, :D_MODEL]) * a + jax.nn.sigmoid(g[:, D_MODEL:]) * b
        x1 = x_ref[rows, :] + jnp.dot(merged.astype(BF16), wo_ref[...], preferred_element_type=F32)
        x1_ref[rows, :] = x1
        h2 = x1 * lax.rsqrt(jnp.mean(x1 * x1, axis=-1, keepdims=True) + EPS) * nf_ref[...]
        h2_ref[rows, :] = _pack_rows(h2)
        h_hi = h2.astype(BF16)
        h_lo = (h2 - h_hi.astype(F32)).astype(BF16)
        logits = (jnp.dot(h_hi, rwh_ref[...], preferred_element_type=F32)
                  + jnp.dot(h_lo, rwh_ref[...], preferred_element_type=F32)
                  + jnp.dot(h_hi, rwl_ref[...], preferred_element_type=F32) + rb_ref[...])
        work = logits.T[:N_EXPERTS]
        picked = jnp.zeros((N_EXPERTS, sub), F32)
        vals, firsts = [], []
        for k in range(TOP_K):
            m = jnp.max(work, axis=0, keepdims=True)
            first = jnp.min(jnp.where(work == m, eid, N_EXPERTS), axis=0, keepdims=True)
            hot = eid == first
            vals.append(m)
            firsts.append(first)
            picked = jnp.where(hot, 1.0, picked)
            work = jnp.where(hot, -jnp.inf, work)
        ex = [jnp.exp(v - vals[0]) for v in vals]
        inv = 1.0 / (ex[0] + ex[1] + ex[2] + ex[3])
        pad = jnp.zeros((LANE - TOP_K, sub), F32)
        ids_t = jnp.concatenate([f.astype(F32) for f in firsts] + [pad], axis=0)
        gates_t = jnp.concatenate([e * inv for e in ex] + [pad], axis=0)
        ids_ref[rows, :] = ids_t.T.astype(jnp.int32)
        gate_ref[rows, :] = gates_t.T
        cnt_ref[...] += jnp.sum(picked, axis=1, keepdims=True)


def _merge(x, yh_t, ym, g, P, tm):
    B, L, _ = x.shape
    nt = L // tm
    w = [P[n] for n in ("w_hy_o", "w_mla_o", "w_out", "norm_ffn", "router_w_hi", "router_w_lo", "router_b")]
    return pl.pallas_call(
        _merge_kernel,
        grid=(B, nt),
        in_specs=[
            pl.BlockSpec((None, tm, D_MODEL), lambda b, i: (b, i, 0)),
            pl.BlockSpec((None, HY_WIDTH, tm), lambda b, i: (b, 0, i)),
            pl.BlockSpec((None, tm, MLA_HEADS * V_HEAD), lambda b, i: (b, i, 0)),
            pl.BlockSpec((None, tm, 2 * D_MODEL), lambda b, i: (b, i, 0)),
        ] + [_full(a.shape) for a in w],
        out_specs=[
            pl.BlockSpec((tm, D_MODEL), lambda b, i: (b * nt + i, 0)),
            pl.BlockSpec((tm, PACKED), lambda b, i: (b * nt + i, 0)),
            pl.BlockSpec((tm, LANE), lambda b, i: (b * nt + i, 0)),
            pl.BlockSpec((tm, LANE), lambda b, i: (b * nt + i, 0)),
            pl.BlockSpec((N_EXPERTS, 1), lambda b, i: (0, 0)),
        ],
        out_shape=[
            jax.ShapeDtypeStruct((B * L, D_MODEL), F32),
            jax.ShapeDtypeStruct((B * L, PACKED), jnp.uint32),
            jax.ShapeDtypeStruct((B * L, LANE), jnp.int32),
            jax.ShapeDtypeStruct((B * L, LANE), F32),
            jax.ShapeDtypeStruct((N_EXPERTS, 1), F32),
        ],
        compiler_params=_cparams(2),
        name="merge_router",
    )(x, yh_t, ym, g, *w)


def _expert_kernel(be_ref, na_ref, first_ref, slot_ref, nxt_ref, x_ref, wgu_hbm, bgu_ref, wd_hbm, bd_ref, y_ref,
                   wgu_buf, wd_buf, sems):
    i = pl.program_id(0)
    active = i < na_ref[0]
    slot = slot_ref[i]

    def weight_copies(expert, s):
        return (pltpu.make_async_copy(wgu_hbm.at[expert], wgu_buf.at[s], sems.at[0, s]),
                pltpu.make_async_copy(wd_hbm.at[expert], wd_buf.at[s], sems.at[1, s]))

    @pl.when(i == 0)
    def _():
        for cp in weight_copies(be_ref[0], 0):
            cp.start()

    @pl.when(jnp.logical_and(active, first_ref[i] == 1))
    def _():
        for cp in weight_copies(be_ref[i], slot):
            cp.wait()

        @pl.when(nxt_ref[i] >= 0)
        def _():
            for cp in weight_copies(nxt_ref[i], 1 - slot):
                cp.start()

    @pl.when(active)
    def _():
        x = _unpack_rows(x_ref[...])
        gu = jnp.dot(x, wgu_buf[slot], preferred_element_type=F32) + bgu_ref[0]
        gate = jnp.minimum(gu[:, :D_FF], SWIGLU_LIMIT)
        up = jnp.clip(gu[:, D_FF:], -SWIGLU_LIMIT, SWIGLU_LIMIT)
        act = (up + 1.0) * (gate * jax.nn.sigmoid(SWIGLU_ALPHA * gate))
        y_ref[...] = _pack_rows(jnp.dot(act, wd_buf[slot], preferred_element_type=F32) + bd_ref[0])

    @pl.when(i >= na_ref[0])
    def _():
        y_ref[...] = jnp.zeros_like(y_ref)


def _experts(plan, xs, P):
    n_rows = xs.shape[0]
    nb = n_rows // MOE_BLOCK
    grid_spec = pltpu.PrefetchScalarGridSpec(
        num_scalar_prefetch=5,
        grid=(nb,),
        in_specs=[
            pl.BlockSpec((MOE_BLOCK, PACKED), lambda i, be, *_: (i, 0)),
            pl.BlockSpec(memory_space=pl.ANY),
            pl.BlockSpec((1, 1, 2 * D_FF), lambda i, be, *_: (be[i], 0, 0)),
            pl.BlockSpec(memory_space=pl.ANY),
            pl.BlockSpec((1, 1, D_MODEL), lambda i, be, *_: (be[i], 0, 0)),
        ],
        out_specs=pl.BlockSpec((MOE_BLOCK, PACKED), lambda i, be, *_: (i, 0)),
        scratch_shapes=[
            pltpu.VMEM((2, D_MODEL, 2 * D_FF), F32),
            pltpu.VMEM((2, D_FF, D_MODEL), F32),
            pltpu.SemaphoreType.DMA((2, 2)),
        ],
    )
    return pl.pallas_call(
        _expert_kernel,
        grid_spec=grid_spec,
        out_shape=jax.ShapeDtypeStruct((n_rows, PACKED), jnp.uint32),
        compiler_params=_cparams(1),
        name="moe_experts",
    )(*plan, xs, P["w_gu"], P["b_gu"], P["w_down"], P["b_down"])


def _combine_kernel(x1_ref, yk_ref, gt_ref, o_ref):
    gt = gt_ref[...]
    acc = _unpack_rows(yk_ref[0]) * gt[:, 0:1]
    for k in range(1, TOP_K):
        acc = acc + _unpack_rows(yk_ref[k]) * gt[:, k:k + 1]
    o_ref[...] = x1_ref[...] + acc


def _combine(x1, yk, gates, tm):
    T = x1.shape[0]
    return pl.pallas_call(
        _combine_kernel,
        grid=(T // tm,),
        in_specs=[
            pl.BlockSpec((tm, D_MODEL), lambda i: (i, 0)),
            pl.BlockSpec((TOP_K, tm, PACKED), lambda i: (0, i, 0)),
            pl.BlockSpec((tm, LANE), lambda i: (i, 0)),
        ],
        out_specs=pl.BlockSpec((tm, D_MODEL), lambda i: (i, 0)),
        out_shape=jax.ShapeDtypeStruct((T, D_MODEL), F32),
        compiler_params=_cparams(1),
        name="moe_combine",
    )(x1, yk, gates)


def _route_kernel(ids_ref, cnt_ref, dest_ref, run_ref, start_ref, *, tm):
    @pl.when(pl.program_id(0) == 0)
    def _():
        cnt = cnt_ref[...]
        padded = jnp.floor((cnt + (MOE_BLOCK - 1)) * (1.0 / MOE_BLOCK)) * MOE_BLOCK
        scan = jnp.broadcast_to(padded, (8, LANE))
        lane8 = lax.broadcasted_iota(jnp.int32, (8, LANE), 1)
        s = 1
        while s < LANE:
            scan = scan + jnp.where(lane8 >= s, pltpu.roll(scan, s, 1), 0.0)
            s *= 2
        start_ref[...] = scan[0:1] - padded
        run_ref[...] = jnp.zeros_like(run_ref)

    lane = lax.broadcasted_iota(jnp.int32, (tm, LANE), 1)
    ids = ids_ref[...]
    hots = [lane == ids[:, k:k + 1] for k in range(TOP_K)]
    chosen = jnp.zeros((tm, LANE), F32)
    for hot in hots:
        chosen = jnp.where(hot, 1.0, chosen)
    r = lax.broadcasted_iota(jnp.int32, (tm, tm), 0)
    c = lax.broadcasted_iota(jnp.int32, (tm, tm), 1)
    earlier = jnp.where(c < r, 1.0, 0.0).astype(BF16)
    before = jnp.dot(earlier, chosen.astype(BF16), preferred_element_type=F32) + (run_ref[...] + start_ref[...])
    dest = jnp.zeros((tm, LANE), F32)
    for k, hot in enumerate(hots):
        dest = jnp.where(lane == k, jnp.sum(jnp.where(hot, before, 0.0), axis=-1, keepdims=True), dest)
    dest_ref[...] = dest.astype(jnp.int32)
    run_ref[...] += jnp.sum(chosen, axis=0, keepdims=True)


def _route(ids, counts, tm):
    T = ids.shape[0]
    dest = pl.pallas_call(
        functools.partial(_route_kernel, tm=tm),
        grid=(T // tm,),
        in_specs=[pl.BlockSpec((tm, LANE), lambda i: (i, 0)), _full((1, LANE))],
        out_specs=pl.BlockSpec((tm, LANE), lambda i: (i, 0)),
        out_shape=jax.ShapeDtypeStruct((T, LANE), jnp.int32),
        scratch_shapes=[pltpu.VMEM((1, LANE), F32), pltpu.VMEM((1, LANE), F32)],
        compiler_params=_cparams(1),
        name="moe_route",
    )(ids, jnp.pad(counts[:, 0], (0, LANE - N_EXPERTS)).reshape(1, LANE))
    counts = counts[:, 0].astype(jnp.int32)
    pad_end = jnp.cumsum((counts + MOE_BLOCK - 1) // MOE_BLOCK * MOE_BLOCK)
    n_blocks = -(-(T * TOP_K + N_EXPERTS * (MOE_BLOCK - 1)) // MOE_BLOCK)
    block_start = jnp.arange(n_blocks, dtype=jnp.int32) * MOE_BLOCK
    block_e = jnp.sum((block_start[:, None] >= pad_end[None, :]).astype(jnp.int32), axis=1)
    block_e = jnp.minimum(block_e, N_EXPERTS - 1)
    n_active = (pad_end[-1] // MOE_BLOCK).astype(jnp.int32)
    blk = jnp.arange(n_blocks, dtype=jnp.int32)
    first = jnp.logical_and(blk < n_active, jnp.logical_or(blk == 0, block_e != jnp.roll(block_e, 1)))
    slot = (jnp.cumsum(first.astype(jnp.int32)) - 1) % 2
    run_end = pad_end[block_e] // MOE_BLOCK
    nxt = jnp.where(run_end < n_active, block_e[jnp.minimum(run_end, n_blocks - 1)], -1)
    plan = (block_e.astype(jnp.int32), n_active.reshape(1), first.astype(jnp.int32), slot.astype(jnp.int32),
            nxt.astype(jnp.int32))
    dest_flat = dest[:, :TOP_K].T.reshape(TOP_K * T)
    return dest_flat, plan, n_blocks * MOE_BLOCK


def _sc_workers():
    info = plsc.get_sparse_core_info()
    return info.num_cores, info.num_cores * info.num_subcores


def _sc_scatter_rows(rows, dest_flat, n_rows):
    T, W = rows.shape
    n_cores, n_workers = _sc_workers()
    per_w = T // n_workers
    assert per_w * n_workers == T and per_w % SC_CHUNK == 0

    def body(rows_hbm, dest_hbm, out_hbm, buf, *idx):
        wid = lax.axis_index("s") * n_cores + lax.axis_index("c")

        @pl.loop(0, per_w // SC_CHUNK)
        def _(j):
            off = pl.multiple_of(wid * per_w + j * SC_CHUNK, SC_CHUNK)
            pltpu.sync_copy(rows_hbm.at[pl.ds(off, SC_CHUNK)], buf)
            for k in range(TOP_K):
                pltpu.sync_copy(dest_hbm.at[pl.ds(k * T + off, SC_CHUNK)], idx[k])
            for k in range(TOP_K):
                pltpu.sync_copy(buf, out_hbm.at[idx[k]])

    return pl.kernel(
        body,
        out_type=jax.ShapeDtypeStruct((n_rows, W), rows.dtype),
        mesh=plsc.VectorSubcoreMesh(core_axis_name="c", subcore_axis_name="s"),
        scratch_types=[pltpu.VMEM((SC_CHUNK, W), rows.dtype)] + [pltpu.VMEM((SC_CHUNK,), jnp.int32)] * TOP_K,
        name="moe_dispatch_sc",
    )(rows, dest_flat)


def _sc_gather_rows(table, dest_flat):
    N = dest_flat.shape[0]
    W = table.shape[1]
    n_cores, n_workers = _sc_workers()
    per_w = N // n_workers
    assert per_w * n_workers == N and per_w % SC_CHUNK == 0

    def body(table_hbm, dest_hbm, out_hbm, buf, idx):
        wid = lax.axis_index("s") * n_cores + lax.axis_index("c")

        @pl.loop(0, per_w // SC_CHUNK)
        def _(j):
            off = pl.multiple_of(wid * per_w + j * SC_CHUNK, SC_CHUNK)
            pltpu.sync_copy(dest_hbm.at[pl.ds(off, SC_CHUNK)], idx)
            pltpu.sync_copy(table_hbm.at[idx], buf)
            pltpu.sync_copy(buf, out_hbm.at[pl.ds(off, SC_CHUNK)])

    return pl.kernel(
        body,
        out_type=jax.ShapeDtypeStruct((N, W), table.dtype),
        mesh=plsc.VectorSubcoreMesh(core_axis_name="c", subcore_axis_name="s"),
        scratch_types=[pltpu.VMEM((SC_CHUNK, W), table.dtype), pltpu.VMEM((SC_CHUNK,), jnp.int32)],
        name="moe_gather_sc",
    )(table, dest_flat)


def _moe(groups, P, tm):
    staged = []
    for x1, h2p, ids, gates, counts in groups:
        dest_flat, plan, n_rows = _route(ids, counts, tm)
        staged.append((dest_flat, plan, _sc_scatter_rows(h2p, dest_flat, n_rows)))
    ys = []
    for dest_flat, plan, xs in staged:
        if ys:
            xs, ys[-1] = lax.optimization_barrier((xs, ys[-1]))
        ys.append(_experts(plan, xs, P))
    outs = []
    for (x1, _, _, gates, _), (dest_flat, _, _), y in zip(groups, staged, ys):
        yk = _sc_gather_rows(y, dest_flat).reshape(TOP_K, x1.shape[0], PACKED)
        outs.append(_combine(x1, yk, gates, tm))
    return outs


def _prep_params(norm_mix, w_in, q_a_norm, w_uq, kv_a_norm, w_ukv, q_norm, k_norm, w_hy_o, w_mla_o, w_out,
                 norm_ffn, router_w, router_b, w_gu, b_gu, w_down, b_down, L):
    P = {}
    P["norm_mix"] = norm_mix.astype(F32).reshape(1, D_MODEL)
    P["w_hy_t"] = w_in[:, :HY_END].T.astype(BF16)
    P["w_m"] = jnp.pad(w_in[:, HY_END:KR_END], ((0, 0), (0, MLA_IN - (KR_END - HY_END)))).astype(BF16)
    P["w_g"] = w_in[:, KR_END:].astype(BF16)

    half = QK_ROPE // 2
    swap = jnp.concatenate([jnp.arange(QK_NOPE), QK_NOPE + half + jnp.arange(half), QK_NOPE + jnp.arange(half)])
    rope_lane = (jnp.arange(QK_HEAD) >= QK_NOPE)

    def slots(w):
        return jnp.pad(w, ((0, 0), (0, 0), (0, HEAD_PAD - QK_HEAD))).reshape(w.shape[0], MLA_HEADS * HEAD_PAD)

    wq = w_uq.astype(F32).reshape(Q_LORA, MLA_HEADS, QK_HEAD)
    P["wqa"] = slots(wq).astype(BF16)
    P["wqb"] = slots(jnp.where(rope_lane, wq[:, :, swap], 0.0)).astype(BF16)
    wkv = w_ukv.astype(F32).reshape(KV_LORA, MLA_HEADS, QK_NOPE + V_HEAD)
    P["wka"] = slots(jnp.pad(wkv[:, :, :QK_NOPE], ((0, 0), (0, 0), (0, QK_ROPE)))).astype(BF16)
    P["wv"] = wkv[:, :, QK_NOPE:].reshape(KV_LORA, MLA_HEADS * V_HEAD).astype(BF16)
    eye = jnp.eye(LANE, QK_ROPE, dtype=F32)
    e_a = jnp.pad(eye, ((0, 0), (QK_NOPE, 0)))
    e_b = e_a[:, swap] * rope_lane
    P["ea"] = slots(jnp.broadcast_to(e_a[:, None, :], (LANE, MLA_HEADS, QK_HEAD))).astype(BF16)
    P["eb"] = slots(jnp.broadcast_to(e_b[:, None, :], (LANE, MLA_HEADS, QK_HEAD))).astype(BF16)

    sign = jnp.concatenate([jnp.zeros(QK_NOPE), -jnp.ones(half), jnp.ones(half)]).astype(F32)

    def lane_pad(g):
        return jnp.pad(g, (0, HEAD_PAD - QK_HEAD)).reshape(1, HEAD_PAD)

    for name, g in (("gq", q_norm.astype(F32)), ("gk", k_norm.astype(F32))):
        P[name] = lane_pad(g)
        P[name + "s"] = lane_pad(g[swap] * sign)
    P["gqa"] = q_a_norm.astype(F32).reshape(1, Q_LORA)
    P["gkva"] = kv_a_norm.astype(F32).reshape(1, KV_LORA)
    inv = 1.0 / (ROPE_THETA ** (jnp.arange(0, QK_ROPE, 2, dtype=F32) / QK_ROPE))
    ang = jnp.arange(L, dtype=F32)[:, None] * inv[None, :]
    cos, sin = jnp.cos(ang), jnp.sin(ang)
    P["ctab"] = jnp.concatenate([jnp.ones((L, QK_NOPE), F32), cos, cos, jnp.zeros((L, HEAD_PAD - QK_HEAD), F32)], axis=1)
    P["stab"] = jnp.concatenate([jnp.zeros((L, QK_NOPE), F32), sin, sin, jnp.zeros((L, HEAD_PAD - QK_HEAD), F32)], axis=1)

    P["w_hy_o"] = w_hy_o.astype(BF16)
    P["w_mla_o"] = w_mla_o.astype(BF16)
    P["w_out"] = w_out.astype(BF16)
    P["norm_ffn"] = norm_ffn.astype(F32).reshape(1, D_MODEL)
    rw = jnp.pad(router_w.astype(F32), ((0, 0), (0, LANE - N_EXPERTS)))
    P["router_w_hi"] = rw.astype(BF16)
    P["router_w_lo"] = (rw - P["router_w_hi"].astype(F32)).astype(BF16)
    P["router_b"] = jnp.pad(router_b.astype(F32), (0, LANE - N_EXPERTS)).reshape(1, LANE)
    P["w_gu"] = w_gu.astype(F32)
    P["b_gu"] = b_gu.astype(F32).reshape(N_EXPERTS, 1, 2 * D_FF)
    P["w_down"] = w_down.astype(F32)
    P["b_down"] = b_down.astype(F32).reshape(N_EXPERTS, 1, D_MODEL)
    return P


def _tile(n, want):
    t = min(n, want)
    assert n % t == 0
    return t


def _layer(xs, norm_mix, w_in, hy_conv_w, hy_conv_b, hy_fw1, hy_fb1, hy_fw2, hy_fb2, hy_fw3, hy_freq, hy_decay, hy_skip,
           q_a_norm, w_uq, kv_a_norm, w_ukv, q_norm, k_norm, w_hy_o, w_mla_o, w_out,
           norm_ffn, router_w, router_b, w_gu, b_gu, w_down, b_down):
    L = xs[0].shape[1]
    assert all(x.shape[1] == L and x.shape[2] == D_MODEL for x in xs) and L % MXU == 0
    P = _prep_params(norm_mix, w_in, q_a_norm, w_uq, kv_a_norm, w_ukv, q_norm, k_norm, w_hy_o, w_mla_o, w_out,
                     norm_ffn, router_w, router_b, w_gu, b_gu, w_down, b_down, L)
    kf = _hyena_filters(L, hy_fw1, hy_fb1, hy_fw2, hy_fb2, hy_fw3, hy_freq, hy_decay)
    tm = _tile(L, 512)
    proj = [_inproj(x, P["norm_mix"], P["w_hy_t"], P["w_m"], P["w_g"], tm) for x in xs]
    y_ts = _hyena([u_t for u_t, _, _ in proj], kf, hy_conv_w, hy_conv_b, hy_skip, L)
    mixed = []
    for x, (_, m, g), y_t in zip(xs, proj, y_ts):
        q, k, v = _mla_prep(m, P, tm)
        y_mla = _attention(q, k, v, _tile(L, 512))
        mixed.append(_merge(x, y_t, y_mla, g, P, _tile(L, 1024)))
    order = sorted(range(len(xs)), key=lambda n: -xs[n].shape[0])
    moe_out = _moe([mixed[n] for n in order], P, tm)
    outs = [None] * len(xs)
    for n, y in zip(order, moe_out):
        outs[n] = y.reshape(xs[n].shape)
    return outs


def kernel(x_prompt, x_sample, norm_mix, w_in, hy_conv_w, hy_conv_b, hy_fw1, hy_fb1, hy_fw2, hy_fb2, hy_fw3, hy_freq, hy_decay, hy_skip, q_a_norm, w_uq, kv_a_norm, w_ukv, q_norm, k_norm, w_hy_o, w_mla_o, w_out, norm_ffn, router_w, router_b, w_gu, b_gu, w_down, b_down):
    depth = norm_mix.shape[0]
    xs = [x_prompt, x_sample]
    weights = (norm_mix, w_in, hy_conv_w, hy_conv_b, hy_fw1, hy_fb1, hy_fw2, hy_fb2, hy_fw3, hy_freq, hy_decay, hy_skip,
               q_a_norm, w_uq, kv_a_norm, w_ukv, q_norm, k_norm, w_hy_o, w_mla_o, w_out,
               norm_ffn, router_w, router_b, w_gu, b_gu, w_down, b_down)
    for l in range(depth):
        xs = _layer(xs, *[w[l] for w in weights])
    return (xs[0], xs[1])
```

```python
import functools
import math

import jax
import jax.numpy as jnp
from jax import lax
from jax.experimental import pallas as pl
from jax.experimental.pallas import tpu as pltpu
from jax.experimental.pallas import tpu_sc as plsc

F32 = jnp.float32
BF16 = jnp.bfloat16

D_MODEL = 1024
HY_WIDTH = 512
HY_ORDER = 2
HY_EMB_BANDS = 16
HY_FILTER_DIM = 64
MLA_HEADS = 8
V_HEAD = 64
QK_NOPE = 64
QK_ROPE = 32
QK_HEAD = QK_NOPE + QK_ROPE
Q_LORA = 256
KV_LORA = 128
ROPE_THETA = 10000.0
HY_END = 3 * HY_WIDTH
Q_END = HY_END + Q_LORA
KV_END = Q_END + KV_LORA
KR_END = KV_END + QK_ROPE
G1_END = KR_END + D_MODEL
N_EXPERTS = 32
TOP_K = 4
D_FF = D_MODEL
SWIGLU_LIMIT = 7.0
SWIGLU_ALPHA = 1.702
EPS = 1e-6

LANE = 128
MXU = 256
HEAD_PAD = LANE
MLA_IN = 512
CONV_CB = 8
CONV_UNROLL = 4
MOE_BLOCK = 512
PACKED = D_MODEL // 2
SC_CHUNK = 64
VMEM_LIMIT = 52 * 1024 * 1024


def _cparams(n_axes):
    return pltpu.CompilerParams(dimension_semantics=("arbitrary",) * n_axes, vmem_limit_bytes=VMEM_LIMIT)


def _full(shape):
    n = len(shape)
    return pl.BlockSpec(shape, lambda *_: (0,) * n)


def _inproj_kernel(x_ref, g_ref, why_ref, wm_ref, wg_ref, u_ref, m_ref, gate_ref):
    x = x_ref[...]
    r = lax.rsqrt(jnp.mean(x * x, axis=-1, keepdims=True) + EPS)
    h = (x * r * g_ref[...]).astype(BF16)
    u_ref[...] = lax.dot_general(why_ref[...], h, (((1,), (1,)), ((), ())), preferred_element_type=F32)
    m_ref[...] = jnp.dot(h, wm_ref[...], preferred_element_type=F32).astype(BF16)
    gate_ref[...] = jnp.dot(h, wg_ref[...], preferred_element_type=F32).astype(BF16)


def _inproj(x, g, w_hy_t, w_m, w_g, tm):
    B, L, _ = x.shape
    return pl.pallas_call(
        _inproj_kernel,
        grid=(B, L // tm),
        in_specs=[
            pl.BlockSpec((None, tm, D_MODEL), lambda b, i: (b, i, 0)),
            _full((1, D_MODEL)),
            _full(w_hy_t.shape),
            _full(w_m.shape),
            _full(w_g.shape),
        ],
        out_specs=[
            pl.BlockSpec((None, HY_END, tm), lambda b, i: (b, 0, i)),
            pl.BlockSpec((None, tm, MLA_IN), lambda b, i: (b, i, 0)),
            pl.BlockSpec((None, tm, 2 * D_MODEL), lambda b, i: (b, i, 0)),
        ],
        out_shape=[
            jax.ShapeDtypeStruct((B, HY_END, L), F32),
            jax.ShapeDtypeStruct((B, L, MLA_IN), BF16),
            jax.ShapeDtypeStruct((B, L, 2 * D_MODEL), BF16),
        ],
        compiler_params=_cparams(2),
        name="inproj",
    )(x, g, w_hy_t, w_m, w_g)


def _filter_kernel(bands_ref, w1t_ref, w1c_ref, w1s_ref, b1_ref, w2_ref, b2_ref, fr_ref, w3f_ref, w3b_ref, dec_ref,
                   out_ref, *, L):
    hi = lax.Precision.HIGHEST
    n = 2 * L
    j = lax.broadcasted_iota(jnp.int32, (1, n), 1)
    pos = jnp.abs(j - L).astype(F32)
    t01 = pos / max(L - 1, 1)
    ang = ((2.0 * math.pi / L) * pos) * bands_ref[...]
    pre = (w1t_ref[...] * t01
           + jnp.dot(w1c_ref[...], jnp.cos(ang), precision=hi, preferred_element_type=F32)
           - jnp.dot(w1s_ref[...], jnp.sin(ang), precision=hi, preferred_element_type=F32)
           + b1_ref[...])
    z = jnp.sin(fr_ref[0] * pre)
    z = jnp.sin(fr_ref[1] * (jnp.dot(w2_ref[...], z, precision=hi, preferred_element_type=F32) + b2_ref[...]))
    win = jnp.exp(-t01 * jnp.abs(dec_ref[...]))
    h_b = jnp.dot(w3b_ref[...], z[:, :L], precision=hi, preferred_element_type=F32) * win[:, :L]
    h_f = jnp.dot(w3f_ref[...], z[:, L:], precision=hi, preferred_element_type=F32) * win[:, L:]
    out_ref[:, :L] = jnp.where(j[:, :L] == 0, 0.0, h_b)
    out_ref[:, L:] = h_f


def _hyena_filters(L, fw1, fb1, fw2, fb2, fw3, freq, decay):
    rows = HY_ORDER * HY_WIDTH
    rb = 256
    bands = jnp.linspace(1e-4, HY_EMB_BANDS - 1, HY_EMB_BANDS, dtype=F32).reshape(HY_EMB_BANDS, 1)
    w1 = fw1.astype(F32).T
    w3 = fw3.astype(F32).T
    fd = HY_FILTER_DIM
    return pl.pallas_call(
        functools.partial(_filter_kernel, L=L),
        grid=(rows // rb,),
        in_specs=[
            _full((HY_EMB_BANDS, 1)),
            _full((fd, 1)),
            _full((fd, HY_EMB_BANDS)),
            _full((fd, HY_EMB_BANDS)),
            _full((fd, 1)),
            _full((fd, fd)),
            _full((fd, 1)),
            _full((2, fd, 1)),
            pl.BlockSpec((rb, fd), lambda i: (i, 0)),
            pl.BlockSpec((rb, fd), lambda i: (i, 0)),
            pl.BlockSpec((rb, 1), lambda i: (i, 0)),
        ],
        out_specs=pl.BlockSpec((rb, 2 * L), lambda i: (i, 0)),
        out_shape=jax.ShapeDtypeStruct((rows, 2 * L), F32),
        compiler_params=_cparams(1),
        name="hyena_filters",
    )(bands, w1[:, :1], w1[:, 1:1 + HY_EMB_BANDS], w1[:, 1 + HY_EMB_BANDS:], fb1.astype(F32).reshape(fd, 1),
      fw2.astype(F32).T, fb2.astype(F32).reshape(fd, 1), freq.astype(F32).reshape(2, fd, 1),
      w3[:rows], w3[rows:], decay.astype(F32).reshape(rows, 1))


def _shift_conv(u, w0, w1, w2, b):
    L = u.shape[1]
    lane = lax.broadcasted_iota(jnp.int32, (1, L), 1)
    w0_row = jnp.where(lane == 0, 0.0, w0)
    w2_row = jnp.where(lane == L - 1, 0.0, w2)
    return pltpu.roll(u, 1, 1) * w0_row + u * w1 + pltpu.roll(u, L - 1, 1) * w2_row + b


def _toeplitz_conv(z, tz_ref, L):
    B = z.shape[0]
    nj = L // MXU
    zb = z.astype(BF16)
    tiles = [zb[:, MXU * J:MXU * (J + 1)] for J in range(nj)]
    acc = [None] * nj
    for d in range(-(nj - 1), nj):
        js = [J for J in range(nj) if 0 <= J + d < nj]
        lhs = tiles[js[0]] if len(js) == 1 else jnp.concatenate([tiles[J] for J in js], axis=0)
        res = jnp.dot(lhs, tz_ref[:, L + MXU * d:L + MXU * (d + 1)], preferred_element_type=F32)
        for n, J in enumerate(js):
            part = res[n * B:(n + 1) * B]
            acc[J + d] = part if acc[J + d] is None else acc[J + d] + part
    return acc[0] if nj == 1 else jnp.concatenate(acc, axis=1)


def _hyena_kernel(cw_ref, cb_ref, skip_ref, kf_ref, *refs, L, n_groups):
    in_refs = refs[:3 * n_groups]
    out_refs = refs[3 * n_groups:4 * n_groups]
    tz_ref = refs[4 * n_groups]
    c0 = pl.program_id(0) * CONV_CB

    def channels(p, carry):
        for u in range(CONV_UNROLL):
            c = p * CONV_UNROLL + u
            ch = c0 + c
            for o in range(HY_ORDER):
                row = jnp.broadcast_to(kf_ref[o, pl.ds(c, 1), :], (MXU, 2 * L))
                tz_ref[u, o] = pltpu.roll(row, 0, 1, stride=1, stride_axis=0).astype(BF16)
            for g in range(n_groups):
                stage = refs[4 * n_groups + 1 + g]
                sec = []
                for s, ref in enumerate(in_refs[3 * g:3 * g + 3]):
                    stage[u, s] = ref[:, c, :]
                    k = s * HY_WIDTH + ch
                    sec.append(_shift_conv(stage[u, s], cw_ref[0, k], cw_ref[1, k], cw_ref[2, k], cb_ref[k]))
                x1, x2, z = sec
                for o, gate in enumerate((x1, x2)):
                    y = _toeplitz_conv(z, tz_ref.at[u, o], L) + z * skip_ref[o, ch]
                    z = gate * y
                out_refs[g][:, c, :] = z
        return carry

    lax.fori_loop(0, CONV_CB // CONV_UNROLL, channels, 0)


def _hyena(u_ts, kf, conv_w, conv_b, skip, L):
    n_groups = len(u_ts)
    nblk = HY_WIDTH // CONV_CB
    smem = pl.BlockSpec(memory_space=pltpu.SMEM)
    in_specs = [smem, smem, smem, pl.BlockSpec((HY_ORDER, CONV_CB, 2 * L), lambda i: (0, i, 0))]
    args = [conv_w.astype(F32), conv_b.astype(F32), skip.astype(F32), kf.reshape(HY_ORDER, HY_WIDTH, 2 * L)]
    out_specs, out_shape = [], []
    for u in u_ts:
        B = u.shape[0]
        for s in range(3):
            in_specs.append(pl.BlockSpec((B, CONV_CB, L), lambda i, s=s: (0, i + s * nblk, 0)))
            args.append(u)
        out_specs.append(pl.BlockSpec((B, CONV_CB, L), lambda i: (0, i, 0)))
        out_shape.append(jax.ShapeDtypeStruct((B, HY_WIDTH, L), F32))
    return pl.pallas_call(
        functools.partial(_hyena_kernel, L=L, n_groups=n_groups),
        grid=(nblk,),
        in_specs=in_specs,
        out_specs=out_specs,
        out_shape=out_shape,
        scratch_shapes=[pltpu.VMEM((CONV_UNROLL, HY_ORDER, MXU, 2 * L), BF16)]
        + [pltpu.VMEM((CONV_UNROLL, 3, u.shape[0], L), F32) for u in u_ts],
        compiler_params=_cparams(1),
        name="hyena_conv",
    )(*args)


def _rms(x, n):
    return lax.rsqrt(jnp.sum(x * x, axis=-1, keepdims=True) * (1.0 / n) + EPS)


def _mla_prep_kernel(m_ref, ct_ref, st_ref, gqa_ref, gkva_ref, gq_ref, gqs_ref, gk_ref, gks_ref,
                     wqa_ref, wqb_ref, wka_ref, ea_ref, eb_ref, wv_ref, q_ref, k_ref, v_ref):
    m = m_ref[0].astype(F32)
    cq = m[:, :Q_LORA]
    ckv = m[:, Q_LORA:Q_LORA + KV_LORA]
    kr = m_ref[0][:, Q_LORA + KV_LORA:Q_LORA + KV_LORA + LANE]
    cqn = (cq * _rms(cq, Q_LORA) * gqa_ref[...]).astype(BF16)
    ckvn = (ckv * _rms(ckv, KV_LORA) * gkva_ref[...]).astype(BF16)
    qa = jnp.dot(cqn, wqa_ref[...], preferred_element_type=F32)
    qb = jnp.dot(cqn, wqb_ref[...], preferred_element_type=F32)
    ka = jnp.dot(ckvn, wka_ref[...], preferred_element_type=F32) + jnp.dot(kr, ea_ref[...], preferred_element_type=F32)
    kb = jnp.dot(kr, eb_ref[...], preferred_element_type=F32)
    v_ref[0] = jnp.dot(ckvn, wv_ref[...], preferred_element_type=F32).astype(BF16)
    ct, st = ct_ref[...], st_ref[...]
    scale = QK_HEAD ** -0.5 * math.log2(math.e)
    qc, qs = gq_ref[...] * ct * scale, gqs_ref[...] * st * scale
    kc, ks = gk_ref[...] * ct, gks_ref[...] * st
    for h in range(MLA_HEADS):
        sl = slice(h * HEAD_PAD, (h + 1) * HEAD_PAD)
        q_ref[0, h] = (_rms(qa[:, sl], QK_HEAD) * (qa[:, sl] * qc + qb[:, sl] * qs)).astype(BF16)
        k_ref[0, h] = (_rms(ka[:, sl], QK_HEAD) * (ka[:, sl] * kc + kb[:, sl] * ks)).astype(BF16)


def _mla_prep(m, P, tm):
    B, L, _ = m.shape
    w = [P[n] for n in ("wqa", "wqb", "wka", "ea", "eb", "wv")]
    g = [P[n] for n in ("gqa", "gkva", "gq", "gqs", "gk", "gks")]
    hp = MLA_HEADS * HEAD_PAD
    return pl.pallas_call(
        _mla_prep_kernel,
        grid=(B, L // tm),
        in_specs=[pl.BlockSpec((1, tm, MLA_IN), lambda b, i: (b, i, 0)),
                  pl.BlockSpec((tm, HEAD_PAD), lambda b, i: (i, 0)),
                  pl.BlockSpec((tm, HEAD_PAD), lambda b, i: (i, 0))]
        + [_full(a.shape) for a in g] + [_full(a.shape) for a in w],
        out_specs=[
            pl.BlockSpec((1, MLA_HEADS, tm, HEAD_PAD), lambda b, i: (b, 0, i, 0)),
            pl.BlockSpec((1, MLA_HEADS, tm, HEAD_PAD), lambda b, i: (b, 0, i, 0)),
            pl.BlockSpec((1, tm, MLA_HEADS * V_HEAD), lambda b, i: (b, i, 0)),
        ],
        out_shape=[
            jax.ShapeDtypeStruct((B, MLA_HEADS, L, HEAD_PAD), BF16),
            jax.ShapeDtypeStruct((B, MLA_HEADS, L, HEAD_PAD), BF16),
            jax.ShapeDtypeStruct((B, L, MLA_HEADS * V_HEAD), BF16),
        ],
        compiler_params=_cparams(2),
        name="mla_prep",
    )(m, P["ctab"][:L], P["stab"][:L], *g, *w)


def _attn_kernel(q_ref, k_ref, v_ref, o_ref):
    L = v_ref.shape[1]
    one_col = jnp.where(lax.broadcasted_iota(jnp.int32, (L, LANE), 1) == 0, 1.0, 0.0).astype(BF16)
    for pair in range(MLA_HEADS // 2):
        lanes = pl.ds(pair * LANE, LANE)
        v_ext = jnp.concatenate([v_ref[0, :, lanes], one_col], axis=1)
        outs = []
        for h in (2 * pair, 2 * pair + 1):
            s = lax.dot_general(q_ref[0, h], k_ref[0, h], (((1,), (1,)), ((), ())), preferred_element_type=F32)
            p = jnp.exp2(s - jnp.max(s, axis=-1, keepdims=True)).astype(BF16)
            pv = jnp.dot(p, v_ext, preferred_element_type=F32)
            outs.append(pv[:, :LANE] / pv[:, LANE:LANE + 1])
        lane = lax.broadcasted_iota(jnp.int32, outs[0].shape, 1)
        o_ref[0, :, lanes] = jnp.where(lane < V_HEAD, outs[0], outs[1]).astype(BF16)


def _attention(q, k, v, tq):
    B, H, L, _ = q.shape
    return pl.pallas_call(
        _attn_kernel,
        grid=(B, L // tq),
        in_specs=[
            pl.BlockSpec((1, H, tq, HEAD_PAD), lambda b, i: (b, 0, i, 0)),
            pl.BlockSpec((1, H, L, HEAD_PAD), lambda b, i: (b, 0, 0, 0)),
            pl.BlockSpec((1, L, H * V_HEAD), lambda b, i: (b, 0, 0)),
        ],
        out_specs=pl.BlockSpec((1, tq, H * V_HEAD), lambda b, i: (b, i, 0)),
        out_shape=jax.ShapeDtypeStruct((B, L, H * V_HEAD), BF16),
        compiler_params=_cparams(2),
        name="mla_attention",
    )(q, k, v)


def _pack_rows(a):
    bits = lax.bitcast_convert_type(a.astype(BF16).astype(F32), jnp.uint32)
    return (bits[:, :PACKED] >> 16) | (bits[:, PACKED:] & jnp.uint32(0xFFFF0000))


def _unpack_rows(p):
    lo = lax.bitcast_convert_type(p << 16, F32)
    hi = lax.bitcast_convert_type(p & jnp.uint32(0xFFFF0000), F32)
    return jnp.concatenate([lo, hi], axis=1)


def _merge_kernel(x_ref, yh_ref, ym_ref, g_ref, who_ref, wmo_ref, wo_ref, nf_ref, rwh_ref, rwl_ref, rb_ref,
                  x1_ref, h2_ref, ids_ref, gate_ref, cnt_ref):
    tm = x_ref.shape[0]
    sub = tm // 2

    @pl.when(jnp.logical_and(pl.program_id(0) == 0, pl.program_id(1) == 0))
    def _():
        cnt_ref[...] = jnp.zeros_like(cnt_ref)

    eid = lax.broadcasted_iota(jnp.int32, (N_EXPERTS, sub), 0)
    for r in range(0, tm, sub):
        rows = pl.ds(r, sub)
        g = g_ref[rows, :].astype(F32)
        a = lax.dot_general(yh_ref[:, rows].astype(BF16), who_ref[...], (((0,), (0,)), ((), ())),
                            preferred_element_type=F32)
        b = jnp.dot(ym_ref[rows, :], wmo_ref[...], preferred_element_type=F32)
        merged = jax.nn.sigmoid(g[:, :D_MODEL]) * a + jax.nn.sigmoid(g[:, D_MODEL:]) * b
        x1 = x_ref[rows, :] + jnp.dot(merged.astype(BF16), wo_ref[...], preferred_element_type=F32)
        x1_ref[rows, :] = x1
        h2 = x1 * lax.rsqrt(jnp.mean(x1 * x1, axis=-1, keepdims=True) + EPS) * nf_ref[...]
        h2_ref[rows, :] = _pack_rows(h2)
        h_hi = h2.astype(BF16)
        h_lo = (h2 - h_hi.astype(F32)).astype(BF16)
        logits = (jnp.dot(h_hi, rwh_ref[...], preferred_element_type=F32)
                  + jnp.dot(h_lo, rwh_ref[...], preferred_element_type=F32)
                  + jnp.dot(h_hi, rwl_ref[...], preferred_element_type=F32) + rb_ref[...])
        work = logits.T[:N_EXPERTS]
        picked = jnp.zeros((N_EXPERTS, sub), F32)
        vals, firsts = [], []
        for k in range(TOP_K):
            m = jnp.max(work, axis=0, keepdims=True)
            first = jnp.min(jnp.where(work == m, eid, N_EXPERTS), axis=0, keepdims=True)
            hot = eid == first
            vals.append(m)
            firsts.append(first)
            picked = jnp.where(hot, 1.0, picked)
            work = jnp.where(hot, -jnp.inf, work)
        ex = [jnp.exp(v - vals[0]) for v in vals]
        inv = 1.0 / (ex[0] + ex[1] + ex[2] + ex[3])
        pad = jnp.zeros((LANE - TOP_K, sub), F32)
        ids_t = jnp.concatenate([f.astype(F32) for f in firsts] + [pad], axis=0)
        gates_t = jnp.concatenate([e * inv for e in ex] + [pad], axis=0)
        ids_ref[rows, :] = ids_t.T.astype(jnp.int32)
        gate_ref[rows, :] = gates_t.T
        cnt_ref[...] += jnp.sum(picked, axis=1, keepdims=True)


def _merge(x, yh_t, ym, g, P, tm):
    B, L, _ = x.shape
    nt = L // tm
    w = [P[n] for n in ("w_hy_o", "w_mla_o", "w_out", "norm_ffn", "router_w_hi", "router_w_lo", "router_b")]
    return pl.pallas_call(
        _merge_kernel,
        grid=(B, nt),
        in_specs=[
            pl.BlockSpec((None, tm, D_MODEL), lambda b, i: (b, i, 0)),
            pl.BlockSpec((None, HY_WIDTH, tm), lambda b, i: (b, 0, i)),
            pl.BlockSpec((None, tm, MLA_HEADS * V_HEAD), lambda b, i: (b, i, 0)),
            pl.BlockSpec((None, tm, 2 * D_MODEL), lambda b, i: (b, i, 0)),
        ] + [_full(a.shape) for a in w],
        out_specs=[
            pl.BlockSpec((tm, D_MODEL), lambda b, i: (b * nt + i, 0)),
            pl.BlockSpec((tm, PACKED), lambda b, i: (b * nt + i, 0)),
            pl.BlockSpec((tm, LANE), lambda b, i: (b * nt + i, 0)),
            pl.BlockSpec((tm, LANE), lambda b, i: (b * nt + i, 0)),
            pl.BlockSpec((N_EXPERTS, 1), lambda b, i: (0, 0)),
        ],
        out_shape=[
            jax.ShapeDtypeStruct((B * L, D_MODEL), F32),
            jax.ShapeDtypeStruct((B * L, PACKED), jnp.uint32),
            jax.ShapeDtypeStruct((B * L, LANE), jnp.int32),
            jax.ShapeDtypeStruct((B * L, LANE), F32),
            jax.ShapeDtypeStruct((N_EXPERTS, 1), F32),
        ],
        compiler_params=_cparams(2),
        name="merge_router",
    )(x, yh_t, ym, g, *w)


def _expert_kernel(be_ref, na_ref, end_ref, x_ref, wgu_hbm, bgu_ref, wd_hbm, bd_ref, y_ref,
                   wgu_buf, wd_buf, sems, slot_ref):
    i = pl.program_id(0)
    active = i < na_ref[0]
    expert = be_ref[i]
    first = jnp.logical_and(active, jnp.logical_or(i == 0, expert != be_ref[jnp.maximum(i - 1, 0)]))

    def weight_copies(e, s):
        return (pltpu.make_async_copy(wgu_hbm.at[e], wgu_buf.at[s], sems.at[0, s]),
                pltpu.make_async_copy(wd_hbm.at[e], wd_buf.at[s], sems.at[1, s]))

    @pl.when(i == 0)
    def _():
        slot_ref[0] = 1
        for cp in weight_copies(expert, 0):
            cp.start()

    @pl.when(first)
    def _():
        s = 1 - slot_ref[0]
        slot_ref[0] = s
        for cp in weight_copies(expert, s):
            cp.wait()
        run_end = end_ref[expert]

        @pl.when(run_end < na_ref[0])
        def _():
            for cp in weight_copies(be_ref[jnp.minimum(run_end, pl.num_programs(0) - 1)], 1 - s):
                cp.start()

    slot = slot_ref[0]

    @pl.when(active)
    def _():
        x = _unpack_rows(x_ref[...])
        gu = jnp.dot(x, wgu_buf[slot], preferred_element_type=F32) + bgu_ref[0]
        gate = jnp.minimum(gu[:, :D_FF], SWIGLU_LIMIT)
        up = jnp.clip(gu[:, D_FF:], -SWIGLU_LIMIT, SWIGLU_LIMIT)
        act = (up + 1.0) * (gate * jax.nn.sigmoid(SWIGLU_ALPHA * gate))
        y_ref[...] = _pack_rows(jnp.dot(act, wd_buf[slot], preferred_element_type=F32) + bd_ref[0])

    @pl.when(i >= na_ref[0])
    def _():
        y_ref[...] = jnp.zeros_like(y_ref)


def _experts(plan, xs, P):
    n_rows = xs.shape[0]
    nb = n_rows // MOE_BLOCK
    grid_spec = pltpu.PrefetchScalarGridSpec(
        num_scalar_prefetch=3,
        grid=(nb,),
        in_specs=[
            pl.BlockSpec((MOE_BLOCK, PACKED), lambda i, be, *_: (i, 0)),
            pl.BlockSpec(memory_space=pl.ANY),
            pl.BlockSpec((1, 1, 2 * D_FF), lambda i, be, *_: (be[i], 0, 0)),
            pl.BlockSpec(memory_space=pl.ANY),
            pl.BlockSpec((1, 1, D_MODEL), lambda i, be, *_: (be[i], 0, 0)),
        ],
        out_specs=pl.BlockSpec((MOE_BLOCK, PACKED), lambda i, be, *_: (i, 0)),
        scratch_shapes=[
            pltpu.VMEM((2, D_MODEL, 2 * D_FF), F32),
            pltpu.VMEM((2, D_FF, D_MODEL), F32),
            pltpu.SemaphoreType.DMA((2, 2)),
            pltpu.SMEM((1,), jnp.int32),
        ],
    )
    return pl.pallas_call(
        _expert_kernel,
        grid_spec=grid_spec,
        out_shape=jax.ShapeDtypeStruct((n_rows, PACKED), jnp.uint32),
        compiler_params=_cparams(1),
        name="moe_experts",
    )(*plan, xs, P["w_gu"], P["b_gu"], P["w_down"], P["b_down"])


def _combine_kernel(x1_ref, yk_ref, gt_ref, o_ref):
    gt = gt_ref[...]
    acc = _unpack_rows(yk_ref[0]) * gt[:, 0:1]
    for k in range(1, TOP_K):
        acc = acc + _unpack_rows(yk_ref[k]) * gt[:, k:k + 1]
    o_ref[...] = x1_ref[...] + acc


def _combine(x1, yk, gates, tm):
    T = x1.shape[0]
    return pl.pallas_call(
        _combine_kernel,
        grid=(T // tm,),
        in_specs=[
            pl.BlockSpec((tm, D_MODEL), lambda i: (i, 0)),
            pl.BlockSpec((TOP_K, tm, PACKED), lambda i: (0, i, 0)),
            pl.BlockSpec((tm, LANE), lambda i: (i, 0)),
        ],
        out_specs=pl.BlockSpec((tm, D_MODEL), lambda i: (i, 0)),
        out_shape=jax.ShapeDtypeStruct((T, D_MODEL), F32),
        compiler_params=_cparams(1),
        name="moe_combine",
    )(x1, yk, gates)


def _route_kernel(ids_ref, cnt_ref, dest_ref, run_ref, start_ref, *, tm):
    @pl.when(pl.program_id(0) == 0)
    def _():
        cnt = cnt_ref[...]
        padded = jnp.floor((cnt + (MOE_BLOCK - 1)) * (1.0 / MOE_BLOCK)) * MOE_BLOCK
        scan = jnp.broadcast_to(padded, (8, LANE))
        lane8 = lax.broadcasted_iota(jnp.int32, (8, LANE), 1)
        s = 1
        while s < LANE:
            scan = scan + jnp.where(lane8 >= s, pltpu.roll(scan, s, 1), 0.0)
            s *= 2
        start_ref[...] = scan[0:1] - padded
        run_ref[...] = jnp.zeros_like(run_ref)

    lane = lax.broadcasted_iota(jnp.int32, (tm, LANE), 1)
    ids = ids_ref[...]
    hots = [lane == ids[:, k:k + 1] for k in range(TOP_K)]
    chosen = jnp.zeros((tm, LANE), F32)
    for hot in hots:
        chosen = jnp.where(hot, 1.0, chosen)
    r = lax.broadcasted_iota(jnp.int32, (tm, tm), 0)
    c = lax.broadcasted_iota(jnp.int32, (tm, tm), 1)
    earlier = jnp.where(c < r, 1.0, 0.0).astype(BF16)
    before = jnp.dot(earlier, chosen.astype(BF16), preferred_element_type=F32) + (run_ref[...] + start_ref[...])
    dest = jnp.zeros((tm, LANE), F32)
    for k, hot in enumerate(hots):
        dest = jnp.where(lane == k, jnp.sum(jnp.where(hot, before, 0.0), axis=-1, keepdims=True), dest)
    dest_ref[...] = dest.T[:8].astype(jnp.int32)
    run_ref[...] += jnp.sum(chosen, axis=0, keepdims=True)


def _route(ids, counts, tm):
    T = ids.shape[0]
    dest = pl.pallas_call(
        functools.partial(_route_kernel, tm=tm),
        grid=(T // tm,),
        in_specs=[pl.BlockSpec((tm, LANE), lambda i: (i, 0)), _full((1, LANE))],
        out_specs=pl.BlockSpec((8, tm), lambda i: (0, i)),
        out_shape=jax.ShapeDtypeStruct((8, T), jnp.int32),
        scratch_shapes=[pltpu.VMEM((1, LANE), F32), pltpu.VMEM((1, LANE), F32)],
        compiler_params=_cparams(1),
        name="moe_route",
    )(ids, jnp.pad(counts[:, 0], (0, LANE - N_EXPERTS)).reshape(1, LANE))
    counts = counts[:, 0].astype(jnp.int32)
    pad_end = jnp.cumsum((counts + MOE_BLOCK - 1) // MOE_BLOCK * MOE_BLOCK)
    n_blocks = -(-(T * TOP_K + N_EXPERTS * (MOE_BLOCK - 1)) // MOE_BLOCK)
    block_start = jnp.arange(n_blocks, dtype=jnp.int32) * MOE_BLOCK
    block_e = jnp.sum((block_start[:, None] >= pad_end[None, :]).astype(jnp.int32), axis=1)
    block_e = jnp.minimum(block_e, N_EXPERTS - 1)
    end_blk = (pad_end // MOE_BLOCK).astype(jnp.int32)
    plan = (block_e.astype(jnp.int32), end_blk[-1:], end_blk)
    dest_flat = dest[:TOP_K].reshape(TOP_K * T)
    return dest_flat, plan, n_blocks * MOE_BLOCK


def _sc_workers():
    info = plsc.get_sparse_core_info()
    return info.num_cores, info.num_cores * info.num_subcores


def _sc_scatter_rows(rows, dest_flat, n_rows):
    T, W = rows.shape
    n_cores, n_workers = _sc_workers()
    per_w = T // n_workers
    assert per_w * n_workers == T and per_w % SC_CHUNK == 0

    def body(rows_hbm, dest_hbm, out_hbm, buf, *idx):
        wid = lax.axis_index("s") * n_cores + lax.axis_index("c")

        @pl.loop(0, per_w // SC_CHUNK)
        def _(j):
            off = pl.multiple_of(wid * per_w + j * SC_CHUNK, SC_CHUNK)
            pltpu.sync_copy(rows_hbm.at[pl.ds(off, SC_CHUNK)], buf)
            for k in range(TOP_K):
                pltpu.sync_copy(dest_hbm.at[pl.ds(k * T + off, SC_CHUNK)], idx[k])
            for k in range(TOP_K):
                pltpu.sync_copy(buf, out_hbm.at[idx[k]])

    return pl.kernel(
        body,
        out_type=jax.ShapeDtypeStruct((n_rows, W), rows.dtype),
        mesh=plsc.VectorSubcoreMesh(core_axis_name="c", subcore_axis_name="s"),
        scratch_types=[pltpu.VMEM((SC_CHUNK, W), rows.dtype)] + [pltpu.VMEM((SC_CHUNK,), jnp.int32)] * TOP_K,
        name="moe_dispatch_sc",
    )(rows, dest_flat)


def _sc_gather_rows(table, dest_flat):
    N = dest_flat.shape[0]
    W = table.shape[1]
    n_cores, n_workers = _sc_workers()
    per_w = N // n_workers
    assert per_w * n_workers == N and per_w % SC_CHUNK == 0

    def body(table_hbm, dest_hbm, out_hbm, buf, idx):
        wid = lax.axis_index("s") * n_cores + lax.axis_index("c")

        @pl.loop(0, per_w // SC_CHUNK)
        def _(j):
            off = pl.multiple_of(wid * per_w + j * SC_CHUNK, SC_CHUNK)
            pltpu.sync_copy(dest_hbm.at[pl.ds(off, SC_CHUNK)], idx)
            pltpu.sync_copy(table_hbm.at[idx], buf)
            pltpu.sync_copy(buf, out_hbm.at[pl.ds(off, SC_CHUNK)])

    return pl.kernel(
        body,
        out_type=jax.ShapeDtypeStruct((N, W), table.dtype),
        mesh=plsc.VectorSubcoreMesh(core_axis_name="c", subcore_axis_name="s"),
        scratch_types=[pltpu.VMEM((SC_CHUNK, W), table.dtype), pltpu.VMEM((SC_CHUNK,), jnp.int32)],
        name="moe_gather_sc",
    )(table, dest_flat)


def _moe(groups, P, tm):
    staged = []
    for x1, h2p, ids, gates, counts in groups:
        dest_flat, plan, n_rows = _route(ids, counts, tm)
        staged.append((dest_flat, plan, _sc_scatter_rows(h2p, dest_flat, n_rows)))
    ys = []
    for dest_flat, plan, xs in staged:
        if ys:
            xs, ys[-1] = lax.optimization_barrier((xs, ys[-1]))
        ys.append(_experts(plan, xs, P))
    outs = []
    for (x1, _, _, gates, _), (dest_flat, _, _), y in zip(groups, staged, ys):
        yk = _sc_gather_rows(y, dest_flat).reshape(TOP_K, x1.shape[0], PACKED)
        outs.append(_combine(x1, yk, gates, tm))
    return outs


def _prep_params(norm_mix, w_in, q_a_norm, w_uq, kv_a_norm, w_ukv, q_norm, k_norm, w_hy_o, w_mla_o, w_out,
                 norm_ffn, router_w, router_b, w_gu, b_gu, w_down, b_down, L):
    P = {}
    P["norm_mix"] = norm_mix.astype(F32).reshape(1, D_MODEL)
    P["w_hy_t"] = w_in[:, :HY_END].T.astype(BF16)
    P["w_m"] = jnp.pad(w_in[:, HY_END:KR_END], ((0, 0), (0, MLA_IN - (KR_END - HY_END)))).astype(BF16)
    P["w_g"] = w_in[:, KR_END:].astype(BF16)

    half = QK_ROPE // 2
    swap = jnp.concatenate([jnp.arange(QK_NOPE), QK_NOPE + half + jnp.arange(half), QK_NOPE + jnp.arange(half)])
    rope_lane = (jnp.arange(QK_HEAD) >= QK_NOPE)

    def slots(w):
        return jnp.pad(w, ((0, 0), (0, 0), (0, HEAD_PAD - QK_HEAD))).reshape(w.shape[0], MLA_HEADS * HEAD_PAD)

    wq = w_uq.astype(F32).reshape(Q_LORA, MLA_HEADS, QK_HEAD)
    P["wqa"] = slots(wq).astype(BF16)
    P["wqb"] = slots(jnp.where(rope_lane, wq[:, :, swap], 0.0)).astype(BF16)
    wkv = w_ukv.astype(F32).reshape(KV_LORA, MLA_HEADS, QK_NOPE + V_HEAD)
    P["wka"] = slots(jnp.pad(wkv[:, :, :QK_NOPE], ((0, 0), (0, 0), (0, QK_ROPE)))).astype(BF16)
    P["wv"] = wkv[:, :, QK_NOPE:].reshape(KV_LORA, MLA_HEADS * V_HEAD).astype(BF16)
    eye = jnp.eye(LANE, QK_ROPE, dtype=F32)
    e_a = jnp.pad(eye, ((0, 0), (QK_NOPE, 0)))
    e_b = e_a[:, swap] * rope_lane
    P["ea"] = slots(jnp.broadcast_to(e_a[:, None, :], (LANE, MLA_HEADS, QK_HEAD))).astype(BF16)
    P["eb"] = slots(jnp.broadcast_to(e_b[:, None, :], (LANE, MLA_HEADS, QK_HEAD))).astype(BF16)

    sign = jnp.concatenate([jnp.zeros(QK_NOPE), -jnp.ones(half), jnp.ones(half)]).astype(F32)

    def lane_pad(g):
        return jnp.pad(g, (0, HEAD_PAD - QK_HEAD)).reshape(1, HEAD_PAD)

    for name, g in (("gq", q_norm.astype(F32)), ("gk", k_norm.astype(F32))):
        P[name] = lane_pad(g)
        P[name + "s"] = lane_pad(g[swap] * sign)
    P["gqa"] = q_a_norm.astype(F32).reshape(1, Q_LORA)
    P["gkva"] = kv_a_norm.astype(F32).reshape(1, KV_LORA)
    inv = 1.0 / (ROPE_THETA ** (jnp.arange(0, QK_ROPE, 2, dtype=F32) / QK_ROPE))
    ang = jnp.arange(L, dtype=F32)[:, None] * inv[None, :]
    cos, sin = jnp.cos(ang), jnp.sin(ang)
    P["ctab"] = jnp.concatenate([jnp.ones((L, QK_NOPE), F32), cos, cos, jnp.zeros((L, HEAD_PAD - QK_HEAD), F32)], axis=1)
    P["stab"] = jnp.concatenate([jnp.zeros((L, QK_NOPE), F32), sin, sin, jnp.zeros((L, HEAD_PAD - QK_HEAD), F32)], axis=1)

    P["w_hy_o"] = w_hy_o.astype(BF16)
    P["w_mla_o"] = w_mla_o.astype(BF16)
    P["w_out"] = w_out.astype(BF16)
    P["norm_ffn"] = norm_ffn.astype(F32).reshape(1, D_MODEL)
    rw = jnp.pad(router_w.astype(F32), ((0, 0), (0, LANE - N_EXPERTS)))
    P["router_w_hi"] = rw.astype(BF16)
    P["router_w_lo"] = (rw - P["router_w_hi"].astype(F32)).astype(BF16)
    P["router_b"] = jnp.pad(router_b.astype(F32), (0, LANE - N_EXPERTS)).reshape(1, LANE)
    P["w_gu"] = w_gu.astype(F32)
    P["b_gu"] = b_gu.astype(F32).reshape(N_EXPERTS, 1, 2 * D_FF)
    P["w_down"] = w_down.astype(F32)
    P["b_down"] = b_down.astype(F32).reshape(N_EXPERTS, 1, D_MODEL)
    return P


def _tile(n, want):
    t = min(n, want)
    assert n % t == 0
    return t


def _layer(xs, norm_mix, w_in, hy_conv_w, hy_conv_b, hy_fw1, hy_fb1, hy_fw2, hy_fb2, hy_fw3, hy_freq, hy_decay, hy_skip,
           q_a_norm, w_uq, kv_a_norm, w_ukv, q_norm, k_norm, w_hy_o, w_mla_o, w_out,
           norm_ffn, router_w, router_b, w_gu, b_gu, w_down, b_down):
    L = xs[0].shape[1]
    assert all(x.shape[1] == L and x.shape[2] == D_MODEL for x in xs) and L % MXU == 0
    P = _prep_params(norm_mix, w_in, q_a_norm, w_uq, kv_a_norm, w_ukv, q_norm, k_norm, w_hy_o, w_mla_o, w_out,
                     norm_ffn, router_w, router_b, w_gu, b_gu, w_down, b_down, L)
    kf = _hyena_filters(L, hy_fw1, hy_fb1, hy_fw2, hy_fb2, hy_fw3, hy_freq, hy_decay)
    tm = _tile(L, 512)
    proj = [_inproj(x, P["norm_mix"], P["w_hy_t"], P["w_m"], P["w_g"], tm) for x in xs]
    y_ts = _hyena([u_t for u_t, _, _ in proj], kf, hy_conv_w, hy_conv_b, hy_skip, L)
    mixed = []
    for x, (_, m, g), y_t in zip(xs, proj, y_ts):
        q, k, v = _mla_prep(m, P, tm)
        y_mla = _attention(q, k, v, _tile(L, 512))
        mixed.append(_merge(x, y_t, y_mla, g, P, _tile(L, 1024)))
    order = sorted(range(len(xs)), key=lambda n: -xs[n].shape[0])
    moe_out = _moe([mixed[n] for n in order], P, tm)
    outs = [None] * len(xs)
    for n, y in zip(order, moe_out):
        outs[n] = y.reshape(xs[n].shape)
    return outs


def kernel(x_prompt, x_sample, norm_mix, w_in, hy_conv_w, hy_conv_b, hy_fw1, hy_fb1, hy_fw2, hy_fb2, hy_fw3, hy_freq, hy_decay, hy_skip, q_a_norm, w_uq, kv_a_norm, w_ukv, q_norm, k_norm, w_hy_o, w_mla_o, w_out, norm_ffn, router_w, router_b, w_gu, b_gu, w_down, b_down):
    depth = norm_mix.shape[0]
    xs = [x_prompt, x_sample]
    weights = (norm_mix, w_in, hy_conv_w, hy_conv_b, hy_fw1, hy_fb1, hy_fw2, hy_fb2, hy_fw3, hy_freq, hy_decay, hy_skip,
               q_a_norm, w_uq, kv_a_norm, w_ukv, q_norm, k_norm, w_hy_o, w_mla_o, w_out,
               norm_ffn, router_w, router_b, w_gu, b_gu, w_down, b_down)
    for l in range(depth):
        xs = _layer(xs, *[w[l] for w in weights])
    return (xs[0], xs[1])
```

```python
import functools
import math

import jax
import jax.numpy as jnp
from jax import lax
from jax.experimental import pallas as pl
from jax.experimental.pallas import tpu as pltpu
from jax.experimental.pallas import tpu_sc as plsc

F32 = jnp.float32
BF16 = jnp.bfloat16

D_MODEL = 1024
HY_WIDTH = 512
HY_ORDER = 2
HY_EMB_BANDS = 16
HY_FILTER_DIM = 64
MLA_HEADS = 8
V_HEAD = 64
QK_NOPE = 64
QK_ROPE = 32
QK_HEAD = QK_NOPE + QK_ROPE
Q_LORA = 256
KV_LORA = 128
ROPE_THETA = 10000.0
HY_END = 3 * HY_WIDTH
Q_END = HY_END + Q_LORA
KV_END = Q_END + KV_LORA
KR_END = KV_END + QK_ROPE
G1_END = KR_END + D_MODEL
N_EXPERTS = 32
TOP_K = 4
D_FF = D_MODEL
SWIGLU_LIMIT = 7.0
SWIGLU_ALPHA = 1.702
EPS = 1e-6

LANE = 128
MXU = 256
HEAD_PAD = LANE
MLA_IN = 512
CONV_CB = 8
CONV_UNROLL = 4
MOE_BLOCK = 512
PACKED = D_MODEL // 2
SC_CHUNK = 64
VMEM_LIMIT = 52 * 1024 * 1024


def _cparams(n_axes):
    return pltpu.CompilerParams(dimension_semantics=("arbitrary",) * n_axes, vmem_limit_bytes=VMEM_LIMIT)


def _full(shape):
    n = len(shape)
    return pl.BlockSpec(shape, lambda *_: (0,) * n)


def _inproj_kernel(x_ref, g_ref, why_ref, wm_ref, wg_ref, u_ref, m_ref, gate_ref):
    x = x_ref[...]
    r = lax.rsqrt(jnp.mean(x * x, axis=-1, keepdims=True) + EPS)
    h = (x * r * g_ref[...]).astype(BF16)
    u_ref[...] = lax.dot_general(why_ref[...], h, (((1,), (1,)), ((), ())), preferred_element_type=F32)
    m_ref[...] = jnp.dot(h, wm_ref[...], preferred_element_type=F32).astype(BF16)
    gate_ref[...] = jnp.dot(h, wg_ref[...], preferred_element_type=F32).astype(BF16)


def _inproj(x, g, w_hy_t, w_m, w_g, tm):
    B, L, _ = x.shape
    return pl.pallas_call(
        _inproj_kernel,
        grid=(B, L // tm),
        in_specs=[
            pl.BlockSpec((None, tm, D_MODEL), lambda b, i: (b, i, 0)),
            _full((1, D_MODEL)),
            _full(w_hy_t.shape),
            _full(w_m.shape),
            _full(w_g.shape),
        ],
        out_specs=[
            pl.BlockSpec((None, HY_END, tm), lambda b, i: (b, 0, i)),
            pl.BlockSpec((None, tm, MLA_IN), lambda b, i: (b, i, 0)),
            pl.BlockSpec((None, tm, 2 * D_MODEL), lambda b, i: (b, i, 0)),
        ],
        out_shape=[
            jax.ShapeDtypeStruct((B, HY_END, L), F32),
            jax.ShapeDtypeStruct((B, L, MLA_IN), BF16),
            jax.ShapeDtypeStruct((B, L, 2 * D_MODEL), BF16),
        ],
        compiler_params=_cparams(2),
        name="inproj",
    )(x, g, w_hy_t, w_m, w_g)


def _filter_kernel(bands_ref, w1t_ref, w1c_ref, w1s_ref, b1_ref, w2_ref, b2_ref, fr_ref, w3f_ref, w3b_ref, dec_ref,
                   out_ref, *, L):
    hi = lax.Precision.HIGHEST
    n = 2 * L
    j = lax.broadcasted_iota(jnp.int32, (1, n), 1)
    pos = jnp.abs(j - L).astype(F32)
    t01 = pos / max(L - 1, 1)
    ang = ((2.0 * math.pi / L) * pos) * bands_ref[...]
    pre = (w1t_ref[...] * t01
           + jnp.dot(w1c_ref[...], jnp.cos(ang), precision=hi, preferred_element_type=F32)
           - jnp.dot(w1s_ref[...], jnp.sin(ang), precision=hi, preferred_element_type=F32)
           + b1_ref[...])
    z = jnp.sin(fr_ref[0] * pre)
    z = jnp.sin(fr_ref[1] * (jnp.dot(w2_ref[...], z, precision=hi, preferred_element_type=F32) + b2_ref[...]))
    win = jnp.exp(-t01 * jnp.abs(dec_ref[...]))
    h_b = jnp.dot(w3b_ref[...], z[:, :L], precision=hi, preferred_element_type=F32) * win[:, :L]
    h_f = jnp.dot(w3f_ref[...], z[:, L:], precision=hi, preferred_element_type=F32) * win[:, L:]
    out_ref[:, :L] = jnp.where(j[:, :L] == 0, 0.0, h_b)
    out_ref[:, L:] = h_f


def _hyena_filters(L, fw1, fb1, fw2, fb2, fw3, freq, decay):
    rows = HY_ORDER * HY_WIDTH
    rb = 256
    bands = jnp.linspace(1e-4, HY_EMB_BANDS - 1, HY_EMB_BANDS, dtype=F32).reshape(HY_EMB_BANDS, 1)
    w1 = fw1.astype(F32).T
    w3 = fw3.astype(F32).T
    fd = HY_FILTER_DIM
    return pl.pallas_call(
        functools.partial(_filter_kernel, L=L),
        grid=(rows // rb,),
        in_specs=[
            _full((HY_EMB_BANDS, 1)),
            _full((fd, 1)),
            _full((fd, HY_EMB_BANDS)),
            _full((fd, HY_EMB_BANDS)),
            _full((fd, 1)),
            _full((fd, fd)),
            _full((fd, 1)),
            _full((2, fd, 1)),
            pl.BlockSpec((rb, fd), lambda i: (i, 0)),
            pl.BlockSpec((rb, fd), lambda i: (i, 0)),
            pl.BlockSpec((rb, 1), lambda i: (i, 0)),
        ],
        out_specs=pl.BlockSpec((rb, 2 * L), lambda i: (i, 0)),
        out_shape=jax.ShapeDtypeStruct((rows, 2 * L), F32),
        compiler_params=_cparams(1),
        name="hyena_filters",
    )(bands, w1[:, :1], w1[:, 1:1 + HY_EMB_BANDS], w1[:, 1 + HY_EMB_BANDS:], fb1.astype(F32).reshape(fd, 1),
      fw2.astype(F32).T, fb2.astype(F32).reshape(fd, 1), freq.astype(F32).reshape(2, fd, 1),
      w3[:rows], w3[rows:], decay.astype(F32).reshape(rows, 1))


def _shift_conv(u, w0, w1, w2, b):
    L = u.shape[1]
    lane = lax.broadcasted_iota(jnp.int32, (1, L), 1)
    w0_row = jnp.where(lane == 0, 0.0, w0)
    w2_row = jnp.where(lane == L - 1, 0.0, w2)
    return pltpu.roll(u, 1, 1) * w0_row + u * w1 + pltpu.roll(u, L - 1, 1) * w2_row + b


def _toeplitz_conv(z, tz_ref, L):
    B = z.shape[0]
    nj = L // MXU
    zb = z.astype(BF16)
    tiles = [zb[:, MXU * J:MXU * (J + 1)] for J in range(nj)]
    acc = [None] * nj
    for d in range(-(nj - 1), nj):
        js = [J for J in range(nj) if 0 <= J + d < nj]
        lhs = tiles[js[0]] if len(js) == 1 else jnp.concatenate([tiles[J] for J in js], axis=0)
        res = jnp.dot(lhs, tz_ref[:, L + MXU * d:L + MXU * (d + 1)], preferred_element_type=F32)
        for n, J in enumerate(js):
            part = res[n * B:(n + 1) * B]
            acc[J + d] = part if acc[J + d] is None else acc[J + d] + part
    return acc[0] if nj == 1 else jnp.concatenate(acc, axis=1)


def _hyena_kernel(cw_ref, cb_ref, skip_ref, kf_ref, *refs, L, n_groups):
    in_refs = refs[:3 * n_groups]
    out_refs = refs[3 * n_groups:4 * n_groups]
    tz_ref = refs[4 * n_groups]
    c0 = pl.program_id(0) * CONV_CB

    def channels(p, carry):
        for u in range(CONV_UNROLL):
            c = p * CONV_UNROLL + u
            ch = c0 + c
            for o in range(HY_ORDER):
                row = jnp.broadcast_to(kf_ref[o, pl.ds(c, 1), :], (MXU, 2 * L))
                tz_ref[u, o] = pltpu.roll(row, 0, 1, stride=1, stride_axis=0).astype(BF16)
            for g in range(n_groups):
                stage = refs[4 * n_groups + 1 + g]
                sec = []
                for s, ref in enumerate(in_refs[3 * g:3 * g + 3]):
                    stage[u, s] = ref[:, c, :]
                    k = s * HY_WIDTH + ch
                    sec.append(_shift_conv(stage[u, s], cw_ref[0, k], cw_ref[1, k], cw_ref[2, k], cb_ref[k]))
                x1, x2, z = sec
                for o, gate in enumerate((x1, x2)):
                    y = _toeplitz_conv(z, tz_ref.at[u, o], L) + z * skip_ref[o, ch]
                    z = gate * y
                out_refs[g][:, c, :] = z
        return carry

    lax.fori_loop(0, CONV_CB // CONV_UNROLL, channels, 0)


def _hyena(u_ts, kf, conv_w, conv_b, skip, L):
    n_groups = len(u_ts)
    nblk = HY_WIDTH // CONV_CB
    smem = pl.BlockSpec(memory_space=pltpu.SMEM)
    in_specs = [smem, smem, smem, pl.BlockSpec((HY_ORDER, CONV_CB, 2 * L), lambda i: (0, i, 0))]
    args = [conv_w.astype(F32), conv_b.astype(F32), skip.astype(F32), kf.reshape(HY_ORDER, HY_WIDTH, 2 * L)]
    out_specs, out_shape = [], []
    for u in u_ts:
        B = u.shape[0]
        for s in range(3):
            in_specs.append(pl.BlockSpec((B, CONV_CB, L), lambda i, s=s: (0, i + s * nblk, 0)))
            args.append(u)
        out_specs.append(pl.BlockSpec((B, CONV_CB, L), lambda i: (0, i, 0)))
        out_shape.append(jax.ShapeDtypeStruct((B, HY_WIDTH, L), F32))
    return pl.pallas_call(
        functools.partial(_hyena_kernel, L=L, n_groups=n_groups),
        grid=(nblk,),
        in_specs=in_specs,
        out_specs=out_specs,
        out_shape=out_shape,
        scratch_shapes=[pltpu.VMEM((CONV_UNROLL, HY_ORDER, MXU, 2 * L), BF16)]
        + [pltpu.VMEM((CONV_UNROLL, 3, u.shape[0], L), F32) for u in u_ts],
        compiler_params=_cparams(1),
        name="hyena_conv",
    )(*args)


def _rms(x, n):
    return lax.rsqrt(jnp.sum(x * x, axis=-1, keepdims=True) * (1.0 / n) + EPS)


def _mla_prep_kernel(m_ref, ct_ref, st_ref, gqa_ref, gkva_ref, gq_ref, gqs_ref, gk_ref, gks_ref,
                     wqa_ref, wqb_ref, wka_ref, ea_ref, eb_ref, wv_ref, q_ref, k_ref, v_ref):
    m = m_ref[0].astype(F32)
    cq = m[:, :Q_LORA]
    ckv = m[:, Q_LORA:Q_LORA + KV_LORA]
    kr = m_ref[0][:, Q_LORA + KV_LORA:Q_LORA + KV_LORA + LANE]
    cqn = (cq * _rms(cq, Q_LORA) * gqa_ref[...]).astype(BF16)
    ckvn = (ckv * _rms(ckv, KV_LORA) * gkva_ref[...]).astype(BF16)
    qa = jnp.dot(cqn, wqa_ref[...], preferred_element_type=F32)
    qb = jnp.dot(cqn, wqb_ref[...], preferred_element_type=F32)
    ka = jnp.dot(ckvn, wka_ref[...], preferred_element_type=F32) + jnp.dot(kr, ea_ref[...], preferred_element_type=F32)
    kb = jnp.dot(kr, eb_ref[...], preferred_element_type=F32)
    v_ref[0] = jnp.dot(ckvn, wv_ref[...], preferred_element_type=F32).astype(BF16)
    ct, st = ct_ref[...], st_ref[...]
    scale = QK_HEAD ** -0.5 * math.log2(math.e)
    qc, qs = gq_ref[...] * ct * scale, gqs_ref[...] * st * scale
    kc, ks = gk_ref[...] * ct, gks_ref[...] * st
    for h in range(MLA_HEADS):
        sl = slice(h * HEAD_PAD, (h + 1) * HEAD_PAD)
        q_ref[0, h] = (_rms(qa[:, sl], QK_HEAD) * (qa[:, sl] * qc + qb[:, sl] * qs)).astype(BF16)
        k_ref[0, h] = (_rms(ka[:, sl], QK_HEAD) * (ka[:, sl] * kc + kb[:, sl] * ks)).astype(BF16)


def _mla_prep(m, P, tm):
    B, L, _ = m.shape
    w = [P[n] for n in ("wqa", "wqb", "wka", "ea", "eb", "wv")]
    g = [P[n] for n in ("gqa", "gkva", "gq", "gqs", "gk", "gks")]
    hp = MLA_HEADS * HEAD_PAD
    return pl.pallas_call(
        _mla_prep_kernel,
        grid=(B, L // tm),
        in_specs=[pl.BlockSpec((1, tm, MLA_IN), lambda b, i: (b, i, 0)),
                  pl.BlockSpec((tm, HEAD_PAD), lambda b, i: (i, 0)),
                  pl.BlockSpec((tm, HEAD_PAD), lambda b, i: (i, 0))]
        + [_full(a.shape) for a in g] + [_full(a.shape) for a in w],
        out_specs=[
            pl.BlockSpec((1, MLA_HEADS, tm, HEAD_PAD), lambda b, i: (b, 0, i, 0)),
            pl.BlockSpec((1, MLA_HEADS, tm, HEAD_PAD), lambda b, i: (b, 0, i, 0)),
            pl.BlockSpec((1, tm, MLA_HEADS * V_HEAD), lambda b, i: (b, i, 0)),
        ],
        out_shape=[
            jax.ShapeDtypeStruct((B, MLA_HEADS, L, HEAD_PAD), BF16),
            jax.ShapeDtypeStruct((B, MLA_HEADS, L, HEAD_PAD), BF16),
            jax.ShapeDtypeStruct((B, L, MLA_HEADS * V_HEAD), BF16),
        ],
        compiler_params=_cparams(2),
        name="mla_prep",
    )(m, P["ctab"][:L], P["stab"][:L], *g, *w)


def _attn_kernel(q_ref, k_ref, v_ref, o_ref):
    L = v_ref.shape[1]
    one_col = jnp.where(lax.broadcasted_iota(jnp.int32, (L, LANE), 1) == 0, 1.0, 0.0).astype(BF16)
    for pair in range(MLA_HEADS // 2):
        lanes = pl.ds(pair * LANE, LANE)
        v_ext = jnp.concatenate([v_ref[0, :, lanes], one_col], axis=1)
        outs = []
        for h in (2 * pair, 2 * pair + 1):
            s = lax.dot_general(q_ref[0, h], k_ref[0, h], (((1,), (1,)), ((), ())), preferred_element_type=F32)
            p = jnp.exp2(s - jnp.max(s, axis=-1, keepdims=True)).astype(BF16)
            pv = jnp.dot(p, v_ext, preferred_element_type=F32)
            outs.append(pv[:, :LANE] / pv[:, LANE:LANE + 1])
        lane = lax.broadcasted_iota(jnp.int32, outs[0].shape, 1)
        o_ref[0, :, lanes] = jnp.where(lane < V_HEAD, outs[0], outs[1]).astype(BF16)


def _attention(q, k, v, tq):
    B, H, L, _ = q.shape
    return pl.pallas_call(
        _attn_kernel,
        grid=(B, L // tq),
        in_specs=[
            pl.BlockSpec((1, H, tq, HEAD_PAD), lambda b, i: (b, 0, i, 0)),
            pl.BlockSpec((1, H, L, HEAD_PAD), lambda b, i: (b, 0, 0, 0)),
            pl.BlockSpec((1, L, H * V_HEAD), lambda b, i: (b, 0, 0)),
        ],
        out_specs=pl.BlockSpec((1, tq, H * V_HEAD), lambda b, i: (b, i, 0)),
        out_shape=jax.ShapeDtypeStruct((B, L, H * V_HEAD), BF16),
        compiler_params=_cparams(2),
        name="mla_attention",
    )(q, k, v)


def _pack_rows(a):
    bits = lax.bitcast_convert_type(a.astype(BF16).astype(F32), jnp.uint32)
    return (bits[:, :PACKED] >> 16) | (bits[:, PACKED:] & jnp.uint32(0xFFFF0000))


def _unpack_rows(p):
    lo = lax.bitcast_convert_type(p << 16, F32)
    hi = lax.bitcast_convert_type(p & jnp.uint32(0xFFFF0000), F32)
    return jnp.concatenate([lo, hi], axis=1)


def _merge_kernel(x_ref, yh_ref, ym_ref, g_ref, who_ref, wmo_ref, wo_ref, nf_ref, rwh_ref, rwl_ref, rb_ref,
                  x1_ref, h2_ref, ids_ref, gate_ref, cnt_ref):
    tm = x_ref.shape[0]
    sub = tm // 2

    @pl.when(jnp.logical_and(pl.program_id(0) == 0, pl.program_id(1) == 0))
    def _():
        cnt_ref[...] = jnp.zeros_like(cnt_ref)

    eid = lax.broadcasted_iota(jnp.int32, (N_EXPERTS, sub), 0)
    for r in range(0, tm, sub):
        rows = pl.ds(r, sub)
        g = g_ref[rows, :].astype(F32)
        a = lax.dot_general(yh_ref[:, rows].astype(BF16), who_ref[...], (((0,), (0,)), ((), ())),
                            preferred_element_type=F32)
        b = jnp.dot(ym_ref[rows, :], wmo_ref[...], preferred_element_type=F32)
        merged = jax.nn.sigmoid(g[:, :D_MODEL]) * a + jax.nn.sigmoid(g[:, D_MODEL:]) * b
        x1 = x_ref[rows, :] + jnp.dot(merged.astype(BF16), wo_ref[...], preferred_element_type=F32)
        x1_ref[rows, :] = x1
        h2 = x1 * lax.rsqrt(jnp.mean(x1 * x1, axis=-1, keepdims=True) + EPS) * nf_ref[...]
        h2_ref[rows, :] = _pack_rows(h2)
        h_hi = h2.astype(BF16)
        h_lo = (h2 - h_hi.astype(F32)).astype(BF16)
        logits = (jnp.dot(h_hi, rwh_ref[...], preferred_element_type=F32)
                  + jnp.dot(h_lo, rwh_ref[...], preferred_element_type=F32)
                  + jnp.dot(h_hi, rwl_ref[...], preferred_element_type=F32) + rb_ref[...])
        work = logits.T[:N_EXPERTS]
        picked = jnp.zeros((N_EXPERTS, sub), F32)
        vals, firsts = [], []
        for k in range(TOP_K):
            m = jnp.max(work, axis=0, keepdims=True)
            first = jnp.min(jnp.where(work == m, eid, N_EXPERTS), axis=0, keepdims=True)
            hot = eid == first
            vals.append(m)
            firsts.append(first)
            picked = jnp.where(hot, 1.0, picked)
            work = jnp.where(hot, -jnp.inf, work)
        ex = [jnp.exp(v - vals[0]) for v in vals]
        inv = 1.0 / (ex[0] + ex[1] + ex[2] + ex[3])
        ids_ref[:, rows] = jnp.concatenate(firsts + [jnp.zeros((8 - TOP_K, sub), jnp.int32)], axis=0)
        gates_t = jnp.concatenate([e * inv for e in ex] + [jnp.zeros((LANE - TOP_K, sub), F32)], axis=0)
        gate_ref[rows, :] = gates_t.T
        cnt_ref[...] += jnp.sum(picked, axis=1, keepdims=True)


def _merge(x, yh_t, ym, g, P, tm):
    B, L, _ = x.shape
    nt = L // tm
    w = [P[n] for n in ("w_hy_o", "w_mla_o", "w_out", "norm_ffn", "router_w_hi", "router_w_lo", "router_b")]
    return pl.pallas_call(
        _merge_kernel,
        grid=(B, nt),
        in_specs=[
            pl.BlockSpec((None, tm, D_MODEL), lambda b, i: (b, i, 0)),
            pl.BlockSpec((None, HY_WIDTH, tm), lambda b, i: (b, 0, i)),
            pl.BlockSpec((None, tm, MLA_HEADS * V_HEAD), lambda b, i: (b, i, 0)),
            pl.BlockSpec((None, tm, 2 * D_MODEL), lambda b, i: (b, i, 0)),
        ] + [_full(a.shape) for a in w],
        out_specs=[
            pl.BlockSpec((tm, D_MODEL), lambda b, i: (b * nt + i, 0)),
            pl.BlockSpec((tm, PACKED), lambda b, i: (b * nt + i, 0)),
            pl.BlockSpec((8, tm), lambda b, i: (0, b * nt + i)),
            pl.BlockSpec((tm, LANE), lambda b, i: (b * nt + i, 0)),
            pl.BlockSpec((N_EXPERTS, 1), lambda b, i: (0, 0)),
        ],
        out_shape=[
            jax.ShapeDtypeStruct((B * L, D_MODEL), F32),
            jax.ShapeDtypeStruct((B * L, PACKED), jnp.uint32),
            jax.ShapeDtypeStruct((8, B * L), jnp.int32),
            jax.ShapeDtypeStruct((B * L, LANE), F32),
            jax.ShapeDtypeStruct((N_EXPERTS, 1), F32),
        ],
        compiler_params=_cparams(2),
        name="merge_router",
    )(x, yh_t, ym, g, *w)


def _expert_kernel(be_ref, na_ref, end_ref, x_ref, wgu_hbm, bgu_ref, wd_hbm, bd_ref, y_ref,
                   wgu_buf, wd_buf, sems, slot_ref):
    i = pl.program_id(0)
    active = i < na_ref[0]
    expert = be_ref[i]
    first = jnp.logical_and(active, jnp.logical_or(i == 0, expert != be_ref[jnp.maximum(i - 1, 0)]))

    def weight_copies(e, s):
        return (pltpu.make_async_copy(wgu_hbm.at[e], wgu_buf.at[s], sems.at[0, s]),
                pltpu.make_async_copy(wd_hbm.at[e], wd_buf.at[s], sems.at[1, s]))

    @pl.when(i == 0)
    def _():
        slot_ref[0] = 1
        for cp in weight_copies(expert, 0):
            cp.start()

    @pl.when(first)
    def _():
        s = 1 - slot_ref[0]
        slot_ref[0] = s
        for cp in weight_copies(expert, s):
            cp.wait()
        run_end = end_ref[expert]

        @pl.when(run_end < na_ref[0])
        def _():
            for cp in weight_copies(be_ref[jnp.minimum(run_end, pl.num_programs(0) - 1)], 1 - s):
                cp.start()

    slot = slot_ref[0]

    @pl.when(active)
    def _():
        x = _unpack_rows(x_ref[...])
        gu = jnp.dot(x, wgu_buf[slot], preferred_element_type=F32) + bgu_ref[0]
        gate = jnp.minimum(gu[:, :D_FF], SWIGLU_LIMIT)
        up = jnp.clip(gu[:, D_FF:], -SWIGLU_LIMIT, SWIGLU_LIMIT)
        act = (up + 1.0) * (gate * jax.nn.sigmoid(SWIGLU_ALPHA * gate))
        y_ref[...] = _pack_rows(jnp.dot(act, wd_buf[slot], preferred_element_type=F32) + bd_ref[0])

    @pl.when(i >= na_ref[0])
    def _():
        y_ref[...] = jnp.zeros_like(y_ref)


def _experts(plan, xs, P):
    n_rows = xs.shape[0]
    nb = n_rows // MOE_BLOCK
    grid_spec = pltpu.PrefetchScalarGridSpec(
        num_scalar_prefetch=3,
        grid=(nb,),
        in_specs=[
            pl.BlockSpec((MOE_BLOCK, PACKED), lambda i, be, *_: (i, 0)),
            pl.BlockSpec(memory_space=pl.ANY),
            pl.BlockSpec((1, 1, 2 * D_FF), lambda i, be, *_: (be[i], 0, 0)),
            pl.BlockSpec(memory_space=pl.ANY),
            pl.BlockSpec((1, 1, D_MODEL), lambda i, be, *_: (be[i], 0, 0)),
        ],
        out_specs=pl.BlockSpec((MOE_BLOCK, PACKED), lambda i, be, *_: (i, 0)),
        scratch_shapes=[
            pltpu.VMEM((2, D_MODEL, 2 * D_FF), F32),
            pltpu.VMEM((2, D_FF, D_MODEL), F32),
            pltpu.SemaphoreType.DMA((2, 2)),
            pltpu.SMEM((1,), jnp.int32),
        ],
    )
    return pl.pallas_call(
        _expert_kernel,
        grid_spec=grid_spec,
        out_shape=jax.ShapeDtypeStruct((n_rows, PACKED), jnp.uint32),
        compiler_params=_cparams(1),
        name="moe_experts",
    )(*plan, xs, P["w_gu"], P["b_gu"], P["w_down"], P["b_down"])


def _combine_kernel(x1_ref, yk_ref, gt_ref, o_ref):
    gt = gt_ref[...]
    acc = _unpack_rows(yk_ref[0]) * gt[:, 0:1]
    for k in range(1, TOP_K):
        acc = acc + _unpack_rows(yk_ref[k]) * gt[:, k:k + 1]
    o_ref[...] = x1_ref[...] + acc


def _combine(x1, yk, gates, tm):
    T = x1.shape[0]
    return pl.pallas_call(
        _combine_kernel,
        grid=(T // tm,),
        in_specs=[
            pl.BlockSpec((tm, D_MODEL), lambda i: (i, 0)),
            pl.BlockSpec((TOP_K, tm, PACKED), lambda i: (0, i, 0)),
            pl.BlockSpec((tm, LANE), lambda i: (i, 0)),
        ],
        out_specs=pl.BlockSpec((tm, D_MODEL), lambda i: (i, 0)),
        out_shape=jax.ShapeDtypeStruct((T, D_MODEL), F32),
        compiler_params=_cparams(1),
        name="moe_combine",
    )(x1, yk, gates)


def _route_kernel(ids_ref, start_ref, dest_ref, run_ref, *, tm):
    @pl.when(pl.program_id(0) == 0)
    def _():
        run_ref[...] = jnp.zeros_like(run_ref)

    eid = lax.broadcasted_iota(jnp.int32, (N_EXPERTS, tm), 0)
    ids = ids_ref[...]
    hots = [eid == ids[k:k + 1] for k in range(TOP_K)]
    chosen = jnp.zeros((N_EXPERTS, tm), F32)
    for hot in hots:
        chosen = jnp.where(hot, 1.0, chosen)
    r = lax.broadcasted_iota(jnp.int32, (tm, tm), 0)
    c = lax.broadcasted_iota(jnp.int32, (tm, tm), 1)
    earlier = jnp.where(r < c, 1.0, 0.0).astype(BF16)
    before = jnp.dot(chosen.astype(BF16), earlier, preferred_element_type=F32) + (run_ref[...] + start_ref[...])
    rows = [jnp.sum(jnp.where(hot, before, 0.0), axis=0, keepdims=True) for hot in hots]
    dest_ref[...] = jnp.concatenate(rows + [jnp.zeros((8 - TOP_K, tm), F32)], axis=0).astype(jnp.int32)
    run_ref[...] += jnp.sum(chosen, axis=1, keepdims=True)


def _route(ids_t, counts, tm):
    T = ids_t.shape[1]
    counts = counts[:, 0].astype(jnp.int32)
    padded = (counts + MOE_BLOCK - 1) // MOE_BLOCK * MOE_BLOCK
    pad_end = jnp.cumsum(padded)
    start = (pad_end - padded).astype(F32).reshape(N_EXPERTS, 1)
    dest = pl.pallas_call(
        functools.partial(_route_kernel, tm=tm),
        grid=(T // tm,),
        in_specs=[pl.BlockSpec((8, tm), lambda i: (0, i)), _full((N_EXPERTS, 1))],
        out_specs=pl.BlockSpec((8, tm), lambda i: (0, i)),
        out_shape=jax.ShapeDtypeStruct((8, T), jnp.int32),
        scratch_shapes=[pltpu.VMEM((N_EXPERTS, 1), F32)],
        compiler_params=_cparams(1),
        name="moe_route",
    )(ids_t, start)
    n_blocks = -(-(T * TOP_K + N_EXPERTS * (MOE_BLOCK - 1)) // MOE_BLOCK)
    block_start = jnp.arange(n_blocks, dtype=jnp.int32) * MOE_BLOCK
    block_e = jnp.sum((block_start[:, None] >= pad_end[None, :]).astype(jnp.int32), axis=1)
    block_e = jnp.minimum(block_e, N_EXPERTS - 1)
    end_blk = (pad_end // MOE_BLOCK).astype(jnp.int32)
    plan = (block_e.astype(jnp.int32), end_blk[-1:], end_blk)
    dest_flat = dest[:TOP_K].reshape(TOP_K * T)
    return dest_flat, plan, n_blocks * MOE_BLOCK


def _sc_workers():
    info = plsc.get_sparse_core_info()
    return info.num_cores, info.num_cores * info.num_subcores


def _sc_scatter_rows(rows, dest_flat, n_rows):
    T, W = rows.shape
    n_cores, n_workers = _sc_workers()
    per_w = T // n_workers
    assert per_w * n_workers == T and per_w % SC_CHUNK == 0

    def body(rows_hbm, dest_hbm, out_hbm, buf, *idx):
        wid = lax.axis_index("s") * n_cores + lax.axis_index("c")

        @pl.loop(0, per_w // SC_CHUNK)
        def _(j):
            off = pl.multiple_of(wid * per_w + j * SC_CHUNK, SC_CHUNK)
            pltpu.sync_copy(rows_hbm.at[pl.ds(off, SC_CHUNK)], buf)
            for k in range(TOP_K):
                pltpu.sync_copy(dest_hbm.at[pl.ds(k * T + off, SC_CHUNK)], idx[k])
            for k in range(TOP_K):
                pltpu.sync_copy(buf, out_hbm.at[idx[k]])

    return pl.kernel(
        body,
        out_type=jax.ShapeDtypeStruct((n_rows, W), rows.dtype),
        mesh=plsc.VectorSubcoreMesh(core_axis_name="c", subcore_axis_name="s"),
        scratch_types=[pltpu.VMEM((SC_CHUNK, W), rows.dtype)] + [pltpu.VMEM((SC_CHUNK,), jnp.int32)] * TOP_K,
        name="moe_dispatch_sc",
    )(rows, dest_flat)


def _sc_gather_rows(table, dest_flat):
    N = dest_flat.shape[0]
    W = table.shape[1]
    n_cores, n_workers = _sc_workers()
    per_w = N // n_workers
    assert per_w * n_workers == N and per_w % SC_CHUNK == 0

    def body(table_hbm, dest_hbm, out_hbm, buf, idx):
        wid = lax.axis_index("s") * n_cores + lax.axis_index("c")

        @pl.loop(0, per_w // SC_CHUNK)
        def _(j):
            off = pl.multiple_of(wid * per_w + j * SC_CHUNK, SC_CHUNK)
            pltpu.sync_copy(dest_hbm.at[pl.ds(off, SC_CHUNK)], idx)
            pltpu.sync_copy(table_hbm.at[idx], buf)
            pltpu.sync_copy(buf, out_hbm.at[pl.ds(off, SC_CHUNK)])

    return pl.kernel(
        body,
        out_type=jax.ShapeDtypeStruct((N, W), table.dtype),
        mesh=plsc.VectorSubcoreMesh(core_axis_name="c", subcore_axis_name="s"),
        scratch_types=[pltpu.VMEM((SC_CHUNK, W), table.dtype), pltpu.VMEM((SC_CHUNK,), jnp.int32)],
        name="moe_gather_sc",
    )(table, dest_flat)


def _moe(groups, P, tm):
    staged = []
    for x1, h2p, ids, gates, counts in groups:
        dest_flat, plan, n_rows = _route(ids, counts, tm)
        staged.append((dest_flat, plan, _sc_scatter_rows(h2p, dest_flat, n_rows)))
    ys = []
    for dest_flat, plan, xs in staged:
        if ys:
            xs, ys[-1] = lax.optimization_barrier((xs, ys[-1]))
        ys.append(_experts(plan, xs, P))
    outs = []
    for (x1, _, _, gates, _), (dest_flat, _, _), y in zip(groups, staged, ys):
        yk = _sc_gather_rows(y, dest_flat).reshape(TOP_K, x1.shape[0], PACKED)
        outs.append(_combine(x1, yk, gates, 2 * tm if x1.shape[0] % (2 * tm) == 0 else tm))
    return outs


def _prep_params(norm_mix, w_in, q_a_norm, w_uq, kv_a_norm, w_ukv, q_norm, k_norm, w_hy_o, w_mla_o, w_out,
                 norm_ffn, router_w, router_b, w_gu, b_gu, w_down, b_down, L):
    P = {}
    P["norm_mix"] = norm_mix.astype(F32).reshape(1, D_MODEL)
    P["w_hy_t"] = w_in[:, :HY_END].T.astype(BF16)
    P["w_m"] = jnp.pad(w_in[:, HY_END:KR_END], ((0, 0), (0, MLA_IN - (KR_END - HY_END)))).astype(BF16)
    P["w_g"] = w_in[:, KR_END:].astype(BF16)

    half = QK_ROPE // 2
    swap = jnp.concatenate([jnp.arange(QK_NOPE), QK_NOPE + half + jnp.arange(half), QK_NOPE + jnp.arange(half)])
    rope_lane = (jnp.arange(QK_HEAD) >= QK_NOPE)

    def slots(w):
        return jnp.pad(w, ((0, 0), (0, 0), (0, HEAD_PAD - QK_HEAD))).reshape(w.shape[0], MLA_HEADS * HEAD_PAD)

    wq = w_uq.astype(F32).reshape(Q_LORA, MLA_HEADS, QK_HEAD)
    P["wqa"] = slots(wq).astype(BF16)
    P["wqb"] = slots(jnp.where(rope_lane, wq[:, :, swap], 0.0)).astype(BF16)
    wkv = w_ukv.astype(F32).reshape(KV_LORA, MLA_HEADS, QK_NOPE + V_HEAD)
    P["wka"] = slots(jnp.pad(wkv[:, :, :QK_NOPE], ((0, 0), (0, 0), (0, QK_ROPE)))).astype(BF16)
    P["wv"] = wkv[:, :, QK_NOPE:].reshape(KV_LORA, MLA_HEADS * V_HEAD).astype(BF16)
    eye = jnp.eye(LANE, QK_ROPE, dtype=F32)
    e_a = jnp.pad(eye, ((0, 0), (QK_NOPE, 0)))
    e_b = e_a[:, swap] * rope_lane
    P["ea"] = slots(jnp.broadcast_to(e_a[:, None, :], (LANE, MLA_HEADS, QK_HEAD))).astype(BF16)
    P["eb"] = slots(jnp.broadcast_to(e_b[:, None, :], (LANE, MLA_HEADS, QK_HEAD))).astype(BF16)

    sign = jnp.concatenate([jnp.zeros(QK_NOPE), -jnp.ones(half), jnp.ones(half)]).astype(F32)

    def lane_pad(g):
        return jnp.pad(g, (0, HEAD_PAD - QK_HEAD)).reshape(1, HEAD_PAD)

    for name, g in (("gq", q_norm.astype(F32)), ("gk", k_norm.astype(F32))):
        P[name] = lane_pad(g)
        P[name + "s"] = lane_pad(g[swap] * sign)
    P["gqa"] = q_a_norm.astype(F32).reshape(1, Q_LORA)
    P["gkva"] = kv_a_norm.astype(F32).reshape(1, KV_LORA)
    inv = 1.0 / (ROPE_THETA ** (jnp.arange(0, QK_ROPE, 2, dtype=F32) / QK_ROPE))
    ang = jnp.arange(L, dtype=F32)[:, None] * inv[None, :]
    cos, sin = jnp.cos(ang), jnp.sin(ang)
    P["ctab"] = jnp.concatenate([jnp.ones((L, QK_NOPE), F32), cos, cos, jnp.zeros((L, HEAD_PAD - QK_HEAD), F32)], axis=1)
    P["stab"] = jnp.concatenate([jnp.zeros((L, QK_NOPE), F32), sin, sin, jnp.zeros((L, HEAD_PAD - QK_HEAD), F32)], axis=1)

    P["w_hy_o"] = w_hy_o.astype(BF16)
    P["w_mla_o"] = w_mla_o.astype(BF16)
    P["w_out"] = w_out.astype(BF16)
    P["norm_ffn"] = norm_ffn.astype(F32).reshape(1, D_MODEL)
    rw = jnp.pad(router_w.astype(F32), ((0, 0), (0, LANE - N_EXPERTS)))
    P["router_w_hi"] = rw.astype(BF16)
    P["router_w_lo"] = (rw - P["router_w_hi"].astype(F32)).astype(BF16)
    P["router_b"] = jnp.pad(router_b.astype(F32), (0, LANE - N_EXPERTS)).reshape(1, LANE)
    P["w_gu"] = w_gu.astype(F32)
    P["b_gu"] = b_gu.astype(F32).reshape(N_EXPERTS, 1, 2 * D_FF)
    P["w_down"] = w_down.astype(F32)
    P["b_down"] = b_down.astype(F32).reshape(N_EXPERTS, 1, D_MODEL)
    return P


def _tile(n, want):
    t = min(n, want)
    assert n % t == 0
    return t


def _layer(xs, norm_mix, w_in, hy_conv_w, hy_conv_b, hy_fw1, hy_fb1, hy_fw2, hy_fb2, hy_fw3, hy_freq, hy_decay, hy_skip,
           q_a_norm, w_uq, kv_a_norm, w_ukv, q_norm, k_norm, w_hy_o, w_mla_o, w_out,
           norm_ffn, router_w, router_b, w_gu, b_gu, w_down, b_down):
    L = xs[0].shape[1]
    assert all(x.shape[1] == L and x.shape[2] == D_MODEL for x in xs) and L % MXU == 0
    P = _prep_params(norm_mix, w_in, q_a_norm, w_uq, kv_a_norm, w_ukv, q_norm, k_norm, w_hy_o, w_mla_o, w_out,
                     norm_ffn, router_w, router_b, w_gu, b_gu, w_down, b_down, L)
    kf = _hyena_filters(L, hy_fw1, hy_fb1, hy_fw2, hy_fb2, hy_fw3, hy_freq, hy_decay)
    tm = _tile(L, 512)
    proj = [_inproj(x, P["norm_mix"], P["w_hy_t"], P["w_m"], P["w_g"], tm) for x in xs]
    y_ts = _hyena([u_t for u_t, _, _ in proj], kf, hy_conv_w, hy_conv_b, hy_skip, L)
    mixed = []
    for x, (_, m, g), y_t in zip(xs, proj, y_ts):
        q, k, v = _mla_prep(m, P, tm)
        y_mla = _attention(q, k, v, _tile(L, 512))
        mixed.append(_merge(x, y_t, y_mla, g, P, _tile(L, 1024)))
    order = sorted(range(len(xs)), key=lambda n: -xs[n].shape[0])
    moe_out = _moe([mixed[n] for n in order], P, tm)
    outs = [None] * len(xs)
    for n, y in zip(order, moe_out):
        outs[n] = y.reshape(xs[n].shape)
    return outs


def kernel(x_prompt, x_sample, norm_mix, w_in, hy_conv_w, hy_conv_b, hy_fw1, hy_fb1, hy_fw2, hy_fb2, hy_fw3, hy_freq, hy_decay, hy_skip, q_a_norm, w_uq, kv_a_norm, w_ukv, q_norm, k_norm, w_hy_o, w_mla_o, w_out, norm_ffn, router_w, router_b, w_gu, b_gu, w_down, b_down):
    depth = norm_mix.shape[0]
    xs = [x_prompt, x_sample]
    weights = (norm_mix, w_in, hy_conv_w, hy_conv_b, hy_fw1, hy_fb1, hy_fw2, hy_fb2, hy_fw3, hy_freq, hy_decay, hy_skip,
               q_a_norm, w_uq, kv_a_norm, w_ukv, q_norm, k_norm, w_hy_o, w_mla_o, w_out,
               norm_ffn, router_w, router_b, w_gu, b_gu, w_down, b_down)
    for l in range(depth):
        xs = _layer(xs, *[w[l] for w in weights])
    return (xs[0], xs[1])
```

```python
import functools
import math

import jax
import jax.numpy as jnp
from jax import lax
from jax.experimental import pallas as pl
from jax.experimental.pallas import tpu as pltpu
from jax.experimental.pallas import tpu_sc as plsc

F32 = jnp.float32
BF16 = jnp.bfloat16

D_MODEL = 1024
HY_WIDTH = 512
HY_ORDER = 2
HY_EMB_BANDS = 16
HY_FILTER_DIM = 64
MLA_HEADS = 8
V_HEAD = 64
QK_NOPE = 64
QK_ROPE = 32
QK_HEAD = QK_NOPE + QK_ROPE
Q_LORA = 256
KV_LORA = 128
ROPE_THETA = 10000.0
HY_END = 3 * HY_WIDTH
Q_END = HY_END + Q_LORA
KV_END = Q_END + KV_LORA
KR_END = KV_END + QK_ROPE
G1_END = KR_END + D_MODEL
N_EXPERTS = 32
TOP_K = 4
D_FF = D_MODEL
SWIGLU_LIMIT = 7.0
SWIGLU_ALPHA = 1.702
EPS = 1e-6

LANE = 128
MXU = 256
HEAD_PAD = LANE
MLA_IN = 512
CONV_CB = 8
CONV_UNROLL = 4
MOE_BLOCK = 512
PACKED = D_MODEL // 2
SC_CHUNK = 64
VMEM_LIMIT = 52 * 1024 * 1024


def _cparams(n_axes):
    return pltpu.CompilerParams(dimension_semantics=("arbitrary",) * n_axes, vmem_limit_bytes=VMEM_LIMIT)


def _full(shape):
    n = len(shape)
    return pl.BlockSpec(shape, lambda *_: (0,) * n)


def _rms(x, n):
    return lax.rsqrt(jnp.sum(x * x, axis=-1, keepdims=True) * (1.0 / n) + EPS)


def _mla_heads(m, ct, st, gqa_ref, gkva_ref, gq_ref, gqs_ref, gk_ref, gks_ref,
               wqa_ref, wqb_ref, wka_ref, ea_ref, eb_ref, wv_ref, q_ref, k_ref, v_ref):
    mf = m.astype(F32)
    cq = mf[:, :Q_LORA]
    ckv = mf[:, Q_LORA:Q_LORA + KV_LORA]
    kr = m[:, Q_LORA + KV_LORA:Q_LORA + KV_LORA + LANE]
    cqn = (cq * _rms(cq, Q_LORA) * gqa_ref[...]).astype(BF16)
    ckvn = (ckv * _rms(ckv, KV_LORA) * gkva_ref[...]).astype(BF16)
    qa = jnp.dot(cqn, wqa_ref[...], preferred_element_type=F32)
    qb = jnp.dot(cqn, wqb_ref[...], preferred_element_type=F32)
    ka = jnp.dot(ckvn, wka_ref[...], preferred_element_type=F32) + jnp.dot(kr, ea_ref[...], preferred_element_type=F32)
    kb = jnp.dot(kr, eb_ref[...], preferred_element_type=F32)
    v_ref[...] = jnp.dot(ckvn, wv_ref[...], preferred_element_type=F32).astype(BF16)
    scale = QK_HEAD ** -0.5 * math.log2(math.e)
    qc, qs = gq_ref[...] * ct * scale, gqs_ref[...] * st * scale
    kc, ks = gk_ref[...] * ct, gks_ref[...] * st
    for h in range(MLA_HEADS):
        sl = slice(h * HEAD_PAD, (h + 1) * HEAD_PAD)
        q_ref[h] = (_rms(qa[:, sl], QK_HEAD) * (qa[:, sl] * qc + qb[:, sl] * qs)).astype(BF16)
        k_ref[h] = (_rms(ka[:, sl], QK_HEAD) * (ka[:, sl] * kc + kb[:, sl] * ks)).astype(BF16)


def _inproj_kernel(x_ref, g_ref, why_ref, wm_ref, wg_ref, ct_ref, st_ref, *refs):
    mla_refs, (u_ref, gate_ref, q_ref, k_ref, v_ref) = refs[:12], refs[12:]
    tm = x_ref.shape[0]
    sub = tm // 2
    for r0 in range(0, tm, sub):
        rows = pl.ds(r0, sub)
        x = x_ref[rows, :]
        r = lax.rsqrt(jnp.mean(x * x, axis=-1, keepdims=True) + EPS)
        h = (x * r * g_ref[...]).astype(BF16)
        m = jnp.dot(h, wm_ref[...], preferred_element_type=F32).astype(BF16)
        _mla_heads(m, ct_ref[rows, :], st_ref[rows, :], *mla_refs,
                   q_ref.at[:, rows, :], k_ref.at[:, rows, :], v_ref.at[rows, :])
        u_ref[:, rows] = lax.dot_general(why_ref[...], h, (((1,), (1,)), ((), ())), preferred_element_type=F32)
        gate_ref[rows, :] = jnp.dot(h, wg_ref[...], preferred_element_type=F32).astype(BF16)


def _inproj(x, P, tm):
    B, L, _ = x.shape
    w = [P[n] for n in ("w_hy_t", "w_m", "w_g")]
    mla = [P[n] for n in ("gqa", "gkva", "gq", "gqs", "gk", "gks", "wqa", "wqb", "wka", "ea", "eb", "wv")]
    return pl.pallas_call(
        _inproj_kernel,
        grid=(B, L // tm),
        in_specs=[pl.BlockSpec((None, tm, D_MODEL), lambda b, i: (b, i, 0)), _full((1, D_MODEL))]
        + [_full(a.shape) for a in w]
        + [pl.BlockSpec((tm, HEAD_PAD), lambda b, i: (i, 0)), pl.BlockSpec((tm, HEAD_PAD), lambda b, i: (i, 0))]
        + [_full(a.shape) for a in mla],
        out_specs=[
            pl.BlockSpec((None, HY_END, tm), lambda b, i: (b, 0, i)),
            pl.BlockSpec((None, tm, 2 * D_MODEL), lambda b, i: (b, i, 0)),
            pl.BlockSpec((None, MLA_HEADS, tm, HEAD_PAD), lambda b, i: (b, 0, i, 0)),
            pl.BlockSpec((None, MLA_HEADS, tm, HEAD_PAD), lambda b, i: (b, 0, i, 0)),
            pl.BlockSpec((None, tm, MLA_HEADS * V_HEAD), lambda b, i: (b, i, 0)),
        ],
        out_shape=[
            jax.ShapeDtypeStruct((B, HY_END, L), F32),
            jax.ShapeDtypeStruct((B, L, 2 * D_MODEL), BF16),
            jax.ShapeDtypeStruct((B, MLA_HEADS, L, HEAD_PAD), BF16),
            jax.ShapeDtypeStruct((B, MLA_HEADS, L, HEAD_PAD), BF16),
            jax.ShapeDtypeStruct((B, L, MLA_HEADS * V_HEAD), BF16),
        ],
        compiler_params=_cparams(2),
        name="inproj",
    )(x, P["norm_mix"], *w, P["ctab"][:L], P["stab"][:L], *mla)


def _filter_kernel(bands_ref, w1t_ref, w1c_ref, w1s_ref, b1_ref, w2_ref, b2_ref, fr_ref, w3f_ref, w3b_ref, dec_ref,
                   out_ref, *, L):
    hi = lax.Precision.HIGHEST
    n = 2 * L
    j = lax.broadcasted_iota(jnp.int32, (1, n), 1)
    pos = jnp.abs(j - L).astype(F32)
    t01 = pos / max(L - 1, 1)
    ang = ((2.0 * math.pi / L) * pos) * bands_ref[...]
    pre = (w1t_ref[...] * t01
           + jnp.dot(w1c_ref[...], jnp.cos(ang), precision=hi, preferred_element_type=F32)
           - jnp.dot(w1s_ref[...], jnp.sin(ang), precision=hi, preferred_element_type=F32)
           + b1_ref[...])
    z = jnp.sin(fr_ref[0] * pre)
    z = jnp.sin(fr_ref[1] * (jnp.dot(w2_ref[...], z, precision=hi, preferred_element_type=F32) + b2_ref[...]))
    win = jnp.exp(-t01 * jnp.abs(dec_ref[...]))
    h_b = jnp.dot(w3b_ref[...], z[:, :L], precision=hi, preferred_element_type=F32) * win[:, :L]
    h_f = jnp.dot(w3f_ref[...], z[:, L:], precision=hi, preferred_element_type=F32) * win[:, L:]
    out_ref[:, :L] = jnp.where(j[:, :L] == 0, 0.0, h_b)
    out_ref[:, L:] = h_f


def _hyena_filters(L, fw1, fb1, fw2, fb2, fw3, freq, decay):
    rows = HY_ORDER * HY_WIDTH
    rb = 256
    bands = jnp.linspace(1e-4, HY_EMB_BANDS - 1, HY_EMB_BANDS, dtype=F32).reshape(HY_EMB_BANDS, 1)
    w1 = fw1.astype(F32).T
    w3 = fw3.astype(F32).T
    fd = HY_FILTER_DIM
    return pl.pallas_call(
        functools.partial(_filter_kernel, L=L),
        grid=(rows // rb,),
        in_specs=[
            _full((HY_EMB_BANDS, 1)),
            _full((fd, 1)),
            _full((fd, HY_EMB_BANDS)),
            _full((fd, HY_EMB_BANDS)),
            _full((fd, 1)),
            _full((fd, fd)),
            _full((fd, 1)),
            _full((2, fd, 1)),
            pl.BlockSpec((rb, fd), lambda i: (i, 0)),
            pl.BlockSpec((rb, fd), lambda i: (i, 0)),
            pl.BlockSpec((rb, 1), lambda i: (i, 0)),
        ],
        out_specs=pl.BlockSpec((rb, 2 * L), lambda i: (i, 0)),
        out_shape=jax.ShapeDtypeStruct((rows, 2 * L), F32),
        compiler_params=_cparams(1),
        name="hyena_filters",
    )(bands, w1[:, :1], w1[:, 1:1 + HY_EMB_BANDS], w1[:, 1 + HY_EMB_BANDS:], fb1.astype(F32).reshape(fd, 1),
      fw2.astype(F32).T, fb2.astype(F32).reshape(fd, 1), freq.astype(F32).reshape(2, fd, 1),
      w3[:rows], w3[rows:], decay.astype(F32).reshape(rows, 1))


def _shift_conv(u, w0, w1, w2, b):
    L = u.shape[1]
    lane = lax.broadcasted_iota(jnp.int32, (1, L), 1)
    w0_row = jnp.where(lane == 0, 0.0, w0)
    w2_row = jnp.where(lane == L - 1, 0.0, w2)
    return pltpu.roll(u, 1, 1) * w0_row + u * w1 + pltpu.roll(u, L - 1, 1) * w2_row + b


def _toeplitz_conv(z, tz_ref, L):
    B = z.shape[0]
    nj = L // MXU
    zb = z.astype(BF16)
    tiles = [zb[:, MXU * J:MXU * (J + 1)] for J in range(nj)]
    acc = [None] * nj
    for d in range(-(nj - 1), nj):
        js = [J for J in range(nj) if 0 <= J + d < nj]
        lhs = tiles[js[0]] if len(js) == 1 else jnp.concatenate([tiles[J] for J in js], axis=0)
        res = jnp.dot(lhs, tz_ref[:, L + MXU * d:L + MXU * (d + 1)], preferred_element_type=F32)
        for n, J in enumerate(js):
            part = res[n * B:(n + 1) * B]
            acc[J + d] = part if acc[J + d] is None else acc[J + d] + part
    return acc[0] if nj == 1 else jnp.concatenate(acc, axis=1)


def _hyena_kernel(cw_ref, cb_ref, skip_ref, kf_ref, *refs, L, n_groups):
    in_refs = refs[:3 * n_groups]
    out_refs = refs[3 * n_groups:4 * n_groups]
    tz_ref = refs[4 * n_groups]
    c0 = pl.program_id(0) * CONV_CB

    def channels(p, carry):
        for u in range(CONV_UNROLL):
            c = p * CONV_UNROLL + u
            ch = c0 + c
            for o in range(HY_ORDER):
                row = jnp.broadcast_to(kf_ref[o, pl.ds(c, 1), :], (MXU, 2 * L))
                tz_ref[u, o] = pltpu.roll(row, 0, 1, stride=1, stride_axis=0).astype(BF16)
            for g in range(n_groups):
                stage = refs[4 * n_groups + 1 + g]
                sec = []
                for s, ref in enumerate(in_refs[3 * g:3 * g + 3]):
                    stage[u, s] = ref[:, c, :]
                    k = s * HY_WIDTH + ch
                    sec.append(_shift_conv(stage[u, s], cw_ref[0, k], cw_ref[1, k], cw_ref[2, k], cb_ref[k]))
                x1, x2, z = sec
                for o, gate in enumerate((x1, x2)):
                    y = _toeplitz_conv(z, tz_ref.at[u, o], L) + z * skip_ref[o, ch]
                    z = gate * y
                out_refs[g][:, c, :] = z
        return carry

    lax.fori_loop(0, CONV_CB // CONV_UNROLL, channels, 0)


def _hyena(u_ts, kf, conv_w, conv_b, skip, L):
    n_groups = len(u_ts)
    nblk = HY_WIDTH // CONV_CB
    smem = pl.BlockSpec(memory_space=pltpu.SMEM)
    in_specs = [smem, smem, smem, pl.BlockSpec((HY_ORDER, CONV_CB, 2 * L), lambda i: (0, i, 0))]
    args = [conv_w.astype(F32), conv_b.astype(F32), skip.astype(F32), kf.reshape(HY_ORDER, HY_WIDTH, 2 * L)]
    out_specs, out_shape = [], []
    for u in u_ts:
        B = u.shape[0]
        for s in range(3):
            in_specs.append(pl.BlockSpec((B, CONV_CB, L), lambda i, s=s: (0, i + s * nblk, 0)))
            args.append(u)
        out_specs.append(pl.BlockSpec((B, CONV_CB, L), lambda i: (0, i, 0)))
        out_shape.append(jax.ShapeDtypeStruct((B, HY_WIDTH, L), F32))
    return pl.pallas_call(
        functools.partial(_hyena_kernel, L=L, n_groups=n_groups),
        grid=(nblk,),
        in_specs=in_specs,
        out_specs=out_specs,
        out_shape=out_shape,
        scratch_shapes=[pltpu.VMEM((CONV_UNROLL, HY_ORDER, MXU, 2 * L), BF16)]
        + [pltpu.VMEM((CONV_UNROLL, 3, u.shape[0], L), F32) for u in u_ts],
        compiler_params=_cparams(1),
        name="hyena_conv",
    )(*args)


def _attn_kernel(q_ref, k_ref, v_ref, o_ref):
    L = v_ref.shape[1]
    one_col = jnp.where(lax.broadcasted_iota(jnp.int32, (L, LANE), 1) == 0, 1.0, 0.0).astype(BF16)
    for pair in range(MLA_HEADS // 2):
        lanes = pl.ds(pair * LANE, LANE)
        v_ext = jnp.concatenate([v_ref[0, :, lanes], one_col], axis=1)
        outs = []
        for h in (2 * pair, 2 * pair + 1):
            s = lax.dot_general(q_ref[0, h], k_ref[0, h], (((1,), (1,)), ((), ())), preferred_element_type=F32)
            p = jnp.exp2(s - jnp.max(s, axis=-1, keepdims=True)).astype(BF16)
            pv = jnp.dot(p, v_ext, preferred_element_type=F32)
            outs.append(pv[:, :LANE] / pv[:, LANE:LANE + 1])
        lane = lax.broadcasted_iota(jnp.int32, outs[0].shape, 1)
        o_ref[0, :, lanes] = jnp.where(lane < V_HEAD, outs[0], outs[1]).astype(BF16)


def _attention(q, k, v, tq):
    B, H, L, _ = q.shape
    return pl.pallas_call(
        _attn_kernel,
        grid=(B, L // tq),
        in_specs=[
            pl.BlockSpec((1, H, tq, HEAD_PAD), lambda b, i: (b, 0, i, 0)),
            pl.BlockSpec((1, H, L, HEAD_PAD), lambda b, i: (b, 0, 0, 0)),
            pl.BlockSpec((1, L, H * V_HEAD), lambda b, i: (b, 0, 0)),
        ],
        out_specs=pl.BlockSpec((1, tq, H * V_HEAD), lambda b, i: (b, i, 0)),
        out_shape=jax.ShapeDtypeStruct((B, L, H * V_HEAD), BF16),
        compiler_params=_cparams(2),
        name="mla_attention",
    )(q, k, v)


def _pack_rows(a):
    bits = lax.bitcast_convert_type(a.astype(BF16).astype(F32), jnp.uint32)
    return (bits[:, :PACKED] >> 16) | (bits[:, PACKED:] & jnp.uint32(0xFFFF0000))


def _unpack_rows(p):
    lo = lax.bitcast_convert_type(p << 16, F32)
    hi = lax.bitcast_convert_type(p & jnp.uint32(0xFFFF0000), F32)
    return jnp.concatenate([lo, hi], axis=1)


def _merge_kernel(x_ref, yh_ref, ym_ref, g_ref, who_ref, wmo_ref, wo_ref, nf_ref, rwh_ref, rwl_ref, rb_ref,
                  x1_ref, h2_ref, ids_ref, gate_ref, cnt_ref):
    tm = x_ref.shape[0]
    sub = tm // 2

    @pl.when(jnp.logical_and(pl.program_id(0) == 0, pl.program_id(1) == 0))
    def _():
        cnt_ref[...] = jnp.zeros_like(cnt_ref)

    eid = lax.broadcasted_iota(jnp.int32, (N_EXPERTS, sub), 0)
    for r in range(0, tm, sub):
        rows = pl.ds(r, sub)
        g = g_ref[rows, :].astype(F32)
        a = lax.dot_general(yh_ref[:, rows].astype(BF16), who_ref[...], (((0,), (0,)), ((), ())),
                            preferred_element_type=F32)
        b = jnp.dot(ym_ref[rows, :], wmo_ref[...], preferred_element_type=F32)
        merged = jax.nn.sigmoid(g[:, :D_MODEL]) * a + jax.nn.sigmoid(g[:, D_MODEL:]) * b
        x1 = x_ref[rows, :] + jnp.dot(merged.astype(BF16), wo_ref[...], preferred_element_type=F32)
        x1_ref[rows, :] = x1
        h2 = x1 * lax.rsqrt(jnp.mean(x1 * x1, axis=-1, keepdims=True) + EPS) * nf_ref[...]
        h2_ref[rows, :] = _pack_rows(h2)
        h_hi = h2.astype(BF16)
        h_lo = (h2 - h_hi.astype(F32)).astype(BF16)
        logits = (jnp.dot(h_hi, rwh_ref[...], preferred_element_type=F32)
                  + jnp.dot(h_lo, rwh_ref[...], preferred_element_type=F32)
                  + jnp.dot(h_hi, rwl_ref[...], preferred_element_type=F32) + rb_ref[...])
        work = logits.T[:N_EXPERTS]
        picked = jnp.zeros((N_EXPERTS, sub), F32)
        vals, firsts = [], []
        for k in range(TOP_K):
            m = jnp.max(work, axis=0, keepdims=True)
            first = jnp.min(jnp.where(work == m, eid, N_EXPERTS), axis=0, keepdims=True)
            hot = eid == first
            vals.append(m)
            firsts.append(first)
            picked = jnp.where(hot, 1.0, picked)
            work = jnp.where(hot, -jnp.inf, work)
        ex = [jnp.exp(v - vals[0]) for v in vals]
        inv = 1.0 / (ex[0] + ex[1] + ex[2] + ex[3])
        ids_ref[:, rows] = jnp.concatenate(firsts + [jnp.zeros((8 - TOP_K, sub), jnp.int32)], axis=0)
        gates_t = jnp.concatenate([e * inv for e in ex] + [jnp.zeros((LANE - TOP_K, sub), F32)], axis=0)
        gate_ref[rows, :] = gates_t.T
        cnt_ref[...] += jnp.sum(picked, axis=1, keepdims=True)


def _merge(x, yh_t, ym, g, P, tm):
    B, L, _ = x.shape
    nt = L // tm
    w = [P[n] for n in ("w_hy_o", "w_mla_o", "w_out", "norm_ffn", "router_w_hi", "router_w_lo", "router_b")]
    return pl.pallas_call(
        _merge_kernel,
        grid=(B, nt),
        in_specs=[
            pl.BlockSpec((None, tm, D_MODEL), lambda b, i: (b, i, 0)),
            pl.BlockSpec((None, HY_WIDTH, tm), lambda b, i: (b, 0, i)),
            pl.BlockSpec((None, tm, MLA_HEADS * V_HEAD), lambda b, i: (b, i, 0)),
            pl.BlockSpec((None, tm, 2 * D_MODEL), lambda b, i: (b, i, 0)),
        ] + [_full(a.shape) for a in w],
        out_specs=[
            pl.BlockSpec((tm, D_MODEL), lambda b, i: (b * nt + i, 0)),
            pl.BlockSpec((tm, PACKED), lambda b, i: (b * nt + i, 0)),
            pl.BlockSpec((8, tm), lambda b, i: (0, b * nt + i)),
            pl.BlockSpec((tm, LANE), lambda b, i: (b * nt + i, 0)),
            pl.BlockSpec((N_EXPERTS, 1), lambda b, i: (0, 0)),
        ],
        out_shape=[
            jax.ShapeDtypeStruct((B * L, D_MODEL), F32),
            jax.ShapeDtypeStruct((B * L, PACKED), jnp.uint32),
            jax.ShapeDtypeStruct((8, B * L), jnp.int32),
            jax.ShapeDtypeStruct((B * L, LANE), F32),
            jax.ShapeDtypeStruct((N_EXPERTS, 1), F32),
        ],
        compiler_params=_cparams(2),
        name="merge_router",
    )(x, yh_t, ym, g, *w)


def _expert_kernel(be_ref, na_ref, end_ref, x_ref, wgu_hbm, bgu_ref, wd_hbm, bd_ref, y_ref,
                   wgu_buf, wd_buf, sems, slot_ref):
    i = pl.program_id(0)
    active = i < na_ref[0]
    expert = be_ref[i]
    first = jnp.logical_and(active, jnp.logical_or(i == 0, expert != be_ref[jnp.maximum(i - 1, 0)]))

    def weight_copies(e, s):
        return (pltpu.make_async_copy(wgu_hbm.at[e], wgu_buf.at[s], sems.at[0, s]),
                pltpu.make_async_copy(wd_hbm.at[e], wd_buf.at[s], sems.at[1, s]))

    @pl.when(i == 0)
    def _():
        slot_ref[0] = 1
        for cp in weight_copies(expert, 0):
            cp.start()

    @pl.when(first)
    def _():
        s = 1 - slot_ref[0]
        slot_ref[0] = s
        for cp in weight_copies(expert, s):
            cp.wait()
        run_end = end_ref[expert]

        @pl.when(run_end < na_ref[0])
        def _():
            for cp in weight_copies(be_ref[jnp.minimum(run_end, pl.num_programs(0) - 1)], 1 - s):
                cp.start()

    slot = slot_ref[0]

    @pl.when(active)
    def _():
        x = _unpack_rows(x_ref[...])
        gu = jnp.dot(x, wgu_buf[slot], preferred_element_type=F32) + bgu_ref[0]
        gate = jnp.minimum(gu[:, :D_FF], SWIGLU_LIMIT)
        up = jnp.clip(gu[:, D_FF:], -SWIGLU_LIMIT, SWIGLU_LIMIT)
        act = (up + 1.0) * (gate * jax.nn.sigmoid(SWIGLU_ALPHA * gate))
        y_ref[...] = _pack_rows(jnp.dot(act, wd_buf[slot], preferred_element_type=F32) + bd_ref[0])

    @pl.when(i >= na_ref[0])
    def _():
        y_ref[...] = jnp.zeros_like(y_ref)


def _experts(plan, xs, P):
    n_rows = xs.shape[0]
    nb = n_rows // MOE_BLOCK
    grid_spec = pltpu.PrefetchScalarGridSpec(
        num_scalar_prefetch=3,
        grid=(nb,),
        in_specs=[
            pl.BlockSpec((MOE_BLOCK, PACKED), lambda i, be, *_: (i, 0)),
            pl.BlockSpec(memory_space=pl.ANY),
            pl.BlockSpec((1, 1, 2 * D_FF), lambda i, be, *_: (be[i], 0, 0)),
            pl.BlockSpec(memory_space=pl.ANY),
            pl.BlockSpec((1, 1, D_MODEL), lambda i, be, *_: (be[i], 0, 0)),
        ],
        out_specs=pl.BlockSpec((MOE_BLOCK, PACKED), lambda i, be, *_: (i, 0)),
        scratch_shapes=[
            pltpu.VMEM((2, D_MODEL, 2 * D_FF), F32),
            pltpu.VMEM((2, D_FF, D_MODEL), F32),
            pltpu.SemaphoreType.DMA((2, 2)),
            pltpu.SMEM((1,), jnp.int32),
        ],
    )
    return pl.pallas_call(
        _expert_kernel,
        grid_spec=grid_spec,
        out_shape=jax.ShapeDtypeStruct((n_rows, PACKED), jnp.uint32),
        compiler_params=_cparams(1),
        name="moe_experts",
    )(*plan, xs, P["w_gu"], P["b_gu"], P["w_down"], P["b_down"])


def _combine_kernel(x1_ref, yk_ref, gt_ref, o_ref):
    gt = gt_ref[...]
    acc = _unpack_rows(yk_ref[0]) * gt[:, 0:1]
    for k in range(1, TOP_K):
        acc = acc + _unpack_rows(yk_ref[k]) * gt[:, k:k + 1]
    o_ref[...] = x1_ref[...] + acc


def _combine(x1, yk, gates, tm):
    T = x1.shape[0]
    return pl.pallas_call(
        _combine_kernel,
        grid=(T // tm,),
        in_specs=[
            pl.BlockSpec((tm, D_MODEL), lambda i: (i, 0)),
            pl.BlockSpec((TOP_K, tm, PACKED), lambda i: (0, i, 0)),
            pl.BlockSpec((tm, LANE), lambda i: (i, 0)),
        ],
        out_specs=pl.BlockSpec((tm, D_MODEL), lambda i: (i, 0)),
        out_shape=jax.ShapeDtypeStruct((T, D_MODEL), F32),
        compiler_params=_cparams(1),
        name="moe_combine",
    )(x1, yk, gates)


def _route_kernel(ids_ref, start_ref, dest_ref, run_ref, *, tm):
    @pl.when(pl.program_id(0) == 0)
    def _():
        run_ref[...] = jnp.zeros_like(run_ref)

    eid = lax.broadcasted_iota(jnp.int32, (N_EXPERTS, tm), 0)
    ids = ids_ref[...]
    hots = [eid == ids[k:k + 1] for k in range(TOP_K)]
    chosen = jnp.zeros((N_EXPERTS, tm), F32)
    for hot in hots:
        chosen = jnp.where(hot, 1.0, chosen)
    r = lax.broadcasted_iota(jnp.int32, (tm, tm), 0)
    c = lax.broadcasted_iota(jnp.int32, (tm, tm), 1)
    earlier = jnp.where(r < c, 1.0, 0.0).astype(BF16)
    before = jnp.dot(chosen.astype(BF16), earlier, preferred_element_type=F32) + (run_ref[...] + start_ref[...])
    rows = [jnp.sum(jnp.where(hot, before, 0.0), axis=0, keepdims=True) for hot in hots]
    dest_ref[...] = jnp.concatenate(rows + [jnp.zeros((8 - TOP_K, tm), F32)], axis=0).astype(jnp.int32)
    run_ref[...] += jnp.sum(chosen, axis=1, keepdims=True)


def _route(ids_t, counts, tm):
    T = ids_t.shape[1]
    counts = counts[:, 0].astype(jnp.int32)
    padded = (counts + MOE_BLOCK - 1) // MOE_BLOCK * MOE_BLOCK
    pad_end = jnp.cumsum(padded)
    start = (pad_end - padded).astype(F32).reshape(N_EXPERTS, 1)
    dest = pl.pallas_call(
        functools.partial(_route_kernel, tm=tm),
        grid=(T // tm,),
        in_specs=[pl.BlockSpec((8, tm), lambda i: (0, i)), _full((N_EXPERTS, 1))],
        out_specs=pl.BlockSpec((8, tm), lambda i: (0, i)),
        out_shape=jax.ShapeDtypeStruct((8, T), jnp.int32),
        scratch_shapes=[pltpu.VMEM((N_EXPERTS, 1), F32)],
        compiler_params=_cparams(1),
        name="moe_route",
    )(ids_t, start)
    n_blocks = -(-(T * TOP_K + N_EXPERTS * (MOE_BLOCK - 1)) // MOE_BLOCK)
    block_start = jnp.arange(n_blocks, dtype=jnp.int32) * MOE_BLOCK
    block_e = jnp.sum((block_start[:, None] >= pad_end[None, :]).astype(jnp.int32), axis=1)
    block_e = jnp.minimum(block_e, N_EXPERTS - 1)
    end_blk = (pad_end // MOE_BLOCK).astype(jnp.int32)
    plan = (block_e.astype(jnp.int32), end_blk[-1:], end_blk)
    dest_flat = dest[:TOP_K].reshape(TOP_K * T)
    return dest_flat, plan, n_blocks * MOE_BLOCK


def _sc_workers():
    info = plsc.get_sparse_core_info()
    return info.num_cores, info.num_cores * info.num_subcores


def _sc_scatter_rows(rows, dest_flat, n_rows):
    T, W = rows.shape
    n_cores, n_workers = _sc_workers()
    per_w = T // n_workers
    assert per_w * n_workers == T and per_w % SC_CHUNK == 0

    def body(rows_hbm, dest_hbm, out_hbm, buf, *idx):
        wid = lax.axis_index("s") * n_cores + lax.axis_index("c")

        @pl.loop(0, per_w // SC_CHUNK)
        def _(j):
            off = pl.multiple_of(wid * per_w + j * SC_CHUNK, SC_CHUNK)
            pltpu.sync_copy(rows_hbm.at[pl.ds(off, SC_CHUNK)], buf)
            for k in range(TOP_K):
                pltpu.sync_copy(dest_hbm.at[pl.ds(k * T + off, SC_CHUNK)], idx[k])
            for k in range(TOP_K):
                pltpu.sync_copy(buf, out_hbm.at[idx[k]])

    return pl.kernel(
        body,
        out_type=jax.ShapeDtypeStruct((n_rows, W), rows.dtype),
        mesh=plsc.VectorSubcoreMesh(core_axis_name="c", subcore_axis_name="s"),
        scratch_types=[pltpu.VMEM((SC_CHUNK, W), rows.dtype)] + [pltpu.VMEM((SC_CHUNK,), jnp.int32)] * TOP_K,
        name="moe_dispatch_sc",
    )(rows, dest_flat)


def _sc_gather_rows(table, dest_flat):
    N = dest_flat.shape[0]
    W = table.shape[1]
    n_cores, n_workers = _sc_workers()
    per_w = N // n_workers
    assert per_w * n_workers == N and per_w % SC_CHUNK == 0

    def body(table_hbm, dest_hbm, out_hbm, buf, idx):
        wid = lax.axis_index("s") * n_cores + lax.axis_index("c")

        @pl.loop(0, per_w // SC_CHUNK)
        def _(j):
            off = pl.multiple_of(wid * per_w + j * SC_CHUNK, SC_CHUNK)
            pltpu.sync_copy(dest_hbm.at[pl.ds(off, SC_CHUNK)], idx)
            pltpu.sync_copy(table_hbm.at[idx], buf)
            pltpu.sync_copy(buf, out_hbm.at[pl.ds(off, SC_CHUNK)])

    return pl.kernel(
        body,
        out_type=jax.ShapeDtypeStruct((N, W), table.dtype),
        mesh=plsc.VectorSubcoreMesh(core_axis_name="c", subcore_axis_name="s"),
        scratch_types=[pltpu.VMEM((SC_CHUNK, W), table.dtype), pltpu.VMEM((SC_CHUNK,), jnp.int32)],
        name="moe_gather_sc",
    )(table, dest_flat)


def _moe(groups, P, tm):
    staged = []
    for x1, h2p, ids, gates, counts in groups:
        dest_flat, plan, n_rows = _route(ids, counts, 4 * tm if x1.shape[0] % (4 * tm) == 0 else tm)
        staged.append((dest_flat, plan, _sc_scatter_rows(h2p, dest_flat, n_rows)))
    ys = []
    for dest_flat, plan, xs in staged:
        if ys:
            xs, ys[-1] = lax.optimization_barrier((xs, ys[-1]))
        ys.append(_experts(plan, xs, P))
    outs = []
    for (x1, _, _, gates, _), (dest_flat, _, _), y in zip(groups, staged, ys):
        yk = _sc_gather_rows(y, dest_flat).reshape(TOP_K, x1.shape[0], PACKED)
        outs.append(_combine(x1, yk, gates, 2 * tm if x1.shape[0] % (2 * tm) == 0 else tm))
    return outs


def _prep_params(norm_mix, w_in, q_a_norm, w_uq, kv_a_norm, w_ukv, q_norm, k_norm, w_hy_o, w_mla_o, w_out,
                 norm_ffn, router_w, router_b, w_gu, b_gu, w_down, b_down, L):
    P = {}
    P["norm_mix"] = norm_mix.astype(F32).reshape(1, D_MODEL)
    P["w_hy_t"] = w_in[:, :HY_END].T.astype(BF16)
    P["w_m"] = jnp.pad(w_in[:, HY_END:KR_END], ((0, 0), (0, MLA_IN - (KR_END - HY_END)))).astype(BF16)
    P["w_g"] = w_in[:, KR_END:].astype(BF16)

    half = QK_ROPE // 2
    swap = jnp.concatenate([jnp.arange(QK_NOPE), QK_NOPE + half + jnp.arange(half), QK_NOPE + jnp.arange(half)])
    rope_lane = (jnp.arange(QK_HEAD) >= QK_NOPE)

    def slots(w):
        return jnp.pad(w, ((0, 0), (0, 0), (0, HEAD_PAD - QK_HEAD))).reshape(w.shape[0], MLA_HEADS * HEAD_PAD)

    wq = w_uq.astype(F32).reshape(Q_LORA, MLA_HEADS, QK_HEAD)
    P["wqa"] = slots(wq).astype(BF16)
    P["wqb"] = slots(jnp.where(rope_lane, wq[:, :, swap], 0.0)).astype(BF16)
    wkv = w_ukv.astype(F32).reshape(KV_LORA, MLA_HEADS, QK_NOPE + V_HEAD)
    P["wka"] = slots(jnp.pad(wkv[:, :, :QK_NOPE], ((0, 0), (0, 0), (0, QK_ROPE)))).astype(BF16)
    P["wv"] = wkv[:, :, QK_NOPE:].reshape(KV_LORA, MLA_HEADS * V_HEAD).astype(BF16)
    eye = jnp.eye(LANE, QK_ROPE, dtype=F32)
    e_a = jnp.pad(eye, ((0, 0), (QK_NOPE, 0)))
    e_b = e_a[:, swap] * rope_lane
    P["ea"] = slots(jnp.broadcast_to(e_a[:, None, :], (LANE, MLA_HEADS, QK_HEAD))).astype(BF16)
    P["eb"] = slots(jnp.broadcast_to(e_b[:, None, :], (LANE, MLA_HEADS, QK_HEAD))).astype(BF16)

    sign = jnp.concatenate([jnp.zeros(QK_NOPE), -jnp.ones(half), jnp.ones(half)]).astype(F32)

    def lane_pad(g):
        return jnp.pad(g, (0, HEAD_PAD - QK_HEAD)).reshape(1, HEAD_PAD)

    for name, g in (("gq", q_norm.astype(F32)), ("gk", k_norm.astype(F32))):
        P[name] = lane_pad(g)
        P[name + "s"] = lane_pad(g[swap] * sign)
    P["gqa"] = q_a_norm.astype(F32).reshape(1, Q_LORA)
    P["gkva"] = kv_a_norm.astype(F32).reshape(1, KV_LORA)
    inv = 1.0 / (ROPE_THETA ** (jnp.arange(0, QK_ROPE, 2, dtype=F32) / QK_ROPE))
    ang = jnp.arange(L, dtype=F32)[:, None] * inv[None, :]
    cos, sin = jnp.cos(ang), jnp.sin(ang)
    P["ctab"] = jnp.concatenate([jnp.ones((L, QK_NOPE), F32), cos, cos, jnp.zeros((L, HEAD_PAD - QK_HEAD), F32)], axis=1)
    P["stab"] = jnp.concatenate([jnp.zeros((L, QK_NOPE), F32), sin, sin, jnp.zeros((L, HEAD_PAD - QK_HEAD), F32)], axis=1)

    P["w_hy_o"] = w_hy_o.astype(BF16)
    P["w_mla_o"] = w_mla_o.astype(BF16)
    P["w_out"] = w_out.astype(BF16)
    P["norm_ffn"] = norm_ffn.astype(F32).reshape(1, D_MODEL)
    rw = jnp.pad(router_w.astype(F32), ((0, 0), (0, LANE - N_EXPERTS)))
    P["router_w_hi"] = rw.astype(BF16)
    P["router_w_lo"] = (rw - P["router_w_hi"].astype(F32)).astype(BF16)
    P["router_b"] = jnp.pad(router_b.astype(F32), (0, LANE - N_EXPERTS)).reshape(1, LANE)
    P["w_gu"] = w_gu.astype(F32)
    P["b_gu"] = b_gu.astype(F32).reshape(N_EXPERTS, 1, 2 * D_FF)
    P["w_down"] = w_down.astype(F32)
    P["b_down"] = b_down.astype(F32).reshape(N_EXPERTS, 1, D_MODEL)
    return P


def _tile(n, want):
    t = min(n, want)
    assert n % t == 0
    return t


def _layer(xs, norm_mix, w_in, hy_conv_w, hy_conv_b, hy_fw1, hy_fb1, hy_fw2, hy_fb2, hy_fw3, hy_freq, hy_decay, hy_skip,
           q_a_norm, w_uq, kv_a_norm, w_ukv, q_norm, k_norm, w_hy_o, w_mla_o, w_out,
           norm_ffn, router_w, router_b, w_gu, b_gu, w_down, b_down):
    L = xs[0].shape[1]
    assert all(x.shape[1] == L and x.shape[2] == D_MODEL for x in xs) and L % MXU == 0
    P = _prep_params(norm_mix, w_in, q_a_norm, w_uq, kv_a_norm, w_ukv, q_norm, k_norm, w_hy_o, w_mla_o, w_out,
                     norm_ffn, router_w, router_b, w_gu, b_gu, w_down, b_down, L)
    kf = _hyena_filters(L, hy_fw1, hy_fb1, hy_fw2, hy_fb2, hy_fw3, hy_freq, hy_decay)
    tm = _tile(L, 512)
    proj = [_inproj(x, P, tm) for x in xs]
    y_ts = _hyena([p[0] for p in proj], kf, hy_conv_w, hy_conv_b, hy_skip, L)
    mixed = []
    for x, (_, g, q, k, v), y_t in zip(xs, proj, y_ts):
        y_mla = _attention(q, k, v, _tile(L, 512))
        mixed.append(_merge(x, y_t, y_mla, g, P, _tile(L, 1024)))
    order = sorted(range(len(xs)), key=lambda n: -xs[n].shape[0])
    moe_out = _moe([mixed[n] for n in order], P, tm)
    outs = [None] * len(xs)
    for n, y in zip(order, moe_out):
        outs[n] = y.reshape(xs[n].shape)
    return outs


def kernel(x_prompt, x_sample, norm_mix, w_in, hy_conv_w, hy_conv_b, hy_fw1, hy_fb1, hy_fw2, hy_fb2, hy_fw3, hy_freq, hy_decay, hy_skip, q_a_norm, w_uq, kv_a_norm, w_ukv, q_norm, k_norm, w_hy_o, w_mla_o, w_out, norm_ffn, router_w, router_b, w_gu, b_gu, w_down, b_down):
    depth = norm_mix.shape[0]
    xs = [x_prompt, x_sample]
    weights = (norm_mix, w_in, hy_conv_w, hy_conv_b, hy_fw1, hy_fb1, hy_fw2, hy_fb2, hy_fw3, hy_freq, hy_decay, hy_skip,
               q_a_norm, w_uq, kv_a_norm, w_ukv, q_norm, k_norm, w_hy_o, w_mla_o, w_out,
               norm_ffn, router_w, router_b, w_gu, b_gu, w_down, b_down)
    for l in range(depth):
        xs = _layer(xs, *[w[l] for w in weights])
    return (xs[0], xs[1])
```

```python
import functools
import math

import jax
import jax.numpy as jnp
from jax import lax
from jax.experimental import pallas as pl
from jax.experimental.pallas import tpu as pltpu
from jax.experimental.pallas import tpu_sc as plsc

F32 = jnp.float32
BF16 = jnp.bfloat16

D_MODEL = 1024
HY_WIDTH = 512
HY_ORDER = 2
HY_EMB_BANDS = 16
HY_FILTER_DIM = 64
MLA_HEADS = 8
V_HEAD = 64
QK_NOPE = 64
QK_ROPE = 32
QK_HEAD = QK_NOPE + QK_ROPE
Q_LORA = 256
KV_LORA = 128
ROPE_THETA = 10000.0
HY_END = 3 * HY_WIDTH
Q_END = HY_END + Q_LORA
KV_END = Q_END + KV_LORA
KR_END = KV_END + QK_ROPE
G1_END = KR_END + D_MODEL
N_EXPERTS = 32
TOP_K = 4
D_FF = D_MODEL
SWIGLU_LIMIT = 7.0
SWIGLU_ALPHA = 1.702
EPS = 1e-6

LANE = 128
MXU = 256
BF16_ROWS = 16
HEAD_PAD = LANE
MLA_IN = 512
CONV_CB = 8
CONV_UNROLL = 4
MOE_BLOCK = 512
PACKED = D_MODEL // 2
SC_CHUNK = 64
VMEM_LIMIT = 52 * 1024 * 1024


def _cparams(n_axes):
    return pltpu.CompilerParams(dimension_semantics=("arbitrary",) * n_axes, vmem_limit_bytes=VMEM_LIMIT)


def _full(shape):
    n = len(shape)
    return pl.BlockSpec(shape, lambda *_: (0,) * n)


def _rms(x, n):
    return lax.rsqrt(jnp.sum(x * x, axis=-1, keepdims=True) * (1.0 / n) + EPS)


def _mla_heads(m, ct, st, gqa_ref, gkva_ref, gq_ref, gqs_ref, gk_ref, gks_ref,
               wqa_ref, wqb_ref, wka_ref, ea_ref, eb_ref, wv_ref, q_ref, k_ref, v_ref):
    mf = m.astype(F32)
    cq = mf[:, :Q_LORA]
    ckv = mf[:, Q_LORA:Q_LORA + KV_LORA]
    kr = m[:, Q_LORA + KV_LORA:Q_LORA + KV_LORA + LANE]
    cqn = (cq * _rms(cq, Q_LORA) * gqa_ref[...]).astype(BF16)
    ckvn = (ckv * _rms(ckv, KV_LORA) * gkva_ref[...]).astype(BF16)
    qa = jnp.dot(cqn, wqa_ref[...], preferred_element_type=F32)
    qb = jnp.dot(cqn, wqb_ref[...], preferred_element_type=F32)
    ka = jnp.dot(ckvn, wka_ref[...], preferred_element_type=F32) + jnp.dot(kr, ea_ref[...], preferred_element_type=F32)
    kb = jnp.dot(kr, eb_ref[...], preferred_element_type=F32)
    v_ref[...] = jnp.dot(ckvn, wv_ref[...], preferred_element_type=F32).astype(BF16)
    scale = QK_HEAD ** -0.5 * math.log2(math.e)
    qc, qs = gq_ref[...] * ct * scale, gqs_ref[...] * st * scale
    kc, ks = gk_ref[...] * ct, gks_ref[...] * st
    for h in range(MLA_HEADS):
        sl = slice(h * HEAD_PAD, (h + 1) * HEAD_PAD)
        q_ref[h] = (_rms(qa[:, sl], QK_HEAD) * (qa[:, sl] * qc + qb[:, sl] * qs)).astype(BF16)
        k_ref[h] = (_rms(ka[:, sl], QK_HEAD) * (ka[:, sl] * kc + kb[:, sl] * ks)).astype(BF16)


def _inproj_kernel(x_ref, g_ref, why_ref, wm_ref, wg_ref, ct_ref, st_ref, *refs):
    mla_refs, (u_ref, gate_ref, q_ref, k_ref, v_ref) = refs[:12], refs[12:]
    tm = x_ref.shape[0]
    sub = tm // 2
    for r0 in range(0, tm, sub):
        rows = pl.ds(r0, sub)
        x = x_ref[rows, :]
        r = lax.rsqrt(jnp.mean(x * x, axis=-1, keepdims=True) + EPS)
        h = (x * r * g_ref[...]).astype(BF16)
        m = jnp.dot(h, wm_ref[...], preferred_element_type=F32).astype(BF16)
        _mla_heads(m, ct_ref[rows, :], st_ref[rows, :], *mla_refs,
                   q_ref.at[:, rows, :], k_ref.at[:, rows, :], v_ref.at[rows, :])
        u_ref[:, rows] = lax.dot_general(why_ref[...], h, (((1,), (1,)), ((), ())), preferred_element_type=F32)
        gate_ref[rows, :] = jnp.dot(h, wg_ref[...], preferred_element_type=F32).astype(BF16)


def _inproj(x, P, tm):
    B, L, _ = x.shape
    w = [P[n] for n in ("w_hy_t", "w_m", "w_g")]
    mla = [P[n] for n in ("gqa", "gkva", "gq", "gqs", "gk", "gks", "wqa", "wqb", "wka", "ea", "eb", "wv")]
    return pl.pallas_call(
        _inproj_kernel,
        grid=(B, L // tm),
        in_specs=[pl.BlockSpec((None, tm, D_MODEL), lambda b, i: (b, i, 0)), _full((1, D_MODEL))]
        + [_full(a.shape) for a in w]
        + [pl.BlockSpec((tm, HEAD_PAD), lambda b, i: (i, 0)), pl.BlockSpec((tm, HEAD_PAD), lambda b, i: (i, 0))]
        + [_full(a.shape) for a in mla],
        out_specs=[
            pl.BlockSpec((None, HY_END, tm), lambda b, i: (b, 0, i)),
            pl.BlockSpec((None, tm, 2 * D_MODEL), lambda b, i: (b, i, 0)),
            pl.BlockSpec((None, MLA_HEADS, tm, HEAD_PAD), lambda b, i: (b, 0, i, 0)),
            pl.BlockSpec((None, MLA_HEADS, tm, HEAD_PAD), lambda b, i: (b, 0, i, 0)),
            pl.BlockSpec((None, tm, MLA_HEADS * V_HEAD), lambda b, i: (b, i, 0)),
        ],
        out_shape=[
            jax.ShapeDtypeStruct((B, HY_END, L), F32),
            jax.ShapeDtypeStruct((B, L, 2 * D_MODEL), BF16),
            jax.ShapeDtypeStruct((B, MLA_HEADS, L, HEAD_PAD), BF16),
            jax.ShapeDtypeStruct((B, MLA_HEADS, L, HEAD_PAD), BF16),
            jax.ShapeDtypeStruct((B, L, MLA_HEADS * V_HEAD), BF16),
        ],
        compiler_params=_cparams(2),
        name="inproj",
    )(x, P["norm_mix"], *w, P["ctab"][:L], P["stab"][:L], *mla)


def _filter_kernel(bands_ref, w1t_ref, w1c_ref, w1s_ref, b1_ref, w2_ref, b2_ref, fr_ref, w3f_ref, w3b_ref, dec_ref,
                   out_ref, *, L):
    hi = lax.Precision.HIGHEST
    n = 2 * L
    j = lax.broadcasted_iota(jnp.int32, (1, n), 1)
    pos = jnp.abs(j - L).astype(F32)
    t01 = pos / max(L - 1, 1)
    ang = ((2.0 * math.pi / L) * pos) * bands_ref[...]
    pre = (w1t_ref[...] * t01
           + jnp.dot(w1c_ref[...], jnp.cos(ang), precision=hi, preferred_element_type=F32)
           - jnp.dot(w1s_ref[...], jnp.sin(ang), precision=hi, preferred_element_type=F32)
           + b1_ref[...])
    z = jnp.sin(fr_ref[0] * pre)
    z = jnp.sin(fr_ref[1] * (jnp.dot(w2_ref[...], z, precision=hi, preferred_element_type=F32) + b2_ref[...]))
    win = jnp.exp(-t01 * jnp.abs(dec_ref[...]))
    h_b = jnp.dot(w3b_ref[...], z[:, :L], precision=hi, preferred_element_type=F32) * win[:, :L]
    h_f = jnp.dot(w3f_ref[...], z[:, L:], precision=hi, preferred_element_type=F32) * win[:, L:]
    out_ref[:, :L] = jnp.where(j[:, :L] == 0, 0.0, h_b)
    out_ref[:, L:] = h_f


def _hyena_filters(L, fw1, fb1, fw2, fb2, fw3, freq, decay):
    rows = HY_ORDER * HY_WIDTH
    rb = 256
    bands = jnp.linspace(1e-4, HY_EMB_BANDS - 1, HY_EMB_BANDS, dtype=F32).reshape(HY_EMB_BANDS, 1)
    w1 = fw1.astype(F32).T
    w3 = fw3.astype(F32).T
    fd = HY_FILTER_DIM
    return pl.pallas_call(
        functools.partial(_filter_kernel, L=L),
        grid=(rows // rb,),
        in_specs=[
            _full((HY_EMB_BANDS, 1)),
            _full((fd, 1)),
            _full((fd, HY_EMB_BANDS)),
            _full((fd, HY_EMB_BANDS)),
            _full((fd, 1)),
            _full((fd, fd)),
            _full((fd, 1)),
            _full((2, fd, 1)),
            pl.BlockSpec((rb, fd), lambda i: (i, 0)),
            pl.BlockSpec((rb, fd), lambda i: (i, 0)),
            pl.BlockSpec((rb, 1), lambda i: (i, 0)),
        ],
        out_specs=pl.BlockSpec((rb, 2 * L), lambda i: (i, 0)),
        out_shape=jax.ShapeDtypeStruct((rows, 2 * L), F32),
        compiler_params=_cparams(1),
        name="hyena_filters",
    )(bands, w1[:, :1], w1[:, 1:1 + HY_EMB_BANDS], w1[:, 1 + HY_EMB_BANDS:], fb1.astype(F32).reshape(fd, 1),
      fw2.astype(F32).T, fb2.astype(F32).reshape(fd, 1), freq.astype(F32).reshape(2, fd, 1),
      w3[:rows], w3[rows:], decay.astype(F32).reshape(rows, 1))


def _shift_conv(u, w0, w1, w2, b):
    L = u.shape[1]
    lane = lax.broadcasted_iota(jnp.int32, (1, L), 1)
    w0_row = jnp.where(lane == 0, 0.0, w0)
    w2_row = jnp.where(lane == L - 1, 0.0, w2)
    return pltpu.roll(u, 1, 1) * w0_row + u * w1 + pltpu.roll(u, L - 1, 1) * w2_row + b


def _toeplitz_conv(zs, tz_ref, L):
    nj = L // MXU
    zbs = [z.astype(BF16) for z in zs]
    accs = [[None] * nj for _ in zs]
    for d in range(-(nj - 1), nj):
        js = [J for J in range(nj) if 0 <= J + d < nj]
        pieces = [zb[:, MXU * J:MXU * (J + 1)] for zb in zbs for J in js]
        lhs = pieces[0] if len(pieces) == 1 else jnp.concatenate(pieces, axis=0)
        res = jnp.dot(lhs, tz_ref[:, L + MXU * d:L + MXU * (d + 1)], preferred_element_type=F32)
        row = 0
        for acc, z in zip(accs, zs):
            B = z.shape[0]
            for J in js:
                part = res[row:row + B]
                acc[J + d] = part if acc[J + d] is None else acc[J + d] + part
                row += B
    return [acc[0] if nj == 1 else jnp.concatenate(acc, axis=1) for acc in accs]


def _hyena_kernel(cw_ref, cb_ref, skip_ref, kf_ref, *refs, L, n_groups):
    in_refs = refs[:3 * n_groups]
    out_refs = refs[3 * n_groups:4 * n_groups]
    tz_ref = refs[4 * n_groups]
    c0 = pl.program_id(0) * CONV_CB

    def channels(p, carry):
        for u in range(CONV_UNROLL):
            c = p * CONV_UNROLL + u
            ch = c0 + c
            for o in range(HY_ORDER):
                row = jnp.broadcast_to(kf_ref[o, pl.ds(c, 1), :], (BF16_ROWS, 2 * L))
                slab = pltpu.bitcast(pltpu.roll(row, 0, 1, stride=1, stride_axis=0).astype(BF16), jnp.uint32)
                for a in range(MXU // BF16_ROWS):
                    shifted = slab if a == 0 else pltpu.roll(slab, BF16_ROWS * a, 1)
                    tz_ref[u, o, pl.ds(BF16_ROWS * a, BF16_ROWS), :] = pltpu.bitcast(shifted, BF16)
            secs = []
            for g in range(n_groups):
                stage = refs[4 * n_groups + 1 + g]
                sec = []
                for s, ref in enumerate(in_refs[3 * g:3 * g + 3]):
                    stage[u, s] = ref[:, c, :]
                    k = s * HY_WIDTH + ch
                    sec.append(_shift_conv(stage[u, s], cw_ref[0, k], cw_ref[1, k], cw_ref[2, k], cb_ref[k]))
                secs.append(sec)
            zs = [sec[2] for sec in secs]
            for o in range(HY_ORDER):
                ys = _toeplitz_conv(zs, tz_ref.at[u, o], L)
                zs = [sec[o] * (y + z * skip_ref[o, ch]) for sec, y, z in zip(secs, ys, zs)]
            for g in range(n_groups):
                out_refs[g][:, c, :] = zs[g]
        return carry

    lax.fori_loop(0, CONV_CB // CONV_UNROLL, channels, 0)


def _hyena(u_ts, kf, conv_w, conv_b, skip, L):
    n_groups = len(u_ts)
    nblk = HY_WIDTH // CONV_CB
    smem = pl.BlockSpec(memory_space=pltpu.SMEM)
    in_specs = [smem, smem, smem, pl.BlockSpec((HY_ORDER, CONV_CB, 2 * L), lambda i: (0, i, 0))]
    args = [conv_w.astype(F32), conv_b.astype(F32), skip.astype(F32), kf.reshape(HY_ORDER, HY_WIDTH, 2 * L)]
    out_specs, out_shape = [], []
    for u in u_ts:
        B = u.shape[0]
        for s in range(3):
            in_specs.append(pl.BlockSpec((B, CONV_CB, L), lambda i, s=s: (0, i + s * nblk, 0)))
            args.append(u)
        out_specs.append(pl.BlockSpec((B, CONV_CB, L), lambda i: (0, i, 0)))
        out_shape.append(jax.ShapeDtypeStruct((B, HY_WIDTH, L), F32))
    return pl.pallas_call(
        functools.partial(_hyena_kernel, L=L, n_groups=n_groups),
        grid=(nblk,),
        in_specs=in_specs,
        out_specs=out_specs,
        out_shape=out_shape,
        scratch_shapes=[pltpu.VMEM((CONV_UNROLL, HY_ORDER, MXU, 2 * L), BF16)]
        + [pltpu.VMEM((CONV_UNROLL, 3, u.shape[0], L), F32) for u in u_ts],
        compiler_params=_cparams(1),
        name="hyena_conv",
    )(*args)


def _attn_kernel(q_ref, k_ref, v_ref, o_ref):
    L = v_ref.shape[1]
    one_col = jnp.where(lax.broadcasted_iota(jnp.int32, (L, LANE), 1) == 0, 1.0, 0.0).astype(BF16)
    for pair in range(MLA_HEADS // 2):
        lanes = pl.ds(pair * LANE, LANE)
        v_ext = jnp.concatenate([v_ref[0, :, lanes], one_col], axis=1)
        outs = []
        for h in (2 * pair, 2 * pair + 1):
            s = lax.dot_general(q_ref[0, h], k_ref[0, h], (((1,), (1,)), ((), ())), preferred_element_type=F32)
            p = jnp.exp2(s - jnp.max(s, axis=-1, keepdims=True)).astype(BF16)
            pv = jnp.dot(p, v_ext, preferred_element_type=F32)
            outs.append(pv[:, :LANE] / pv[:, LANE:LANE + 1])
        lane = lax.broadcasted_iota(jnp.int32, outs[0].shape, 1)
        o_ref[0, :, lanes] = jnp.where(lane < V_HEAD, outs[0], outs[1]).astype(BF16)


def _attention(q, k, v, tq):
    B, H, L, _ = q.shape
    return pl.pallas_call(
        _attn_kernel,
        grid=(B, L // tq),
        in_specs=[
            pl.BlockSpec((1, H, tq, HEAD_PAD), lambda b, i: (b, 0, i, 0)),
            pl.BlockSpec((1, H, L, HEAD_PAD), lambda b, i: (b, 0, 0, 0)),
            pl.BlockSpec((1, L, H * V_HEAD), lambda b, i: (b, 0, 0)),
        ],
        out_specs=pl.BlockSpec((1, tq, H * V_HEAD), lambda b, i: (b, i, 0)),
        out_shape=jax.ShapeDtypeStruct((B, L, H * V_HEAD), BF16),
        compiler_params=_cparams(2),
        name="mla_attention",
    )(q, k, v)


def _pack_rows(a):
    bits = lax.bitcast_convert_type(a.astype(BF16).astype(F32), jnp.uint32)
    return (bits[:, :PACKED] >> 16) | (bits[:, PACKED:] & jnp.uint32(0xFFFF0000))


def _unpack_rows(p):
    lo = lax.bitcast_convert_type(p << 16, F32)
    hi = lax.bitcast_convert_type(p & jnp.uint32(0xFFFF0000), F32)
    return jnp.concatenate([lo, hi], axis=1)


def _merge_kernel(x_ref, yh_ref, ym_ref, g_ref, who_ref, wmo_ref, wo_ref, nf_ref, rwh_ref, rwl_ref, rb_ref,
                  x1_ref, h2_ref, ids_ref, gate_ref, cnt_ref):
    tm = x_ref.shape[0]
    sub = tm // 2

    @pl.when(jnp.logical_and(pl.program_id(0) == 0, pl.program_id(1) == 0))
    def _():
        cnt_ref[...] = jnp.zeros_like(cnt_ref)

    eid = lax.broadcasted_iota(jnp.int32, (N_EXPERTS, sub), 0)
    for r in range(0, tm, sub):
        rows = pl.ds(r, sub)
        g = g_ref[rows, :].astype(F32)
        a = lax.dot_general(yh_ref[:, rows].astype(BF16), who_ref[...], (((0,), (0,)), ((), ())),
                            preferred_element_type=F32)
        b = jnp.dot(ym_ref[rows, :], wmo_ref[...], preferred_element_type=F32)
        merged = jax.nn.sigmoid(g[:, :D_MODEL]) * a + jax.nn.sigmoid(g[:, D_MODEL:]) * b
        x1 = x_ref[rows, :] + jnp.dot(merged.astype(BF16), wo_ref[...], preferred_element_type=F32)
        x1_ref[rows, :] = x1
        h2 = x1 * lax.rsqrt(jnp.mean(x1 * x1, axis=-1, keepdims=True) + EPS) * nf_ref[...]
        h2_ref[rows, :] = _pack_rows(h2)
        h_hi = h2.astype(BF16)
        h_lo = (h2 - h_hi.astype(F32)).astype(BF16)
        logits = (jnp.dot(h_hi, rwh_ref[...], preferred_element_type=F32)
                  + jnp.dot(h_lo, rwh_ref[...], preferred_element_type=F32)
                  + jnp.dot(h_hi, rwl_ref[...], preferred_element_type=F32) + rb_ref[...])
        work = logits.T[:N_EXPERTS]
        picked = jnp.zeros((N_EXPERTS, sub), F32)
        vals, firsts = [], []
        for k in range(TOP_K):
            m = jnp.max(work, axis=0, keepdims=True)
            first = jnp.min(jnp.where(work == m, eid, N_EXPERTS), axis=0, keepdims=True)
            hot = eid == first
            vals.append(m)
            firsts.append(first)
            picked = jnp.where(hot, 1.0, picked)
            work = jnp.where(hot, -jnp.inf, work)
        ex = [jnp.exp(v - vals[0]) for v in vals]
        inv = 1.0 / (ex[0] + ex[1] + ex[2] + ex[3])
        ids_ref[:, rows] = jnp.concatenate(firsts + [jnp.zeros((8 - TOP_K, sub), jnp.int32)], axis=0)
        gates_t = jnp.concatenate([e * inv for e in ex] + [jnp.zeros((LANE - TOP_K, sub), F32)], axis=0)
        gate_ref[rows, :] = gates_t.T
        cnt_ref[...] += jnp.sum(picked, axis=1, keepdims=True)


def _merge(x, yh_t, ym, g, P, tm):
    B, L, _ = x.shape
    nt = L // tm
    w = [P[n] for n in ("w_hy_o", "w_mla_o", "w_out", "norm_ffn", "router_w_hi", "router_w_lo", "router_b")]
    return pl.pallas_call(
        _merge_kernel,
        grid=(B, nt),
        in_specs=[
            pl.BlockSpec((None, tm, D_MODEL), lambda b, i: (b, i, 0)),
            pl.BlockSpec((None, HY_WIDTH, tm), lambda b, i: (b, 0, i)),
            pl.BlockSpec((None, tm, MLA_HEADS * V_HEAD), lambda b, i: (b, i, 0)),
            pl.BlockSpec((None, tm, 2 * D_MODEL), lambda b, i: (b, i, 0)),
        ] + [_full(a.shape) for a in w],
        out_specs=[
            pl.BlockSpec((tm, D_MODEL), lambda b, i: (b * nt + i, 0)),
            pl.BlockSpec((tm, PACKED), lambda b, i: (b * nt + i, 0)),
            pl.BlockSpec((8, tm), lambda b, i: (0, b * nt + i)),
            pl.BlockSpec((tm, LANE), lambda b, i: (b * nt + i, 0)),
            pl.BlockSpec((N_EXPERTS, 1), lambda b, i: (0, 0)),
        ],
        out_shape=[
            jax.ShapeDtypeStruct((B * L, D_MODEL), F32),
            jax.ShapeDtypeStruct((B * L, PACKED), jnp.uint32),
            jax.ShapeDtypeStruct((8, B * L), jnp.int32),
            jax.ShapeDtypeStruct((B * L, LANE), F32),
            jax.ShapeDtypeStruct((N_EXPERTS, 1), F32),
        ],
        compiler_params=_cparams(2),
        name="merge_router",
    )(x, yh_t, ym, g, *w)


def _expert_kernel(be_ref, na_ref, end_ref, x_ref, wgu_hbm, bgu_ref, wd_hbm, bd_ref, y_ref,
                   wgu_buf, wd_buf, sems, slot_ref):
    i = pl.program_id(0)
    active = i < na_ref[0]
    expert = be_ref[i]
    first = jnp.logical_and(active, jnp.logical_or(i == 0, expert != be_ref[jnp.maximum(i - 1, 0)]))

    def weight_copies(e, s):
        return (pltpu.make_async_copy(wgu_hbm.at[e], wgu_buf.at[s], sems.at[0, s]),
                pltpu.make_async_copy(wd_hbm.at[e], wd_buf.at[s], sems.at[1, s]))

    @pl.when(i == 0)
    def _():
        slot_ref[0] = 1
        for cp in weight_copies(expert, 0):
            cp.start()

    @pl.when(first)
    def _():
        s = 1 - slot_ref[0]
        slot_ref[0] = s
        for cp in weight_copies(expert, s):
            cp.wait()
        run_end = end_ref[expert]

        @pl.when(run_end < na_ref[0])
        def _():
            for cp in weight_copies(be_ref[jnp.minimum(run_end, pl.num_programs(0) - 1)], 1 - s):
                cp.start()

    slot = slot_ref[0]

    @pl.when(active)
    def _():
        x = _unpack_rows(x_ref[...])
        gu = jnp.dot(x, wgu_buf[slot], preferred_element_type=F32) + bgu_ref[0]
        gate = jnp.minimum(gu[:, :D_FF], SWIGLU_LIMIT)
        up = jnp.clip(gu[:, D_FF:], -SWIGLU_LIMIT, SWIGLU_LIMIT)
        act = (up + 1.0) * (gate * jax.nn.sigmoid(SWIGLU_ALPHA * gate))
        y_ref[...] = _pack_rows(jnp.dot(act, wd_buf[slot], preferred_element_type=F32) + bd_ref[0])

    @pl.when(i >= na_ref[0])
    def _():
        y_ref[...] = jnp.zeros_like(y_ref)


def _experts(plan, xs, P):
    n_rows = xs.shape[0]
    nb = n_rows // MOE_BLOCK
    grid_spec = pltpu.PrefetchScalarGridSpec(
        num_scalar_prefetch=3,
        grid=(nb,),
        in_specs=[
            pl.BlockSpec((MOE_BLOCK, PACKED), lambda i, be, *_: (i, 0)),
            pl.BlockSpec(memory_space=pl.ANY),
            pl.BlockSpec((1, 1, 2 * D_FF), lambda i, be, *_: (be[i], 0, 0)),
            pl.BlockSpec(memory_space=pl.ANY),
            pl.BlockSpec((1, 1, D_MODEL), lambda i, be, *_: (be[i], 0, 0)),
        ],
        out_specs=pl.BlockSpec((MOE_BLOCK, PACKED), lambda i, be, *_: (i, 0)),
        scratch_shapes=[
            pltpu.VMEM((2, D_MODEL, 2 * D_FF), F32),
            pltpu.VMEM((2, D_FF, D_MODEL), F32),
            pltpu.SemaphoreType.DMA((2, 2)),
            pltpu.SMEM((1,), jnp.int32),
        ],
    )
    return pl.pallas_call(
        _expert_kernel,
        grid_spec=grid_spec,
        out_shape=jax.ShapeDtypeStruct((n_rows, PACKED), jnp.uint32),
        compiler_params=_cparams(1),
        name="moe_experts",
    )(*plan, xs, P["w_gu"], P["b_gu"], P["w_down"], P["b_down"])


def _combine_kernel(x1_ref, yk_ref, gt_ref, o_ref):
    gt = gt_ref[...]
    acc = _unpack_rows(yk_ref[0]) * gt[:, 0:1]
    for k in range(1, TOP_K):
        acc = acc + _unpack_rows(yk_ref[k]) * gt[:, k:k + 1]
    o_ref[...] = x1_ref[...] + acc


def _combine(x1, yk, gates, tm):
    T = x1.shape[0]
    return pl.pallas_call(
        _combine_kernel,
        grid=(T // tm,),
        in_specs=[
            pl.BlockSpec((tm, D_MODEL), lambda i: (i, 0)),
            pl.BlockSpec((TOP_K, tm, PACKED), lambda i: (0, i, 0)),
            pl.BlockSpec((tm, LANE), lambda i: (i, 0)),
        ],
        out_specs=pl.BlockSpec((tm, D_MODEL), lambda i: (i, 0)),
        out_shape=jax.ShapeDtypeStruct((T, D_MODEL), F32),
        compiler_params=_cparams(1),
        name="moe_combine",
    )(x1, yk, gates)


def _route_kernel(ids_ref, start_ref, dest_ref, run_ref, *, tm):
    @pl.when(pl.program_id(0) == 0)
    def _():
        run_ref[...] = jnp.zeros_like(run_ref)

    eid = lax.broadcasted_iota(jnp.int32, (N_EXPERTS, tm), 0)
    ids = ids_ref[...]
    hots = [eid == ids[k:k + 1] for k in range(TOP_K)]
    chosen = jnp.zeros((N_EXPERTS, tm), F32)
    for hot in hots:
        chosen = jnp.where(hot, 1.0, chosen)
    r = lax.broadcasted_iota(jnp.int32, (tm, tm), 0)
    c = lax.broadcasted_iota(jnp.int32, (tm, tm), 1)
    earlier = jnp.where(r < c, 1.0, 0.0).astype(BF16)
    before = jnp.dot(chosen.astype(BF16), earlier, preferred_element_type=F32) + (run_ref[...] + start_ref[...])
    rows = [jnp.sum(jnp.where(hot, before, 0.0), axis=0, keepdims=True) for hot in hots]
    dest_ref[...] = jnp.concatenate(rows + [jnp.zeros((8 - TOP_K, tm), F32)], axis=0).astype(jnp.int32)
    run_ref[...] += jnp.sum(chosen, axis=1, keepdims=True)


def _route(ids_t, counts, tm):
    T = ids_t.shape[1]
    counts = counts[:, 0].astype(jnp.int32)
    padded = (counts + MOE_BLOCK - 1) // MOE_BLOCK * MOE_BLOCK
    pad_end = jnp.cumsum(padded)
    start = (pad_end - padded).astype(F32).reshape(N_EXPERTS, 1)
    dest = pl.pallas_call(
        functools.partial(_route_kernel, tm=tm),
        grid=(T // tm,),
        in_specs=[pl.BlockSpec((8, tm), lambda i: (0, i)), _full((N_EXPERTS, 1))],
        out_specs=pl.BlockSpec((8, tm), lambda i: (0, i)),
        out_shape=jax.ShapeDtypeStruct((8, T), jnp.int32),
        scratch_shapes=[pltpu.VMEM((N_EXPERTS, 1), F32)],
        compiler_params=_cparams(1),
        name="moe_route",
    )(ids_t, start)
    n_blocks = -(-(T * TOP_K + N_EXPERTS * (MOE_BLOCK - 1)) // MOE_BLOCK)
    block_start = jnp.arange(n_blocks, dtype=jnp.int32) * MOE_BLOCK
    block_e = jnp.sum((block_start[:, None] >= pad_end[None, :]).astype(jnp.int32), axis=1)
    block_e = jnp.minimum(block_e, N_EXPERTS - 1)
    end_blk = (pad_end // MOE_BLOCK).astype(jnp.int32)
    plan = (block_e.astype(jnp.int32), end_blk[-1:], end_blk)
    dest_flat = dest[:TOP_K].reshape(TOP_K * T)
    return dest_flat, plan, n_blocks * MOE_BLOCK


def _sc_workers():
    info = plsc.get_sparse_core_info()
    return info.num_cores, info.num_cores * info.num_subcores


def _sc_scatter_rows(rows, dest_flat, n_rows):
    T, W = rows.shape
    n_cores, n_workers = _sc_workers()
    per_w = T // n_workers
    assert per_w * n_workers == T and per_w % SC_CHUNK == 0

    def body(rows_hbm, dest_hbm, out_hbm, buf, *idx):
        wid = lax.axis_index("s") * n_cores + lax.axis_index("c")

        @pl.loop(0, per_w // SC_CHUNK)
        def _(j):
            off = pl.multiple_of(wid * per_w + j * SC_CHUNK, SC_CHUNK)
            pltpu.sync_copy(rows_hbm.at[pl.ds(off, SC_CHUNK)], buf)
            for k in range(TOP_K):
                pltpu.sync_copy(dest_hbm.at[pl.ds(k * T + off, SC_CHUNK)], idx[k])
            for k in range(TOP_K):
                pltpu.sync_copy(buf, out_hbm.at[idx[k]])

    return pl.kernel(
        body,
        out_type=jax.ShapeDtypeStruct((n_rows, W), rows.dtype),
        mesh=plsc.VectorSubcoreMesh(core_axis_name="c", subcore_axis_name="s"),
        scratch_types=[pltpu.VMEM((SC_CHUNK, W), rows.dtype)] + [pltpu.VMEM((SC_CHUNK,), jnp.int32)] * TOP_K,
        name="moe_dispatch_sc",
    )(rows, dest_flat)


def _sc_gather_rows(table, dest_flat):
    N = dest_flat.shape[0]
    W = table.shape[1]
    n_cores, n_workers = _sc_workers()
    per_w = N // n_workers
    assert per_w * n_workers == N and per_w % SC_CHUNK == 0

    def body(table_hbm, dest_hbm, out_hbm, buf, idx):
        wid = lax.axis_index("s") * n_cores + lax.axis_index("c")

        @pl.loop(0, per_w // SC_CHUNK)
        def _(j):
            off = pl.multiple_of(wid * per_w + j * SC_CHUNK, SC_CHUNK)
            pltpu.sync_copy(dest_hbm.at[pl.ds(off, SC_CHUNK)], idx)
            pltpu.sync_copy(table_hbm.at[idx], buf)
            pltpu.sync_copy(buf, out_hbm.at[pl.ds(off, SC_CHUNK)])

    return pl.kernel(
        body,
        out_type=jax.ShapeDtypeStruct((N, W), table.dtype),
        mesh=plsc.VectorSubcoreMesh(core_axis_name="c", subcore_axis_name="s"),
        scratch_types=[pltpu.VMEM((SC_CHUNK, W), table.dtype), pltpu.VMEM((SC_CHUNK,), jnp.int32)],
        name="moe_gather_sc",
    )(table, dest_flat)


def _moe(groups, P, tm):
    staged = []
    for x1, h2p, ids, gates, counts in groups:
        dest_flat, plan, n_rows = _route(ids, counts, 4 * tm if x1.shape[0] % (4 * tm) == 0 else tm)
        staged.append((dest_flat, plan, _sc_scatter_rows(h2p, dest_flat, n_rows)))
    ys = []
    for dest_flat, plan, xs in staged:
        if ys:
            xs, ys[-1] = lax.optimization_barrier((xs, ys[-1]))
        ys.append(_experts(plan, xs, P))
    outs = []
    for (x1, _, _, gates, _), (dest_flat, _, _), y in zip(groups, staged, ys):
        yk = _sc_gather_rows(y, dest_flat).reshape(TOP_K, x1.shape[0], PACKED)
        outs.append(_combine(x1, yk, gates, 2 * tm if x1.shape[0] % (2 * tm) == 0 else tm))
    return outs


def _prep_params(norm_mix, w_in, q_a_norm, w_uq, kv_a_norm, w_ukv, q_norm, k_norm, w_hy_o, w_mla_o, w_out,
                 norm_ffn, router_w, router_b, w_gu, b_gu, w_down, b_down, L):
    P = {}
    P["norm_mix"] = norm_mix.astype(F32).reshape(1, D_MODEL)
    P["w_hy_t"] = w_in[:, :HY_END].T.astype(BF16)
    P["w_m"] = jnp.pad(w_in[:, HY_END:KR_END], ((0, 0), (0, MLA_IN - (KR_END - HY_END)))).astype(BF16)
    P["w_g"] = w_in[:, KR_END:].astype(BF16)

    half = QK_ROPE // 2
    swap = jnp.concatenate([jnp.arange(QK_NOPE), QK_NOPE + half + jnp.arange(half), QK_NOPE + jnp.arange(half)])
    rope_lane = (jnp.arange(QK_HEAD) >= QK_NOPE)

    def slots(w):
        return jnp.pad(w, ((0, 0), (0, 0), (0, HEAD_PAD - QK_HEAD))).reshape(w.shape[0], MLA_HEADS * HEAD_PAD)

    wq = w_uq.astype(F32).reshape(Q_LORA, MLA_HEADS, QK_HEAD)
    P["wqa"] = slots(wq).astype(BF16)
    P["wqb"] = slots(jnp.where(rope_lane, wq[:, :, swap], 0.0)).astype(BF16)
    wkv = w_ukv.astype(F32).reshape(KV_LORA, MLA_HEADS, QK_NOPE + V_HEAD)
    P["wka"] = slots(jnp.pad(wkv[:, :, :QK_NOPE], ((0, 0), (0, 0), (0, QK_ROPE)))).astype(BF16)
    P["wv"] = wkv[:, :, QK_NOPE:].reshape(KV_LORA, MLA_HEADS * V_HEAD).astype(BF16)
    eye = jnp.eye(LANE, QK_ROPE, dtype=F32)
    e_a = jnp.pad(eye, ((0, 0), (QK_NOPE, 0)))
    e_b = e_a[:, swap] * rope_lane
    P["ea"] = slots(jnp.broadcast_to(e_a[:, None, :], (LANE, MLA_HEADS, QK_HEAD))).astype(BF16)
    P["eb"] = slots(jnp.broadcast_to(e_b[:, None, :], (LANE, MLA_HEADS, QK_HEAD))).astype(BF16)

    sign = jnp.concatenate([jnp.zeros(QK_NOPE), -jnp.ones(half), jnp.ones(half)]).astype(F32)

    def lane_pad(g):
        return jnp.pad(g, (0, HEAD_PAD - QK_HEAD)).reshape(1, HEAD_PAD)

    for name, g in (("gq", q_norm.astype(F32)), ("gk", k_norm.astype(F32))):
        P[name] = lane_pad(g)
        P[name + "s"] = lane_pad(g[swap] * sign)
    P["gqa"] = q_a_norm.astype(F32).reshape(1, Q_LORA)
    P["gkva"] = kv_a_norm.astype(F32).reshape(1, KV_LORA)
    inv = 1.0 / (ROPE_THETA ** (jnp.arange(0, QK_ROPE, 2, dtype=F32) / QK_ROPE))
    ang = jnp.arange(L, dtype=F32)[:, None] * inv[None, :]
    cos, sin = jnp.cos(ang), jnp.sin(ang)
    P["ctab"] = jnp.concatenate([jnp.ones((L, QK_NOPE), F32), cos, cos, jnp.zeros((L, HEAD_PAD - QK_HEAD), F32)], axis=1)
    P["stab"] = jnp.concatenate([jnp.zeros((L, QK_NOPE), F32), sin, sin, jnp.zeros((L, HEAD_PAD - QK_HEAD), F32)], axis=1)

    P["w_hy_o"] = w_hy_o.astype(BF16)
    P["w_mla_o"] = w_mla_o.astype(BF16)
    P["w_out"] = w_out.astype(BF16)
    P["norm_ffn"] = norm_ffn.astype(F32).reshape(1, D_MODEL)
    rw = jnp.pad(router_w.astype(F32), ((0, 0), (0, LANE - N_EXPERTS)))
    P["router_w_hi"] = rw.astype(BF16)
    P["router_w_lo"] = (rw - P["router_w_hi"].astype(F32)).astype(BF16)
    P["router_b"] = jnp.pad(router_b.astype(F32), (0, LANE - N_EXPERTS)).reshape(1, LANE)
    P["w_gu"] = w_gu.astype(F32)
    P["b_gu"] = b_gu.astype(F32).reshape(N_EXPERTS, 1, 2 * D_FF)
    P["w_down"] = w_down.astype(F32)
    P["b_down"] = b_down.astype(F32).reshape(N_EXPERTS, 1, D_MODEL)
    return P


def _tile(n, want):
    t = min(n, want)
    assert n % t == 0
    return t


def _layer(xs, norm_mix, w_in, hy_conv_w, hy_conv_b, hy_fw1, hy_fb1, hy_fw2, hy_fb2, hy_fw3, hy_freq, hy_decay, hy_skip,
           q_a_norm, w_uq, kv_a_norm, w_ukv, q_norm, k_norm, w_hy_o, w_mla_o, w_out,
           norm_ffn, router_w, router_b, w_gu, b_gu, w_down, b_down):
    L = xs[0].shape[1]
    assert all(x.shape[1] == L and x.shape[2] == D_MODEL for x in xs) and L % MXU == 0
    P = _prep_params(norm_mix, w_in, q_a_norm, w_uq, kv_a_norm, w_ukv, q_norm, k_norm, w_hy_o, w_mla_o, w_out,
                     norm_ffn, router_w, router_b, w_gu, b_gu, w_down, b_down, L)
    kf = _hyena_filters(L, hy_fw1, hy_fb1, hy_fw2, hy_fb2, hy_fw3, hy_freq, hy_decay)
    tm = _tile(L, 512)
    proj = [_inproj(x, P, tm) for x in xs]
    y_ts = _hyena([p[0] for p in proj], kf, hy_conv_w, hy_conv_b, hy_skip, L)
    mixed = []
    for x, (_, g, q, k, v), y_t in zip(xs, proj, y_ts):
        y_mla = _attention(q, k, v, _tile(L, 512))
        mixed.append(_merge(x, y_t, y_mla, g, P, _tile(L, 1024)))
    order = sorted(range(len(xs)), key=lambda n: -xs[n].shape[0])
    moe_out = _moe([mixed[n] for n in order], P, tm)
    outs = [None] * len(xs)
    for n, y in zip(order, moe_out):
        outs[n] = y.reshape(xs[n].shape)
    return outs


def kernel(x_prompt, x_sample, norm_mix, w_in, hy_conv_w, hy_conv_b, hy_fw1, hy_fb1, hy_fw2, hy_fb2, hy_fw3, hy_freq, hy_decay, hy_skip, q_a_norm, w_uq, kv_a_norm, w_ukv, q_norm, k_norm, w_hy_o, w_mla_o, w_out, norm_ffn, router_w, router_b, w_gu, b_gu, w_down, b_down):
    depth = norm_mix.shape[0]
    xs = [x_prompt, x_sample]
    weights = (norm_mix, w_in, hy_conv_w, hy_conv_b, hy_fw1, hy_fb1, hy_fw2, hy_fb2, hy_fw3, hy_freq, hy_decay, hy_skip,
               q_a_norm, w_uq, kv_a_norm, w_ukv, q_norm, k_norm, w_hy_o, w_mla_o, w_out,
               norm_ffn, router_w, router_b, w_gu, b_gu, w_down, b_down)
    for l in range(depth):
        xs = _layer(xs, *[w[l] for w in weights])
    return (xs[0], xs[1])
```

```python
import functools
import math

import jax
import jax.numpy as jnp
from jax import lax
from jax.experimental import pallas as pl
from jax.experimental.pallas import tpu as pltpu
from jax.experimental.pallas import tpu_sc as plsc

F32 = jnp.float32
BF16 = jnp.bfloat16

D_MODEL = 1024
HY_WIDTH = 512
HY_ORDER = 2
HY_EMB_BANDS = 16
HY_FILTER_DIM = 64
MLA_HEADS = 8
V_HEAD = 64
QK_NOPE = 64
QK_ROPE = 32
QK_HEAD = QK_NOPE + QK_ROPE
Q_LORA = 256
KV_LORA = 128
ROPE_THETA = 10000.0
HY_END = 3 * HY_WIDTH
Q_END = HY_END + Q_LORA
KV_END = Q_END + KV_LORA
KR_END = KV_END + QK_ROPE
G1_END = KR_END + D_MODEL
N_EXPERTS = 32
TOP_K = 4
D_FF = D_MODEL
SWIGLU_LIMIT = 7.0
SWIGLU_ALPHA = 1.702
EPS = 1e-6

LANE = 128
MXU = 256
BF16_ROWS = 16
HEAD_PAD = LANE
MLA_IN = 512
CONV_CB = 8
CONV_UNROLL = 4
MOE_BLOCK = 512
PACKED = D_MODEL // 2
SC_CHUNK = 64
VMEM_LIMIT = 52 * 1024 * 1024


def _cparams(n_axes):
    return pltpu.CompilerParams(dimension_semantics=("arbitrary",) * n_axes, vmem_limit_bytes=VMEM_LIMIT)


def _full(shape):
    n = len(shape)
    return pl.BlockSpec(shape, lambda *_: (0,) * n)


def _rms(x, n):
    return lax.rsqrt(jnp.sum(x * x, axis=-1, keepdims=True) * (1.0 / n) + EPS)


def _mla_heads(m, ct, st, gqa_ref, gkva_ref, gq_ref, gqs_ref, gk_ref, gks_ref,
               wqa_ref, wqb_ref, wka_ref, ea_ref, eb_ref, wv_ref, q_ref, k_ref, v_ref):
    mf = m.astype(F32)
    cq = mf[:, :Q_LORA]
    ckv = mf[:, Q_LORA:Q_LORA + KV_LORA]
    kr = m[:, Q_LORA + KV_LORA:Q_LORA + KV_LORA + LANE]
    cqn = (cq * _rms(cq, Q_LORA) * gqa_ref[...]).astype(BF16)
    ckvn = (ckv * _rms(ckv, KV_LORA) * gkva_ref[...]).astype(BF16)
    qa = jnp.dot(cqn, wqa_ref[...], preferred_element_type=F32)
    qb = jnp.dot(cqn, wqb_ref[...], preferred_element_type=F32)
    ka = jnp.dot(ckvn, wka_ref[...], preferred_element_type=F32) + jnp.dot(kr, ea_ref[...], preferred_element_type=F32)
    kb = jnp.dot(kr, eb_ref[...], preferred_element_type=F32)
    v_ref[...] = jnp.dot(ckvn, wv_ref[...], preferred_element_type=F32).astype(BF16)
    scale = QK_HEAD ** -0.5 * math.log2(math.e)
    qc, qs = gq_ref[...] * ct * scale, gqs_ref[...] * st * scale
    kc, ks = gk_ref[...] * ct, gks_ref[...] * st
    for h in range(MLA_HEADS):
        sl = slice(h * HEAD_PAD, (h + 1) * HEAD_PAD)
        q_ref[h] = (_rms(qa[:, sl], QK_HEAD) * (qa[:, sl] * qc + qb[:, sl] * qs)).astype(BF16)
        k_ref[h] = (_rms(ka[:, sl], QK_HEAD) * (ka[:, sl] * kc + kb[:, sl] * ks)).astype(BF16)


def _inproj_kernel(x_ref, g_ref, why_ref, wm_ref, wg_ref, ct_ref, st_ref, *refs):
    mla_refs, (u_ref, gate_ref, q_ref, k_ref, v_ref) = refs[:12], refs[12:]
    tm = x_ref.shape[0]
    sub = tm // 2
    for r0 in range(0, tm, sub):
        rows = pl.ds(r0, sub)
        x = x_ref[rows, :]
        r = lax.rsqrt(jnp.mean(x * x, axis=-1, keepdims=True) + EPS)
        h = (x * r * g_ref[...]).astype(BF16)
        m = jnp.dot(h, wm_ref[...], preferred_element_type=F32).astype(BF16)
        _mla_heads(m, ct_ref[rows, :], st_ref[rows, :], *mla_refs,
                   q_ref.at[:, rows, :], k_ref.at[:, rows, :], v_ref.at[rows, :])
        u_ref[:, rows] = lax.dot_general(why_ref[...], h, (((1,), (1,)), ((), ())), preferred_element_type=F32)
        gate_ref[rows, :] = jnp.dot(h, wg_ref[...], preferred_element_type=F32).astype(BF16)


def _inproj(x, P, tm):
    B, L, _ = x.shape
    w = [P[n] for n in ("w_hy_t", "w_m", "w_g")]
    mla = [P[n] for n in ("gqa", "gkva", "gq", "gqs", "gk", "gks", "wqa", "wqb", "wka", "ea", "eb", "wv")]
    return pl.pallas_call(
        _inproj_kernel,
        grid=(B, L // tm),
        in_specs=[pl.BlockSpec((None, tm, D_MODEL), lambda b, i: (b, i, 0)), _full((1, D_MODEL))]
        + [_full(a.shape) for a in w]
        + [pl.BlockSpec((tm, HEAD_PAD), lambda b, i: (i, 0)), pl.BlockSpec((tm, HEAD_PAD), lambda b, i: (i, 0))]
        + [_full(a.shape) for a in mla],
        out_specs=[
            pl.BlockSpec((None, HY_END, tm), lambda b, i: (b, 0, i)),
            pl.BlockSpec((None, tm, 2 * D_MODEL), lambda b, i: (b, i, 0)),
            pl.BlockSpec((None, MLA_HEADS, tm, HEAD_PAD), lambda b, i: (b, 0, i, 0)),
            pl.BlockSpec((None, MLA_HEADS, tm, HEAD_PAD), lambda b, i: (b, 0, i, 0)),
            pl.BlockSpec((None, tm, MLA_HEADS * V_HEAD), lambda b, i: (b, i, 0)),
        ],
        out_shape=[
            jax.ShapeDtypeStruct((B, HY_END, L), F32),
            jax.ShapeDtypeStruct((B, L, 2 * D_MODEL), BF16),
            jax.ShapeDtypeStruct((B, MLA_HEADS, L, HEAD_PAD), BF16),
            jax.ShapeDtypeStruct((B, MLA_HEADS, L, HEAD_PAD), BF16),
            jax.ShapeDtypeStruct((B, L, MLA_HEADS * V_HEAD), BF16),
        ],
        compiler_params=_cparams(2),
        name="inproj",
    )(x, P["norm_mix"], *w, P["ctab"][:L], P["stab"][:L], *mla)


def _filter_kernel(bands_ref, w1t_ref, w1c_ref, w1s_ref, b1_ref, w2_ref, b2_ref, fr_ref, w3f_ref, w3b_ref, dec_ref,
                   out_ref, *, L):
    hi = lax.Precision.HIGHEST
    n = 2 * L
    j = lax.broadcasted_iota(jnp.int32, (1, n), 1)
    pos = jnp.abs(j - L).astype(F32)
    t01 = pos / max(L - 1, 1)
    ang = ((2.0 * math.pi / L) * pos) * bands_ref[...]
    pre = (w1t_ref[...] * t01
           + jnp.dot(w1c_ref[...], jnp.cos(ang), precision=hi, preferred_element_type=F32)
           - jnp.dot(w1s_ref[...], jnp.sin(ang), precision=hi, preferred_element_type=F32)
           + b1_ref[...])
    z = jnp.sin(fr_ref[0] * pre)
    z = jnp.sin(fr_ref[1] * (jnp.dot(w2_ref[...], z, precision=hi, preferred_element_type=F32) + b2_ref[...]))
    win = jnp.exp(-t01 * jnp.abs(dec_ref[...]))
    h_b = jnp.dot(w3b_ref[...], z[:, :L], precision=hi, preferred_element_type=F32) * win[:, :L]
    h_f = jnp.dot(w3f_ref[...], z[:, L:], precision=hi, preferred_element_type=F32) * win[:, L:]
    out_ref[:, :L] = jnp.where(j[:, :L] == 0, 0.0, h_b)
    out_ref[:, L:] = h_f


def _hyena_filters(L, fw1, fb1, fw2, fb2, fw3, freq, decay):
    rows = HY_ORDER * HY_WIDTH
    rb = 256
    bands = jnp.linspace(1e-4, HY_EMB_BANDS - 1, HY_EMB_BANDS, dtype=F32).reshape(HY_EMB_BANDS, 1)
    w1 = fw1.astype(F32).T
    w3 = fw3.astype(F32).T
    fd = HY_FILTER_DIM
    return pl.pallas_call(
        functools.partial(_filter_kernel, L=L),
        grid=(rows // rb,),
        in_specs=[
            _full((HY_EMB_BANDS, 1)),
            _full((fd, 1)),
            _full((fd, HY_EMB_BANDS)),
            _full((fd, HY_EMB_BANDS)),
            _full((fd, 1)),
            _full((fd, fd)),
            _full((fd, 1)),
            _full((2, fd, 1)),
            pl.BlockSpec((rb, fd), lambda i: (i, 0)),
            pl.BlockSpec((rb, fd), lambda i: (i, 0)),
            pl.BlockSpec((rb, 1), lambda i: (i, 0)),
        ],
        out_specs=pl.BlockSpec((rb, 2 * L), lambda i: (i, 0)),
        out_shape=jax.ShapeDtypeStruct((rows, 2 * L), F32),
        compiler_params=_cparams(1),
        name="hyena_filters",
    )(bands, w1[:, :1], w1[:, 1:1 + HY_EMB_BANDS], w1[:, 1 + HY_EMB_BANDS:], fb1.astype(F32).reshape(fd, 1),
      fw2.astype(F32).T, fb2.astype(F32).reshape(fd, 1), freq.astype(F32).reshape(2, fd, 1),
      w3[:rows], w3[rows:], decay.astype(F32).reshape(rows, 1))


def _shift_conv(u, w0, w1, w2, b):
    L = u.shape[1]
    lane = lax.broadcasted_iota(jnp.int32, (1, L), 1)
    w0_row = jnp.where(lane == 0, 0.0, w0)
    w2_row = jnp.where(lane == L - 1, 0.0, w2)
    return pltpu.roll(u, 1, 1) * w0_row + u * w1 + pltpu.roll(u, L - 1, 1) * w2_row + b


def _toeplitz_conv(zs, tz_ref, L):
    nj = L // MXU
    zbs = [z.astype(BF16) for z in zs]
    accs = [[None] * nj for _ in zs]
    for d in range(-(nj - 1), nj):
        js = [J for J in range(nj) if 0 <= J + d < nj]
        pieces = [zb[:, MXU * J:MXU * (J + 1)] for zb in zbs for J in js]
        lhs = pieces[0] if len(pieces) == 1 else jnp.concatenate(pieces, axis=0)
        res = jnp.dot(lhs, tz_ref[:, L + MXU * d:L + MXU * (d + 1)], preferred_element_type=F32)
        row = 0
        for acc, z in zip(accs, zs):
            B = z.shape[0]
            for J in js:
                part = res[row:row + B]
                acc[J + d] = part if acc[J + d] is None else acc[J + d] + part
                row += B
    return [acc[0] if nj == 1 else jnp.concatenate(acc, axis=1) for acc in accs]


def _hyena_kernel(cw_ref, cb_ref, skip_ref, kf_ref, *refs, L, n_groups):
    in_refs = refs[:3 * n_groups]
    out_refs = refs[3 * n_groups:4 * n_groups]
    tz_ref = refs[4 * n_groups]
    c0 = pl.program_id(0) * CONV_CB

    def channels(p, carry):
        for u in range(CONV_UNROLL):
            c = p * CONV_UNROLL + u
            ch = c0 + c
            for o in range(HY_ORDER):
                row = jnp.broadcast_to(kf_ref[o, pl.ds(c, 1), :], (BF16_ROWS, 2 * L))
                slab = pltpu.bitcast(pltpu.roll(row, 0, 1, stride=1, stride_axis=0).astype(BF16), jnp.uint32)
                for a in range(MXU // BF16_ROWS):
                    shifted = slab if a == 0 else pltpu.roll(slab, BF16_ROWS * a, 1)
                    tz_ref[u, o, pl.ds(BF16_ROWS * a, BF16_ROWS), :] = pltpu.bitcast(shifted, BF16)
            secs = []
            for g in range(n_groups):
                stage = refs[4 * n_groups + 1 + g]
                sec = []
                for s, ref in enumerate(in_refs[3 * g:3 * g + 3]):
                    stage[u, s] = ref[:, c, :]
                    k = s * HY_WIDTH + ch
                    sec.append(_shift_conv(stage[u, s], cw_ref[0, k], cw_ref[1, k], cw_ref[2, k], cb_ref[k]))
                secs.append(sec)
            zs = [sec[2] for sec in secs]
            for o in range(HY_ORDER):
                ys = _toeplitz_conv(zs, tz_ref.at[u, o], L)
                zs = [sec[o] * (y + z * skip_ref[o, ch]) for sec, y, z in zip(secs, ys, zs)]
            for g in range(n_groups):
                out_refs[g][:, c, :] = zs[g]
        return carry

    lax.fori_loop(0, CONV_CB // CONV_UNROLL, channels, 0)


def _hyena(u_ts, kf, conv_w, conv_b, skip, L):
    n_groups = len(u_ts)
    nblk = HY_WIDTH // CONV_CB
    smem = pl.BlockSpec(memory_space=pltpu.SMEM)
    in_specs = [smem, smem, smem, pl.BlockSpec((HY_ORDER, CONV_CB, 2 * L), lambda i: (0, i, 0))]
    args = [conv_w.astype(F32), conv_b.astype(F32), skip.astype(F32), kf.reshape(HY_ORDER, HY_WIDTH, 2 * L)]
    out_specs, out_shape = [], []
    for u in u_ts:
        B = u.shape[0]
        for s in range(3):
            in_specs.append(pl.BlockSpec((B, CONV_CB, L), lambda i, s=s: (0, i + s * nblk, 0)))
            args.append(u)
        out_specs.append(pl.BlockSpec((B, CONV_CB, L), lambda i: (0, i, 0)))
        out_shape.append(jax.ShapeDtypeStruct((B, HY_WIDTH, L), F32))
    return pl.pallas_call(
        functools.partial(_hyena_kernel, L=L, n_groups=n_groups),
        grid=(nblk,),
        in_specs=in_specs,
        out_specs=out_specs,
        out_shape=out_shape,
        scratch_shapes=[pltpu.VMEM((CONV_UNROLL, HY_ORDER, MXU, 2 * L), BF16)]
        + [pltpu.VMEM((CONV_UNROLL, 3, u.shape[0], L), F32) for u in u_ts],
        compiler_params=_cparams(1),
        name="hyena_conv",
    )(*args)


def _attn_kernel(q_ref, k_ref, v_ref, o_ref):
    L = v_ref.shape[1]
    one_col = jnp.where(lax.broadcasted_iota(jnp.int32, (L, LANE), 1) == 0, 1.0, 0.0).astype(BF16)
    for pair in range(MLA_HEADS // 2):
        lanes = pl.ds(pair * LANE, LANE)
        v_ext = jnp.concatenate([v_ref[0, :, lanes], one_col], axis=1)
        outs = []
        for h in (2 * pair, 2 * pair + 1):
            s = lax.dot_general(q_ref[0, h], k_ref[0, h], (((1,), (1,)), ((), ())), preferred_element_type=F32)
            p = jnp.exp2(s - jnp.max(s, axis=-1, keepdims=True)).astype(BF16)
            pv = jnp.dot(p, v_ext, preferred_element_type=F32)
            outs.append(pv[:, :LANE] / pv[:, LANE:LANE + 1])
        lane = lax.broadcasted_iota(jnp.int32, outs[0].shape, 1)
        o_ref[0, :, lanes] = jnp.where(lane < V_HEAD, outs[0], outs[1]).astype(BF16)


def _attention(q, k, v, tq):
    B, H, L, _ = q.shape
    return pl.pallas_call(
        _attn_kernel,
        grid=(B, L // tq),
        in_specs=[
            pl.BlockSpec((1, H, tq, HEAD_PAD), lambda b, i: (b, 0, i, 0)),
            pl.BlockSpec((1, H, L, HEAD_PAD), lambda b, i: (b, 0, 0, 0)),
            pl.BlockSpec((1, L, H * V_HEAD), lambda b, i: (b, 0, 0)),
        ],
        out_specs=pl.BlockSpec((1, tq, H * V_HEAD), lambda b, i: (b, i, 0)),
        out_shape=jax.ShapeDtypeStruct((B, L, H * V_HEAD), BF16),
        compiler_params=_cparams(2),
        name="mla_attention",
    )(q, k, v)


def _pack_rows(a):
    bits = lax.bitcast_convert_type(a.astype(BF16).astype(F32), jnp.uint32)
    return (bits[:, :PACKED] >> 16) | (bits[:, PACKED:] & jnp.uint32(0xFFFF0000))


def _unpack_rows(p):
    lo = lax.bitcast_convert_type(p << 16, F32)
    hi = lax.bitcast_convert_type(p & jnp.uint32(0xFFFF0000), F32)
    return jnp.concatenate([lo, hi], axis=1)


def _merge_kernel(x_ref, yh_ref, ym_ref, g_ref, who_ref, wmo_ref, wo_ref, nf_ref, rwh_ref, rwl_ref, rb_ref,
                  x1_ref, h2_ref, ids_ref, gate_ref, cnt_ref):
    tm = x_ref.shape[0]
    sub = tm // 2

    @pl.when(jnp.logical_and(pl.program_id(0) == 0, pl.program_id(1) == 0))
    def _():
        cnt_ref[...] = jnp.zeros_like(cnt_ref)

    eid = lax.broadcasted_iota(jnp.int32, (N_EXPERTS, sub), 0)
    for r in range(0, tm, sub):
        rows = pl.ds(r, sub)
        g = g_ref[rows, :].astype(F32)
        a = lax.dot_general(yh_ref[:, rows].astype(BF16), who_ref[...], (((0,), (0,)), ((), ())),
                            preferred_element_type=F32)
        b = jnp.dot(ym_ref[rows, :], wmo_ref[...], preferred_element_type=F32)
        merged = jax.nn.sigmoid(g[:, :D_MODEL]) * a + jax.nn.sigmoid(g[:, D_MODEL:]) * b
        x1 = x_ref[rows, :] + jnp.dot(merged.astype(BF16), wo_ref[...], preferred_element_type=F32)
        x1_ref[rows, :] = x1
        h2 = x1 * lax.rsqrt(jnp.mean(x1 * x1, axis=-1, keepdims=True) + EPS) * nf_ref[...]
        h2_ref[rows, :] = _pack_rows(h2)
        h_hi = h2.astype(BF16)
        h_lo = (h2 - h_hi.astype(F32)).astype(BF16)
        logits = (jnp.dot(h_hi, rwh_ref[...], preferred_element_type=F32)
                  + jnp.dot(h_lo, rwh_ref[...], preferred_element_type=F32)
                  + jnp.dot(h_hi, rwl_ref[...], preferred_element_type=F32) + rb_ref[...])
        work = logits.T[:N_EXPERTS]
        picked = jnp.zeros((N_EXPERTS, sub), F32)
        vals, firsts = [], []
        for k in range(TOP_K):
            m = jnp.max(work, axis=0, keepdims=True)
            first = jnp.min(jnp.where(work == m, eid, N_EXPERTS), axis=0, keepdims=True)
            hot = eid == first
            vals.append(m)
            firsts.append(first)
            picked = jnp.where(hot, 1.0, picked)
            work = jnp.where(hot, -jnp.inf, work)
        ex = [jnp.exp(v - vals[0]) for v in vals]
        inv = 1.0 / (ex[0] + ex[1] + ex[2] + ex[3])
        ids_ref[:, rows] = jnp.concatenate(firsts + [jnp.zeros((8 - TOP_K, sub), jnp.int32)], axis=0)
        gates_t = jnp.concatenate([e * inv for e in ex] + [jnp.zeros((LANE - TOP_K, sub), F32)], axis=0)
        gate_ref[rows, :] = gates_t.T
        cnt_ref[...] += jnp.sum(picked, axis=1, keepdims=True)


def _merge(x, yh_t, ym, g, P, tm):
    B, L, _ = x.shape
    nt = L // tm
    w = [P[n] for n in ("w_hy_o", "w_mla_o", "w_out", "norm_ffn", "router_w_hi", "router_w_lo", "router_b")]
    return pl.pallas_call(
        _merge_kernel,
        grid=(B, nt),
        in_specs=[
            pl.BlockSpec((None, tm, D_MODEL), lambda b, i: (b, i, 0)),
            pl.BlockSpec((None, HY_WIDTH, tm), lambda b, i: (b, 0, i)),
            pl.BlockSpec((None, tm, MLA_HEADS * V_HEAD), lambda b, i: (b, i, 0)),
            pl.BlockSpec((None, tm, 2 * D_MODEL), lambda b, i: (b, i, 0)),
        ] + [_full(a.shape) for a in w],
        out_specs=[
            pl.BlockSpec((tm, D_MODEL), lambda b, i: (b * nt + i, 0)),
            pl.BlockSpec((tm, PACKED), lambda b, i: (b * nt + i, 0)),
            pl.BlockSpec((8, tm), lambda b, i: (0, b * nt + i)),
            pl.BlockSpec((tm, LANE), lambda b, i: (b * nt + i, 0)),
            pl.BlockSpec((N_EXPERTS, 1), lambda b, i: (0, 0)),
        ],
        out_shape=[
            jax.ShapeDtypeStruct((B * L, D_MODEL), F32),
            jax.ShapeDtypeStruct((B * L, PACKED), jnp.uint32),
            jax.ShapeDtypeStruct((8, B * L), jnp.int32),
            jax.ShapeDtypeStruct((B * L, LANE), F32),
            jax.ShapeDtypeStruct((N_EXPERTS, 1), F32),
        ],
        compiler_params=_cparams(2),
        name="merge_router",
    )(x, yh_t, ym, g, *w)


def _expert_kernel(be_ref, na_ref, end_ref, x_ref, wgu_hbm, bgu_ref, wd_hbm, bd_ref, y_ref,
                   wgu_buf, wd_buf, sems, slot_ref):
    i = pl.program_id(0)
    active = i < na_ref[0]
    expert = be_ref[i]
    first = jnp.logical_and(active, jnp.logical_or(i == 0, expert != be_ref[jnp.maximum(i - 1, 0)]))

    def weight_copies(e, s):
        return (pltpu.make_async_copy(wgu_hbm.at[e], wgu_buf.at[s], sems.at[0, s]),
                pltpu.make_async_copy(wd_hbm.at[e], wd_buf.at[s], sems.at[1, s]))

    @pl.when(i == 0)
    def _():
        slot_ref[0] = 1
        for cp in weight_copies(expert, 0):
            cp.start()

    @pl.when(first)
    def _():
        s = 1 - slot_ref[0]
        slot_ref[0] = s
        for cp in weight_copies(expert, s):
            cp.wait()
        run_end = end_ref[expert]

        @pl.when(run_end < na_ref[0])
        def _():
            for cp in weight_copies(be_ref[jnp.minimum(run_end, pl.num_programs(0) - 1)], 1 - s):
                cp.start()

    slot = slot_ref[0]

    @pl.when(active)
    def _():
        x = _unpack_rows(x_ref[...]).astype(BF16)
        gu = jnp.dot(x, wgu_buf[slot].astype(BF16), preferred_element_type=F32) + bgu_ref[0]
        gate = jnp.minimum(gu[:, :D_FF], SWIGLU_LIMIT)
        up = jnp.clip(gu[:, D_FF:], -SWIGLU_LIMIT, SWIGLU_LIMIT)
        act = (up + 1.0) * (gate * jax.nn.sigmoid(SWIGLU_ALPHA * gate))
        y = jnp.dot(act.astype(BF16), wd_buf[slot].astype(BF16), preferred_element_type=F32) + bd_ref[0]
        y_ref[...] = _pack_rows(y)

    @pl.when(i >= na_ref[0])
    def _():
        y_ref[...] = jnp.zeros_like(y_ref)


def _experts(plan, xs, P):
    n_rows = xs.shape[0]
    nb = n_rows // MOE_BLOCK
    grid_spec = pltpu.PrefetchScalarGridSpec(
        num_scalar_prefetch=3,
        grid=(nb,),
        in_specs=[
            pl.BlockSpec((MOE_BLOCK, PACKED), lambda i, be, *_: (i, 0)),
            pl.BlockSpec(memory_space=pl.ANY),
            pl.BlockSpec((1, 1, 2 * D_FF), lambda i, be, *_: (be[i], 0, 0)),
            pl.BlockSpec(memory_space=pl.ANY),
            pl.BlockSpec((1, 1, D_MODEL), lambda i, be, *_: (be[i], 0, 0)),
        ],
        out_specs=pl.BlockSpec((MOE_BLOCK, PACKED), lambda i, be, *_: (i, 0)),
        scratch_shapes=[
            pltpu.VMEM((2, D_MODEL, 2 * D_FF), F32),
            pltpu.VMEM((2, D_FF, D_MODEL), F32),
            pltpu.SemaphoreType.DMA((2, 2)),
            pltpu.SMEM((1,), jnp.int32),
        ],
    )
    return pl.pallas_call(
        _expert_kernel,
        grid_spec=grid_spec,
        out_shape=jax.ShapeDtypeStruct((n_rows, PACKED), jnp.uint32),
        compiler_params=_cparams(1),
        name="moe_experts",
    )(*plan, xs, P["w_gu"], P["b_gu"], P["w_down"], P["b_down"])


def _combine_kernel(x1_ref, yk_ref, gt_ref, o_ref):
    gt = gt_ref[...]
    acc = _unpack_rows(yk_ref[0]) * gt[:, 0:1]
    for k in range(1, TOP_K):
        acc = acc + _unpack_rows(yk_ref[k]) * gt[:, k:k + 1]
    o_ref[...] = x1_ref[...] + acc


def _combine(x1, yk, gates, tm):
    T = x1.shape[0]
    return pl.pallas_call(
        _combine_kernel,
        grid=(T // tm,),
        in_specs=[
            pl.BlockSpec((tm, D_MODEL), lambda i: (i, 0)),
            pl.BlockSpec((TOP_K, tm, PACKED), lambda i: (0, i, 0)),
            pl.BlockSpec((tm, LANE), lambda i: (i, 0)),
        ],
        out_specs=pl.BlockSpec((tm, D_MODEL), lambda i: (i, 0)),
        out_shape=jax.ShapeDtypeStruct((T, D_MODEL), F32),
        compiler_params=_cparams(1),
        name="moe_combine",
    )(x1, yk, gates)


def _route_kernel(ids_ref, start_ref, dest_ref, run_ref, *, tm):
    @pl.when(pl.program_id(0) == 0)
    def _():
        run_ref[...] = jnp.zeros_like(run_ref)

    eid = lax.broadcasted_iota(jnp.int32, (N_EXPERTS, tm), 0)
    ids = ids_ref[...]
    hots = [eid == ids[k:k + 1] for k in range(TOP_K)]
    chosen = jnp.zeros((N_EXPERTS, tm), F32)
    for hot in hots:
        chosen = jnp.where(hot, 1.0, chosen)
    r = lax.broadcasted_iota(jnp.int32, (tm, tm), 0)
    c = lax.broadcasted_iota(jnp.int32, (tm, tm), 1)
    earlier = jnp.where(r < c, 1.0, 0.0).astype(BF16)
    before = jnp.dot(chosen.astype(BF16), earlier, preferred_element_type=F32) + (run_ref[...] + start_ref[...])
    rows = [jnp.sum(jnp.where(hot, before, 0.0), axis=0, keepdims=True) for hot in hots]
    dest_ref[...] = jnp.concatenate(rows + [jnp.zeros((8 - TOP_K, tm), F32)], axis=0).astype(jnp.int32)
    run_ref[...] += jnp.sum(chosen, axis=1, keepdims=True)


def _route(ids_t, counts, tm):
    T = ids_t.shape[1]
    counts = counts[:, 0].astype(jnp.int32)
    padded = (counts + MOE_BLOCK - 1) // MOE_BLOCK * MOE_BLOCK
    pad_end = jnp.cumsum(padded)
    start = (pad_end - padded).astype(F32).reshape(N_EXPERTS, 1)
    dest = pl.pallas_call(
        functools.partial(_route_kernel, tm=tm),
        grid=(T // tm,),
        in_specs=[pl.BlockSpec((8, tm), lambda i: (0, i)), _full((N_EXPERTS, 1))],
        out_specs=pl.BlockSpec((8, tm), lambda i: (0, i)),
        out_shape=jax.ShapeDtypeStruct((8, T), jnp.int32),
        scratch_shapes=[pltpu.VMEM((N_EXPERTS, 1), F32)],
        compiler_params=_cparams(1),
        name="moe_route",
    )(ids_t, start)
    n_blocks = -(-(T * TOP_K + N_EXPERTS * (MOE_BLOCK - 1)) // MOE_BLOCK)
    block_start = jnp.arange(n_blocks, dtype=jnp.int32) * MOE_BLOCK
    block_e = jnp.sum((block_start[:, None] >= pad_end[None, :]).astype(jnp.int32), axis=1)
    block_e = jnp.minimum(block_e, N_EXPERTS - 1)
    end_blk = (pad_end // MOE_BLOCK).astype(jnp.int32)
    plan = (block_e.astype(jnp.int32), end_blk[-1:], end_blk)
    dest_flat = dest[:TOP_K].reshape(TOP_K * T)
    return dest_flat, plan, n_blocks * MOE_BLOCK


def _sc_workers():
    info = plsc.get_sparse_core_info()
    return info.num_cores, info.num_cores * info.num_subcores


def _sc_scatter_rows(rows, dest_flat, n_rows):
    T, W = rows.shape
    n_cores, n_workers = _sc_workers()
    per_w = T // n_workers
    assert per_w * n_workers == T and per_w % SC_CHUNK == 0

    def body(rows_hbm, dest_hbm, out_hbm, buf, *idx):
        wid = lax.axis_index("s") * n_cores + lax.axis_index("c")

        @pl.loop(0, per_w // SC_CHUNK)
        def _(j):
            off = pl.multiple_of(wid * per_w + j * SC_CHUNK, SC_CHUNK)
            pltpu.sync_copy(rows_hbm.at[pl.ds(off, SC_CHUNK)], buf)
            for k in range(TOP_K):
                pltpu.sync_copy(dest_hbm.at[pl.ds(k * T + off, SC_CHUNK)], idx[k])
            for k in range(TOP_K):
                pltpu.sync_copy(buf, out_hbm.at[idx[k]])

    return pl.kernel(
        body,
        out_type=jax.ShapeDtypeStruct((n_rows, W), rows.dtype),
        mesh=plsc.VectorSubcoreMesh(core_axis_name="c", subcore_axis_name="s"),
        scratch_types=[pltpu.VMEM((SC_CHUNK, W), rows.dtype)] + [pltpu.VMEM((SC_CHUNK,), jnp.int32)] * TOP_K,
        name="moe_dispatch_sc",
    )(rows, dest_flat)


def _sc_gather_rows(table, dest_flat):
    N = dest_flat.shape[0]
    W = table.shape[1]
    n_cores, n_workers = _sc_workers()
    per_w = N // n_workers
    assert per_w * n_workers == N and per_w % SC_CHUNK == 0

    def body(table_hbm, dest_hbm, out_hbm, buf, idx):
        wid = lax.axis_index("s") * n_cores + lax.axis_index("c")

        @pl.loop(0, per_w // SC_CHUNK)
        def _(j):
            off = pl.multiple_of(wid * per_w + j * SC_CHUNK, SC_CHUNK)
            pltpu.sync_copy(dest_hbm.at[pl.ds(off, SC_CHUNK)], idx)
            pltpu.sync_copy(table_hbm.at[idx], buf)
            pltpu.sync_copy(buf, out_hbm.at[pl.ds(off, SC_CHUNK)])

    return pl.kernel(
        body,
        out_type=jax.ShapeDtypeStruct((N, W), table.dtype),
        mesh=plsc.VectorSubcoreMesh(core_axis_name="c", subcore_axis_name="s"),
        scratch_types=[pltpu.VMEM((SC_CHUNK, W), table.dtype), pltpu.VMEM((SC_CHUNK,), jnp.int32)],
        name="moe_gather_sc",
    )(table, dest_flat)


def _moe(groups, P, tm):
    staged = []
    for x1, h2p, ids, gates, counts in groups:
        dest_flat, plan, n_rows = _route(ids, counts, 4 * tm if x1.shape[0] % (4 * tm) == 0 else tm)
        staged.append((dest_flat, plan, _sc_scatter_rows(h2p, dest_flat, n_rows)))
    ys = []
    for dest_flat, plan, xs in staged:
        if ys:
            xs, ys[-1] = lax.optimization_barrier((xs, ys[-1]))
        ys.append(_experts(plan, xs, P))
    outs = []
    for (x1, _, _, gates, _), (dest_flat, _, _), y in zip(groups, staged, ys):
        yk = _sc_gather_rows(y, dest_flat).reshape(TOP_K, x1.shape[0], PACKED)
        outs.append(_combine(x1, yk, gates, 2 * tm if x1.shape[0] % (2 * tm) == 0 else tm))
    return outs


def _prep_params(norm_mix, w_in, q_a_norm, w_uq, kv_a_norm, w_ukv, q_norm, k_norm, w_hy_o, w_mla_o, w_out,
                 norm_ffn, router_w, router_b, w_gu, b_gu, w_down, b_down, L):
    P = {}
    P["norm_mix"] = norm_mix.astype(F32).reshape(1, D_MODEL)
    P["w_hy_t"] = w_in[:, :HY_END].T.astype(BF16)
    P["w_m"] = jnp.pad(w_in[:, HY_END:KR_END], ((0, 0), (0, MLA_IN - (KR_END - HY_END)))).astype(BF16)
    P["w_g"] = w_in[:, KR_END:].astype(BF16)

    half = QK_ROPE // 2
    swap = jnp.concatenate([jnp.arange(QK_NOPE), QK_NOPE + half + jnp.arange(half), QK_NOPE + jnp.arange(half)])
    rope_lane = (jnp.arange(QK_HEAD) >= QK_NOPE)

    def slots(w):
        return jnp.pad(w, ((0, 0), (0, 0), (0, HEAD_PAD - QK_HEAD))).reshape(w.shape[0], MLA_HEADS * HEAD_PAD)

    wq = w_uq.astype(F32).reshape(Q_LORA, MLA_HEADS, QK_HEAD)
    P["wqa"] = slots(wq).astype(BF16)
    P["wqb"] = slots(jnp.where(rope_lane, wq[:, :, swap], 0.0)).astype(BF16)
    wkv = w_ukv.astype(F32).reshape(KV_LORA, MLA_HEADS, QK_NOPE + V_HEAD)
    P["wka"] = slots(jnp.pad(wkv[:, :, :QK_NOPE], ((0, 0), (0, 0), (0, QK_ROPE)))).astype(BF16)
    P["wv"] = wkv[:, :, QK_NOPE:].reshape(KV_LORA, MLA_HEADS * V_HEAD).astype(BF16)
    eye = jnp.eye(LANE, QK_ROPE, dtype=F32)
    e_a = jnp.pad(eye, ((0, 0), (QK_NOPE, 0)))
    e_b = e_a[:, swap] * rope_lane
    P["ea"] = slots(jnp.broadcast_to(e_a[:, None, :], (LANE, MLA_HEADS, QK_HEAD))).astype(BF16)
    P["eb"] = slots(jnp.broadcast_to(e_b[:, None, :], (LANE, MLA_HEADS, QK_HEAD))).astype(BF16)

    sign = jnp.concatenate([jnp.zeros(QK_NOPE), -jnp.ones(half), jnp.ones(half)]).astype(F32)

    def lane_pad(g):
        return jnp.pad(g, (0, HEAD_PAD - QK_HEAD)).reshape(1, HEAD_PAD)

    for name, g in (("gq", q_norm.astype(F32)), ("gk", k_norm.astype(F32))):
        P[name] = lane_pad(g)
        P[name + "s"] = lane_pad(g[swap] * sign)
    P["gqa"] = q_a_norm.astype(F32).reshape(1, Q_LORA)
    P["gkva"] = kv_a_norm.astype(F32).reshape(1, KV_LORA)
    inv = 1.0 / (ROPE_THETA ** (jnp.arange(0, QK_ROPE, 2, dtype=F32) / QK_ROPE))
    ang = jnp.arange(L, dtype=F32)[:, None] * inv[None, :]
    cos, sin = jnp.cos(ang), jnp.sin(ang)
    P["ctab"] = jnp.concatenate([jnp.ones((L, QK_NOPE), F32), cos, cos, jnp.zeros((L, HEAD_PAD - QK_HEAD), F32)], axis=1)
    P["stab"] = jnp.concatenate([jnp.zeros((L, QK_NOPE), F32), sin, sin, jnp.zeros((L, HEAD_PAD - QK_HEAD), F32)], axis=1)

    P["w_hy_o"] = w_hy_o.astype(BF16)
    P["w_mla_o"] = w_mla_o.astype(BF16)
    P["w_out"] = w_out.astype(BF16)
    P["norm_ffn"] = norm_ffn.astype(F32).reshape(1, D_MODEL)
    rw = jnp.pad(router_w.astype(F32), ((0, 0), (0, LANE - N_EXPERTS)))
    P["router_w_hi"] = rw.astype(BF16)
    P["router_w_lo"] = (rw - P["router_w_hi"].astype(F32)).astype(BF16)
    P["router_b"] = jnp.pad(router_b.astype(F32), (0, LANE - N_EXPERTS)).reshape(1, LANE)
    P["w_gu"] = w_gu.astype(F32)
    P["b_gu"] = b_gu.astype(F32).reshape(N_EXPERTS, 1, 2 * D_FF)
    P["w_down"] = w_down.astype(F32)
    P["b_down"] = b_down.astype(F32).reshape(N_EXPERTS, 1, D_MODEL)
    return P


def _tile(n, want):
    t = min(n, want)
    assert n % t == 0
    return t


def _layer(xs, norm_mix, w_in, hy_conv_w, hy_conv_b, hy_fw1, hy_fb1, hy_fw2, hy_fb2, hy_fw3, hy_freq, hy_decay, hy_skip,
           q_a_norm, w_uq, kv_a_norm, w_ukv, q_norm, k_norm, w_hy_o, w_mla_o, w_out,
           norm_ffn, router_w, router_b, w_gu, b_gu, w_down, b_down):
    L = xs[0].shape[1]
    assert all(x.shape[1] == L and x.shape[2] == D_MODEL for x in xs) and L % MXU == 0
    P = _prep_params(norm_mix, w_in, q_a_norm, w_uq, kv_a_norm, w_ukv, q_norm, k_norm, w_hy_o, w_mla_o, w_out,
                     norm_ffn, router_w, router_b, w_gu, b_gu, w_down, b_down, L)
    kf = _hyena_filters(L, hy_fw1, hy_fb1, hy_fw2, hy_fb2, hy_fw3, hy_freq, hy_decay)
    tm = _tile(L, 512)
    proj = [_inproj(x, P, tm) for x in xs]
    y_ts = _hyena([p[0] for p in proj], kf, hy_conv_w, hy_conv_b, hy_skip, L)
    mixed = []
    for x, (_, g, q, k, v), y_t in zip(xs, proj, y_ts):
        y_mla = _attention(q, k, v, _tile(L, 512))
        mixed.append(_merge(x, y_t, y_mla, g, P, _tile(L, 1024)))
    order = sorted(range(len(xs)), key=lambda n: -xs[n].shape[0])
    moe_out = _moe([mixed[n] for n in order], P, tm)
    outs = [None] * len(xs)
    for n, y in zip(order, moe_out):
        outs[n] = y.reshape(xs[n].shape)
    return outs


def kernel(x_prompt, x_sample, norm_mix, w_in, hy_conv_w, hy_conv_b, hy_fw1, hy_fb1, hy_fw2, hy_fb2, hy_fw3, hy_freq, hy_decay, hy_skip, q_a_norm, w_uq, kv_a_norm, w_ukv, q_norm, k_norm, w_hy_o, w_mla_o, w_out, norm_ffn, router_w, router_b, w_gu, b_gu, w_down, b_down):
    depth = norm_mix.shape[0]
    xs = [x_prompt, x_sample]
    weights = (norm_mix, w_in, hy_conv_w, hy_conv_b, hy_fw1, hy_fb1, hy_fw2, hy_fb2, hy_fw3, hy_freq, hy_decay, hy_skip,
               q_a_norm, w_uq, kv_a_norm, w_ukv, q_norm, k_norm, w_hy_o, w_mla_o, w_out,
               norm_ffn, router_w, router_b, w_gu, b_gu, w_down, b_down)
    for l in range(depth):
        xs = _layer(xs, *[w[l] for w in weights])
    return (xs[0], xs[1])
```

```python
import functools
import math

import jax
import jax.numpy as jnp
from jax import lax
from jax.experimental import pallas as pl
from jax.experimental.pallas import tpu as pltpu
from jax.experimental.pallas import tpu_sc as plsc

F32 = jnp.float32
BF16 = jnp.bfloat16

D_MODEL = 1024
HY_WIDTH = 512
HY_ORDER = 2
HY_EMB_BANDS = 16
HY_FILTER_DIM = 64
MLA_HEADS = 8
V_HEAD = 64
QK_NOPE = 64
QK_ROPE = 32
QK_HEAD = QK_NOPE + QK_ROPE
Q_LORA = 256
KV_LORA = 128
ROPE_THETA = 10000.0
HY_END = 3 * HY_WIDTH
Q_END = HY_END + Q_LORA
KV_END = Q_END + KV_LORA
KR_END = KV_END + QK_ROPE
G1_END = KR_END + D_MODEL
N_EXPERTS = 32
TOP_K = 4
D_FF = D_MODEL
SWIGLU_LIMIT = 7.0
SWIGLU_ALPHA = 1.702
EPS = 1e-6

LANE = 128
MXU = 256
BF16_ROWS = 16
HEAD_PAD = LANE
MLA_IN = 512
CONV_CB = 8
CONV_UNROLL = 4
MOE_BLOCK = 512
PACKED = D_MODEL // 2
SC_CHUNK = 64
VMEM_LIMIT = 52 * 1024 * 1024


def _cparams(n_axes):
    return pltpu.CompilerParams(dimension_semantics=("arbitrary",) * n_axes, vmem_limit_bytes=VMEM_LIMIT)


def _full(shape):
    n = len(shape)
    return pl.BlockSpec(shape, lambda *_: (0,) * n)


def _rms(x, n):
    return lax.rsqrt(jnp.sum(x * x, axis=-1, keepdims=True) * (1.0 / n) + EPS)


def _mla_heads(m, ct, st, gqa_ref, gkva_ref, gq_ref, gqs_ref, gk_ref, gks_ref,
               wqa_ref, wqb_ref, wka_ref, ea_ref, eb_ref, wv_ref, q_ref, k_ref, v_ref):
    mf = m.astype(F32)
    cq = mf[:, :Q_LORA]
    ckv = mf[:, Q_LORA:Q_LORA + KV_LORA]
    kr = m[:, Q_LORA + KV_LORA:Q_LORA + KV_LORA + LANE]
    cqn = (cq * _rms(cq, Q_LORA) * gqa_ref[...]).astype(BF16)
    ckvn = (ckv * _rms(ckv, KV_LORA) * gkva_ref[...]).astype(BF16)
    qa = jnp.dot(cqn, wqa_ref[...], preferred_element_type=F32)
    qb = jnp.dot(cqn, wqb_ref[...], preferred_element_type=F32)
    ka = jnp.dot(ckvn, wka_ref[...], preferred_element_type=F32) + jnp.dot(kr, ea_ref[...], preferred_element_type=F32)
    kb = jnp.dot(kr, eb_ref[...], preferred_element_type=F32)
    v_ref[...] = jnp.dot(ckvn, wv_ref[...], preferred_element_type=F32).astype(BF16)
    scale = QK_HEAD ** -0.5 * math.log2(math.e)
    qc, qs = gq_ref[...] * ct * scale, gqs_ref[...] * st * scale
    kc, ks = gk_ref[...] * ct, gks_ref[...] * st
    for h in range(MLA_HEADS):
        sl = slice(h * HEAD_PAD, (h + 1) * HEAD_PAD)
        q_ref[h] = (_rms(qa[:, sl], QK_HEAD) * (qa[:, sl] * qc + qb[:, sl] * qs)).astype(BF16)
        k_ref[h] = (_rms(ka[:, sl], QK_HEAD) * (ka[:, sl] * kc + kb[:, sl] * ks)).astype(BF16)


def _inproj_kernel(x_ref, g_ref, why_ref, wm_ref, wg_ref, ct_ref, st_ref, *refs):
    mla_refs, (u_ref, gate_ref, q_ref, k_ref, v_ref) = refs[:12], refs[12:]
    tm = x_ref.shape[0]
    sub = tm // 2
    for r0 in range(0, tm, sub):
        rows = pl.ds(r0, sub)
        x = x_ref[rows, :]
        r = lax.rsqrt(jnp.mean(x * x, axis=-1, keepdims=True) + EPS)
        h = (x * r * g_ref[...]).astype(BF16)
        m = jnp.dot(h, wm_ref[...], preferred_element_type=F32).astype(BF16)
        _mla_heads(m, ct_ref[rows, :], st_ref[rows, :], *mla_refs,
                   q_ref.at[:, rows, :], k_ref.at[:, rows, :], v_ref.at[rows, :])
        u_ref[:, rows] = lax.dot_general(why_ref[...], h, (((1,), (1,)), ((), ())), preferred_element_type=F32)
        gate_ref[rows, :] = jnp.dot(h, wg_ref[...], preferred_element_type=F32).astype(BF16)


def _inproj(x, P, tm):
    B, L, _ = x.shape
    w = [P[n] for n in ("w_hy_t", "w_m", "w_g")]
    mla = [P[n] for n in ("gqa", "gkva", "gq", "gqs", "gk", "gks", "wqa", "wqb", "wka", "ea", "eb", "wv")]
    return pl.pallas_call(
        _inproj_kernel,
        grid=(B, L // tm),
        in_specs=[pl.BlockSpec((None, tm, D_MODEL), lambda b, i: (b, i, 0)), _full((1, D_MODEL))]
        + [_full(a.shape) for a in w]
        + [pl.BlockSpec((tm, HEAD_PAD), lambda b, i: (i, 0)), pl.BlockSpec((tm, HEAD_PAD), lambda b, i: (i, 0))]
        + [_full(a.shape) for a in mla],
        out_specs=[
            pl.BlockSpec((None, HY_END, tm), lambda b, i: (b, 0, i)),
            pl.BlockSpec((None, tm, 2 * D_MODEL), lambda b, i: (b, i, 0)),
            pl.BlockSpec((None, MLA_HEADS, tm, HEAD_PAD), lambda b, i: (b, 0, i, 0)),
            pl.BlockSpec((None, MLA_HEADS, tm, HEAD_PAD), lambda b, i: (b, 0, i, 0)),
            pl.BlockSpec((None, tm, MLA_HEADS * V_HEAD), lambda b, i: (b, i, 0)),
        ],
        out_shape=[
            jax.ShapeDtypeStruct((B, HY_END, L), F32),
            jax.ShapeDtypeStruct((B, L, 2 * D_MODEL), BF16),
            jax.ShapeDtypeStruct((B, MLA_HEADS, L, HEAD_PAD), BF16),
            jax.ShapeDtypeStruct((B, MLA_HEADS, L, HEAD_PAD), BF16),
            jax.ShapeDtypeStruct((B, L, MLA_HEADS * V_HEAD), BF16),
        ],
        compiler_params=_cparams(2),
        name="inproj",
    )(x, P["norm_mix"], *w, P["ctab"][:L], P["stab"][:L], *mla)


def _filter_kernel(bands_ref, w1t_ref, w1c_ref, w1s_ref, b1_ref, w2_ref, b2_ref, fr_ref, w3f_ref, w3b_ref, dec_ref,
                   out_ref, *, L):
    hi = lax.Precision.HIGHEST
    n = 2 * L
    j = lax.broadcasted_iota(jnp.int32, (1, n), 1)
    pos = jnp.abs(j - L).astype(F32)
    t01 = pos / max(L - 1, 1)
    ang = ((2.0 * math.pi / L) * pos) * bands_ref[...]
    pre = (w1t_ref[...] * t01
           + jnp.dot(w1c_ref[...], jnp.cos(ang), precision=hi, preferred_element_type=F32)
           - jnp.dot(w1s_ref[...], jnp.sin(ang), precision=hi, preferred_element_type=F32)
           + b1_ref[...])
    z = jnp.sin(fr_ref[0] * pre)
    z = jnp.sin(fr_ref[1] * (jnp.dot(w2_ref[...], z, precision=hi, preferred_element_type=F32) + b2_ref[...]))
    win = jnp.exp(-t01 * jnp.abs(dec_ref[...]))
    h_b = jnp.dot(w3b_ref[...], z[:, :L], precision=hi, preferred_element_type=F32) * win[:, :L]
    h_f = jnp.dot(w3f_ref[...], z[:, L:], precision=hi, preferred_element_type=F32) * win[:, L:]
    out_ref[:, :L] = jnp.where(j[:, :L] == 0, 0.0, h_b)
    out_ref[:, L:] = h_f


def _hyena_filters(L, fw1, fb1, fw2, fb2, fw3, freq, decay):
    rows = HY_ORDER * HY_WIDTH
    rb = 256
    bands = jnp.linspace(1e-4, HY_EMB_BANDS - 1, HY_EMB_BANDS, dtype=F32).reshape(HY_EMB_BANDS, 1)
    w1 = fw1.astype(F32).T
    w3 = fw3.astype(F32).T
    fd = HY_FILTER_DIM
    return pl.pallas_call(
        functools.partial(_filter_kernel, L=L),
        grid=(rows // rb,),
        in_specs=[
            _full((HY_EMB_BANDS, 1)),
            _full((fd, 1)),
            _full((fd, HY_EMB_BANDS)),
            _full((fd, HY_EMB_BANDS)),
            _full((fd, 1)),
            _full((fd, fd)),
            _full((fd, 1)),
            _full((2, fd, 1)),
            pl.BlockSpec((rb, fd), lambda i: (i, 0)),
            pl.BlockSpec((rb, fd), lambda i: (i, 0)),
            pl.BlockSpec((rb, 1), lambda i: (i, 0)),
        ],
        out_specs=pl.BlockSpec((rb, 2 * L), lambda i: (i, 0)),
        out_shape=jax.ShapeDtypeStruct((rows, 2 * L), F32),
        compiler_params=_cparams(1),
        name="hyena_filters",
    )(bands, w1[:, :1], w1[:, 1:1 + HY_EMB_BANDS], w1[:, 1 + HY_EMB_BANDS:], fb1.astype(F32).reshape(fd, 1),
      fw2.astype(F32).T, fb2.astype(F32).reshape(fd, 1), freq.astype(F32).reshape(2, fd, 1),
      w3[:rows], w3[rows:], decay.astype(F32).reshape(rows, 1))


def _shift_conv(u, w0, w1, w2, b):
    L = u.shape[1]
    lane = lax.broadcasted_iota(jnp.int32, (1, L), 1)
    w0_row = jnp.where(lane == 0, 0.0, w0)
    w2_row = jnp.where(lane == L - 1, 0.0, w2)
    return pltpu.roll(u, 1, 1) * w0_row + u * w1 + pltpu.roll(u, L - 1, 1) * w2_row + b


def _toeplitz_conv(zs, tz_ref, L):
    nj = L // MXU
    zbs = [z.astype(BF16) for z in zs]
    accs = [[None] * nj for _ in zs]
    for d in range(-(nj - 1), nj):
        js = [J for J in range(nj) if 0 <= J + d < nj]
        pieces = [zb[:, MXU * J:MXU * (J + 1)] for zb in zbs for J in js]
        lhs = pieces[0] if len(pieces) == 1 else jnp.concatenate(pieces, axis=0)
        res = jnp.dot(lhs, tz_ref[:, L + MXU * d:L + MXU * (d + 1)], preferred_element_type=F32)
        row = 0
        for acc, z in zip(accs, zs):
            B = z.shape[0]
            for J in js:
                part = res[row:row + B]
                acc[J + d] = part if acc[J + d] is None else acc[J + d] + part
                row += B
    return [acc[0] if nj == 1 else jnp.concatenate(acc, axis=1) for acc in accs]


def _hyena_kernel(cw_ref, cb_ref, skip_ref, kf_ref, *refs, L, n_groups):
    in_refs = refs[:3 * n_groups]
    out_refs = refs[3 * n_groups:4 * n_groups]
    tz_ref = refs[4 * n_groups]
    c0 = pl.program_id(0) * CONV_CB

    def channels(p, carry):
        for u in range(CONV_UNROLL):
            c = p * CONV_UNROLL + u
            ch = c0 + c
            for o in range(HY_ORDER):
                row = jnp.broadcast_to(kf_ref[o, pl.ds(c, 1), :], (BF16_ROWS, 2 * L))
                slab = pltpu.bitcast(pltpu.roll(row, 0, 1, stride=1, stride_axis=0).astype(BF16), jnp.uint32)
                for a in range(MXU // BF16_ROWS):
                    shifted = slab if a == 0 else pltpu.roll(slab, BF16_ROWS * a, 1)
                    tz_ref[u, o, pl.ds(BF16_ROWS * a, BF16_ROWS), :] = pltpu.bitcast(shifted, BF16)
            secs = []
            for g in range(n_groups):
                stage = refs[4 * n_groups + 1 + g]
                sec = []
                for s, ref in enumerate(in_refs[3 * g:3 * g + 3]):
                    stage[u, s] = ref[:, c, :]
                    k = s * HY_WIDTH + ch
                    sec.append(_shift_conv(stage[u, s], cw_ref[0, k], cw_ref[1, k], cw_ref[2, k], cb_ref[k]))
                secs.append(sec)
            zs = [sec[2] for sec in secs]
            for o in range(HY_ORDER):
                ys = _toeplitz_conv(zs, tz_ref.at[u, o], L)
                zs = [sec[o] * (y + z * skip_ref[o, ch]) for sec, y, z in zip(secs, ys, zs)]
            for g in range(n_groups):
                out_refs[g][:, c, :] = zs[g]
        return carry

    lax.fori_loop(0, CONV_CB // CONV_UNROLL, channels, 0)


def _hyena(u_ts, kf, conv_w, conv_b, skip, L):
    n_groups = len(u_ts)
    nblk = HY_WIDTH // CONV_CB
    smem = pl.BlockSpec(memory_space=pltpu.SMEM)
    in_specs = [smem, smem, smem, pl.BlockSpec((HY_ORDER, CONV_CB, 2 * L), lambda i: (0, i, 0))]
    args = [conv_w.astype(F32), conv_b.astype(F32), skip.astype(F32), kf.reshape(HY_ORDER, HY_WIDTH, 2 * L)]
    out_specs, out_shape = [], []
    for u in u_ts:
        B = u.shape[0]
        for s in range(3):
            in_specs.append(pl.BlockSpec((B, CONV_CB, L), lambda i, s=s: (0, i + s * nblk, 0)))
            args.append(u)
        out_specs.append(pl.BlockSpec((B, CONV_CB, L), lambda i: (0, i, 0)))
        out_shape.append(jax.ShapeDtypeStruct((B, HY_WIDTH, L), F32))
    return pl.pallas_call(
        functools.partial(_hyena_kernel, L=L, n_groups=n_groups),
        grid=(nblk,),
        in_specs=in_specs,
        out_specs=out_specs,
        out_shape=out_shape,
        scratch_shapes=[pltpu.VMEM((CONV_UNROLL, HY_ORDER, MXU, 2 * L), BF16)]
        + [pltpu.VMEM((CONV_UNROLL, 3, u.shape[0], L), F32) for u in u_ts],
        compiler_params=_cparams(1),
        name="hyena_conv",
    )(*args)


def _attn_kernel(q_ref, k_ref, v_ref, o_ref):
    L = v_ref.shape[1]
    one_col = jnp.where(lax.broadcasted_iota(jnp.int32, (L, LANE), 1) == 0, 1.0, 0.0).astype(BF16)
    for pair in range(MLA_HEADS // 2):
        lanes = pl.ds(pair * LANE, LANE)
        v_ext = jnp.concatenate([v_ref[0, :, lanes], one_col], axis=1)
        outs = []
        for h in (2 * pair, 2 * pair + 1):
            s = lax.dot_general(q_ref[0, h], k_ref[0, h], (((1,), (1,)), ((), ())), preferred_element_type=F32)
            p = jnp.exp2(s - jnp.max(s, axis=-1, keepdims=True)).astype(BF16)
            pv = jnp.dot(p, v_ext, preferred_element_type=F32)
            outs.append(pv[:, :LANE] / pv[:, LANE:LANE + 1])
        lane = lax.broadcasted_iota(jnp.int32, outs[0].shape, 1)
        o_ref[0, :, lanes] = jnp.where(lane < V_HEAD, outs[0], outs[1]).astype(BF16)


def _attention(q, k, v, tq):
    B, H, L, _ = q.shape
    return pl.pallas_call(
        _attn_kernel,
        grid=(B, L // tq),
        in_specs=[
            pl.BlockSpec((1, H, tq, HEAD_PAD), lambda b, i: (b, 0, i, 0)),
            pl.BlockSpec((1, H, L, HEAD_PAD), lambda b, i: (b, 0, 0, 0)),
            pl.BlockSpec((1, L, H * V_HEAD), lambda b, i: (b, 0, 0)),
        ],
        out_specs=pl.BlockSpec((1, tq, H * V_HEAD), lambda b, i: (b, i, 0)),
        out_shape=jax.ShapeDtypeStruct((B, L, H * V_HEAD), BF16),
        compiler_params=_cparams(2),
        name="mla_attention",
    )(q, k, v)


def _pack_rows(a):
    bits = lax.bitcast_convert_type(a.astype(BF16).astype(F32), jnp.uint32)
    return (bits[:, :PACKED] >> 16) | (bits[:, PACKED:] & jnp.uint32(0xFFFF0000))


def _unpack_rows(p):
    lo = lax.bitcast_convert_type(p << 16, F32)
    hi = lax.bitcast_convert_type(p & jnp.uint32(0xFFFF0000), F32)
    return jnp.concatenate([lo, hi], axis=1)


def _merge_kernel(x_ref, yh_ref, ym_ref, g_ref, who_ref, wmo_ref, wo_ref, nf_ref, rwh_ref, rwl_ref, rb_ref,
                  x1_ref, h2_ref, ids_ref, gate_ref, cnt_ref):
    tm = x_ref.shape[0]
    sub = tm // 2

    @pl.when(jnp.logical_and(pl.program_id(0) == 0, pl.program_id(1) == 0))
    def _():
        cnt_ref[...] = jnp.zeros_like(cnt_ref)

    eid = lax.broadcasted_iota(jnp.int32, (N_EXPERTS, sub), 0)
    for r in range(0, tm, sub):
        rows = pl.ds(r, sub)
        g = g_ref[rows, :].astype(F32)
        a = lax.dot_general(yh_ref[:, rows].astype(BF16), who_ref[...], (((0,), (0,)), ((), ())),
                            preferred_element_type=F32)
        b = jnp.dot(ym_ref[rows, :], wmo_ref[...], preferred_element_type=F32)
        merged = jax.nn.sigmoid(g[:, :D_MODEL]) * a + jax.nn.sigmoid(g[:, D_MODEL:]) * b
        x1 = x_ref[rows, :] + jnp.dot(merged.astype(BF16), wo_ref[...], preferred_element_type=F32)
        x1_ref[rows, :] = x1
        h2 = x1 * lax.rsqrt(jnp.mean(x1 * x1, axis=-1, keepdims=True) + EPS) * nf_ref[...]
        h2_ref[rows, :] = _pack_rows(h2)
        h_hi = h2.astype(BF16)
        h_lo = (h2 - h_hi.astype(F32)).astype(BF16)
        logits = (jnp.dot(h_hi, rwh_ref[...], preferred_element_type=F32)
                  + jnp.dot(h_lo, rwh_ref[...], preferred_element_type=F32)
                  + jnp.dot(h_hi, rwl_ref[...], preferred_element_type=F32) + rb_ref[...])
        work = logits.T[:N_EXPERTS]
        picked = jnp.zeros((N_EXPERTS, sub), F32)
        vals, firsts = [], []
        for k in range(TOP_K):
            m = jnp.max(work, axis=0, keepdims=True)
            first = jnp.min(jnp.where(work == m, eid, N_EXPERTS), axis=0, keepdims=True)
            hot = eid == first
            vals.append(m)
            firsts.append(first)
            picked = jnp.where(hot, 1.0, picked)
            work = jnp.where(hot, -jnp.inf, work)
        ex = [jnp.exp(v - vals[0]) for v in vals]
        inv = 1.0 / (ex[0] + ex[1] + ex[2] + ex[3])
        ids_ref[:, rows] = jnp.concatenate(firsts + [jnp.zeros((8 - TOP_K, sub), jnp.int32)], axis=0)
        gates_t = jnp.concatenate([e * inv for e in ex] + [jnp.zeros((LANE - TOP_K, sub), F32)], axis=0)
        gate_ref[rows, :] = gates_t.T
        cnt_ref[...] += jnp.sum(picked, axis=1, keepdims=True)


def _merge(x, yh_t, ym, g, P, tm):
    B, L, _ = x.shape
    nt = L // tm
    w = [P[n] for n in ("w_hy_o", "w_mla_o", "w_out", "norm_ffn", "router_w_hi", "router_w_lo", "router_b")]
    return pl.pallas_call(
        _merge_kernel,
        grid=(B, nt),
        in_specs=[
            pl.BlockSpec((None, tm, D_MODEL), lambda b, i: (b, i, 0)),
            pl.BlockSpec((None, HY_WIDTH, tm), lambda b, i: (b, 0, i)),
            pl.BlockSpec((None, tm, MLA_HEADS * V_HEAD), lambda b, i: (b, i, 0)),
            pl.BlockSpec((None, tm, 2 * D_MODEL), lambda b, i: (b, i, 0)),
        ] + [_full(a.shape) for a in w],
        out_specs=[
            pl.BlockSpec((tm, D_MODEL), lambda b, i: (b * nt + i, 0)),
            pl.BlockSpec((tm, PACKED), lambda b, i: (b * nt + i, 0)),
            pl.BlockSpec((8, tm), lambda b, i: (0, b * nt + i)),
            pl.BlockSpec((tm, LANE), lambda b, i: (b * nt + i, 0)),
            pl.BlockSpec((N_EXPERTS, 1), lambda b, i: (0, 0)),
        ],
        out_shape=[
            jax.ShapeDtypeStruct((B * L, D_MODEL), F32),
            jax.ShapeDtypeStruct((B * L, PACKED), jnp.uint32),
            jax.ShapeDtypeStruct((8, B * L), jnp.int32),
            jax.ShapeDtypeStruct((B * L, LANE), F32),
            jax.ShapeDtypeStruct((N_EXPERTS, 1), F32),
        ],
        compiler_params=_cparams(2),
        name="merge_router",
    )(x, yh_t, ym, g, *w)


def _expert_kernel(be_ref, na_ref, end_ref, x_ref, wgu_hbm, bgu_ref, wd_hbm, bd_ref, y_ref,
                   wgu_buf, wd_buf, sems, slot_ref):
    i = pl.program_id(0)
    active = i < na_ref[0]
    expert = be_ref[i]
    first = jnp.logical_and(active, jnp.logical_or(i == 0, expert != be_ref[jnp.maximum(i - 1, 0)]))

    def weight_copies(e, s):
        return (pltpu.make_async_copy(wgu_hbm.at[e], wgu_buf.at[s], sems.at[0, s]),
                pltpu.make_async_copy(wd_hbm.at[e], wd_buf.at[s], sems.at[1, s]))

    @pl.when(i == 0)
    def _():
        slot_ref[0] = 1
        for cp in weight_copies(expert, 0):
            cp.start()

    @pl.when(first)
    def _():
        s = 1 - slot_ref[0]
        slot_ref[0] = s
        for cp in weight_copies(expert, s):
            cp.wait()
        run_end = end_ref[expert]

        @pl.when(run_end < na_ref[0])
        def _():
            for cp in weight_copies(be_ref[jnp.minimum(run_end, pl.num_programs(0) - 1)], 1 - s):
                cp.start()

    slot = slot_ref[0]

    @pl.when(active)
    def _():
        x = _unpack_rows(x_ref[...]).astype(BF16)
        gu = jnp.dot(x, wgu_buf[slot].astype(BF16), preferred_element_type=F32) + bgu_ref[0]
        gate = jnp.minimum(gu[:, :D_FF], SWIGLU_LIMIT)
        up = jnp.clip(gu[:, D_FF:], -SWIGLU_LIMIT, SWIGLU_LIMIT)
        act = (up + 1.0) * (gate * jax.nn.sigmoid(SWIGLU_ALPHA * gate))
        y = jnp.dot(act.astype(BF16), wd_buf[slot].astype(BF16), preferred_element_type=F32) + bd_ref[0]
        y_ref[...] = _pack_rows(y)

    @pl.when(i >= na_ref[0])
    def _():
        y_ref[...] = jnp.zeros_like(y_ref)


def _experts(plan, xs, P):
    n_rows = xs.shape[0]
    nb = n_rows // MOE_BLOCK
    grid_spec = pltpu.PrefetchScalarGridSpec(
        num_scalar_prefetch=3,
        grid=(nb,),
        in_specs=[
            pl.BlockSpec((MOE_BLOCK, PACKED), lambda i, be, *_: (i, 0)),
            pl.BlockSpec(memory_space=pl.ANY),
            pl.BlockSpec((1, 1, 2 * D_FF), lambda i, be, *_: (be[i], 0, 0)),
            pl.BlockSpec(memory_space=pl.ANY),
            pl.BlockSpec((1, 1, D_MODEL), lambda i, be, *_: (be[i], 0, 0)),
        ],
        out_specs=pl.BlockSpec((MOE_BLOCK, PACKED), lambda i, be, *_: (i, 0)),
        scratch_shapes=[
            pltpu.VMEM((2, D_MODEL, 2 * D_FF), F32),
            pltpu.VMEM((2, D_FF, D_MODEL), F32),
            pltpu.SemaphoreType.DMA((2, 2)),
            pltpu.SMEM((1,), jnp.int32),
        ],
    )
    return pl.pallas_call(
        _expert_kernel,
        grid_spec=grid_spec,
        out_shape=jax.ShapeDtypeStruct((n_rows, PACKED), jnp.uint32),
        compiler_params=_cparams(1),
        name="moe_experts",
    )(*plan, xs, P["w_gu"], P["b_gu"], P["w_down"], P["b_down"])


def _combine_kernel(x1_ref, yk_ref, gt_ref, o_ref):
    gt = gt_ref[...]
    acc = _unpack_rows(yk_ref[0]) * gt[:, 0:1]
    for k in range(1, TOP_K):
        acc = acc + _unpack_rows(yk_ref[k]) * gt[:, k:k + 1]
    o_ref[...] = x1_ref[...] + acc


def _combine(x1, yk, gates, tm):
    T = x1.shape[0]
    return pl.pallas_call(
        _combine_kernel,
        grid=(T // tm,),
        in_specs=[
            pl.BlockSpec((tm, D_MODEL), lambda i: (i, 0)),
            pl.BlockSpec((TOP_K, tm, PACKED), lambda i: (0, i, 0)),
            pl.BlockSpec((tm, LANE), lambda i: (i, 0)),
        ],
        out_specs=pl.BlockSpec((tm, D_MODEL), lambda i: (i, 0)),
        out_shape=jax.ShapeDtypeStruct((T, D_MODEL), F32),
        compiler_params=_cparams(1),
        name="moe_combine",
    )(x1, yk, gates)


def _route_kernel(ids_ref, start_ref, dest_ref, run_ref, *, tm):
    @pl.when(pl.program_id(0) == 0)
    def _():
        run_ref[...] = jnp.zeros_like(run_ref)

    eid = lax.broadcasted_iota(jnp.int32, (N_EXPERTS, tm), 0)
    ids = ids_ref[...]
    hots = [eid == ids[k:k + 1] for k in range(TOP_K)]
    chosen = jnp.zeros((N_EXPERTS, tm), F32)
    for hot in hots:
        chosen = jnp.where(hot, 1.0, chosen)
    r = lax.broadcasted_iota(jnp.int32, (tm, tm), 0)
    c = lax.broadcasted_iota(jnp.int32, (tm, tm), 1)
    earlier = jnp.where(r < c, 1.0, 0.0).astype(BF16)
    before = jnp.dot(chosen.astype(BF16), earlier, preferred_element_type=F32) + (run_ref[...] + start_ref[...])
    rows = [jnp.sum(jnp.where(hot, before, 0.0), axis=0, keepdims=True) for hot in hots]
    dest_ref[...] = jnp.concatenate(rows + [jnp.zeros((8 - TOP_K, tm), F32)], axis=0).astype(jnp.int32)
    run_ref[...] += jnp.sum(chosen, axis=1, keepdims=True)


def _route(ids_t, counts, tm):
    T = ids_t.shape[1]
    counts = counts[:, 0].astype(jnp.int32)
    padded = (counts + MOE_BLOCK - 1) // MOE_BLOCK * MOE_BLOCK
    pad_end = jnp.cumsum(padded)
    start = (pad_end - padded).astype(F32).reshape(N_EXPERTS, 1)
    dest = pl.pallas_call(
        functools.partial(_route_kernel, tm=tm),
        grid=(T // tm,),
        in_specs=[pl.BlockSpec((8, tm), lambda i: (0, i)), _full((N_EXPERTS, 1))],
        out_specs=pl.BlockSpec((8, tm), lambda i: (0, i)),
        out_shape=jax.ShapeDtypeStruct((8, T), jnp.int32),
        scratch_shapes=[pltpu.VMEM((N_EXPERTS, 1), F32)],
        compiler_params=_cparams(1),
        name="moe_route",
    )(ids_t, start)
    n_blocks = -(-(T * TOP_K + N_EXPERTS * (MOE_BLOCK - 1)) // MOE_BLOCK)
    block_start = jnp.arange(n_blocks, dtype=jnp.int32) * MOE_BLOCK
    block_e = jnp.sum((block_start[:, None] >= pad_end[None, :]).astype(jnp.int32), axis=1)
    block_e = jnp.minimum(block_e, N_EXPERTS - 1)
    end_blk = (pad_end // MOE_BLOCK).astype(jnp.int32)
    plan = (block_e.astype(jnp.int32), end_blk[-1:], end_blk)
    dest_flat = dest[:TOP_K].reshape(TOP_K * T)
    return dest_flat, plan, n_blocks * MOE_BLOCK


def _sc_workers():
    info = plsc.get_sparse_core_info()
    return info.num_cores, info.num_cores * info.num_subcores


def _sc_scatter_rows(rows, dest_flat, n_rows):
    T, W = rows.shape
    n_cores, n_workers = _sc_workers()
    per_w = T // n_workers
    assert per_w * n_workers == T and per_w % SC_CHUNK == 0

    def body(rows_hbm, dest_hbm, out_hbm, buf, *idx):
        wid = lax.axis_index("s") * n_cores + lax.axis_index("c")

        @pl.loop(0, per_w // SC_CHUNK)
        def _(j):
            off = pl.multiple_of(wid * per_w + j * SC_CHUNK, SC_CHUNK)
            pltpu.sync_copy(rows_hbm.at[pl.ds(off, SC_CHUNK)], buf)
            for k in range(TOP_K):
                pltpu.sync_copy(dest_hbm.at[pl.ds(k * T + off, SC_CHUNK)], idx[k])
            for k in range(TOP_K):
                pltpu.sync_copy(buf, out_hbm.at[idx[k]])

    return pl.kernel(
        body,
        out_type=jax.ShapeDtypeStruct((n_rows, W), rows.dtype),
        mesh=plsc.VectorSubcoreMesh(core_axis_name="c", subcore_axis_name="s"),
        scratch_types=[pltpu.VMEM((SC_CHUNK, W), rows.dtype)] + [pltpu.VMEM((SC_CHUNK,), jnp.int32)] * TOP_K,
        name="moe_dispatch_sc",
    )(rows, dest_flat)


def _sc_gather_rows(table, dest_flat):
    N = dest_flat.shape[0]
    W = table.shape[1]
    n_cores, n_workers = _sc_workers()
    per_w = N // n_workers
    assert per_w * n_workers == N and per_w % SC_CHUNK == 0

    def body(table_hbm, dest_hbm, out_hbm, buf, idx):
        wid = lax.axis_index("s") * n_cores + lax.axis_index("c")

        @pl.loop(0, per_w // SC_CHUNK)
        def _(j):
            off = pl.multiple_of(wid * per_w + j * SC_CHUNK, SC_CHUNK)
            pltpu.sync_copy(dest_hbm.at[pl.ds(off, SC_CHUNK)], idx)
            pltpu.sync_copy(table_hbm.at[idx], buf)
            pltpu.sync_copy(buf, out_hbm.at[pl.ds(off, SC_CHUNK)])

    return pl.kernel(
        body,
        out_type=jax.ShapeDtypeStruct((N, W), table.dtype),
        mesh=plsc.VectorSubcoreMesh(core_axis_name="c", subcore_axis_name="s"),
        scratch_types=[pltpu.VMEM((SC_CHUNK, W), table.dtype), pltpu.VMEM((SC_CHUNK,), jnp.int32)],
        name="moe_gather_sc",
    )(table, dest_flat)


def _moe(groups, P, tm):
    staged = []
    for x1, h2p, ids, gates, counts in groups:
        dest_flat, plan, n_rows = _route(ids, counts, 4 * tm if x1.shape[0] % (4 * tm) == 0 else tm)
        staged.append((dest_flat, plan, _sc_scatter_rows(h2p, dest_flat, n_rows)))
    ys = []
    for dest_flat, plan, xs in staged:
        if ys:
            xs, ys[-1] = lax.optimization_barrier((xs, ys[-1]))
        ys.append(_experts(plan, xs, P))
    outs = []
    for (x1, _, _, gates, _), (dest_flat, _, _), y in zip(groups, staged, ys):
        yk = _sc_gather_rows(y, dest_flat).reshape(TOP_K, x1.shape[0], PACKED)
        outs.append(_combine(x1, yk, gates, 2 * tm if x1.shape[0] % (2 * tm) == 0 else tm))
    return outs


def _prep_params(norm_mix, w_in, q_a_norm, w_uq, kv_a_norm, w_ukv, q_norm, k_norm, w_hy_o, w_mla_o, w_out,
                 norm_ffn, router_w, router_b, w_gu, b_gu, w_down, b_down, L):
    P = {}
    P["norm_mix"] = norm_mix.astype(F32).reshape(1, D_MODEL)
    P["w_hy_t"] = w_in[:, :HY_END].T.astype(BF16)
    P["w_m"] = jnp.pad(w_in[:, HY_END:KR_END], ((0, 0), (0, MLA_IN - (KR_END - HY_END)))).astype(BF16)
    P["w_g"] = w_in[:, KR_END:].astype(BF16)

    half = QK_ROPE // 2
    swap = jnp.concatenate([jnp.arange(QK_NOPE), QK_NOPE + half + jnp.arange(half), QK_NOPE + jnp.arange(half)])
    rope_lane = (jnp.arange(QK_HEAD) >= QK_NOPE)

    def slots(w):
        return jnp.pad(w, ((0, 0), (0, 0), (0, HEAD_PAD - QK_HEAD))).reshape(w.shape[0], MLA_HEADS * HEAD_PAD)

    wq = w_uq.astype(F32).reshape(Q_LORA, MLA_HEADS, QK_HEAD)
    P["wqa"] = slots(wq).astype(BF16)
    P["wqb"] = slots(jnp.where(rope_lane, wq[:, :, swap], 0.0)).astype(BF16)
    wkv = w_ukv.astype(F32).reshape(KV_LORA, MLA_HEADS, QK_NOPE + V_HEAD)
    P["wka"] = slots(jnp.pad(wkv[:, :, :QK_NOPE], ((0, 0), (0, 0), (0, QK_ROPE)))).astype(BF16)
    P["wv"] = wkv[:, :, QK_NOPE:].reshape(KV_LORA, MLA_HEADS * V_HEAD).astype(BF16)
    eye = jnp.eye(LANE, QK_ROPE, dtype=F32)
    e_a = jnp.pad(eye, ((0, 0), (QK_NOPE, 0)))
    e_b = e_a[:, swap] * rope_lane
    P["ea"] = slots(jnp.broadcast_to(e_a[:, None, :], (LANE, MLA_HEADS, QK_HEAD))).astype(BF16)
    P["eb"] = slots(jnp.broadcast_to(e_b[:, None, :], (LANE, MLA_HEADS, QK_HEAD))).astype(BF16)

    sign = jnp.concatenate([jnp.zeros(QK_NOPE), -jnp.ones(half), jnp.ones(half)]).astype(F32)

    def lane_pad(g):
        return jnp.pad(g, (0, HEAD_PAD - QK_HEAD)).reshape(1, HEAD_PAD)

    for name, g in (("gq", q_norm.astype(F32)), ("gk", k_norm.astype(F32))):
        P[name] = lane_pad(g)
        P[name + "s"] = lane_pad(g[swap] * sign)
    P["gqa"] = q_a_norm.astype(F32).reshape(1, Q_LORA)
    P["gkva"] = kv_a_norm.astype(F32).reshape(1, KV_LORA)
    inv = 1.0 / (ROPE_THETA ** (jnp.arange(0, QK_ROPE, 2, dtype=F32) / QK_ROPE))
    ang = jnp.arange(L, dtype=F32)[:, None] * inv[None, :]
    cos, sin = jnp.cos(ang), jnp.sin(ang)
    P["ctab"] = jnp.concatenate([jnp.ones((L, QK_NOPE), F32), cos, cos, jnp.zeros((L, HEAD_PAD - QK_HEAD), F32)], axis=1)
    P["stab"] = jnp.concatenate([jnp.zeros((L, QK_NOPE), F32), sin, sin, jnp.zeros((L, HEAD_PAD - QK_HEAD), F32)], axis=1)

    P["w_hy_o"] = w_hy_o.astype(BF16)
    P["w_mla_o"] = w_mla_o.astype(BF16)
    P["w_out"] = w_out.astype(BF16)
    P["norm_ffn"] = norm_ffn.astype(F32).reshape(1, D_MODEL)
    rw = jnp.pad(router_w.astype(F32), ((0, 0), (0, LANE - N_EXPERTS)))
    P["router_w_hi"] = rw.astype(BF16)
    P["router_w_lo"] = (rw - P["router_w_hi"].astype(F32)).astype(BF16)
    P["router_b"] = jnp.pad(router_b.astype(F32), (0, LANE - N_EXPERTS)).reshape(1, LANE)
    P["w_gu"] = w_gu.astype(F32)
    P["b_gu"] = b_gu.astype(F32).reshape(N_EXPERTS, 1, 2 * D_FF)
    P["w_down"] = w_down.astype(F32)
    P["b_down"] = b_down.astype(F32).reshape(N_EXPERTS, 1, D_MODEL)
    return P


def _tile(n, want):
    t = min(n, want)
    assert n % t == 0
    return t


def _layer(xs, norm_mix, w_in, hy_conv_w, hy_conv_b, hy_fw1, hy_fb1, hy_fw2, hy_fb2, hy_fw3, hy_freq, hy_decay, hy_skip,
           q_a_norm, w_uq, kv_a_norm, w_ukv, q_norm, k_norm, w_hy_o, w_mla_o, w_out,
           norm_ffn, router_w, router_b, w_gu, b_gu, w_down, b_down):
    L = xs[0].shape[1]
    assert all(x.shape[1] == L and x.shape[2] == D_MODEL for x in xs) and L % MXU == 0
    P = _prep_params(norm_mix, w_in, q_a_norm, w_uq, kv_a_norm, w_ukv, q_norm, k_norm, w_hy_o, w_mla_o, w_out,
                     norm_ffn, router_w, router_b, w_gu, b_gu, w_down, b_down, L)
    kf = _hyena_filters(L, hy_fw1, hy_fb1, hy_fw2, hy_fb2, hy_fw3, hy_freq, hy_decay)
    tm = _tile(L, 512)
    proj = [_inproj(x, P, tm) for x in xs]
    y_ts = _hyena([p[0] for p in proj], kf, hy_conv_w, hy_conv_b, hy_skip, L)
    mixed = []
    for x, (_, g, q, k, v), y_t in zip(xs, proj, y_ts):
        y_mla = _attention(q, k, v, _tile(L, 1024))
        mixed.append(_merge(x, y_t, y_mla, g, P, _tile(L, 1024)))
    order = sorted(range(len(xs)), key=lambda n: -xs[n].shape[0])
    moe_out = _moe([mixed[n] for n in order], P, tm)
    outs = [None] * len(xs)
    for n, y in zip(order, moe_out):
        outs[n] = y.reshape(xs[n].shape)
    return outs


def kernel(x_prompt, x_sample, norm_mix, w_in, hy_conv_w, hy_conv_b, hy_fw1, hy_fb1, hy_fw2, hy_fb2, hy_fw3, hy_freq, hy_decay, hy_skip, q_a_norm, w_uq, kv_a_norm, w_ukv, q_norm, k_norm, w_hy_o, w_mla_o, w_out, norm_ffn, router_w, router_b, w_gu, b_gu, w_down, b_down):
    depth = norm_mix.shape[0]
    xs = [x_prompt, x_sample]
    weights = (norm_mix, w_in, hy_conv_w, hy_conv_b, hy_fw1, hy_fb1, hy_fw2, hy_fb2, hy_fw3, hy_freq, hy_decay, hy_skip,
               q_a_norm, w_uq, kv_a_norm, w_ukv, q_norm, k_norm, w_hy_o, w_mla_o, w_out,
               norm_ffn, router_w, router_b, w_gu, b_gu, w_down, b_down)
    for l in range(depth):
        xs = _layer(xs, *[w[l] for w in weights])
    return (xs[0], xs[1])
```
